```python
import math
import jax, jax.numpy as jnp
from jax import lax
import numpy as np

D_MODEL = 1024
BATCH = 16
SEQ = 2048
DEPTH = 1

PLE_DIM = 256
EPS = 1e-6

GLA_HEADS = 4
GLA_DK = 64
GLA_DV = 128
GLA_LOWRANK = 16
GLA_TAU = 16.0
GLA_CHUNK = 64

DIFF_HEADS = 4
DIFF_DH = 64
DIFF_DV = 2 * DIFF_DH
ROPE_THETA = 500000.0
ROPE_DIM = DIFF_DH // 4
Q_BLOCK = 128

N_GROUPS = 4
EXPERTS_PER_GROUP = 8
N_EXPERTS = N_GROUPS * EXPERTS_PER_GROUP
TOP_K = 2
D_EXPERT = 256

SPLIT_SIZES = (
    GLA_HEADS * GLA_DK,
    GLA_HEADS * GLA_DK,
    GLA_HEADS * GLA_DV,
    GLA_HEADS * GLA_DV,
    GLA_LOWRANK,
    DIFF_HEADS * 2 * DIFF_DH,
    DIFF_HEADS * 2 * DIFF_DH,
    DIFF_HEADS * DIFF_DV,
    D_MODEL,
    D_MODEL,
)
D_IN = sum(SPLIT_SIZES)

kernel_name = "hybrid_gla_diffattn_hmoe_block"


def rmsnorm(x, g):
    xf = x.astype(jnp.float32)
    y = xf * lax.rsqrt(jnp.mean(xf * xf, axis=-1, keepdims=True) + EPS)
    return (y * g.astype(jnp.float32)).astype(x.dtype)


def rope_partial(x, pos):
    half = ROPE_DIM // 2
    inv = ROPE_THETA ** (-jnp.arange(0, ROPE_DIM, 2, dtype=jnp.float32) / ROPE_DIM)
    ang = pos.astype(jnp.float32)[..., None] * inv
    cos = jnp.cos(ang)[:, :, None, :]
    sin = jnp.sin(ang)[:, :, None, :]
    xr = x[..., :ROPE_DIM].astype(jnp.float32)
    x1, x2 = xr[..., :half], xr[..., half:]
    rot = jnp.concatenate([x1 * cos - x2 * sin, x2 * cos + x1 * sin], axis=-1)
    return jnp.concatenate([rot.astype(x.dtype), x[..., ROPE_DIM:]], axis=-1)


def gla_branch(q, k, v, log_a):
    B, S, H, DK = q.shape
    DV = v.shape[-1]
    N = S // GLA_CHUNK

    def chunks(t):
        return t.astype(jnp.float32).reshape(B, N, GLA_CHUNK, H, t.shape[-1]).transpose(0, 3, 1, 2, 4)

    qc, kc, vc, lac = chunks(q), chunks(k), chunks(v), chunks(log_a)
    b = jnp.cumsum(lac, axis=3)
    b_last = b[:, :, :, -1, :]
    q_dec = qc * jnp.exp(b)
    k_inv = kc * jnp.exp(-b)
    causal = jnp.tril(jnp.ones((GLA_CHUNK, GLA_CHUNK), dtype=bool))
    att = jnp.einsum('bhncd,bhnsd->bhncs', q_dec, k_inv)
    att = jnp.where(causal, att, 0.0)
    o_intra = jnp.einsum('bhncs,bhnsv->bhncv', att, vc)

    k_end = kc * jnp.exp(b_last[:, :, :, None, :] - b)
    u = jnp.einsum('bhncd,bhncv->bhndv', k_end, vc)
    decay = jnp.exp(b_last)

    def step(state, inp):
        dec, un = inp
        return state * dec[..., None] + un, state

    s0 = jnp.zeros((B, H, DK, DV), jnp.float32)
    _, s_prev = lax.scan(step, s0, (jnp.moveaxis(decay, 2, 0), jnp.moveaxis(u, 2, 0)))
    s_prev = jnp.moveaxis(s_prev, 0, 2)
    o_inter = jnp.einsum('bhncd,bhndv->bhncv', q_dec, s_prev)
    o = o_intra + o_inter
    return o.transpose(0, 2, 3, 1, 4).reshape(B, S, H, DV)


def diff_attention(q, k, v, lam):
    B, H, _, S, Dh = q.shape
    DV = v.shape[-1]
    nb = S // Q_BLOCK
    scale = Dh ** -0.5
    qb = q.reshape(B, H, 2, nb, Q_BLOCK, Dh).transpose(3, 0, 1, 2, 4, 5)
    kpos = jnp.arange(S)
    neg = jnp.finfo(jnp.float32).min

    def block(args):
        qblk, i = args
        s = jnp.einsum('bhcqd,bhckd->bhcqk', qblk, k).astype(jnp.float32) * scale
        qpos = i * Q_BLOCK + jnp.arange(Q_BLOCK)
        mask = qpos[:, None] >= kpos[None, :]
        s = jnp.where(mask, s, neg)
        a = jax.nn.softmax(s, axis=-1)
        w = a[:, :, 0] - lam * a[:, :, 1]
        return jnp.einsum('bhqk,bhkv->bhqv', w.astype(v.dtype), v)

    out = lax.map(block, (qb, jnp.arange(nb)))
    return out.transpose(1, 0, 3, 2, 4).reshape(B, S, H, DV)


def token_mixers(n, positions, w_in, w_a2, b_a, gla_norm, lq1, lk1, lq2, lk2,
                 diff_subln, w_branch_a, w_branch_b, w_out, lambda_init):
    B, S, _ = n.shape
    proj = n @ w_in
    idx = [int(c) for c in np.cumsum(SPLIT_SIZES)[:-1]]
    (g_q, g_k, g_v, g_r, g_al, d_q, d_k, d_v, gate_a, gate_b) = jnp.split(proj, idx, axis=-1)

    gq = g_q.reshape(B, S, GLA_HEADS, GLA_DK) * (GLA_DK ** -0.5)
    gk = g_k.reshape(B, S, GLA_HEADS, GLA_DK)
    gv = g_v.reshape(B, S, GLA_HEADS, GLA_DV)
    a_logit = (g_al @ w_a2 + b_a).astype(jnp.float32)
    log_a = (jax.nn.log_sigmoid(a_logit) / GLA_TAU).reshape(B, S, GLA_HEADS, GLA_DK)
    o_a = gla_branch(gq, gk, gv, log_a)
    o_a = rmsnorm(o_a, gla_norm).reshape(B, S, GLA_HEADS * GLA_DV).astype(n.dtype)
    o_a = o_a * jax.nn.silu(g_r)
    y_a = o_a @ w_branch_a

    dq = rope_partial(d_q.reshape(B, S, DIFF_HEADS * 2, DIFF_DH), positions)
    dk = rope_partial(d_k.reshape(B, S, DIFF_HEADS * 2, DIFF_DH), positions)
    dq = dq.reshape(B, S, DIFF_HEADS, 2, DIFF_DH).transpose(0, 2, 3, 1, 4)
    dk = dk.reshape(B, S, DIFF_HEADS, 2, DIFF_DH).transpose(0, 2, 3, 1, 4)
    dv = d_v.reshape(B, S, DIFF_HEADS, DIFF_DV).transpose(0, 2, 1, 3)
    lam = (jnp.exp(jnp.sum(lq1.astype(jnp.float32) * lk1.astype(jnp.float32)))
           - jnp.exp(jnp.sum(lq2.astype(jnp.float32) * lk2.astype(jnp.float32)))
           + lambda_init)
    o_b = diff_attention(dq, dk, dv, lam)
    o_b = rmsnorm(o_b, diff_subln) * (1.0 - lambda_init)
    y_b = o_b.reshape(B, S, DIFF_HEADS * DIFF_DV).astype(n.dtype) @ w_branch_b

    merged = jax.nn.sigmoid(gate_a) * y_a + jax.nn.sigmoid(gate_b) * y_b
    return merged @ w_out


def hierarchical_moe(n, w_rg, b_rg, w_re, b_re, w_gate, w_up, w_down):
    B, S, D = n.shape
    xt = n.reshape(B * S, D)
    T = xt.shape[0]
    g_logits = (xt @ w_rg + b_rg).astype(jnp.float32)
    g_prob = jax.nn.softmax(g_logits, axis=-1)
    g_p, g_idx = lax.top_k(g_prob, 1)
    e_logits = (xt @ w_re + b_re).astype(jnp.float32).reshape(T, N_GROUPS, EXPERTS_PER_GROUP)
    e_logits = jnp.take_along_axis(e_logits, g_idx[:, :, None], axis=1)[:, 0]
    e_prob = jax.nn.softmax(e_logits, axis=-1)
    e_p, e_idx = lax.top_k(e_prob, TOP_K)
    e_p = e_p / jnp.sum(e_p, axis=-1, keepdims=True)
    weights = g_p * e_p
    eid = g_idx * EXPERTS_PER_GROUP + e_idx
    comb = jnp.sum(jax.nn.one_hot(eid, N_EXPERTS, dtype=jnp.float32) * weights[..., None], axis=1)
    comb = comb.astype(n.dtype)
    y = jnp.zeros_like(xt)
    for e in range(N_EXPERTS):
        he = jax.nn.silu(xt @ w_gate[e]) * (xt @ w_up[e])
        y = y + comb[:, e:e + 1] * (he @ w_down[e])
    return y.reshape(B, S, D)


def setup_inputs(seed: int = 0) -> dict:
    key = jax.random.key(seed)
    ks = iter(jax.random.split(key, 40))

    def nrm(shape, fan_in):
        return jax.random.normal(next(ks), shape, jnp.float32) * (fan_in ** -0.5)

    def gain(shape):
        return 1.0 + 0.01 * jax.random.normal(next(ks), shape, jnp.float32)

    x = jax.random.normal(next(ks), (BATCH, SEQ, D_MODEL), jnp.float32)
    p = jax.random.normal(next(ks), (DEPTH, BATCH, SEQ, PLE_DIM), jnp.float32)
    offs = jax.random.randint(next(ks), (BATCH, 1), 0, 4096, dtype=jnp.int32)
    positions = offs + jnp.arange(SEQ, dtype=jnp.int32)[None, :]
    return {
        "x": x,
        "p": p,
        "positions": positions,
        "attn_norm": gain((DEPTH, D_MODEL)),
        "w_in": nrm((DEPTH, D_MODEL, D_IN), D_MODEL),
        "w_a2": nrm((DEPTH, GLA_LOWRANK, GLA_HEADS * GLA_DK), GLA_LOWRANK),
        "b_a": 0.1 * jax.random.normal(next(ks), (DEPTH, GLA_HEADS * GLA_DK), jnp.float32),
        "gla_norm": gain((DEPTH, GLA_DV)),
        "lambda_q1": 0.1 * jax.random.normal(next(ks), (DEPTH, DIFF_DH), jnp.float32),
        "lambda_k1": 0.1 * jax.random.normal(next(ks), (DEPTH, DIFF_DH), jnp.float32),
        "lambda_q2": 0.1 * jax.random.normal(next(ks), (DEPTH, DIFF_DH), jnp.float32),
        "lambda_k2": 0.1 * jax.random.normal(next(ks), (DEPTH, DIFF_DH), jnp.float32),
        "diff_subln": gain((DEPTH, DIFF_DV)),
        "w_branch_a": nrm((DEPTH, GLA_HEADS * GLA_DV, D_MODEL), GLA_HEADS * GLA_DV),
        "w_branch_b": nrm((DEPTH, DIFF_HEADS * DIFF_DV, D_MODEL), DIFF_HEADS * DIFF_DV),
        "w_out": nrm((DEPTH, D_MODEL, D_MODEL), D_MODEL),
        "ffn_norm": gain((DEPTH, D_MODEL)),
        "w_router_group": nrm((DEPTH, D_MODEL, N_GROUPS), D_MODEL),
        "b_router_group": 0.01 * jax.random.normal(next(ks), (DEPTH, N_GROUPS), jnp.float32),
        "w_router_expert": nrm((DEPTH, D_MODEL, N_EXPERTS), D_MODEL),
        "b_router_expert": 0.01 * jax.random.normal(next(ks), (DEPTH, N_EXPERTS), jnp.float32),
        "w_gate": nrm((DEPTH, N_EXPERTS, D_MODEL, D_EXPERT), D_MODEL),
        "w_up": nrm((DEPTH, N_EXPERTS, D_MODEL, D_EXPERT), D_MODEL),
        "w_down": nrm((DEPTH, N_EXPERTS, D_EXPERT, D_MODEL), D_EXPERT),
        "w_ple": nrm((DEPTH, PLE_DIM, D_MODEL), PLE_DIM),
        "ple_norm": gain((DEPTH, D_MODEL)),
        "w_ple_gate": nrm((DEPTH, D_MODEL, D_MODEL), D_MODEL),
        "final_norm": gain((D_MODEL,)),
    }


def reference(x, p, positions, attn_norm, w_in, w_a2, b_a, gla_norm,
              lambda_q1, lambda_k1, lambda_q2, lambda_k2, diff_subln,
              w_branch_a, w_branch_b, w_out, ffn_norm,
              w_router_group, b_router_group, w_router_expert, b_router_expert,
              w_gate, w_up, w_down, w_ple, ple_norm, w_ple_gate, final_norm):
    h = x
    for i in range(DEPTH):
        lambda_init = 0.8 - 0.6 * math.exp(-0.3 * i)
        n = rmsnorm(h, attn_norm[i])
        h = h + token_mixers(n, positions, w_in[i], w_a2[i], b_a[i], gla_norm[i],
                             lambda_q1[i], lambda_k1[i], lambda_q2[i], lambda_k2[i],
                             diff_subln[i], w_branch_a[i], w_branch_b[i], w_out[i],
                             lambda_init)
        n2 = rmsnorm(h, ffn_norm[i])
        h = h + hierarchical_moe(n2, w_router_group[i], b_router_group[i],
                                 w_router_expert[i], b_router_expert[i],
                                 w_gate[i], w_up[i], w_down[i])
        e = rmsnorm(p[i] @ w_ple[i], ple_norm[i])
        h = h + jax.nn.sigmoid(h @ w_ple_gate[i]) * e
    return rmsnorm(h, final_norm)
```

```python
import functools
import math

import jax
import jax.numpy as jnp
import numpy as np
from jax import lax
from jax.experimental import pallas as pl
from jax.experimental.pallas import tpu as pltpu

EPS = 1e-6

GLA_HEADS = 4
GLA_DK = 64
GLA_DV = 128
GLA_LOWRANK = 16
GLA_TAU = 16.0
GLA_CHUNK = 64
GLA_QK = GLA_HEADS * GLA_DK
GLA_V = GLA_HEADS * GLA_DV

DIFF_HEADS = 4
DIFF_DH = 64
DIFF_DV = 2 * DIFF_DH
DIFF_W = DIFF_HEADS * DIFF_DV
ROPE_THETA = 500000.0
ROPE_DIM = DIFF_DH // 4
ROPE_HALF = ROPE_DIM // 2

N_GROUPS = 4
EXPERTS_PER_GROUP = 8
N_EXPERTS = N_GROUPS * EXPERTS_PER_GROUP
D_EXPERT = 256

LANES = 128
GAL_PAD = LANES
VMEM_LIMIT = 56 * 1024 * 1024

BF16 = jnp.bfloat16
F32 = jnp.float32
NT_DIMS = (((1,), (1,)), ((), ()))
TN_DIMS = (((0,), (0,)), ((), ()))
NEG_BIG = -1e30


def _rms(x, g):
    return x * lax.rsqrt(jnp.mean(x * x, axis=-1, keepdims=True) + EPS) * g


def _cparams(semantics):
    return pltpu.CompilerParams(dimension_semantics=semantics, vmem_limit_bytes=VMEM_LIMIT)


C_GQ = 0
C_GK = C_GQ + GLA_QK
C_GV = C_GK + GLA_QK
C_GR = C_GV + GLA_V
C_AL = C_GR + GLA_V
C_DQ = C_AL + GAL_PAD
C_DK = C_DQ + DIFF_W
C_DV = C_DK + DIFF_W
C_END = C_DV + DIFF_W


def _inproj_kernel(x_ref, pos_ref, g_ref, w_ref, wa2_ref, ba_ref, inv_ref, selc_ref, seln_ref,
                   selp_ref, one_ref,
                   gq_ref, gk_ref, gv_ref, gr_ref, la_ref, dq_ref, dk_ref, dv_ref):
    nb = _rms(x_ref[...], g_ref[...]).astype(BF16)

    def proj(c0, c1):
        return jnp.dot(nb, w_ref[:, c0:c1], preferred_element_type=F32)

    gq_ref[...] = proj(C_GQ, C_GK) * (GLA_DK ** -0.5)
    gk_ref[...] = proj(C_GK, C_GV)
    gv_ref[...] = proj(C_GV, C_GR).astype(BF16)
    gr_ref[...] = proj(C_GR, C_AL).astype(BF16)

    gal = proj(C_AL, C_DQ).astype(BF16)
    a_logit = jnp.dot(gal, wa2_ref[...], preferred_element_type=F32) + ba_ref[...]
    la_ref[...] = (jnp.minimum(a_logit, 0.0) - jnp.log1p(jnp.exp(-jnp.abs(a_logit)))) / GLA_TAU

    ang = inv_ref[...] * pos_ref[0].astype(F32)
    cs = jnp.concatenate([jnp.cos(ang), jnp.sin(ang)], axis=0)

    def spread(sel_ref):
        return lax.dot_general(cs, sel_ref[...], TN_DIMS, precision=lax.Precision.HIGHEST,
                               preferred_element_type=F32)

    cosf = spread(selc_ref) + one_ref[...]
    sneg = spread(seln_ref)
    spos = spread(selp_ref)

    def rope_store(c0, out_ref, scale):
        for j in range(DIFF_W // LANES):
            blk = proj(c0 + j * LANES, c0 + (j + 1) * LANES)
            rot = (blk * cosf + pltpu.roll(blk, LANES - ROPE_HALF, 1) * sneg
                   + pltpu.roll(blk, ROPE_HALF, 1) * spos)
            out_ref[:, j * LANES:(j + 1) * LANES] = (rot * scale).astype(BF16)

    rope_store(C_DQ, dq_ref, DIFF_DH ** -0.5)
    rope_store(C_DK, dk_ref, 1.0)
    dv_ref[...] = proj(C_DV, C_END).astype(BF16)


def _rope_tables():
    lane = np.arange(LANES)
    r = lane % DIFF_DH
    selc = np.zeros((2 * ROPE_HALF, LANES), np.float32)
    seln = np.zeros_like(selc)
    selp = np.zeros_like(selc)
    one = np.zeros((1, LANES), np.float32)
    for l in range(LANES):
        if r[l] < ROPE_DIM:
            selc[r[l] % ROPE_HALF, l] = 1.0
            if r[l] < ROPE_HALF:
                seln[ROPE_HALF + r[l], l] = -1.0
            else:
                selp[ROPE_HALF + r[l] - ROPE_HALF, l] = 1.0
        else:
            one[0, l] = 1.0
    return jnp.asarray(selc), jnp.asarray(seln), jnp.asarray(selp), jnp.asarray(one)


def _inproj(x2, positions, attn_norm, w1, wa2p, b_a, *, tm):
    T, D = x2.shape
    nt = T // tm
    pos3 = positions.reshape(nt, 1, tm)
    inv = (ROPE_THETA ** (-jnp.arange(0, ROPE_DIM, 2, dtype=F32) / ROPE_DIM)).reshape(ROPE_HALF, 1)
    selc, seln, selp, one = _rope_tables()
    row = lambda i: (i, 0)
    fixed = lambda i: (0, 0)
    out_shapes = (
        jax.ShapeDtypeStruct((T, GLA_QK), F32), jax.ShapeDtypeStruct((T, GLA_QK), F32),
        jax.ShapeDtypeStruct((T, GLA_V), BF16), jax.ShapeDtypeStruct((T, GLA_V), BF16),
        jax.ShapeDtypeStruct((T, GLA_QK), F32),
        jax.ShapeDtypeStruct((T, DIFF_W), BF16), jax.ShapeDtypeStruct((T, DIFF_W), BF16),
        jax.ShapeDtypeStruct((T, DIFF_W), BF16),
    )
    return pl.pallas_call(
        _inproj_kernel,
        grid=(nt,),
        in_specs=[
            pl.BlockSpec((tm, D), row),
            pl.BlockSpec((1, 1, tm), lambda i: (i, 0, 0)),
            pl.BlockSpec((1, D), fixed),
            pl.BlockSpec((D, C_END), fixed),
            pl.BlockSpec((GAL_PAD, GLA_QK), fixed),
            pl.BlockSpec((1, GLA_QK), fixed),
            pl.BlockSpec((ROPE_HALF, 1), fixed),
            pl.BlockSpec((2 * ROPE_HALF, LANES), fixed),
            pl.BlockSpec((2 * ROPE_HALF, LANES), fixed),
            pl.BlockSpec((2 * ROPE_HALF, LANES), fixed),
            pl.BlockSpec((1, LANES), fixed),
        ],
        out_specs=[pl.BlockSpec((tm, s.shape[1]), row) for s in out_shapes],
        out_shape=out_shapes,
        compiler_params=_cparams(("parallel",)),
        name="inproj",
    )(x2, pos3, attn_norm.reshape(1, D), w1, wa2p, b_a.reshape(1, GLA_QK), inv, selc, seln, selp, one)


def _gla_kernel(gq_ref, gk_ref, gv_ref, gr_ref, la_ref, gn_ref, oa_ref, st_ref, *, n_chunks):
    @pl.when(pl.program_id(1) == 0)
    def _():
        st_ref[...] = jnp.zeros_like(st_ref)

    C = GLA_CHUNK
    tril = lax.broadcasted_iota(jnp.int32, (C, C), 0) >= lax.broadcasted_iota(jnp.int32, (C, C), 1)
    trilf = tril.astype(F32)
    head_of_lane = lax.broadcasted_iota(jnp.int32, (1, GLA_QK), 1) // GLA_DK
    hmask = [head_of_lane == h for h in range(GLA_HEADS)]
    gn = gn_ref[...]

    for c in range(n_chunks):
        sl = pl.ds(c * C, C)
        q = gq_ref[0, sl, :]
        k = gk_ref[0, sl, :]
        la = la_ref[0, sl, :]
        v = gv_ref[0, sl, :]
        b = jnp.dot(trilf, la, precision=lax.Precision.HIGHEST, preferred_element_type=F32)
        b_last = b[C - 1:C, :]
        q_dec = (q * jnp.exp(b)).astype(BF16)
        k_inv = (k * jnp.exp(-b)).astype(BF16)
        k_end = (k * jnp.exp(b_last - b)).astype(BF16)
        decay = jnp.exp(b_last)

        q_heads = jnp.concatenate([jnp.where(hmask[h], q_dec, jnp.zeros_like(q_dec))
                                   for h in range(GLA_HEADS)], axis=0)
        att = lax.dot_general(q_heads, k_inv, NT_DIMS, preferred_element_type=F32)
        st = st_ref[...]
        inter = lax.dot_general(q_heads, st.astype(BF16), NT_DIMS,
                                preferred_element_type=F32)
        u = jnp.zeros_like(st)
        for h in range(GLA_HEADS):
            rows = slice(h * C, (h + 1) * C)
            cols = slice(h * GLA_DV, (h + 1) * GLA_DV)
            a_h = jnp.where(tril, att[rows], 0.0).astype(BF16)
            v_h = v[:, cols]
            o_h = jnp.dot(a_h, v_h, preferred_element_type=F32) + inter[rows]
            y = _rms(o_h, gn)
            r = gr_ref[0, sl, cols].astype(F32)
            oa_ref[0, sl, cols] = (y * (r * jax.nn.sigmoid(r))).astype(BF16)
            u_h = lax.dot_general(v_h, k_end, TN_DIMS, preferred_element_type=F32)
            u = u + jnp.where(hmask[h], u_h, 0.0)
        st_ref[...] = st * decay + u


def _gla(gq, gk, gv, gr, la, gla_norm, *, B, S, ts):
    n_chunks = ts // GLA_CHUNK
    blk = lambda w: pl.BlockSpec((1, ts, w), lambda b, j: (b, j, 0))
    r3 = lambda a: a.reshape(B, S, a.shape[-1])
    return pl.pallas_call(
        functools.partial(_gla_kernel, n_chunks=n_chunks),
        grid=(B, S // ts),
        in_specs=[blk(GLA_QK), blk(GLA_QK), blk(GLA_V), blk(GLA_V), blk(GLA_QK),
                  pl.BlockSpec((1, GLA_DV), lambda b, j: (0, 0))],
        out_specs=blk(GLA_V),
        out_shape=jax.ShapeDtypeStruct((B, S, GLA_V), BF16),
        scratch_shapes=[pltpu.VMEM((GLA_DV, GLA_QK), F32)],
        compiler_params=_cparams(("parallel", "arbitrary")),
        name="gla",
    )(r3(gq), r3(gk), r3(gv), r3(gr), r3(la), gla_norm.reshape(1, GLA_DV))


def _diffattn_kernel(dq_ref, dk_ref, dv_ref, lq1_ref, lk1_ref, lq2_ref, lk2_ref, sub_ref,
                     ob_ref, m_ref, l_ref, acc_ref, *, tq, tk, lambda_init):
    S = dq_ref.shape[1]
    nq = S // tq
    R = 2 * tq
    lam = (jnp.exp(jnp.sum(lq1_ref[...] * lk1_ref[...], axis=-1, keepdims=True))
           - jnp.exp(jnp.sum(lq2_ref[...] * lk2_ref[...], axis=-1, keepdims=True))
           + lambda_init)
    first_comp = lax.broadcasted_iota(jnp.int32, (1, DIFF_DV), 1) < DIFF_DH
    sub = sub_ref[...]

    for h in range(DIFF_HEADS):
        cols = slice(h * DIFF_DV, (h + 1) * DIFF_DV)
        for qi in range(nq):
            qb = dq_ref[0, qi * tq:(qi + 1) * tq, cols]
            zero = jnp.zeros_like(qb)
            qs = jnp.concatenate([jnp.where(first_comp, qb, zero),
                                  jnp.where(first_comp, zero, qb)], axis=0)
            m_ref[...] = jnp.full(m_ref.shape, NEG_BIG, F32)
            l_ref[...] = jnp.zeros(l_ref.shape, F32)
            acc_ref[...] = jnp.zeros(acc_ref.shape, F32)

            def step(kstart, masked, qs=qs, cols=cols, qi=qi):
                kb = dk_ref[0, pl.ds(kstart, tk), cols]
                vb = dv_ref[0, pl.ds(kstart, tk), cols]
                s = lax.dot_general(qs, kb, NT_DIMS, preferred_element_type=F32)
                if masked:
                    qpos = qi * tq + lax.broadcasted_iota(jnp.int32, (R, tk), 0) % tq
                    kpos = kstart + lax.broadcasted_iota(jnp.int32, (R, tk), 1)
                    s = jnp.where(qpos >= kpos, s, NEG_BIG)
                m_prev = m_ref[...]
                m_new = jnp.maximum(m_prev, jnp.max(s, axis=-1, keepdims=True))
                alpha = jnp.exp(m_prev - m_new)
                p = jnp.exp(s - m_new)
                l_ref[...] = alpha * l_ref[...] + jnp.sum(p, axis=-1, keepdims=True)
                acc_ref[...] = alpha * acc_ref[...] + jnp.dot(p.astype(BF16), vb,
                                                              preferred_element_type=F32)
                m_ref[...] = m_new

            n_full = (qi * tq) // tk

            def body(i, carry, step=step):
                step(pl.multiple_of(i * tk, tk), False)
                return carry

            if n_full > 0:
                lax.fori_loop(0, n_full, body, 0)
            for d in range(tq // tk):
                step(qi * tq + d * tk, True)

            o_all = acc_ref[...] / l_ref[...]
            o = o_all[:tq] - lam * o_all[tq:]
            y = _rms(o, sub) * (1.0 - lambda_init)
            ob_ref[0, qi * tq:(qi + 1) * tq, cols] = y.astype(BF16)


def _diffattn(dq, dk, dv, lq1, lk1, lq2, lk2, diff_subln, *, B, S, tq, tk, lambda_init):
    seq = pl.BlockSpec((1, S, DIFF_W), lambda b: (b, 0, 0))
    vec = lambda w: pl.BlockSpec((1, w), lambda b: (0, 0))
    r3 = lambda a: a.reshape(B, S, DIFF_W)
    return pl.pallas_call(
        functools.partial(_diffattn_kernel, tq=tq, tk=tk, lambda_init=lambda_init),
        grid=(B,),
        in_specs=[seq, seq, seq, vec(DIFF_DH), vec(DIFF_DH), vec(DIFF_DH), vec(DIFF_DH), vec(DIFF_DV)],
        out_specs=seq,
        out_shape=jax.ShapeDtypeStruct((B, S, DIFF_W), BF16),
        scratch_shapes=[pltpu.VMEM((2 * tq, 1), F32), pltpu.VMEM((2 * tq, 1), F32),
                        pltpu.VMEM((2 * tq, DIFF_DV), F32)],
        compiler_params=_cparams(("parallel",)),
        name="diffattn",
    )(r3(dq), r3(dk), r3(dv), lq1.reshape(1, -1), lk1.reshape(1, -1), lq2.reshape(1, -1),
      lk2.reshape(1, -1), diff_subln.reshape(1, -1))


ROUTE_W = 8


def _first_index_of(mask, lane):
    return jnp.min(jnp.where(mask, lane, LANES), axis=-1, keepdims=True)


def _postmix_kernel(x_ref, oa_ref, ob_ref, g_ref, wg_ref, wba_ref, wbb_ref, wo_ref, fn_ref,
                    wr_ref, br_ref, h_ref, n2_ref, route_ref):
    D = x_ref.shape[1]
    x = x_ref[...]
    nb = _rms(x, g_ref[...]).astype(BF16)
    y_a = jnp.dot(oa_ref[...], wba_ref[...], preferred_element_type=F32)
    g_a = jnp.dot(nb, wg_ref[:, :D], preferred_element_type=F32)
    merged = jax.nn.sigmoid(g_a) * y_a
    y_b = jnp.dot(ob_ref[...], wbb_ref[...], preferred_element_type=F32)
    g_b = jnp.dot(nb, wg_ref[:, D:], preferred_element_type=F32)
    merged = merged + jax.nn.sigmoid(g_b) * y_b
    h = x + jnp.dot(merged.astype(BF16), wo_ref[...], preferred_element_type=F32)
    h_ref[...] = h
    n2 = _rms(h, fn_ref[...])
    n2_ref[...] = n2.astype(BF16)

    lg = jnp.dot(n2, wr_ref[...], precision=lax.Precision.HIGHEST,
                 preferred_element_type=F32) + br_ref[...]
    lane = lax.broadcasted_iota(jnp.int32, lg.shape, 1)
    is_g = lane < N_GROUPS
    g_max = jnp.max(jnp.where(is_g, lg, -jnp.inf), axis=-1, keepdims=True)
    g_exp = jnp.where(is_g, jnp.exp(lg - g_max), 0.0)
    g_prob = g_exp / jnp.sum(g_exp, axis=-1, keepdims=True)
    g_p = jnp.max(g_prob, axis=-1, keepdims=True)
    g_idx = _first_index_of(is_g & (g_prob == g_p), lane)

    e_lo = N_GROUPS + EXPERTS_PER_GROUP * g_idx
    is_e = (lane >= e_lo) & (lane < e_lo + EXPERTS_PER_GROUP)
    e_max = jnp.max(jnp.where(is_e, lg, -jnp.inf), axis=-1, keepdims=True)
    e_exp = jnp.where(is_e, jnp.exp(lg - e_max), 0.0)
    e_prob = e_exp / jnp.sum(e_exp, axis=-1, keepdims=True)
    p1 = jnp.max(jnp.where(is_e, e_prob, -1.0), axis=-1, keepdims=True)
    i1 = _first_index_of(is_e & (e_prob == p1), lane)
    rest = is_e & (lane != i1)
    p2 = jnp.max(jnp.where(rest, e_prob, -1.0), axis=-1, keepdims=True)
    i2 = _first_index_of(rest & (e_prob == p2), lane)
    den = p1 + p2
    w1 = g_p * (p1 / den)
    w2 = g_p * (p2 / den)
    rec = jnp.where(lane == 0, w1, 0.0)
    rec = jnp.where(lane == 1, w2, rec)
    rec = jnp.where(lane == 2, (i1 - N_GROUPS).astype(F32), rec)
    rec = jnp.where(lane == 3, (i2 - N_GROUPS).astype(F32), rec)
    route_ref[...] = rec[:, :ROUTE_W]


def _postmix(x2, oa, ob, attn_norm, wg, wba, wbb, wo, ffn_norm, wr, br, *, tm):
    T, D = x2.shape
    row = lambda i: (i, 0)
    fixed = lambda i: (0, 0)
    full = lambda a: pl.BlockSpec(a.shape, fixed)
    g = attn_norm.reshape(1, D)
    fn = ffn_norm.reshape(1, D)
    return pl.pallas_call(
        _postmix_kernel,
        grid=(T // tm,),
        in_specs=[pl.BlockSpec((tm, D), row), pl.BlockSpec((tm, GLA_V), row),
                  pl.BlockSpec((tm, DIFF_W), row), full(g), full(wg), full(wba), full(wbb),
                  full(wo), full(fn), full(wr), full(br)],
        out_specs=[pl.BlockSpec((tm, D), row), pl.BlockSpec((tm, D), row),
                   pl.BlockSpec((tm, ROUTE_W), row)],
        out_shape=(jax.ShapeDtypeStruct((T, D), F32), jax.ShapeDtypeStruct((T, D), BF16),
                   jax.ShapeDtypeStruct((T, ROUTE_W), F32)),
        compiler_params=_cparams(("parallel",)),
        name="postmix",
    )(x2, oa, ob, g, wg, wba, wbb, wo, fn, wr, br)


def _experts_kernel(tile_e_ref, n_tiles_ref, xg_ref, wgu_ref, wd_ref, y_ref):
    i = pl.program_id(0)

    @pl.when(i < n_tiles_ref[0])
    def _():
        gu = jnp.dot(xg_ref[...], wgu_ref[0], preferred_element_type=F32)
        g = gu[:, :D_EXPERT]
        he = (g * jax.nn.sigmoid(g)) * gu[:, D_EXPERT:]
        y_ref[...] = jnp.dot(he.astype(BF16), wd_ref[0], preferred_element_type=F32).astype(BF16)

    @pl.when(i >= n_tiles_ref[0])
    def _():
        y_ref[...] = jnp.zeros_like(y_ref)


def _experts(tile_e, n_tiles, xg, wgu, wd, *, tm):
    P, D = xg.shape
    grid_spec = pltpu.PrefetchScalarGridSpec(
        num_scalar_prefetch=2,
        grid=(P // tm,),
        in_specs=[pl.BlockSpec((tm, D), lambda i, te, nt: (i, 0)),
                  pl.BlockSpec((1, D, 2 * D_EXPERT), lambda i, te, nt: (te[i], 0, 0)),
                  pl.BlockSpec((1, D_EXPERT, D), lambda i, te, nt: (te[i], 0, 0))],
        out_specs=pl.BlockSpec((tm, D), lambda i, te, nt: (i, 0)),
    )
    return pl.pallas_call(
        _experts_kernel,
        grid_spec=grid_spec,
        out_shape=jax.ShapeDtypeStruct((P, D), BF16),
        compiler_params=_cparams(("arbitrary",)),
        name="experts",
    )(tile_e, n_tiles, xg, wgu, wd)


def _final_kernel(h_ref, y1_ref, y2_ref, route_ref, p_ref, wple_ref, pn_ref, wpg_ref, fn_ref,
                  o_ref, *, apply_final_norm):
    route = route_ref[...]
    y = route[:, 0:1] * y1_ref[...].astype(F32)
    y = y + route[:, 1:2] * y2_ref[...].astype(F32)
    h = h_ref[...] + y
    e = _rms(jnp.dot(p_ref[...].astype(BF16), wple_ref[...], preferred_element_type=F32), pn_ref[...])
    gate = jax.nn.sigmoid(jnp.dot(h.astype(BF16), wpg_ref[...], preferred_element_type=F32))
    h = h + gate * e
    if apply_final_norm:
        h = _rms(h, fn_ref[...])
    o_ref[...] = h


def _final(h1, y1, y2, route, p2, wple, ple_norm, wpg, final_norm, *, tm, apply_final_norm):
    T, D = h1.shape
    row = lambda i: (i, 0)
    fixed = lambda i: (0, 0)
    full = lambda a: pl.BlockSpec(a.shape, fixed)
    pn = ple_norm.reshape(1, D)
    fn = final_norm.reshape(1, D)
    return pl.pallas_call(
        functools.partial(_final_kernel, apply_final_norm=apply_final_norm),
        grid=(T // tm,),
        in_specs=[pl.BlockSpec((tm, D), row), pl.BlockSpec((tm, D), row), pl.BlockSpec((tm, D), row),
                  pl.BlockSpec((tm, ROUTE_W), row), pl.BlockSpec((tm, p2.shape[1]), row),
                  full(wple), full(pn), full(wpg), full(fn)],
        out_specs=pl.BlockSpec((tm, D), row),
        out_shape=jax.ShapeDtypeStruct((T, D), F32),
        compiler_params=_cparams(("parallel",)),
        name="final",
    )(h1, y1, y2, route, p2, wple, pn, wpg, fn)


def _routing_tables(eid, *, tm):
    T = eid.shape[0]
    A = 2 * T
    n_rows = A + N_EXPERTS * tm
    flat_e = eid.reshape(A)
    order = jnp.argsort(flat_e, stable=True).astype(jnp.int32)
    counts = jnp.sum(flat_e[:, None] == jnp.arange(N_EXPERTS, dtype=jnp.int32)[None, :], axis=0,
                     dtype=jnp.int32)
    padded = (counts + tm - 1) // tm * tm
    p_end = jnp.cumsum(padded)
    p_start = p_end - padded
    c_start = jnp.cumsum(counts) - counts
    n_tiles = (p_end[-1] // tm).astype(jnp.int32).reshape(1)
    tile_start = jnp.arange(n_rows // tm, dtype=jnp.int32) * tm
    tile_e = jnp.minimum(jnp.searchsorted(p_end, tile_start, side="right"), N_EXPERTS - 1).astype(jnp.int32)
    last_e = tile_e[jnp.maximum(n_tiles[0] - 1, 0)]
    tile_e = jnp.where(jnp.arange(n_rows // tm) < n_tiles[0], tile_e, last_e)
    row = jnp.arange(n_rows, dtype=jnp.int32)
    row_e = jnp.repeat(tile_e, tm)
    j = row - p_start[row_e]
    src = c_start[row_e] + jnp.clip(j, 0, jnp.maximum(counts[row_e] - 1, 0))
    row_token = order[jnp.clip(src, 0, A - 1)] // 2
    sorted_e = flat_e[order]
    sorted_pos = p_start[sorted_e] + (jnp.arange(A, dtype=jnp.int32) - c_start[sorted_e])
    pos = jnp.zeros((A,), jnp.int32).at[order].set(sorted_pos).reshape(T, 2)
    return row_token, pos, tile_e, n_tiles


SPLIT_SIZES = (GLA_QK, GLA_QK, GLA_V, GLA_V, GLA_LOWRANK, DIFF_W, DIFF_W, DIFF_W)


def _layer(h2, p2, positions, B, S, lambda_init, apply_final_norm, attn_norm, w_in, w_a2, b_a,
           gla_norm, lq1, lk1, lq2, lk2, diff_subln, w_branch_a, w_branch_b, w_out, ffn_norm,
           w_rg, b_rg, w_re, b_re, w_gate, w_up, w_down, w_ple, ple_norm, w_ple_gate, final_norm,
           *, tm, ts, tq, tk, te):
    T, D = h2.shape
    n_mix = sum(SPLIT_SIZES)
    c_al = GLA_QK * 2 + GLA_V * 2
    w1 = jnp.concatenate([w_in[:, :c_al + GLA_LOWRANK],
                          jnp.zeros((D, GAL_PAD - GLA_LOWRANK), w_in.dtype),
                          w_in[:, c_al + GLA_LOWRANK:n_mix]], axis=1).astype(BF16)
    wg = w_in[:, n_mix:].astype(BF16)
    wa2p = jnp.concatenate([w_a2, jnp.zeros((GAL_PAD - GLA_LOWRANK, GLA_QK), w_a2.dtype)],
                           axis=0).astype(BF16)

    gq, gk, gv, gr, la, dq, dk, dv = _inproj(h2, positions, attn_norm, w1, wa2p, b_a, tm=tm)
    oa = _gla(gq, gk, gv, gr, la, gla_norm, B=B, S=S, ts=ts).reshape(T, GLA_V)
    ob = _diffattn(dq, dk, dv, lq1, lk1, lq2, lk2, diff_subln, B=B, S=S, tq=tq, tk=tk,
                   lambda_init=lambda_init).reshape(T, DIFF_W)

    wr = jnp.concatenate([w_rg, w_re, jnp.zeros((D, LANES - N_GROUPS - N_EXPERTS), F32)], axis=1)
    br = jnp.concatenate([b_rg, b_re, jnp.zeros((LANES - N_GROUPS - N_EXPERTS,), F32)]).reshape(1, LANES)
    h1, n2, route = _postmix(h2, oa, ob, attn_norm, wg, w_branch_a.astype(BF16),
                             w_branch_b.astype(BF16), w_out.astype(BF16), ffn_norm, wr, br, tm=tm)

    eid = route[:, 2:4].astype(jnp.int32)
    row_token, pos, tile_e, n_tiles = _routing_tables(eid, tm=te)
    xg = jnp.take(n2, row_token, axis=0)
    wgu = jnp.concatenate([w_gate, w_up], axis=2).astype(BF16)
    ys = _experts(tile_e, n_tiles, xg, wgu, w_down.astype(BF16), tm=te)
    y1 = jnp.take(ys, pos[:, 0], axis=0)
    y2 = jnp.take(ys, pos[:, 1], axis=0)
    return _final(h1, y1, y2, route, p2, w_ple.astype(BF16), ple_norm, w_ple_gate.astype(BF16),
                  final_norm, tm=tm, apply_final_norm=apply_final_norm)


def _block(x, p, positions, attn_norm, w_in, w_a2, b_a, gla_norm, lambda_q1, lambda_k1, lambda_q2,
           lambda_k2, diff_subln, w_branch_a, w_branch_b, w_out, ffn_norm, w_router_group,
           b_router_group, w_router_expert, b_router_expert, w_gate, w_up, w_down, w_ple, ple_norm,
           w_ple_gate, final_norm, *, tm, ts, tq, tk, te):
    B, S, D = x.shape
    depth = w_in.shape[0]
    h = x.reshape(B * S, D)
    for i in range(depth):
        lambda_init = 0.8 - 0.6 * math.exp(-0.3 * i)
        h = _layer(h, p[i].reshape(B * S, -1), positions, B, S, lambda_init, i == depth - 1,
                   attn_norm[i], w_in[i], w_a2[i], b_a[i], gla_norm[i], lambda_q1[i], lambda_k1[i],
                   lambda_q2[i], lambda_k2[i], diff_subln[i], w_branch_a[i], w_branch_b[i], w_out[i],
                   ffn_norm[i], w_router_group[i], b_router_group[i], w_router_expert[i],
                   b_router_expert[i], w_gate[i], w_up[i], w_down[i], w_ple[i], ple_norm[i],
                   w_ple_gate[i], final_norm, tm=tm, ts=ts, tq=tq, tk=tk, te=te)
    return h.reshape(B, S, D)


def kernel(x, p, positions, attn_norm, w_in, w_a2, b_a, gla_norm, lambda_q1, lambda_k1, lambda_q2, lambda_k2, diff_subln, w_branch_a, w_branch_b, w_out, ffn_norm, w_router_group, b_router_group, w_router_expert, b_router_expert, w_gate, w_up, w_down, w_ple, ple_norm, w_ple_gate, final_norm):
    S = x.shape[1]
    return _block(x, p, positions, attn_norm, w_in, w_a2, b_a, gla_norm, lambda_q1, lambda_k1,
                  lambda_q2, lambda_k2, diff_subln, w_branch_a, w_branch_b, w_out, ffn_norm,
                  w_router_group, b_router_group, w_router_expert, b_router_expert, w_gate, w_up,
                  w_down, w_ple, ple_norm, w_ple_gate, final_norm,
                  tm=512, ts=min(512, S), tq=min(512, S), tk=min(512, S), te=512)
```

```python
import functools
import math

import jax
import jax.numpy as jnp
import numpy as np
from jax import lax
from jax.experimental import pallas as pl
from jax.experimental.pallas import tpu as pltpu

EPS = 1e-6

GLA_HEADS = 4
GLA_DK = 64
GLA_DV = 128
GLA_LOWRANK = 16
GLA_TAU = 16.0
GLA_CHUNK = 64
GLA_QK = GLA_HEADS * GLA_DK
GLA_V = GLA_HEADS * GLA_DV

DIFF_HEADS = 4
DIFF_DH = 64
DIFF_DV = 2 * DIFF_DH
DIFF_W = DIFF_HEADS * DIFF_DV
ROPE_THETA = 500000.0
ROPE_DIM = DIFF_DH // 4
ROPE_HALF = ROPE_DIM // 2

N_GROUPS = 4
EXPERTS_PER_GROUP = 8
N_EXPERTS = N_GROUPS * EXPERTS_PER_GROUP
D_EXPERT = 256

LANES = 128
GAL_PAD = LANES
VMEM_LIMIT = 56 * 1024 * 1024

BF16 = jnp.bfloat16
F32 = jnp.float32
NT_DIMS = (((1,), (1,)), ((), ()))
TN_DIMS = (((0,), (0,)), ((), ()))
NEG_BIG = -1e30


def _rms(x, g):
    return x * lax.rsqrt(jnp.mean(x * x, axis=-1, keepdims=True) + EPS) * g


def _cparams(semantics):
    return pltpu.CompilerParams(dimension_semantics=semantics, vmem_limit_bytes=VMEM_LIMIT)


C_GQ = 0
C_GK = C_GQ + GLA_QK
C_GV = C_GK + GLA_QK
C_GR = C_GV + GLA_V
C_AL = C_GR + GLA_V
C_DQ = C_AL + GAL_PAD
C_DK = C_DQ + DIFF_W
C_DV = C_DK + DIFF_W
C_END = C_DV + DIFF_W


def _inproj_kernel(x_ref, pos_ref, g_ref, w_ref, wa2_ref, ba_ref, inv_ref, selc_ref, seln_ref,
                   selp_ref, one_ref,
                   gq_ref, gk_ref, gv_ref, gr_ref, la_ref, dq_ref, dk_ref, dv_ref):
    nb = _rms(x_ref[...], g_ref[...]).astype(BF16)

    def proj(c0, c1):
        return jnp.dot(nb, w_ref[:, c0:c1], preferred_element_type=F32)

    gq_ref[...] = proj(C_GQ, C_GK) * (GLA_DK ** -0.5)
    gk_ref[...] = proj(C_GK, C_GV)
    gv_ref[...] = proj(C_GV, C_GR).astype(BF16)
    gr_ref[...] = proj(C_GR, C_AL).astype(BF16)

    gal = proj(C_AL, C_DQ).astype(BF16)
    a_logit = jnp.dot(gal, wa2_ref[...], preferred_element_type=F32) + ba_ref[...]
    la_ref[...] = (jnp.minimum(a_logit, 0.0) - jnp.log1p(jnp.exp(-jnp.abs(a_logit)))) / GLA_TAU

    ang = inv_ref[...] * pos_ref[0].astype(F32)
    cs = jnp.concatenate([jnp.cos(ang), jnp.sin(ang)], axis=0)

    def spread(sel_ref):
        return lax.dot_general(cs, sel_ref[...], TN_DIMS, precision=lax.Precision.HIGHEST,
                               preferred_element_type=F32)

    cosf = spread(selc_ref) + one_ref[...]
    sneg = spread(seln_ref)
    spos = spread(selp_ref)

    def rope_store(c0, out_ref, scale):
        for j in range(DIFF_W // LANES):
            blk = proj(c0 + j * LANES, c0 + (j + 1) * LANES)
            rot = (blk * cosf + pltpu.roll(blk, LANES - ROPE_HALF, 1) * sneg
                   + pltpu.roll(blk, ROPE_HALF, 1) * spos)
            out_ref[:, j * LANES:(j + 1) * LANES] = (rot * scale).astype(BF16)

    rope_store(C_DQ, dq_ref, DIFF_DH ** -0.5)
    rope_store(C_DK, dk_ref, 1.0)
    dv_ref[...] = proj(C_DV, C_END).astype(BF16)


def _rope_tables():
    lane = np.arange(LANES)
    r = lane % DIFF_DH
    selc = np.zeros((2 * ROPE_HALF, LANES), np.float32)
    seln = np.zeros_like(selc)
    selp = np.zeros_like(selc)
    one = np.zeros((1, LANES), np.float32)
    for l in range(LANES):
        if r[l] < ROPE_DIM:
            selc[r[l] % ROPE_HALF, l] = 1.0
            if r[l] < ROPE_HALF:
                seln[ROPE_HALF + r[l], l] = -1.0
            else:
                selp[ROPE_HALF + r[l] - ROPE_HALF, l] = 1.0
        else:
            one[0, l] = 1.0
    return jnp.asarray(selc), jnp.asarray(seln), jnp.asarray(selp), jnp.asarray(one)


def _inproj(x2, positions, attn_norm, w1, wa2p, b_a, *, tm):
    T, D = x2.shape
    nt = T // tm
    pos3 = positions.reshape(nt, 1, tm)
    inv = (ROPE_THETA ** (-jnp.arange(0, ROPE_DIM, 2, dtype=F32) / ROPE_DIM)).reshape(ROPE_HALF, 1)
    selc, seln, selp, one = _rope_tables()
    row = lambda i: (i, 0)
    fixed = lambda i: (0, 0)
    out_shapes = (
        jax.ShapeDtypeStruct((T, GLA_QK), F32), jax.ShapeDtypeStruct((T, GLA_QK), F32),
        jax.ShapeDtypeStruct((T, GLA_V), BF16), jax.ShapeDtypeStruct((T, GLA_V), BF16),
        jax.ShapeDtypeStruct((T, GLA_QK), F32),
        jax.ShapeDtypeStruct((T, DIFF_W), BF16), jax.ShapeDtypeStruct((T, DIFF_W), BF16),
        jax.ShapeDtypeStruct((T, DIFF_W), BF16),
    )
    return pl.pallas_call(
        _inproj_kernel,
        grid=(nt,),
        in_specs=[
            pl.BlockSpec((tm, D), row),
            pl.BlockSpec((1, 1, tm), lambda i: (i, 0, 0)),
            pl.BlockSpec((1, D), fixed),
            pl.BlockSpec((D, C_END), fixed),
            pl.BlockSpec((GAL_PAD, GLA_QK), fixed),
            pl.BlockSpec((1, GLA_QK), fixed),
            pl.BlockSpec((ROPE_HALF, 1), fixed),
            pl.BlockSpec((2 * ROPE_HALF, LANES), fixed),
            pl.BlockSpec((2 * ROPE_HALF, LANES), fixed),
            pl.BlockSpec((2 * ROPE_HALF, LANES), fixed),
            pl.BlockSpec((1, LANES), fixed),
        ],
        out_specs=[pl.BlockSpec((tm, s.shape[1]), row) for s in out_shapes],
        out_shape=out_shapes,
        compiler_params=_cparams(("parallel",)),
        name="inproj",
    )(x2, pos3, attn_norm.reshape(1, D), w1, wa2p, b_a.reshape(1, GLA_QK), inv, selc, seln, selp, one)


def _gla_kernel(gq_ref, gk_ref, gv_ref, gr_ref, la_ref, gn_ref, oa_ref, st_ref, *, n_chunks):
    @pl.when(pl.program_id(1) == 0)
    def _():
        st_ref[...] = jnp.zeros_like(st_ref)

    C = GLA_CHUNK
    tril = lax.broadcasted_iota(jnp.int32, (C, C), 0) >= lax.broadcasted_iota(jnp.int32, (C, C), 1)
    trilf = tril.astype(F32)
    head_of_lane = lax.broadcasted_iota(jnp.int32, (1, GLA_QK), 1) // GLA_DK
    hmask = [head_of_lane == h for h in range(GLA_HEADS)]
    gn = gn_ref[...]

    for c in range(n_chunks):
        sl = pl.ds(c * C, C)
        q = gq_ref[0, sl, :]
        k = gk_ref[0, sl, :]
        la = la_ref[0, sl, :]
        v = gv_ref[0, sl, :]
        b = jnp.dot(trilf, la, precision=lax.Precision.HIGHEST, preferred_element_type=F32)
        b_last = b[C - 1:C, :]
        q_dec = (q * jnp.exp(b)).astype(BF16)
        k_inv = (k * jnp.exp(-b)).astype(BF16)
        k_end = (k * jnp.exp(b_last - b)).astype(BF16)
        decay = jnp.exp(b_last)

        q_heads = jnp.concatenate([jnp.where(hmask[h], q_dec, jnp.zeros_like(q_dec))
                                   for h in range(GLA_HEADS)], axis=0)
        att = lax.dot_general(q_heads, k_inv, NT_DIMS, preferred_element_type=F32)
        st = st_ref[...]
        inter = lax.dot_general(q_heads, st.astype(BF16), NT_DIMS,
                                preferred_element_type=F32)
        u = jnp.zeros_like(st)
        for h in range(GLA_HEADS):
            rows = slice(h * C, (h + 1) * C)
            cols = slice(h * GLA_DV, (h + 1) * GLA_DV)
            a_h = jnp.where(tril, att[rows], 0.0).astype(BF16)
            v_h = v[:, cols]
            o_h = jnp.dot(a_h, v_h, preferred_element_type=F32) + inter[rows]
            y = _rms(o_h, gn)
            r = gr_ref[0, sl, cols].astype(F32)
            oa_ref[0, sl, cols] = (y * (r * jax.nn.sigmoid(r))).astype(BF16)
            u_h = lax.dot_general(v_h, k_end, TN_DIMS, preferred_element_type=F32)
            u = u + jnp.where(hmask[h], u_h, 0.0)
        st_ref[...] = st * decay + u


def _gla(gq, gk, gv, gr, la, gla_norm, *, B, S, ts):
    n_chunks = ts // GLA_CHUNK
    blk = lambda w: pl.BlockSpec((1, ts, w), lambda b, j: (b, j, 0))
    r3 = lambda a: a.reshape(B, S, a.shape[-1])
    return pl.pallas_call(
        functools.partial(_gla_kernel, n_chunks=n_chunks),
        grid=(B, S // ts),
        in_specs=[blk(GLA_QK), blk(GLA_QK), blk(GLA_V), blk(GLA_V), blk(GLA_QK),
                  pl.BlockSpec((1, GLA_DV), lambda b, j: (0, 0))],
        out_specs=blk(GLA_V),
        out_shape=jax.ShapeDtypeStruct((B, S, GLA_V), BF16),
        scratch_shapes=[pltpu.VMEM((GLA_DV, GLA_QK), F32)],
        compiler_params=_cparams(("parallel", "arbitrary")),
        name="gla",
    )(r3(gq), r3(gk), r3(gv), r3(gr), r3(la), gla_norm.reshape(1, GLA_DV))


def _diffattn_kernel(dq_ref, dk_ref, dv_ref, lq1_ref, lk1_ref, lq2_ref, lk2_ref, sub_ref,
                     ob_ref, m_ref, l_ref, acc_ref, *, tq, tk, lambda_init):
    S = dq_ref.shape[1]
    nq = S // tq
    R = 2 * tq
    lam = (jnp.exp(jnp.sum(lq1_ref[...] * lk1_ref[...], axis=-1, keepdims=True))
           - jnp.exp(jnp.sum(lq2_ref[...] * lk2_ref[...], axis=-1, keepdims=True))
           + lambda_init)
    first_comp = lax.broadcasted_iota(jnp.int32, (1, DIFF_DV), 1) < DIFF_DH
    sub = sub_ref[...]

    for h in range(DIFF_HEADS):
        cols = slice(h * DIFF_DV, (h + 1) * DIFF_DV)
        for qi in range(nq):
            qb = dq_ref[0, qi * tq:(qi + 1) * tq, cols]
            zero = jnp.zeros_like(qb)
            qs = jnp.concatenate([jnp.where(first_comp, qb, zero),
                                  jnp.where(first_comp, zero, qb)], axis=0)
            m_ref[...] = jnp.full(m_ref.shape, NEG_BIG, F32)
            l_ref[...] = jnp.zeros(l_ref.shape, F32)
            acc_ref[...] = jnp.zeros(acc_ref.shape, F32)

            def step(kstart, masked, qs=qs, cols=cols, qi=qi):
                kb = dk_ref[0, pl.ds(kstart, tk), cols]
                vb = dv_ref[0, pl.ds(kstart, tk), cols]
                s = lax.dot_general(kb, qs, NT_DIMS, preferred_element_type=F32)
                if masked:
                    kpos = kstart + lax.broadcasted_iota(jnp.int32, (tk, R), 0)
                    qpos = qi * tq + lax.broadcasted_iota(jnp.int32, (tk, R), 1) % tq
                    s = jnp.where(qpos >= kpos, s, NEG_BIG)
                m_prev = m_ref[...]
                m_new = jnp.maximum(m_prev, jnp.max(s, axis=0, keepdims=True))
                alpha = jnp.exp(m_prev - m_new)
                p = jnp.exp(s - m_new)
                l_ref[...] = alpha * l_ref[...] + jnp.sum(p, axis=0, keepdims=True)
                pv = lax.dot_general(vb, p.astype(BF16), TN_DIMS, preferred_element_type=F32)
                acc_ref[...] = alpha * acc_ref[...] + pv
                m_ref[...] = m_new

            n_full = (qi * tq) // tk

            def body(i, carry, step=step):
                step(pl.multiple_of(i * tk, tk), False)
                return carry

            if n_full > 0:
                lax.fori_loop(0, n_full, body, 0)
            for d in range(tq // tk):
                step(qi * tq + d * tk, True)

            o_all = acc_ref[...] / l_ref[...]
            o = o_all[:, :tq] - lam * o_all[:, tq:]
            y = o * lax.rsqrt(jnp.mean(o * o, axis=0, keepdims=True) + EPS) * sub
            y = y * (1.0 - lambda_init)
            ob_ref[0, qi * tq:(qi + 1) * tq, cols] = y.T.astype(BF16)


def _diffattn(dq, dk, dv, lq1, lk1, lq2, lk2, diff_subln, *, B, S, tq, tk, lambda_init):
    seq = pl.BlockSpec((1, S, DIFF_W), lambda b: (b, 0, 0))
    vec = lambda w: pl.BlockSpec((1, w), lambda b: (0, 0))
    r3 = lambda a: a.reshape(B, S, DIFF_W)
    return pl.pallas_call(
        functools.partial(_diffattn_kernel, tq=tq, tk=tk, lambda_init=lambda_init),
        grid=(B,),
        in_specs=[seq, seq, seq, vec(DIFF_DH), vec(DIFF_DH), vec(DIFF_DH), vec(DIFF_DH),
                  pl.BlockSpec((DIFF_DV, 1), lambda b: (0, 0))],
        out_specs=seq,
        out_shape=jax.ShapeDtypeStruct((B, S, DIFF_W), BF16),
        scratch_shapes=[pltpu.VMEM((1, 2 * tq), F32), pltpu.VMEM((1, 2 * tq), F32),
                        pltpu.VMEM((DIFF_DV, 2 * tq), F32)],
        compiler_params=_cparams(("parallel",)),
        name="diffattn",
    )(r3(dq), r3(dk), r3(dv), lq1.reshape(1, -1), lk1.reshape(1, -1), lq2.reshape(1, -1),
      lk2.reshape(1, -1), diff_subln.reshape(-1, 1))


ROUTE_W = 8


def _first_index_of(mask, lane):
    return jnp.min(jnp.where(mask, lane, LANES), axis=-1, keepdims=True)


def _postmix_kernel(x_ref, oa_ref, ob_ref, g_ref, wg_ref, wba_ref, wbb_ref, wo_ref, fn_ref,
                    wr_ref, br_ref, h_ref, n2_ref, route_ref):
    D = x_ref.shape[1]
    x = x_ref[...]
    nb = _rms(x, g_ref[...]).astype(BF16)
    y_a = jnp.dot(oa_ref[...], wba_ref[...], preferred_element_type=F32)
    g_a = jnp.dot(nb, wg_ref[:, :D], preferred_element_type=F32)
    merged = jax.nn.sigmoid(g_a) * y_a
    y_b = jnp.dot(ob_ref[...], wbb_ref[...], preferred_element_type=F32)
    g_b = jnp.dot(nb, wg_ref[:, D:], preferred_element_type=F32)
    merged = merged + jax.nn.sigmoid(g_b) * y_b
    h = x + jnp.dot(merged.astype(BF16), wo_ref[...], preferred_element_type=F32)
    h_ref[...] = h
    n2 = _rms(h, fn_ref[...])
    n2_ref[...] = n2.astype(BF16)

    lg = jnp.dot(n2, wr_ref[...], precision=lax.Precision.HIGHEST,
                 preferred_element_type=F32) + br_ref[...]
    lane = lax.broadcasted_iota(jnp.int32, lg.shape, 1)
    is_g = lane < N_GROUPS
    g_max = jnp.max(jnp.where(is_g, lg, -jnp.inf), axis=-1, keepdims=True)
    g_exp = jnp.where(is_g, jnp.exp(lg - g_max), 0.0)
    g_prob = g_exp / jnp.sum(g_exp, axis=-1, keepdims=True)
    g_p = jnp.max(g_prob, axis=-1, keepdims=True)
    g_idx = _first_index_of(is_g & (g_prob == g_p), lane)

    e_lo = N_GROUPS + EXPERTS_PER_GROUP * g_idx
    is_e = (lane >= e_lo) & (lane < e_lo + EXPERTS_PER_GROUP)
    e_max = jnp.max(jnp.where(is_e, lg, -jnp.inf), axis=-1, keepdims=True)
    e_exp = jnp.where(is_e, jnp.exp(lg - e_max), 0.0)
    e_prob = e_exp / jnp.sum(e_exp, axis=-1, keepdims=True)
    p1 = jnp.max(jnp.where(is_e, e_prob, -1.0), axis=-1, keepdims=True)
    i1 = _first_index_of(is_e & (e_prob == p1), lane)
    rest = is_e & (lane != i1)
    p2 = jnp.max(jnp.where(rest, e_prob, -1.0), axis=-1, keepdims=True)
    i2 = _first_index_of(rest & (e_prob == p2), lane)
    den = p1 + p2
    w1 = g_p * (p1 / den)
    w2 = g_p * (p2 / den)
    rec = jnp.where(lane == 0, w1, 0.0)
    rec = jnp.where(lane == 1, w2, rec)
    rec = jnp.where(lane == 2, (i1 - N_GROUPS).astype(F32), rec)
    rec = jnp.where(lane == 3, (i2 - N_GROUPS).astype(F32), rec)
    route_ref[...] = rec[:, :ROUTE_W]


def _postmix(x2, oa, ob, attn_norm, wg, wba, wbb, wo, ffn_norm, wr, br, *, tm):
    T, D = x2.shape
    row = lambda i: (i, 0)
    fixed = lambda i: (0, 0)
    full = lambda a: pl.BlockSpec(a.shape, fixed)
    g = attn_norm.reshape(1, D)
    fn = ffn_norm.reshape(1, D)
    return pl.pallas_call(
        _postmix_kernel,
        grid=(T // tm,),
        in_specs=[pl.BlockSpec((tm, D), row), pl.BlockSpec((tm, GLA_V), row),
                  pl.BlockSpec((tm, DIFF_W), row), full(g), full(wg), full(wba), full(wbb),
                  full(wo), full(fn), full(wr), full(br)],
        out_specs=[pl.BlockSpec((tm, D), row), pl.BlockSpec((tm, D), row),
                   pl.BlockSpec((tm, ROUTE_W), row)],
        out_shape=(jax.ShapeDtypeStruct((T, D), F32), jax.ShapeDtypeStruct((T, D), BF16),
                   jax.ShapeDtypeStruct((T, ROUTE_W), F32)),
        compiler_params=_cparams(("parallel",)),
        name="postmix",
    )(x2, oa, ob, g, wg, wba, wbb, wo, fn, wr, br)


def _experts_kernel(tile_e_ref, n_tiles_ref, xg_ref, wgu_ref, wd_ref, y_ref):
    i = pl.program_id(0)

    @pl.when(i < n_tiles_ref[0])
    def _():
        gu = jnp.dot(xg_ref[...], wgu_ref[0], preferred_element_type=F32)
        g = gu[:, :D_EXPERT]
        he = (g * jax.nn.sigmoid(g)) * gu[:, D_EXPERT:]
        y_ref[...] = jnp.dot(he.astype(BF16), wd_ref[0], preferred_element_type=F32).astype(BF16)

    @pl.when(i >= n_tiles_ref[0])
    def _():
        y_ref[...] = jnp.zeros_like(y_ref)


def _experts(tile_e, n_tiles, xg, wgu, wd, *, tm):
    P, D = xg.shape
    grid_spec = pltpu.PrefetchScalarGridSpec(
        num_scalar_prefetch=2,
        grid=(P // tm,),
        in_specs=[pl.BlockSpec((tm, D), lambda i, te, nt: (i, 0)),
                  pl.BlockSpec((1, D, 2 * D_EXPERT), lambda i, te, nt: (te[i], 0, 0)),
                  pl.BlockSpec((1, D_EXPERT, D), lambda i, te, nt: (te[i], 0, 0))],
        out_specs=pl.BlockSpec((tm, D), lambda i, te, nt: (i, 0)),
    )
    return pl.pallas_call(
        _experts_kernel,
        grid_spec=grid_spec,
        out_shape=jax.ShapeDtypeStruct((P, D), BF16),
        compiler_params=_cparams(("arbitrary",)),
        name="experts",
    )(tile_e, n_tiles, xg, wgu, wd)


def _final_kernel(h_ref, y1_ref, y2_ref, route_ref, p_ref, wple_ref, pn_ref, wpg_ref, fn_ref,
                  o_ref, *, apply_final_norm):
    route = route_ref[...]
    y = route[:, 0:1] * y1_ref[...].astype(F32)
    y = y + route[:, 1:2] * y2_ref[...].astype(F32)
    h = h_ref[...] + y
    e = _rms(jnp.dot(p_ref[...].astype(BF16), wple_ref[...], preferred_element_type=F32), pn_ref[...])
    gate = jax.nn.sigmoid(jnp.dot(h.astype(BF16), wpg_ref[...], preferred_element_type=F32))
    h = h + gate * e
    if apply_final_norm:
        h = _rms(h, fn_ref[...])
    o_ref[...] = h


def _final(h1, y1, y2, route, p2, wple, ple_norm, wpg, final_norm, *, tm, apply_final_norm):
    T, D = h1.shape
    row = lambda i: (i, 0)
    fixed = lambda i: (0, 0)
    full = lambda a: pl.BlockSpec(a.shape, fixed)
    pn = ple_norm.reshape(1, D)
    fn = final_norm.reshape(1, D)
    return pl.pallas_call(
        functools.partial(_final_kernel, apply_final_norm=apply_final_norm),
        grid=(T // tm,),
        in_specs=[pl.BlockSpec((tm, D), row), pl.BlockSpec((tm, D), row), pl.BlockSpec((tm, D), row),
                  pl.BlockSpec((tm, ROUTE_W), row), pl.BlockSpec((tm, p2.shape[1]), row),
                  full(wple), full(pn), full(wpg), full(fn)],
        out_specs=pl.BlockSpec((tm, D), row),
        out_shape=jax.ShapeDtypeStruct((T, D), F32),
        compiler_params=_cparams(("parallel",)),
        name="final",
    )(h1, y1, y2, route, p2, wple, pn, wpg, fn)


def _routing_tables(eid, *, tm):
    T = eid.shape[0]
    A = 2 * T
    n_tiles_max = A // tm + N_EXPERTS
    flat_e = eid.reshape(A)
    experts = jnp.arange(N_EXPERTS, dtype=jnp.int32)
    onehot = flat_e[:, None] == experts[None, :]
    order = jnp.argsort(flat_e, stable=True).astype(jnp.int32)
    rank_sorted = jnp.argsort(order).astype(jnp.int32)
    counts = jnp.sum(onehot, axis=0, dtype=jnp.int32)
    padded = (counts + tm - 1) // tm * tm
    p_end = jnp.cumsum(padded)
    p_start = p_end - padded
    c_start = jnp.cumsum(counts) - counts
    n_tiles = (p_end[-1] // tm).astype(jnp.int32).reshape(1)
    tile_start = jnp.arange(n_tiles_max, dtype=jnp.int32) * tm
    tile_e = jnp.sum(tile_start[:, None] >= p_end[None, :], axis=1, dtype=jnp.int32)
    tile_e = jnp.minimum(tile_e, jnp.sum(tile_start[n_tiles[0] - 1] >= p_end, dtype=jnp.int32))
    shift = p_start - c_start
    pos = (rank_sorted + jnp.sum(jnp.where(onehot, shift[None, :], 0), axis=1)).reshape(T, 2)
    te = tile_e[:, None] == experts[None, :]
    t_shift = jnp.sum(jnp.where(te, shift[None, :], 0), axis=1)
    t_last = jnp.sum(jnp.where(te, (c_start + counts - 1)[None, :], 0), axis=1)
    row = tile_start[:, None] + jnp.arange(tm, dtype=jnp.int32)[None, :]
    src = jnp.clip(jnp.minimum(row - t_shift[:, None], t_last[:, None]), 0, A - 1)
    row_token = order[src.reshape(-1)] // 2
    return row_token, pos, tile_e, n_tiles


SPLIT_SIZES = (GLA_QK, GLA_QK, GLA_V, GLA_V, GLA_LOWRANK, DIFF_W, DIFF_W, DIFF_W)


def _layer(h2, p2, positions, B, S, lambda_init, apply_final_norm, attn_norm, w_in, w_a2, b_a,
           gla_norm, lq1, lk1, lq2, lk2, diff_subln, w_branch_a, w_branch_b, w_out, ffn_norm,
           w_rg, b_rg, w_re, b_re, w_gate, w_up, w_down, w_ple, ple_norm, w_ple_gate, final_norm,
           *, tm, ts, tq, tk, te):
    T, D = h2.shape
    n_mix = sum(SPLIT_SIZES)
    c_al = GLA_QK * 2 + GLA_V * 2
    w1 = jnp.concatenate([w_in[:, :c_al + GLA_LOWRANK],
                          jnp.zeros((D, GAL_PAD - GLA_LOWRANK), w_in.dtype),
                          w_in[:, c_al + GLA_LOWRANK:n_mix]], axis=1).astype(BF16)
    wg = w_in[:, n_mix:].astype(BF16)
    wa2p = jnp.concatenate([w_a2, jnp.zeros((GAL_PAD - GLA_LOWRANK, GLA_QK), w_a2.dtype)],
                           axis=0).astype(BF16)

    gq, gk, gv, gr, la, dq, dk, dv = _inproj(h2, positions, attn_norm, w1, wa2p, b_a, tm=tm)
    oa = _gla(gq, gk, gv, gr, la, gla_norm, B=B, S=S, ts=ts).reshape(T, GLA_V)
    ob = _diffattn(dq, dk, dv, lq1, lk1, lq2, lk2, diff_subln, B=B, S=S, tq=tq, tk=tk,
                   lambda_init=lambda_init).reshape(T, DIFF_W)

    wr = jnp.concatenate([w_rg, w_re, jnp.zeros((D, LANES - N_GROUPS - N_EXPERTS), F32)], axis=1)
    br = jnp.concatenate([b_rg, b_re, jnp.zeros((LANES - N_GROUPS - N_EXPERTS,), F32)]).reshape(1, LANES)
    h1, n2, route = _postmix(h2, oa, ob, attn_norm, wg, w_branch_a.astype(BF16),
                             w_branch_b.astype(BF16), w_out.astype(BF16), ffn_norm, wr, br, tm=tm)

    eid = route[:, 2:4].astype(jnp.int32)
    row_token, pos, tile_e, n_tiles = _routing_tables(eid, tm=te)
    xg = jnp.take(n2, row_token, axis=0)
    wgu = jnp.concatenate([w_gate, w_up], axis=2).astype(BF16)
    ys = _experts(tile_e, n_tiles, xg, wgu, w_down.astype(BF16), tm=te)
    y1 = jnp.take(ys, pos[:, 0], axis=0)
    y2 = jnp.take(ys, pos[:, 1], axis=0)
    return _final(h1, y1, y2, route, p2, w_ple.astype(BF16), ple_norm, w_ple_gate.astype(BF16),
                  final_norm, tm=tm, apply_final_norm=apply_final_norm)


def _block(x, p, positions, attn_norm, w_in, w_a2, b_a, gla_norm, lambda_q1, lambda_k1, lambda_q2,
           lambda_k2, diff_subln, w_branch_a, w_branch_b, w_out, ffn_norm, w_router_group,
           b_router_group, w_router_expert, b_router_expert, w_gate, w_up, w_down, w_ple, ple_norm,
           w_ple_gate, final_norm, *, tm, ts, tq, tk, te):
    B, S, D = x.shape
    depth = w_in.shape[0]
    h = x.reshape(B * S, D)
    for i in range(depth):
        lambda_init = 0.8 - 0.6 * math.exp(-0.3 * i)
        h = _layer(h, p[i].reshape(B * S, -1), positions, B, S, lambda_init, i == depth - 1,
                   attn_norm[i], w_in[i], w_a2[i], b_a[i], gla_norm[i], lambda_q1[i], lambda_k1[i],
                   lambda_q2[i], lambda_k2[i], diff_subln[i], w_branch_a[i], w_branch_b[i], w_out[i],
                   ffn_norm[i], w_router_group[i], b_router_group[i], w_router_expert[i],
                   b_router_expert[i], w_gate[i], w_up[i], w_down[i], w_ple[i], ple_norm[i],
                   w_ple_gate[i], final_norm, tm=tm, ts=ts, tq=tq, tk=tk, te=te)
    return h.reshape(B, S, D)


def kernel(x, p, positions, attn_norm, w_in, w_a2, b_a, gla_norm, lambda_q1, lambda_k1, lambda_q2, lambda_k2, diff_subln, w_branch_a, w_branch_b, w_out, ffn_norm, w_router_group, b_router_group, w_router_expert, b_router_expert, w_gate, w_up, w_down, w_ple, ple_norm, w_ple_gate, final_norm):
    S = x.shape[1]
    return _block(x, p, positions, attn_norm, w_in, w_a2, b_a, gla_norm, lambda_q1, lambda_k1,
                  lambda_q2, lambda_k2, diff_subln, w_branch_a, w_branch_b, w_out, ffn_norm,
                  w_router_group, b_router_group, w_router_expert, b_router_expert, w_gate, w_up,
                  w_down, w_ple, ple_norm, w_ple_gate, final_norm,
                  tm=512, ts=min(512, S), tq=min(512, S), tk=min(512, S), te=512)
```

```python
import functools
import math

import jax
import jax.numpy as jnp
import numpy as np
from jax import lax
from jax.experimental import pallas as pl
from jax.experimental.pallas import tpu as pltpu

EPS = 1e-6

GLA_HEADS = 4
GLA_DK = 64
GLA_DV = 128
GLA_LOWRANK = 16
GLA_TAU = 16.0
GLA_CHUNK = 64
GLA_QK = GLA_HEADS * GLA_DK
GLA_V = GLA_HEADS * GLA_DV

DIFF_HEADS = 4
DIFF_DH = 64
DIFF_DV = 2 * DIFF_DH
DIFF_W = DIFF_HEADS * DIFF_DV
ROPE_THETA = 500000.0
ROPE_DIM = DIFF_DH // 4
ROPE_HALF = ROPE_DIM // 2

N_GROUPS = 4
EXPERTS_PER_GROUP = 8
N_EXPERTS = N_GROUPS * EXPERTS_PER_GROUP
D_EXPERT = 256

LANES = 128
VMEM_LIMIT = 56 * 1024 * 1024

BF16 = jnp.bfloat16
F32 = jnp.float32
NT_DIMS = (((1,), (1,)), ((), ()))
TN_DIMS = (((0,), (0,)), ((), ()))
NEG_BIG = -1e30


def _rms(x, g):
    return x * lax.rsqrt(jnp.mean(x * x, axis=-1, keepdims=True) + EPS) * g


def _cparams(semantics):
    return pltpu.CompilerParams(dimension_semantics=semantics, vmem_limit_bytes=VMEM_LIMIT)


HI16 = 0xFFFF0000


def _pack_bf16_pairs(x):
    w = x.shape[1] // 2
    bits = lax.bitcast_convert_type(x.astype(BF16).astype(F32), jnp.uint32)
    return (bits[:, w:] & jnp.uint32(HI16)) | (bits[:, :w] >> 16)


def _unpack_bf16_pairs(u):
    lo = lax.bitcast_convert_type(u << 16, F32)
    hi = lax.bitcast_convert_type(u & jnp.uint32(HI16), F32)
    return jnp.concatenate([lo, hi], axis=1)


def _start_row_gather(idx_ref, n_rows, src_hbm, dst, sem):
    def body(r, carry):
        pltpu.make_async_copy(src_hbm.at[pl.ds(idx_ref[0, 0, r], 1), :], dst.at[pl.ds(r, 1), :],
                              sem).start()
        return carry
    lax.fori_loop(0, n_rows, body, 0, unroll=8)


def _wait_row_gather(n_rows, src_hbm, dst, sem):
    pltpu.make_async_copy(src_hbm.at[pl.ds(0, n_rows), :], dst, sem).wait()


C_GQ = 0
C_GK = C_GQ + GLA_QK
C_GV = C_GK + GLA_QK
C_GR = C_GV + GLA_V
C_AL = C_GR + GLA_V
C_GLA_END = C_AL + GLA_LOWRANK
C_DQ = 0
C_DK = C_DQ + DIFF_W
C_DV = C_DK + DIFF_W
C_END = C_DV + DIFF_W


def _inproj_kernel(x_ref, pos_ref, g_ref, wgla_ref, wdiff_ref, wa2_ref, ba_ref, inv_ref, selc_ref,
                   seln_ref, selp_ref, one_ref,
                   gq_ref, gk_ref, gv_ref, gr_ref, la_ref, dq_ref, dk_ref, dv_ref):
    nb = _rms(x_ref[...], g_ref[...]).astype(BF16)

    def proj_gla(c0, c1):
        return jnp.dot(nb, wgla_ref[:, c0:c1], preferred_element_type=F32)

    def proj(c0, c1):
        return jnp.dot(nb, wdiff_ref[:, c0:c1], preferred_element_type=F32)

    gq_ref[...] = proj_gla(C_GQ, C_GK) * (GLA_DK ** -0.5)
    gk_ref[...] = proj_gla(C_GK, C_GV)
    gv_ref[...] = proj_gla(C_GV, C_GR).astype(BF16)
    gr_ref[...] = proj_gla(C_GR, C_AL).astype(BF16)

    gal = proj_gla(C_AL, C_GLA_END).astype(BF16)
    a_logit = jnp.dot(gal, wa2_ref[...], preferred_element_type=F32) + ba_ref[...]
    la_ref[...] = (jnp.minimum(a_logit, 0.0) - jnp.log1p(jnp.exp(-jnp.abs(a_logit)))) / GLA_TAU

    ang = inv_ref[...] * pos_ref[0].astype(F32)
    cs = jnp.concatenate([jnp.cos(ang), jnp.sin(ang)], axis=0)

    def spread(sel_ref):
        return lax.dot_general(cs, sel_ref[...], TN_DIMS, precision=lax.Precision.HIGHEST,
                               preferred_element_type=F32)

    cosf = spread(selc_ref) + one_ref[...]
    sneg = spread(seln_ref)
    spos = spread(selp_ref)

    def rope_store(c0, out_ref, scale):
        for j in range(DIFF_W // LANES):
            blk = proj(c0 + j * LANES, c0 + (j + 1) * LANES)
            rot = (blk * cosf + pltpu.roll(blk, LANES - ROPE_HALF, 1) * sneg
                   + pltpu.roll(blk, ROPE_HALF, 1) * spos)
            out_ref[:, j * LANES:(j + 1) * LANES] = (rot * scale).astype(BF16)

    rope_store(C_DQ, dq_ref, DIFF_DH ** -0.5)
    rope_store(C_DK, dk_ref, 1.0)
    dv_ref[...] = proj(C_DV, C_END).astype(BF16)


def _rope_tables():
    lane = np.arange(LANES)
    r = lane % DIFF_DH
    selc = np.zeros((2 * ROPE_HALF, LANES), np.float32)
    seln = np.zeros_like(selc)
    selp = np.zeros_like(selc)
    one = np.zeros((1, LANES), np.float32)
    for l in range(LANES):
        if r[l] < ROPE_DIM:
            selc[r[l] % ROPE_HALF, l] = 1.0
            if r[l] < ROPE_HALF:
                seln[ROPE_HALF + r[l], l] = -1.0
            else:
                selp[ROPE_HALF + r[l] - ROPE_HALF, l] = 1.0
        else:
            one[0, l] = 1.0
    return jnp.asarray(selc), jnp.asarray(seln), jnp.asarray(selp), jnp.asarray(one)


def _inproj(x2, positions, attn_norm, wgla, wdiff, wa2, b_a, *, tm):
    T, D = x2.shape
    nt = T // tm
    pos3 = positions.reshape(nt, 1, tm)
    inv = (ROPE_THETA ** (-jnp.arange(0, ROPE_DIM, 2, dtype=F32) / ROPE_DIM)).reshape(ROPE_HALF, 1)
    selc, seln, selp, one = _rope_tables()
    row = lambda i: (i, 0)
    fixed = lambda i: (0, 0)
    out_shapes = (
        jax.ShapeDtypeStruct((T, GLA_QK), F32), jax.ShapeDtypeStruct((T, GLA_QK), F32),
        jax.ShapeDtypeStruct((T, GLA_V), BF16), jax.ShapeDtypeStruct((T, GLA_V), BF16),
        jax.ShapeDtypeStruct((T, GLA_QK), F32),
        jax.ShapeDtypeStruct((T, DIFF_W), BF16), jax.ShapeDtypeStruct((T, DIFF_W), BF16),
        jax.ShapeDtypeStruct((T, DIFF_W), BF16),
    )
    return pl.pallas_call(
        _inproj_kernel,
        grid=(nt,),
        in_specs=[
            pl.BlockSpec((tm, D), row),
            pl.BlockSpec((1, 1, tm), lambda i: (i, 0, 0)),
            pl.BlockSpec((1, D), fixed),
            pl.BlockSpec((D, C_GLA_END), fixed),
            pl.BlockSpec((D, C_END), fixed),
            pl.BlockSpec((GLA_LOWRANK, GLA_QK), fixed),
            pl.BlockSpec((1, GLA_QK), fixed),
            pl.BlockSpec((ROPE_HALF, 1), fixed),
            pl.BlockSpec((2 * ROPE_HALF, LANES), fixed),
            pl.BlockSpec((2 * ROPE_HALF, LANES), fixed),
            pl.BlockSpec((2 * ROPE_HALF, LANES), fixed),
            pl.BlockSpec((1, LANES), fixed),
        ],
        out_specs=[pl.BlockSpec((tm, s.shape[1]), row) for s in out_shapes],
        out_shape=out_shapes,
        compiler_params=_cparams(("parallel",)),
        name="inproj",
    )(x2, pos3, attn_norm.reshape(1, D), wgla, wdiff, wa2, b_a.reshape(1, GLA_QK), inv, selc, seln,
      selp, one)


def _gla_kernel(gq_ref, gk_ref, gv_ref, gr_ref, la_ref, gn_ref, oa_ref, st_ref, *, n_chunks):
    @pl.when(pl.program_id(1) == 0)
    def _():
        st_ref[...] = jnp.zeros_like(st_ref)

    C = GLA_CHUNK
    tril = lax.broadcasted_iota(jnp.int32, (C, C), 0) >= lax.broadcasted_iota(jnp.int32, (C, C), 1)
    trilf = tril.astype(F32)
    head_of_lane = lax.broadcasted_iota(jnp.int32, (1, GLA_QK), 1) // GLA_DK
    hmask = [head_of_lane == h for h in range(GLA_HEADS)]
    gn = gn_ref[...]

    for c in range(n_chunks):
        sl = pl.ds(c * C, C)
        q = gq_ref[0, sl, :]
        k = gk_ref[0, sl, :]
        la = la_ref[0, sl, :]
        v = gv_ref[0, sl, :]
        b = jnp.dot(trilf, la, precision=lax.Precision.HIGHEST, preferred_element_type=F32)
        b_last = b[C - 1:C, :]
        q_dec = (q * jnp.exp(b)).astype(BF16)
        k_inv = (k * jnp.exp(-b)).astype(BF16)
        k_end = (k * jnp.exp(b_last - b)).astype(BF16)
        decay = jnp.exp(b_last)

        q_heads = jnp.concatenate([jnp.where(hmask[h], q_dec, jnp.zeros_like(q_dec))
                                   for h in range(GLA_HEADS)], axis=0)
        att = lax.dot_general(q_heads, k_inv, NT_DIMS, preferred_element_type=F32)
        st = st_ref[...]
        inter = lax.dot_general(q_heads, st.astype(BF16), NT_DIMS,
                                preferred_element_type=F32)
        u = jnp.zeros_like(st)
        for h in range(GLA_HEADS):
            rows = slice(h * C, (h + 1) * C)
            cols = slice(h * GLA_DV, (h + 1) * GLA_DV)
            a_h = jnp.where(tril, att[rows], 0.0).astype(BF16)
            v_h = v[:, cols]
            o_h = jnp.dot(a_h, v_h, preferred_element_type=F32) + inter[rows]
            y = _rms(o_h, gn)
            r = gr_ref[0, sl, cols].astype(F32)
            oa_ref[0, sl, cols] = (y * (r * jax.nn.sigmoid(r))).astype(BF16)
            u_h = lax.dot_general(v_h, k_end, TN_DIMS, preferred_element_type=F32)
            u = u + jnp.where(hmask[h], u_h, 0.0)
        st_ref[...] = st * decay + u


def _gla(gq, gk, gv, gr, la, gla_norm, *, B, S, ts):
    n_chunks = ts // GLA_CHUNK
    blk = lambda w: pl.BlockSpec((1, ts, w), lambda b, j: (b, j, 0))
    r3 = lambda a: a.reshape(B, S, a.shape[-1])
    return pl.pallas_call(
        functools.partial(_gla_kernel, n_chunks=n_chunks),
        grid=(B, S // ts),
        in_specs=[blk(GLA_QK), blk(GLA_QK), blk(GLA_V), blk(GLA_V), blk(GLA_QK),
                  pl.BlockSpec((1, GLA_DV), lambda b, j: (0, 0))],
        out_specs=blk(GLA_V),
        out_shape=jax.ShapeDtypeStruct((B, S, GLA_V), BF16),
        scratch_shapes=[pltpu.VMEM((GLA_DV, GLA_QK), F32)],
        compiler_params=_cparams(("parallel", "arbitrary")),
        name="gla",
    )(r3(gq), r3(gk), r3(gv), r3(gr), r3(la), gla_norm.reshape(1, GLA_DV))


def _diffattn_kernel(dq_ref, dk_ref, dv_ref, lq1_ref, lk1_ref, lq2_ref, lk2_ref, sub_ref,
                     ob_ref, m_ref, l_ref, acc_ref, *, tq, tk, lambda_init):
    S = dq_ref.shape[1]
    nq = S // tq
    R = 2 * tq
    lam = (jnp.exp(jnp.sum(lq1_ref[...] * lk1_ref[...], axis=-1, keepdims=True))
           - jnp.exp(jnp.sum(lq2_ref[...] * lk2_ref[...], axis=-1, keepdims=True))
           + lambda_init)
    first_comp = lax.broadcasted_iota(jnp.int32, (1, DIFF_DV), 1) < DIFF_DH
    sub = sub_ref[...]

    for h in range(DIFF_HEADS):
        cols = slice(h * DIFF_DV, (h + 1) * DIFF_DV)
        for qi in range(nq):
            qb = dq_ref[0, qi * tq:(qi + 1) * tq, cols]
            zero = jnp.zeros_like(qb)
            qs = jnp.concatenate([jnp.where(first_comp, qb, zero),
                                  jnp.where(first_comp, zero, qb)], axis=0)
            m_ref[...] = jnp.full(m_ref.shape, NEG_BIG, F32)
            l_ref[...] = jnp.zeros(l_ref.shape, F32)
            acc_ref[...] = jnp.zeros(acc_ref.shape, F32)

            def step(kstart, masked, qs=qs, cols=cols, qi=qi):
                kb = dk_ref[0, pl.ds(kstart, tk), cols]
                vb = dv_ref[0, pl.ds(kstart, tk), cols]
                s = lax.dot_general(kb, qs, NT_DIMS, preferred_element_type=F32)
                if masked:
                    kpos = kstart + lax.broadcasted_iota(jnp.int32, (tk, R), 0)
                    qpos = qi * tq + lax.broadcasted_iota(jnp.int32, (tk, R), 1) % tq
                    s = jnp.where(qpos >= kpos, s, NEG_BIG)
                m_prev = m_ref[...]
                m_new = jnp.maximum(m_prev, jnp.max(s, axis=0, keepdims=True))
                alpha = jnp.exp(m_prev - m_new)
                p = jnp.exp(s - m_new)
                l_ref[...] = alpha * l_ref[...] + jnp.sum(p, axis=0, keepdims=True)
                pv = lax.dot_general(vb, p.astype(BF16), TN_DIMS, preferred_element_type=F32)
                acc_ref[...] = alpha * acc_ref[...] + pv
                m_ref[...] = m_new

            n_full = (qi * tq) // tk

            def body(i, carry, step=step):
                step(pl.multiple_of(i * tk, tk), False)
                return carry

            if n_full > 0:
                lax.fori_loop(0, n_full, body, 0)
            for d in range(tq // tk):
                step(qi * tq + d * tk, True)

            o_all = acc_ref[...] / l_ref[...]
            o = o_all[:, :tq] - lam * o_all[:, tq:]
            y = o * lax.rsqrt(jnp.mean(o * o, axis=0, keepdims=True) + EPS) * sub
            y = y * (1.0 - lambda_init)
            ob_ref[0, qi * tq:(qi + 1) * tq, cols] = y.T.astype(BF16)


def _diffattn(dq, dk, dv, lq1, lk1, lq2, lk2, diff_subln, *, B, S, tq, tk, lambda_init):
    seq = pl.BlockSpec((1, S, DIFF_W), lambda b: (b, 0, 0))
    vec = lambda w: pl.BlockSpec((1, w), lambda b: (0, 0))
    r3 = lambda a: a.reshape(B, S, DIFF_W)
    return pl.pallas_call(
        functools.partial(_diffattn_kernel, tq=tq, tk=tk, lambda_init=lambda_init),
        grid=(B,),
        in_specs=[seq, seq, seq, vec(DIFF_DH), vec(DIFF_DH), vec(DIFF_DH), vec(DIFF_DH),
                  pl.BlockSpec((DIFF_DV, 1), lambda b: (0, 0))],
        out_specs=seq,
        out_shape=jax.ShapeDtypeStruct((B, S, DIFF_W), BF16),
        scratch_shapes=[pltpu.VMEM((1, 2 * tq), F32), pltpu.VMEM((1, 2 * tq), F32),
                        pltpu.VMEM((DIFF_DV, 2 * tq), F32)],
        compiler_params=_cparams(("parallel",)),
        name="diffattn",
    )(r3(dq), r3(dk), r3(dv), lq1.reshape(1, -1), lk1.reshape(1, -1), lq2.reshape(1, -1),
      lk2.reshape(1, -1), diff_subln.reshape(-1, 1))


ROUTE_W = 8


def _first_index_of(mask, lane):
    return jnp.min(jnp.where(mask, lane, LANES), axis=-1, keepdims=True)


def _postmix_kernel(x_ref, oa_ref, ob_ref, g_ref, wg_ref, wba_ref, wbb_ref, wo_ref, fn_ref,
                    wr_ref, br_ref, h_ref, n2_ref, route_ref):
    D = x_ref.shape[1]
    x = x_ref[...]
    nb = _rms(x, g_ref[...]).astype(BF16)
    y_a = jnp.dot(oa_ref[...], wba_ref[...], preferred_element_type=F32)
    g_a = jnp.dot(nb, wg_ref[:, :D], preferred_element_type=F32)
    merged = jax.nn.sigmoid(g_a) * y_a
    y_b = jnp.dot(ob_ref[...], wbb_ref[...], preferred_element_type=F32)
    g_b = jnp.dot(nb, wg_ref[:, D:], preferred_element_type=F32)
    merged = merged + jax.nn.sigmoid(g_b) * y_b
    h = x + jnp.dot(merged.astype(BF16), wo_ref[...], preferred_element_type=F32)
    h_ref[...] = h
    n2 = _rms(h, fn_ref[...])
    n2_ref[...] = _pack_bf16_pairs(n2)

    lg = jnp.dot(n2, wr_ref[...], precision=lax.Precision.HIGHEST,
                 preferred_element_type=F32) + br_ref[...]
    lane = lax.broadcasted_iota(jnp.int32, lg.shape, 1)
    is_g = lane < N_GROUPS
    g_max = jnp.max(jnp.where(is_g, lg, -jnp.inf), axis=-1, keepdims=True)
    g_exp = jnp.where(is_g, jnp.exp(lg - g_max), 0.0)
    g_prob = g_exp / jnp.sum(g_exp, axis=-1, keepdims=True)
    g_p = jnp.max(g_prob, axis=-1, keepdims=True)
    g_idx = _first_index_of(is_g & (g_prob == g_p), lane)

    e_lo = N_GROUPS + EXPERTS_PER_GROUP * g_idx
    is_e = (lane >= e_lo) & (lane < e_lo + EXPERTS_PER_GROUP)
    e_max = jnp.max(jnp.where(is_e, lg, -jnp.inf), axis=-1, keepdims=True)
    e_exp = jnp.where(is_e, jnp.exp(lg - e_max), 0.0)
    e_prob = e_exp / jnp.sum(e_exp, axis=-1, keepdims=True)
    p1 = jnp.max(jnp.where(is_e, e_prob, -1.0), axis=-1, keepdims=True)
    i1 = _first_index_of(is_e & (e_prob == p1), lane)
    rest = is_e & (lane != i1)
    p2 = jnp.max(jnp.where(rest, e_prob, -1.0), axis=-1, keepdims=True)
    i2 = _first_index_of(rest & (e_prob == p2), lane)
    den = p1 + p2
    w1 = g_p * (p1 / den)
    w2 = g_p * (p2 / den)
    rec = jnp.where(lane == 0, w1, 0.0)
    rec = jnp.where(lane == 1, w2, rec)
    rec = jnp.where(lane == 2, (i1 - N_GROUPS).astype(F32), rec)
    rec = jnp.where(lane == 3, (i2 - N_GROUPS).astype(F32), rec)
    route_ref[...] = rec[:, :ROUTE_W]


def _postmix(x2, oa, ob, attn_norm, wg, wba, wbb, wo, ffn_norm, wr, br, *, tm):
    T, D = x2.shape
    row = lambda i: (i, 0)
    fixed = lambda i: (0, 0)
    full = lambda a: pl.BlockSpec(a.shape, fixed)
    g = attn_norm.reshape(1, D)
    fn = ffn_norm.reshape(1, D)
    return pl.pallas_call(
        _postmix_kernel,
        grid=(T // tm,),
        in_specs=[pl.BlockSpec((tm, D), row), pl.BlockSpec((tm, GLA_V), row),
                  pl.BlockSpec((tm, DIFF_W), row), full(g), full(wg), full(wba), full(wbb),
                  full(wo), full(fn), full(wr), full(br)],
        out_specs=[pl.BlockSpec((tm, D), row), pl.BlockSpec((tm, D // 2), row),
                   pl.BlockSpec((tm, ROUTE_W), row)],
        out_shape=(jax.ShapeDtypeStruct((T, D), F32), jax.ShapeDtypeStruct((T, D // 2), jnp.uint32),
                   jax.ShapeDtypeStruct((T, ROUTE_W), F32)),
        compiler_params=_cparams(("parallel",)),
        name="postmix",
    )(x2, oa, ob, g, wg, wba, wbb, wo, fn, wr, br)


def _experts_kernel(tile_e_ref, n_tiles_ref, idx_ref, idx_next_ref, x_hbm, wg_ref, wu_ref, wd_ref,
                    y_ref, xbuf, sem):
    tm = xbuf.shape[1]
    i = pl.program_id(0)
    n = n_tiles_ref[0]
    slot = i % 2

    @pl.when(i == 0)
    def _():
        _start_row_gather(idx_ref, tm, x_hbm, xbuf.at[0], sem.at[0])

    @pl.when(i + 1 < n)
    def _():
        _start_row_gather(idx_next_ref, tm, x_hbm, xbuf.at[1 - slot], sem.at[1 - slot])

    @pl.when(i < n)
    def _():
        _wait_row_gather(tm, x_hbm, xbuf.at[slot], sem.at[slot])
        x = _unpack_bf16_pairs(xbuf[slot]).astype(BF16)
        g = jnp.dot(x, wg_ref[0], preferred_element_type=F32)
        u = jnp.dot(x, wu_ref[0], preferred_element_type=F32)
        he = (g * jax.nn.sigmoid(g)) * u
        y = jnp.dot(he.astype(BF16), wd_ref[0], preferred_element_type=F32)
        y_ref[...] = _pack_bf16_pairs(y)

    @pl.when(i >= n)
    def _():
        y_ref[...] = jnp.zeros_like(y_ref)


def _experts(tile_e, n_tiles, row_token, n2p, wg, wu, wd, *, tm):
    D = wg.shape[1]
    n_tiles_max = row_token.shape[0] // tm
    idx = row_token.reshape(n_tiles_max, 1, tm)
    last = n_tiles_max - 1
    smem_rows = lambda f: pl.BlockSpec((1, 1, tm), f, memory_space=pltpu.SMEM)
    grid_spec = pltpu.PrefetchScalarGridSpec(
        num_scalar_prefetch=2,
        grid=(n_tiles_max,),
        in_specs=[smem_rows(lambda i, te, nt: (i, 0, 0)),
                  smem_rows(lambda i, te, nt: (jnp.minimum(i + 1, last), 0, 0)),
                  pl.BlockSpec(memory_space=pl.ANY),
                  pl.BlockSpec((1, D, D_EXPERT), lambda i, te, nt: (te[i], 0, 0)),
                  pl.BlockSpec((1, D, D_EXPERT), lambda i, te, nt: (te[i], 0, 0)),
                  pl.BlockSpec((1, D_EXPERT, D), lambda i, te, nt: (te[i], 0, 0))],
        out_specs=pl.BlockSpec((tm, D // 2), lambda i, te, nt: (i, 0)),
        scratch_shapes=[pltpu.VMEM((2, tm, D // 2), jnp.uint32), pltpu.SemaphoreType.DMA((2,))],
    )
    return pl.pallas_call(
        _experts_kernel,
        grid_spec=grid_spec,
        out_shape=jax.ShapeDtypeStruct((n_tiles_max * tm, D // 2), jnp.uint32),
        compiler_params=_cparams(("arbitrary",)),
        name="experts",
    )(tile_e, n_tiles, idx, idx, n2p, wg, wu, wd)


def _final_kernel(idx_ref, idx_next_ref, h_ref, ys_hbm, route_ref, p_ref, wple_ref, pn_ref, wpg_ref,
                  fn_ref, o_ref, ybuf, sem, *, apply_final_norm):
    tm = h_ref.shape[0]
    i = pl.program_id(0)
    slot = i % 2

    @pl.when(i == 0)
    def _():
        _start_row_gather(idx_ref, 2 * tm, ys_hbm, ybuf.at[0], sem.at[0])

    @pl.when(i + 1 < pl.num_programs(0))
    def _():
        _start_row_gather(idx_next_ref, 2 * tm, ys_hbm, ybuf.at[1 - slot], sem.at[1 - slot])

    _wait_row_gather(2 * tm, ys_hbm, ybuf.at[slot], sem.at[slot])
    route = route_ref[...]
    y = route[:, 0:1] * _unpack_bf16_pairs(ybuf[slot, :tm, :])
    y = y + route[:, 1:2] * _unpack_bf16_pairs(ybuf[slot, tm:, :])
    h = h_ref[...] + y
    e = _rms(jnp.dot(p_ref[...].astype(BF16), wple_ref[...], preferred_element_type=F32), pn_ref[...])
    gate = jax.nn.sigmoid(jnp.dot(h.astype(BF16), wpg_ref[...], preferred_element_type=F32))
    h = h + gate * e
    if apply_final_norm:
        h = _rms(h, fn_ref[...])
    o_ref[...] = h


def _final(h1, ys, pos, route, p2, wple, ple_norm, wpg, final_norm, *, tm, apply_final_norm):
    T, D = h1.shape
    nt = T // tm
    row = lambda i: (i, 0)
    fixed = lambda i: (0, 0)
    full = lambda a: pl.BlockSpec(a.shape, fixed)
    pn = ple_norm.reshape(1, D)
    fn = final_norm.reshape(1, D)
    idx = pos.reshape(nt, tm, 2).transpose(0, 2, 1).reshape(nt, 1, 2 * tm)
    smem_rows = lambda f: pl.BlockSpec((1, 1, 2 * tm), f, memory_space=pltpu.SMEM)
    return pl.pallas_call(
        functools.partial(_final_kernel, apply_final_norm=apply_final_norm),
        grid=(nt,),
        in_specs=[smem_rows(lambda i: (i, 0, 0)),
                  smem_rows(lambda i: (jnp.minimum(i + 1, nt - 1), 0, 0)),
                  pl.BlockSpec((tm, D), row), pl.BlockSpec(memory_space=pl.ANY),
                  pl.BlockSpec((tm, ROUTE_W), row), pl.BlockSpec((tm, p2.shape[1]), row),
                  full(wple), full(pn), full(wpg), full(fn)],
        out_specs=pl.BlockSpec((tm, D), row),
        out_shape=jax.ShapeDtypeStruct((T, D), F32),
        scratch_shapes=[pltpu.VMEM((2, 2 * tm, D // 2), jnp.uint32), pltpu.SemaphoreType.DMA((2,))],
        compiler_params=_cparams(("arbitrary",)),
        name="final",
    )(idx, idx, h1, ys, route, p2, wple, pn, wpg, fn)


def _routing_tables(eid, *, tm):
    T = eid.shape[0]
    A = 2 * T
    n_tiles_max = A // tm + N_EXPERTS
    flat_e = eid.reshape(A)
    experts = jnp.arange(N_EXPERTS, dtype=jnp.int32)
    onehot = flat_e[:, None] == experts[None, :]
    order = jnp.argsort(flat_e, stable=True).astype(jnp.int32)
    rank_sorted = jnp.argsort(order).astype(jnp.int32)
    counts = jnp.sum(onehot, axis=0, dtype=jnp.int32)
    padded = (counts + tm - 1) // tm * tm
    p_end = jnp.cumsum(padded)
    p_start = p_end - padded
    c_start = jnp.cumsum(counts) - counts
    n_tiles = (p_end[-1] // tm).astype(jnp.int32).reshape(1)
    tile_start = jnp.arange(n_tiles_max, dtype=jnp.int32) * tm
    tile_e = jnp.sum(tile_start[:, None] >= p_end[None, :], axis=1, dtype=jnp.int32)
    tile_e = jnp.minimum(tile_e, jnp.sum(tile_start[n_tiles[0] - 1] >= p_end, dtype=jnp.int32))
    shift = p_start - c_start
    pos = (rank_sorted + jnp.sum(jnp.where(onehot, shift[None, :], 0), axis=1)).reshape(T, 2)
    te = tile_e[:, None] == experts[None, :]
    t_shift = jnp.sum(jnp.where(te, shift[None, :], 0), axis=1)
    t_last = jnp.sum(jnp.where(te, (c_start + counts - 1)[None, :], 0), axis=1)
    row = tile_start[:, None] + jnp.arange(tm, dtype=jnp.int32)[None, :]
    src = jnp.clip(jnp.minimum(row - t_shift[:, None], t_last[:, None]), 0, A - 1)
    row_token = order[src.reshape(-1)] // 2
    return row_token, pos, tile_e, n_tiles


SPLIT_SIZES = (GLA_QK, GLA_QK, GLA_V, GLA_V, GLA_LOWRANK, DIFF_W, DIFF_W, DIFF_W)


def _layer(h2, p2, positions, B, S, lambda_init, apply_final_norm, attn_norm, w_in, w_a2, b_a,
           gla_norm, lq1, lk1, lq2, lk2, diff_subln, w_branch_a, w_branch_b, w_out, ffn_norm,
           w_rg, b_rg, w_re, b_re, w_gate, w_up, w_down, w_ple, ple_norm, w_ple_gate, final_norm,
           *, tm, ts, tq, tk, te):
    T, D = h2.shape
    n_mix = sum(SPLIT_SIZES)
    wgla = w_in[:, :C_GLA_END].astype(BF16)
    wdiff = w_in[:, C_GLA_END:n_mix].astype(BF16)
    wg = w_in[:, n_mix:].astype(BF16)

    gq, gk, gv, gr, la, dq, dk, dv = _inproj(h2, positions, attn_norm, wgla, wdiff,
                                             w_a2.astype(BF16), b_a, tm=tm)
    oa = _gla(gq, gk, gv, gr, la, gla_norm, B=B, S=S, ts=ts).reshape(T, GLA_V)
    ob = _diffattn(dq, dk, dv, lq1, lk1, lq2, lk2, diff_subln, B=B, S=S, tq=tq, tk=tk,
                   lambda_init=lambda_init).reshape(T, DIFF_W)

    wr = jnp.concatenate([w_rg, w_re, jnp.zeros((D, LANES - N_GROUPS - N_EXPERTS), F32)], axis=1)
    br = jnp.concatenate([b_rg, b_re, jnp.zeros((LANES - N_GROUPS - N_EXPERTS,), F32)]).reshape(1, LANES)
    h1, n2, route = _postmix(h2, oa, ob, attn_norm, wg, w_branch_a.astype(BF16),
                             w_branch_b.astype(BF16), w_out.astype(BF16), ffn_norm, wr, br, tm=tm)

    eid = route[:, 2:4].astype(jnp.int32)
    row_token, pos, tile_e, n_tiles = _routing_tables(eid, tm=te)
    ys = _experts(tile_e, n_tiles, row_token, n2, w_gate.astype(BF16), w_up.astype(BF16),
                  w_down.astype(BF16), tm=te)
    return _final(h1, ys, pos, route, p2, w_ple.astype(BF16), ple_norm, w_ple_gate.astype(BF16),
                  final_norm, tm=tm, apply_final_norm=apply_final_norm)


def _block(x, p, positions, attn_norm, w_in, w_a2, b_a, gla_norm, lambda_q1, lambda_k1, lambda_q2,
           lambda_k2, diff_subln, w_branch_a, w_branch_b, w_out, ffn_norm, w_router_group,
           b_router_group, w_router_expert, b_router_expert, w_gate, w_up, w_down, w_ple, ple_norm,
           w_ple_gate, final_norm, *, tm, ts, tq, tk, te):
    B, S, D = x.shape
    depth = w_in.shape[0]
    h = x.reshape(B * S, D)
    for i in range(depth):
        lambda_init = 0.8 - 0.6 * math.exp(-0.3 * i)
        h = _layer(h, p[i].reshape(B * S, -1), positions, B, S, lambda_init, i == depth - 1,
                   attn_norm[i], w_in[i], w_a2[i], b_a[i], gla_norm[i], lambda_q1[i], lambda_k1[i],
                   lambda_q2[i], lambda_k2[i], diff_subln[i], w_branch_a[i], w_branch_b[i], w_out[i],
                   ffn_norm[i], w_router_group[i], b_router_group[i], w_router_expert[i],
                   b_router_expert[i], w_gate[i], w_up[i], w_down[i], w_ple[i], ple_norm[i],
                   w_ple_gate[i], final_norm, tm=tm, ts=ts, tq=tq, tk=tk, te=te)
    return h.reshape(B, S, D)


def kernel(x, p, positions, attn_norm, w_in, w_a2, b_a, gla_norm, lambda_q1, lambda_k1, lambda_q2, lambda_k2, diff_subln, w_branch_a, w_branch_b, w_out, ffn_norm, w_router_group, b_router_group, w_router_expert, b_router_expert, w_gate, w_up, w_down, w_ple, ple_norm, w_ple_gate, final_norm):
    S = x.shape[1]
    return _block(x, p, positions, attn_norm, w_in, w_a2, b_a, gla_norm, lambda_q1, lambda_k1,
                  lambda_q2, lambda_k2, diff_subln, w_branch_a, w_branch_b, w_out, ffn_norm,
                  w_router_group, b_router_group, w_router_expert, b_router_expert, w_gate, w_up,
                  w_down, w_ple, ple_norm, w_ple_gate, final_norm,
                  tm=512, ts=min(512, S), tq=min(512, S), tk=min(512, S), te=512)
```

```python
import functools
import math

import jax
import jax.numpy as jnp
import numpy as np
from jax import lax
from jax.experimental import pallas as pl
from jax.experimental.pallas import tpu as pltpu

EPS = 1e-6

GLA_HEADS = 4
GLA_DK = 64
GLA_DV = 128
GLA_LOWRANK = 16
GLA_TAU = 16.0
GLA_CHUNK = 64
GLA_QK = GLA_HEADS * GLA_DK
GLA_V = GLA_HEADS * GLA_DV

DIFF_HEADS = 4
DIFF_DH = 64
DIFF_DV = 2 * DIFF_DH
DIFF_W = DIFF_HEADS * DIFF_DV
ROPE_THETA = 500000.0
ROPE_DIM = DIFF_DH // 4
ROPE_HALF = ROPE_DIM // 2

N_GROUPS = 4
EXPERTS_PER_GROUP = 8
N_EXPERTS = N_GROUPS * EXPERTS_PER_GROUP
D_EXPERT = 256

LANES = 128
VMEM_LIMIT = 56 * 1024 * 1024

BF16 = jnp.bfloat16
F32 = jnp.float32
NT_DIMS = (((1,), (1,)), ((), ()))
TN_DIMS = (((0,), (0,)), ((), ()))
NEG_BIG = -1e30


def _rms(x, g):
    return x * lax.rsqrt(jnp.mean(x * x, axis=-1, keepdims=True) + EPS) * g


def _cparams(semantics):
    return pltpu.CompilerParams(dimension_semantics=semantics, vmem_limit_bytes=VMEM_LIMIT)


SUBLANES = 8


def _store_slab(ref, x):
    n = x.shape[0]
    for j in range(SUBLANES):
        ref[pl.ds(j, n, stride=SUBLANES), :] = x[:, j * LANES:(j + 1) * LANES]


def _load_slab(ref, n):
    return jnp.concatenate([ref[pl.ds(j, n, stride=SUBLANES), :] for j in range(SUBLANES)], axis=1)


def _start_token_gather(idx_ref, n_tokens, src_hbm, dst, sem):
    def body(r, carry):
        src_row = pl.multiple_of(idx_ref[0, 0, r] * SUBLANES, SUBLANES)
        dst_row = pl.multiple_of(r * SUBLANES, SUBLANES)
        pltpu.make_async_copy(src_hbm.at[pl.ds(src_row, SUBLANES), :],
                              dst.at[pl.ds(dst_row, SUBLANES), :], sem).start()
        return carry
    lax.fori_loop(0, n_tokens, body, 0, unroll=8)


def _wait_token_gather(n_tokens, src_hbm, dst, sem):
    pltpu.make_async_copy(src_hbm.at[pl.ds(0, n_tokens * SUBLANES), :], dst, sem).wait()


C_GQ = 0
C_GK = C_GQ + GLA_QK
C_GV = C_GK + GLA_QK
C_GR = C_GV + GLA_V
C_AL = C_GR + GLA_V
C_GLA_END = C_AL + GLA_LOWRANK
C_DQ = 0
C_DK = C_DQ + DIFF_W
C_DV = C_DK + DIFF_W
C_END = C_DV + DIFF_W


def _inproj_kernel(x_ref, pos_ref, g_ref, wgla_ref, wdiff_ref, wa2_ref, ba_ref, inv_ref, selc_ref,
                   seln_ref, selp_ref, one_ref,
                   gq_ref, gk_ref, gv_ref, gr_ref, la_ref, dq_ref, dk_ref, dv_ref):
    nb = _rms(x_ref[...], g_ref[...]).astype(BF16)

    def proj_gla(c0, c1):
        return jnp.dot(nb, wgla_ref[:, c0:c1], preferred_element_type=F32)

    def proj(c0, c1):
        return jnp.dot(nb, wdiff_ref[:, c0:c1], preferred_element_type=F32)

    gq_ref[...] = proj_gla(C_GQ, C_GK) * (GLA_DK ** -0.5)
    gk_ref[...] = proj_gla(C_GK, C_GV)
    gv_ref[...] = proj_gla(C_GV, C_GR).astype(BF16)
    gr_ref[...] = proj_gla(C_GR, C_AL).astype(BF16)

    gal = proj_gla(C_AL, C_GLA_END).astype(BF16)
    a_logit = jnp.dot(gal, wa2_ref[...], preferred_element_type=F32) + ba_ref[...]
    la_ref[...] = (jnp.minimum(a_logit, 0.0) - jnp.log1p(jnp.exp(-jnp.abs(a_logit)))) / GLA_TAU

    ang = inv_ref[...] * pos_ref[0].astype(F32)
    cs = jnp.concatenate([jnp.cos(ang), jnp.sin(ang)], axis=0)

    def spread(sel_ref):
        return lax.dot_general(cs, sel_ref[...], TN_DIMS, precision=lax.Precision.HIGHEST,
                               preferred_element_type=F32)

    cosf = spread(selc_ref) + one_ref[...]
    sneg = spread(seln_ref)
    spos = spread(selp_ref)

    def rope_store(c0, out_ref, scale):
        for j in range(DIFF_W // LANES):
            blk = proj(c0 + j * LANES, c0 + (j + 1) * LANES)
            rot = (blk * cosf + pltpu.roll(blk, LANES - ROPE_HALF, 1) * sneg
                   + pltpu.roll(blk, ROPE_HALF, 1) * spos)
            out_ref[:, j * LANES:(j + 1) * LANES] = (rot * scale).astype(BF16)

    rope_store(C_DQ, dq_ref, DIFF_DH ** -0.5)
    rope_store(C_DK, dk_ref, 1.0)
    dv_ref[...] = proj(C_DV, C_END).astype(BF16)


def _rope_tables():
    lane = np.arange(LANES)
    r = lane % DIFF_DH
    selc = np.zeros((2 * ROPE_HALF, LANES), np.float32)
    seln = np.zeros_like(selc)
    selp = np.zeros_like(selc)
    one = np.zeros((1, LANES), np.float32)
    for l in range(LANES):
        if r[l] < ROPE_DIM:
            selc[r[l] % ROPE_HALF, l] = 1.0
            if r[l] < ROPE_HALF:
                seln[ROPE_HALF + r[l], l] = -1.0
            else:
                selp[ROPE_HALF + r[l] - ROPE_HALF, l] = 1.0
        else:
            one[0, l] = 1.0
    return jnp.asarray(selc), jnp.asarray(seln), jnp.asarray(selp), jnp.asarray(one)


def _inproj(x2, positions, attn_norm, wgla, wdiff, wa2, b_a, *, tm):
    T, D = x2.shape
    nt = T // tm
    pos3 = positions.reshape(nt, 1, tm)
    inv = (ROPE_THETA ** (-jnp.arange(0, ROPE_DIM, 2, dtype=F32) / ROPE_DIM)).reshape(ROPE_HALF, 1)
    selc, seln, selp, one = _rope_tables()
    row = lambda i: (i, 0)
    fixed = lambda i: (0, 0)
    out_shapes = (
        jax.ShapeDtypeStruct((T, GLA_QK), F32), jax.ShapeDtypeStruct((T, GLA_QK), F32),
        jax.ShapeDtypeStruct((T, GLA_V), BF16), jax.ShapeDtypeStruct((T, GLA_V), BF16),
        jax.ShapeDtypeStruct((T, GLA_QK), F32),
        jax.ShapeDtypeStruct((T, DIFF_W), BF16), jax.ShapeDtypeStruct((T, DIFF_W), BF16),
        jax.ShapeDtypeStruct((T, DIFF_W), BF16),
    )
    return pl.pallas_call(
        _inproj_kernel,
        grid=(nt,),
        in_specs=[
            pl.BlockSpec((tm, D), row),
            pl.BlockSpec((1, 1, tm), lambda i: (i, 0, 0)),
            pl.BlockSpec((1, D), fixed),
            pl.BlockSpec((D, C_GLA_END), fixed),
            pl.BlockSpec((D, C_END), fixed),
            pl.BlockSpec((GLA_LOWRANK, GLA_QK), fixed),
            pl.BlockSpec((1, GLA_QK), fixed),
            pl.BlockSpec((ROPE_HALF, 1), fixed),
            pl.BlockSpec((2 * ROPE_HALF, LANES), fixed),
            pl.BlockSpec((2 * ROPE_HALF, LANES), fixed),
            pl.BlockSpec((2 * ROPE_HALF, LANES), fixed),
            pl.BlockSpec((1, LANES), fixed),
        ],
        out_specs=[pl.BlockSpec((tm, s.shape[1]), row) for s in out_shapes],
        out_shape=out_shapes,
        compiler_params=_cparams(("parallel",)),
        name="inproj",
    )(x2, pos3, attn_norm.reshape(1, D), wgla, wdiff, wa2, b_a.reshape(1, GLA_QK), inv, selc, seln,
      selp, one)


def _gla_kernel(gq_ref, gk_ref, gv_ref, gr_ref, la_ref, gn_ref, oa_ref, st_ref, *, n_chunks):
    @pl.when(pl.program_id(1) == 0)
    def _():
        st_ref[...] = jnp.zeros_like(st_ref)

    C = GLA_CHUNK
    tril = lax.broadcasted_iota(jnp.int32, (C, C), 0) >= lax.broadcasted_iota(jnp.int32, (C, C), 1)
    trilf = tril.astype(F32)
    head_of_lane = lax.broadcasted_iota(jnp.int32, (1, GLA_QK), 1) // GLA_DK
    hmask = [head_of_lane == h for h in range(GLA_HEADS)]
    gn = gn_ref[...]

    for c in range(n_chunks):
        sl = pl.ds(c * C, C)
        q = gq_ref[0, sl, :]
        k = gk_ref[0, sl, :]
        la = la_ref[0, sl, :]
        v = gv_ref[0, sl, :]
        b = jnp.dot(trilf, la, precision=lax.Precision.HIGHEST, preferred_element_type=F32)
        b_last = b[C - 1:C, :]
        q_dec = (q * jnp.exp(b)).astype(BF16)
        k_inv = (k * jnp.exp(-b)).astype(BF16)
        k_end = (k * jnp.exp(b_last - b)).astype(BF16)
        decay = jnp.exp(b_last)

        q_heads = jnp.concatenate([jnp.where(hmask[h], q_dec, jnp.zeros_like(q_dec))
                                   for h in range(GLA_HEADS)], axis=0)
        att = lax.dot_general(q_heads, k_inv, NT_DIMS, preferred_element_type=F32)
        st = st_ref[...]
        inter = lax.dot_general(q_heads, st.astype(BF16), NT_DIMS,
                                preferred_element_type=F32)
        u = jnp.zeros_like(st)
        for h in range(GLA_HEADS):
            rows = slice(h * C, (h + 1) * C)
            cols = slice(h * GLA_DV, (h + 1) * GLA_DV)
            a_h = jnp.where(tril, att[rows], 0.0).astype(BF16)
            v_h = v[:, cols]
            o_h = jnp.dot(a_h, v_h, preferred_element_type=F32) + inter[rows]
            y = _rms(o_h, gn)
            r = gr_ref[0, sl, cols].astype(F32)
            oa_ref[0, sl, cols] = (y * (r * jax.nn.sigmoid(r))).astype(BF16)
            u_h = lax.dot_general(v_h, k_end, TN_DIMS, preferred_element_type=F32)
            u = u + jnp.where(hmask[h], u_h, 0.0)
        st_ref[...] = st * decay + u


def _gla(gq, gk, gv, gr, la, gla_norm, *, B, S, ts):
    n_chunks = ts // GLA_CHUNK
    blk = lambda w: pl.BlockSpec((1, ts, w), lambda b, j: (b, j, 0))
    r3 = lambda a: a.reshape(B, S, a.shape[-1])
    return pl.pallas_call(
        functools.partial(_gla_kernel, n_chunks=n_chunks),
        grid=(B, S // ts),
        in_specs=[blk(GLA_QK), blk(GLA_QK), blk(GLA_V), blk(GLA_V), blk(GLA_QK),
                  pl.BlockSpec((1, GLA_DV), lambda b, j: (0, 0))],
        out_specs=blk(GLA_V),
        out_shape=jax.ShapeDtypeStruct((B, S, GLA_V), BF16),
        scratch_shapes=[pltpu.VMEM((GLA_DV, GLA_QK), F32)],
        compiler_params=_cparams(("parallel", "arbitrary")),
        name="gla",
    )(r3(gq), r3(gk), r3(gv), r3(gr), r3(la), gla_norm.reshape(1, GLA_DV))


def _diffattn_kernel(dq_ref, dk_ref, dv_ref, lq1_ref, lk1_ref, lq2_ref, lk2_ref, sub_ref,
                     ob_ref, m_ref, l_ref, acc_ref, *, tq, tk, lambda_init):
    S = dq_ref.shape[1]
    nq = S // tq
    R = 2 * tq
    lam = (jnp.exp(jnp.sum(lq1_ref[...] * lk1_ref[...], axis=-1, keepdims=True))
           - jnp.exp(jnp.sum(lq2_ref[...] * lk2_ref[...], axis=-1, keepdims=True))
           + lambda_init)
    first_comp = lax.broadcasted_iota(jnp.int32, (1, DIFF_DV), 1) < DIFF_DH
    sub = sub_ref[...]

    for h in range(DIFF_HEADS):
        cols = slice(h * DIFF_DV, (h + 1) * DIFF_DV)
        for qi in range(nq):
            qb = dq_ref[0, qi * tq:(qi + 1) * tq, cols]
            zero = jnp.zeros_like(qb)
            qs = jnp.concatenate([jnp.where(first_comp, qb, zero),
                                  jnp.where(first_comp, zero, qb)], axis=0)
            m_ref[...] = jnp.full(m_ref.shape, NEG_BIG, F32)
            l_ref[...] = jnp.zeros(l_ref.shape, F32)
            acc_ref[...] = jnp.zeros(acc_ref.shape, F32)

            def step(kstart, masked, qs=qs, cols=cols, qi=qi):
                kb = dk_ref[0, pl.ds(kstart, tk), cols]
                vb = dv_ref[0, pl.ds(kstart, tk), cols]
                s = lax.dot_general(kb, qs, NT_DIMS, preferred_element_type=F32)
                if masked:
                    kpos = kstart + lax.broadcasted_iota(jnp.int32, (tk, R), 0)
                    qpos = qi * tq + lax.broadcasted_iota(jnp.int32, (tk, R), 1) % tq
                    s = jnp.where(qpos >= kpos, s, NEG_BIG)
                m_prev = m_ref[...]
                m_new = jnp.maximum(m_prev, jnp.max(s, axis=0, keepdims=True))
                alpha = jnp.exp(m_prev - m_new)
                p = jnp.exp(s - m_new)
                l_ref[...] = alpha * l_ref[...] + jnp.sum(p, axis=0, keepdims=True)
                pv = lax.dot_general(vb, p.astype(BF16), TN_DIMS, preferred_element_type=F32)
                acc_ref[...] = alpha * acc_ref[...] + pv
                m_ref[...] = m_new

            n_full = (qi * tq) // tk

            def body(i, carry, step=step):
                step(pl.multiple_of(i * tk, tk), False)
                return carry

            if n_full > 0:
                lax.fori_loop(0, n_full, body, 0)
            for d in range(tq // tk):
                step(qi * tq + d * tk, True)

            o_all = acc_ref[...] / l_ref[...]
            o = o_all[:, :tq] - lam * o_all[:, tq:]
            y = o * lax.rsqrt(jnp.mean(o * o, axis=0, keepdims=True) + EPS) * sub
            y = y * (1.0 - lambda_init)
            ob_ref[0, qi * tq:(qi + 1) * tq, cols] = y.T.astype(BF16)


def _diffattn(dq, dk, dv, lq1, lk1, lq2, lk2, diff_subln, *, B, S, tq, tk, lambda_init):
    seq = pl.BlockSpec((1, S, DIFF_W), lambda b: (b, 0, 0))
    vec = lambda w: pl.BlockSpec((1, w), lambda b: (0, 0))
    r3 = lambda a: a.reshape(B, S, DIFF_W)
    return pl.pallas_call(
        functools.partial(_diffattn_kernel, tq=tq, tk=tk, lambda_init=lambda_init),
        grid=(B,),
        in_specs=[seq, seq, seq, vec(DIFF_DH), vec(DIFF_DH), vec(DIFF_DH), vec(DIFF_DH),
                  pl.BlockSpec((DIFF_DV, 1), lambda b: (0, 0))],
        out_specs=seq,
        out_shape=jax.ShapeDtypeStruct((B, S, DIFF_W), BF16),
        scratch_shapes=[pltpu.VMEM((1, 2 * tq), F32), pltpu.VMEM((1, 2 * tq), F32),
                        pltpu.VMEM((DIFF_DV, 2 * tq), F32)],
        compiler_params=_cparams(("parallel",)),
        name="diffattn",
    )(r3(dq), r3(dk), r3(dv), lq1.reshape(1, -1), lk1.reshape(1, -1), lq2.reshape(1, -1),
      lk2.reshape(1, -1), diff_subln.reshape(-1, 1))


ROUTE_W = 8


def _first_index_of(mask, lane):
    return jnp.min(jnp.where(mask, lane, LANES), axis=-1, keepdims=True)


def _postmix_kernel(x_ref, oa_ref, ob_ref, g_ref, wg_ref, wba_ref, wbb_ref, wo_ref, fn_ref,
                    wr_ref, br_ref, h_ref, n2_ref, route_ref):
    D = x_ref.shape[1]
    x = x_ref[...]
    nb = _rms(x, g_ref[...]).astype(BF16)
    y_a = jnp.dot(oa_ref[...], wba_ref[...], preferred_element_type=F32)
    g_a = jnp.dot(nb, wg_ref[:, :D], preferred_element_type=F32)
    merged = jax.nn.sigmoid(g_a) * y_a
    y_b = jnp.dot(ob_ref[...], wbb_ref[...], preferred_element_type=F32)
    g_b = jnp.dot(nb, wg_ref[:, D:], preferred_element_type=F32)
    merged = merged + jax.nn.sigmoid(g_b) * y_b
    h = x + jnp.dot(merged.astype(BF16), wo_ref[...], preferred_element_type=F32)
    h_ref[...] = h
    n2 = _rms(h, fn_ref[...])
    _store_slab(n2_ref, n2)

    lg = jnp.dot(n2, wr_ref[...], precision=lax.Precision.HIGHEST,
                 preferred_element_type=F32) + br_ref[...]
    lane = lax.broadcasted_iota(jnp.int32, lg.shape, 1)
    is_g = lane < N_GROUPS
    g_max = jnp.max(jnp.where(is_g, lg, -jnp.inf), axis=-1, keepdims=True)
    g_exp = jnp.where(is_g, jnp.exp(lg - g_max), 0.0)
    g_prob = g_exp / jnp.sum(g_exp, axis=-1, keepdims=True)
    g_p = jnp.max(g_prob, axis=-1, keepdims=True)
    g_idx = _first_index_of(is_g & (g_prob == g_p), lane)

    e_lo = N_GROUPS + EXPERTS_PER_GROUP * g_idx
    is_e = (lane >= e_lo) & (lane < e_lo + EXPERTS_PER_GROUP)
    e_max = jnp.max(jnp.where(is_e, lg, -jnp.inf), axis=-1, keepdims=True)
    e_exp = jnp.where(is_e, jnp.exp(lg - e_max), 0.0)
    e_prob = e_exp / jnp.sum(e_exp, axis=-1, keepdims=True)
    p1 = jnp.max(jnp.where(is_e, e_prob, -1.0), axis=-1, keepdims=True)
    i1 = _first_index_of(is_e & (e_prob == p1), lane)
    rest = is_e & (lane != i1)
    p2 = jnp.max(jnp.where(rest, e_prob, -1.0), axis=-1, keepdims=True)
    i2 = _first_index_of(rest & (e_prob == p2), lane)
    den = p1 + p2
    w1 = g_p * (p1 / den)
    w2 = g_p * (p2 / den)
    rec = jnp.where(lane == 0, w1, 0.0)
    rec = jnp.where(lane == 1, w2, rec)
    rec = jnp.where(lane == 2, (i1 - N_GROUPS).astype(F32), rec)
    rec = jnp.where(lane == 3, (i2 - N_GROUPS).astype(F32), rec)
    route_ref[...] = rec[:, :ROUTE_W]


def _postmix(x2, oa, ob, attn_norm, wg, wba, wbb, wo, ffn_norm, wr, br, *, tm):
    T, D = x2.shape
    row = lambda i: (i, 0)
    fixed = lambda i: (0, 0)
    full = lambda a: pl.BlockSpec(a.shape, fixed)
    g = attn_norm.reshape(1, D)
    fn = ffn_norm.reshape(1, D)
    return pl.pallas_call(
        _postmix_kernel,
        grid=(T // tm,),
        in_specs=[pl.BlockSpec((tm, D), row), pl.BlockSpec((tm, GLA_V), row),
                  pl.BlockSpec((tm, DIFF_W), row), full(g), full(wg), full(wba), full(wbb),
                  full(wo), full(fn), full(wr), full(br)],
        out_specs=[pl.BlockSpec((tm, D), row), pl.BlockSpec((tm * SUBLANES, LANES), row),
                   pl.BlockSpec((tm, ROUTE_W), row)],
        out_shape=(jax.ShapeDtypeStruct((T, D), F32),
                   jax.ShapeDtypeStruct((T * SUBLANES, LANES), F32),
                   jax.ShapeDtypeStruct((T, ROUTE_W), F32)),
        compiler_params=_cparams(("parallel",)),
        name="postmix",
    )(x2, oa, ob, g, wg, wba, wbb, wo, fn, wr, br)


def _experts_kernel(tile_e_ref, n_tiles_ref, idx_ref, idx_next_ref, x_hbm, wg_ref, wu_ref, wd_ref,
                    y_ref, xbuf, sem):
    tm = xbuf.shape[1] // SUBLANES
    i = pl.program_id(0)
    n = n_tiles_ref[0]
    slot = i % 2

    @pl.when(i == 0)
    def _():
        _start_token_gather(idx_ref, tm, x_hbm, xbuf.at[0], sem.at[0])

    @pl.when(i + 1 < n)
    def _():
        _start_token_gather(idx_next_ref, tm, x_hbm, xbuf.at[1 - slot], sem.at[1 - slot])

    @pl.when(i < n)
    def _():
        _wait_token_gather(tm, x_hbm, xbuf.at[slot], sem.at[slot])
        x = _load_slab(xbuf.at[slot], tm).astype(BF16)
        g = jnp.dot(x, wg_ref[0], preferred_element_type=F32)
        u = jnp.dot(x, wu_ref[0], preferred_element_type=F32)
        he = (g * jax.nn.sigmoid(g)) * u
        _store_slab(y_ref, jnp.dot(he.astype(BF16), wd_ref[0], preferred_element_type=F32))

    @pl.when(i >= n)
    def _():
        y_ref[...] = jnp.zeros_like(y_ref)


def _experts(tile_e, n_tiles, row_token, n2p, wg, wu, wd, *, tm):
    D = wg.shape[1]
    n_tiles_max = row_token.shape[0] // tm
    idx = row_token.reshape(n_tiles_max, 1, tm)
    last = n_tiles_max - 1
    smem_rows = lambda f: pl.BlockSpec((1, 1, tm), f, memory_space=pltpu.SMEM)
    grid_spec = pltpu.PrefetchScalarGridSpec(
        num_scalar_prefetch=2,
        grid=(n_tiles_max,),
        in_specs=[smem_rows(lambda i, te, nt: (i, 0, 0)),
                  smem_rows(lambda i, te, nt: (jnp.minimum(i + 1, last), 0, 0)),
                  pl.BlockSpec(memory_space=pl.ANY),
                  pl.BlockSpec((1, D, D_EXPERT), lambda i, te, nt: (te[i], 0, 0)),
                  pl.BlockSpec((1, D, D_EXPERT), lambda i, te, nt: (te[i], 0, 0)),
                  pl.BlockSpec((1, D_EXPERT, D), lambda i, te, nt: (te[i], 0, 0))],
        out_specs=pl.BlockSpec((tm * SUBLANES, LANES), lambda i, te, nt: (i, 0)),
        scratch_shapes=[pltpu.VMEM((2, tm * SUBLANES, LANES), F32), pltpu.SemaphoreType.DMA((2,))],
    )
    return pl.pallas_call(
        _experts_kernel,
        grid_spec=grid_spec,
        out_shape=jax.ShapeDtypeStruct((n_tiles_max * tm * SUBLANES, LANES), F32),
        compiler_params=_cparams(("arbitrary",)),
        name="experts",
    )(tile_e, n_tiles, idx, idx, n2p, wg, wu, wd)


def _final_kernel(idx_ref, idx_next_ref, h_ref, ys_hbm, route_ref, p_ref, wple_ref, pn_ref, wpg_ref,
                  fn_ref, o_ref, ybuf, sem, *, apply_final_norm):
    tm = h_ref.shape[0]
    i = pl.program_id(0)
    slot = i % 2

    @pl.when(i == 0)
    def _():
        _start_token_gather(idx_ref, 2 * tm, ys_hbm, ybuf.at[0], sem.at[0])

    @pl.when(i + 1 < pl.num_programs(0))
    def _():
        _start_token_gather(idx_next_ref, 2 * tm, ys_hbm, ybuf.at[1 - slot], sem.at[1 - slot])

    _wait_token_gather(2 * tm, ys_hbm, ybuf.at[slot], sem.at[slot])
    route = route_ref[...]
    y = route[:, 0:1] * _load_slab(ybuf.at[slot, pl.ds(0, tm * SUBLANES), :], tm)
    y = y + route[:, 1:2] * _load_slab(ybuf.at[slot, pl.ds(tm * SUBLANES, tm * SUBLANES), :], tm)
    h = h_ref[...] + y
    e = _rms(jnp.dot(p_ref[...].astype(BF16), wple_ref[...], preferred_element_type=F32), pn_ref[...])
    gate = jax.nn.sigmoid(jnp.dot(h.astype(BF16), wpg_ref[...], preferred_element_type=F32))
    h = h + gate * e
    if apply_final_norm:
        h = _rms(h, fn_ref[...])
    o_ref[...] = h


def _final(h1, ys, pos, route, p2, wple, ple_norm, wpg, final_norm, *, tm, apply_final_norm):
    T, D = h1.shape
    nt = T // tm
    row = lambda i: (i, 0)
    fixed = lambda i: (0, 0)
    full = lambda a: pl.BlockSpec(a.shape, fixed)
    pn = ple_norm.reshape(1, D)
    fn = final_norm.reshape(1, D)
    idx = pos.reshape(nt, tm, 2).transpose(0, 2, 1).reshape(nt, 1, 2 * tm)
    smem_rows = lambda f: pl.BlockSpec((1, 1, 2 * tm), f, memory_space=pltpu.SMEM)
    return pl.pallas_call(
        functools.partial(_final_kernel, apply_final_norm=apply_final_norm),
        grid=(nt,),
        in_specs=[smem_rows(lambda i: (i, 0, 0)),
                  smem_rows(lambda i: (jnp.minimum(i + 1, nt - 1), 0, 0)),
                  pl.BlockSpec((tm, D), row), pl.BlockSpec(memory_space=pl.ANY),
                  pl.BlockSpec((tm, ROUTE_W), row), pl.BlockSpec((tm, p2.shape[1]), row),
                  full(wple), full(pn), full(wpg), full(fn)],
        out_specs=pl.BlockSpec((tm, D), row),
        out_shape=jax.ShapeDtypeStruct((T, D), F32),
        scratch_shapes=[pltpu.VMEM((2, 2 * tm * SUBLANES, LANES), F32),
                        pltpu.SemaphoreType.DMA((2,))],
        compiler_params=_cparams(("arbitrary",)),
        name="final",
    )(idx, idx, h1, ys, route, p2, wple, pn, wpg, fn)


def _routing_tables(eid, *, tm):
    T = eid.shape[0]
    A = 2 * T
    n_tiles_max = A // tm + N_EXPERTS
    flat_e = eid.reshape(A)
    experts = jnp.arange(N_EXPERTS, dtype=jnp.int32)
    onehot = flat_e[:, None] == experts[None, :]
    order = jnp.argsort(flat_e, stable=True).astype(jnp.int32)
    rank_sorted = jnp.argsort(order).astype(jnp.int32)
    counts = jnp.sum(onehot, axis=0, dtype=jnp.int32)
    padded = (counts + tm - 1) // tm * tm
    p_end = jnp.cumsum(padded)
    p_start = p_end - padded
    c_start = jnp.cumsum(counts) - counts
    n_tiles = (p_end[-1] // tm).astype(jnp.int32).reshape(1)
    tile_start = jnp.arange(n_tiles_max, dtype=jnp.int32) * tm
    tile_e = jnp.sum(tile_start[:, None] >= p_end[None, :], axis=1, dtype=jnp.int32)
    tile_e = jnp.minimum(tile_e, jnp.sum(tile_start[n_tiles[0] - 1] >= p_end, dtype=jnp.int32))
    shift = p_start - c_start
    pos = (rank_sorted + jnp.sum(jnp.where(onehot, shift[None, :], 0), axis=1)).reshape(T, 2)
    te = tile_e[:, None] == experts[None, :]
    t_shift = jnp.sum(jnp.where(te, shift[None, :], 0), axis=1)
    t_last = jnp.sum(jnp.where(te, (c_start + counts - 1)[None, :], 0), axis=1)
    row = tile_start[:, None] + jnp.arange(tm, dtype=jnp.int32)[None, :]
    src = jnp.clip(jnp.minimum(row - t_shift[:, None], t_last[:, None]), 0, A - 1)
    row_token = order[src.reshape(-1)] // 2
    return row_token, pos, tile_e, n_tiles


SPLIT_SIZES = (GLA_QK, GLA_QK, GLA_V, GLA_V, GLA_LOWRANK, DIFF_W, DIFF_W, DIFF_W)


def _layer(h2, p2, positions, B, S, lambda_init, apply_final_norm, attn_norm, w_in, w_a2, b_a,
           gla_norm, lq1, lk1, lq2, lk2, diff_subln, w_branch_a, w_branch_b, w_out, ffn_norm,
           w_rg, b_rg, w_re, b_re, w_gate, w_up, w_down, w_ple, ple_norm, w_ple_gate, final_norm,
           *, tm, ts, tq, tk, te):
    T, D = h2.shape
    assert D == SUBLANES * LANES, "token-slab gathers need one (8, 128) tile per token"
    n_mix = sum(SPLIT_SIZES)
    wgla = w_in[:, :C_GLA_END].astype(BF16)
    wdiff = w_in[:, C_GLA_END:n_mix].astype(BF16)
    wg = w_in[:, n_mix:].astype(BF16)

    gq, gk, gv, gr, la, dq, dk, dv = _inproj(h2, positions, attn_norm, wgla, wdiff,
                                             w_a2.astype(BF16), b_a, tm=tm)
    oa = _gla(gq, gk, gv, gr, la, gla_norm, B=B, S=S, ts=ts).reshape(T, GLA_V)
    ob = _diffattn(dq, dk, dv, lq1, lk1, lq2, lk2, diff_subln, B=B, S=S, tq=tq, tk=tk,
                   lambda_init=lambda_init).reshape(T, DIFF_W)

    wr = jnp.concatenate([w_rg, w_re, jnp.zeros((D, LANES - N_GROUPS - N_EXPERTS), F32)], axis=1)
    br = jnp.concatenate([b_rg, b_re, jnp.zeros((LANES - N_GROUPS - N_EXPERTS,), F32)]).reshape(1, LANES)
    h1, n2, route = _postmix(h2, oa, ob, attn_norm, wg, w_branch_a.astype(BF16),
                             w_branch_b.astype(BF16), w_out.astype(BF16), ffn_norm, wr, br, tm=tm)

    eid = route[:, 2:4].astype(jnp.int32)
    row_token, pos, tile_e, n_tiles = _routing_tables(eid, tm=te)
    ys = _experts(tile_e, n_tiles, row_token, n2, w_gate.astype(BF16), w_up.astype(BF16),
                  w_down.astype(BF16), tm=te)
    return _final(h1, ys, pos, route, p2, w_ple.astype(BF16), ple_norm, w_ple_gate.astype(BF16),
                  final_norm, tm=tm, apply_final_norm=apply_final_norm)


def _block(x, p, positions, attn_norm, w_in, w_a2, b_a, gla_norm, lambda_q1, lambda_k1, lambda_q2,
           lambda_k2, diff_subln, w_branch_a, w_branch_b, w_out, ffn_norm, w_router_group,
           b_router_group, w_router_expert, b_router_expert, w_gate, w_up, w_down, w_ple, ple_norm,
           w_ple_gate, final_norm, *, tm, ts, tq, tk, te):
    B, S, D = x.shape
    depth = w_in.shape[0]
    h = x.reshape(B * S, D)
    for i in range(depth):
        lambda_init = 0.8 - 0.6 * math.exp(-0.3 * i)
        h = _layer(h, p[i].reshape(B * S, -1), positions, B, S, lambda_init, i == depth - 1,
                   attn_norm[i], w_in[i], w_a2[i], b_a[i], gla_norm[i], lambda_q1[i], lambda_k1[i],
                   lambda_q2[i], lambda_k2[i], diff_subln[i], w_branch_a[i], w_branch_b[i], w_out[i],
                   ffn_norm[i], w_router_group[i], b_router_group[i], w_router_expert[i],
                   b_router_expert[i], w_gate[i], w_up[i], w_down[i], w_ple[i], ple_norm[i],
                   w_ple_gate[i], final_norm, tm=tm, ts=ts, tq=tq, tk=tk, te=te)
    return h.reshape(B, S, D)


def kernel(x, p, positions, attn_norm, w_in, w_a2, b_a, gla_norm, lambda_q1, lambda_k1, lambda_q2, lambda_k2, diff_subln, w_branch_a, w_branch_b, w_out, ffn_norm, w_router_group, b_router_group, w_router_expert, b_router_expert, w_gate, w_up, w_down, w_ple, ple_norm, w_ple_gate, final_norm):
    S = x.shape[1]
    return _block(x, p, positions, attn_norm, w_in, w_a2, b_a, gla_norm, lambda_q1, lambda_k1,
                  lambda_q2, lambda_k2, diff_subln, w_branch_a, w_branch_b, w_out, ffn_norm,
                  w_router_group, b_router_group, w_router_expert, b_router_expert, w_gate, w_up,
                  w_down, w_ple, ple_norm, w_ple_gate, final_norm,
                  tm=512, ts=min(512, S), tq=min(512, S), tk=min(512, S), te=512)
```

```python
import functools
import math

import jax
import jax.numpy as jnp
import numpy as np
from jax import lax
from jax.experimental import pallas as pl
from jax.experimental.pallas import tpu as pltpu

EPS = 1e-6

GLA_HEADS = 4
GLA_DK = 64
GLA_DV = 128
GLA_LOWRANK = 16
GLA_TAU = 16.0
GLA_CHUNK = 64
GLA_QK = GLA_HEADS * GLA_DK
GLA_V = GLA_HEADS * GLA_DV

DIFF_HEADS = 4
DIFF_DH = 64
DIFF_DV = 2 * DIFF_DH
DIFF_W = DIFF_HEADS * DIFF_DV
ROPE_THETA = 500000.0
ROPE_DIM = DIFF_DH // 4
ROPE_HALF = ROPE_DIM // 2

N_GROUPS = 4
EXPERTS_PER_GROUP = 8
N_EXPERTS = N_GROUPS * EXPERTS_PER_GROUP
D_EXPERT = 256

LANES = 128
VMEM_LIMIT = 56 * 1024 * 1024

BF16 = jnp.bfloat16
F32 = jnp.float32
NT_DIMS = (((1,), (1,)), ((), ()))
TN_DIMS = (((0,), (0,)), ((), ()))
NEG_BIG = -1e30


def _rms(x, g):
    return x * lax.rsqrt(jnp.mean(x * x, axis=-1, keepdims=True) + EPS) * g


def _cparams(semantics):
    return pltpu.CompilerParams(dimension_semantics=semantics, vmem_limit_bytes=VMEM_LIMIT)


SUBLANES = 8


def _store_slab(ref, x):
    n = x.shape[0]
    for j in range(SUBLANES):
        ref[pl.ds(j, n, stride=SUBLANES), :] = x[:, j * LANES:(j + 1) * LANES]


def _load_slab(ref, n):
    return jnp.concatenate([ref[pl.ds(j, n, stride=SUBLANES), :] for j in range(SUBLANES)], axis=1)


def _start_token_gather(idx_ref, n_tokens, src_hbm, dst, sem):
    def body(r, carry):
        src_row = pl.multiple_of(idx_ref[0, 0, r] * SUBLANES, SUBLANES)
        dst_row = pl.multiple_of(r * SUBLANES, SUBLANES)
        pltpu.make_async_copy(src_hbm.at[pl.ds(src_row, SUBLANES), :],
                              dst.at[pl.ds(dst_row, SUBLANES), :], sem).start(priority=1)
        return carry
    lax.fori_loop(0, n_tokens, body, 0, unroll=8)


def _wait_token_gather(n_tokens, src_hbm, dst, sem):
    pltpu.make_async_copy(src_hbm.at[pl.ds(0, n_tokens * SUBLANES), :], dst, sem).wait()


C_GQ = 0
C_GK = C_GQ + GLA_QK
C_GV = C_GK + GLA_QK
C_GR = C_GV + GLA_V
C_AL = C_GR + GLA_V
C_GLA_END = C_AL + GLA_LOWRANK
C_DQ = 0
C_DK = C_DQ + DIFF_W
C_DV = C_DK + DIFF_W
C_END = C_DV + DIFF_W


def _inproj_kernel(x_ref, pos_ref, g_ref, wgla_ref, wdiff_ref, wa2_ref, ba_ref, inv_ref, selc_ref,
                   seln_ref, selp_ref, one_ref,
                   gq_ref, gk_ref, gv_ref, gr_ref, la_ref, dq_ref, dk_ref, dv_ref):
    nb = _rms(x_ref[...], g_ref[...]).astype(BF16)

    def proj_gla(c0, c1):
        return jnp.dot(nb, wgla_ref[:, c0:c1], preferred_element_type=F32)

    def proj(c0, c1):
        return jnp.dot(nb, wdiff_ref[:, c0:c1], preferred_element_type=F32)

    gq_ref[...] = proj_gla(C_GQ, C_GK) * (GLA_DK ** -0.5)
    gk_ref[...] = proj_gla(C_GK, C_GV)
    gv_ref[...] = proj_gla(C_GV, C_GR).astype(BF16)
    gr_ref[...] = proj_gla(C_GR, C_AL).astype(BF16)

    gal = proj_gla(C_AL, C_GLA_END).astype(BF16)
    a_logit = jnp.dot(gal, wa2_ref[...], preferred_element_type=F32) + ba_ref[...]
    la_ref[...] = (jnp.minimum(a_logit, 0.0) - jnp.log1p(jnp.exp(-jnp.abs(a_logit)))) / GLA_TAU

    ang = inv_ref[...] * pos_ref[0].astype(F32)
    cs = jnp.concatenate([jnp.cos(ang), jnp.sin(ang)], axis=0)

    def spread(sel_ref):
        return lax.dot_general(cs, sel_ref[...], TN_DIMS, precision=lax.Precision.HIGHEST,
                               preferred_element_type=F32)

    cosf = spread(selc_ref) + one_ref[...]
    sneg = spread(seln_ref)
    spos = spread(selp_ref)

    def rope_store(c0, out_ref, scale):
        for j in range(DIFF_W // LANES):
            blk = proj(c0 + j * LANES, c0 + (j + 1) * LANES)
            rot = (blk * cosf + pltpu.roll(blk, LANES - ROPE_HALF, 1) * sneg
                   + pltpu.roll(blk, ROPE_HALF, 1) * spos)
            out_ref[:, j * LANES:(j + 1) * LANES] = (rot * scale).astype(BF16)

    rope_store(C_DQ, dq_ref, DIFF_DH ** -0.5)
    rope_store(C_DK, dk_ref, 1.0)
    dv_ref[...] = proj(C_DV, C_END).astype(BF16)


def _rope_tables():
    lane = np.arange(LANES)
    r = lane % DIFF_DH
    selc = np.zeros((2 * ROPE_HALF, LANES), np.float32)
    seln = np.zeros_like(selc)
    selp = np.zeros_like(selc)
    one = np.zeros((1, LANES), np.float32)
    for l in range(LANES):
        if r[l] < ROPE_DIM:
            selc[r[l] % ROPE_HALF, l] = 1.0
            if r[l] < ROPE_HALF:
                seln[ROPE_HALF + r[l], l] = -1.0
            else:
                selp[ROPE_HALF + r[l] - ROPE_HALF, l] = 1.0
        else:
            one[0, l] = 1.0
    return jnp.asarray(selc), jnp.asarray(seln), jnp.asarray(selp), jnp.asarray(one)


def _inproj(x2, positions, attn_norm, wgla, wdiff, wa2, b_a, *, tm):
    T, D = x2.shape
    nt = T // tm
    pos3 = positions.reshape(nt, 1, tm)
    inv = (ROPE_THETA ** (-jnp.arange(0, ROPE_DIM, 2, dtype=F32) / ROPE_DIM)).reshape(ROPE_HALF, 1)
    selc, seln, selp, one = _rope_tables()
    row = lambda i: (i, 0)
    fixed = lambda i: (0, 0)
    out_shapes = (
        jax.ShapeDtypeStruct((T, GLA_QK), F32), jax.ShapeDtypeStruct((T, GLA_QK), F32),
        jax.ShapeDtypeStruct((T, GLA_V), BF16), jax.ShapeDtypeStruct((T, GLA_V), BF16),
        jax.ShapeDtypeStruct((T, GLA_QK), F32),
        jax.ShapeDtypeStruct((T, DIFF_W), BF16), jax.ShapeDtypeStruct((T, DIFF_W), BF16),
        jax.ShapeDtypeStruct((T, DIFF_W), BF16),
    )
    return pl.pallas_call(
        _inproj_kernel,
        grid=(nt,),
        in_specs=[
            pl.BlockSpec((tm, D), row),
            pl.BlockSpec((1, 1, tm), lambda i: (i, 0, 0)),
            pl.BlockSpec((1, D), fixed),
            pl.BlockSpec((D, C_GLA_END), fixed),
            pl.BlockSpec((D, C_END), fixed),
            pl.BlockSpec((GLA_LOWRANK, GLA_QK), fixed),
            pl.BlockSpec((1, GLA_QK), fixed),
            pl.BlockSpec((ROPE_HALF, 1), fixed),
            pl.BlockSpec((2 * ROPE_HALF, LANES), fixed),
            pl.BlockSpec((2 * ROPE_HALF, LANES), fixed),
            pl.BlockSpec((2 * ROPE_HALF, LANES), fixed),
            pl.BlockSpec((1, LANES), fixed),
        ],
        out_specs=[pl.BlockSpec((tm, s.shape[1]), row) for s in out_shapes],
        out_shape=out_shapes,
        compiler_params=_cparams(("parallel",)),
        name="inproj",
    )(x2, pos3, attn_norm.reshape(1, D), wgla, wdiff, wa2, b_a.reshape(1, GLA_QK), inv, selc, seln,
      selp, one)


def _gla_kernel(gq_ref, gk_ref, gv_ref, gr_ref, la_ref, gn_ref, oa_ref, st_ref, *, n_chunks):
    @pl.when(pl.program_id(1) == 0)
    def _():
        st_ref[...] = jnp.zeros_like(st_ref)

    C = GLA_CHUNK
    tril = lax.broadcasted_iota(jnp.int32, (C, C), 0) >= lax.broadcasted_iota(jnp.int32, (C, C), 1)
    trilf = tril.astype(F32)
    head_of_lane = lax.broadcasted_iota(jnp.int32, (1, GLA_QK), 1) // GLA_DK
    hmask = [head_of_lane == h for h in range(GLA_HEADS)]
    gn = gn_ref[...]

    for c in range(n_chunks):
        sl = pl.ds(c * C, C)
        q = gq_ref[0, sl, :]
        k = gk_ref[0, sl, :]
        la = la_ref[0, sl, :]
        v = gv_ref[0, sl, :]
        b = jnp.dot(trilf, la, precision=lax.Precision.HIGHEST, preferred_element_type=F32)
        b_last = b[C - 1:C, :]
        q_dec = (q * jnp.exp(b)).astype(BF16)
        k_inv = (k * jnp.exp(-b)).astype(BF16)
        k_end = (k * jnp.exp(b_last - b)).astype(BF16)
        decay = jnp.exp(b_last)

        q_heads = jnp.concatenate([jnp.where(hmask[h], q_dec, jnp.zeros_like(q_dec))
                                   for h in range(GLA_HEADS)], axis=0)
        att = lax.dot_general(q_heads, k_inv, NT_DIMS, preferred_element_type=F32)
        st = st_ref[...]
        inter = lax.dot_general(q_heads, st.astype(BF16), NT_DIMS,
                                preferred_element_type=F32)
        u = jnp.zeros_like(st)
        for h in range(GLA_HEADS):
            rows = slice(h * C, (h + 1) * C)
            cols = slice(h * GLA_DV, (h + 1) * GLA_DV)
            a_h = jnp.where(tril, att[rows], 0.0).astype(BF16)
            v_h = v[:, cols]
            o_h = jnp.dot(a_h, v_h, preferred_element_type=F32) + inter[rows]
            y = _rms(o_h, gn)
            r = gr_ref[0, sl, cols].astype(F32)
            oa_ref[0, sl, cols] = (y * (r * jax.nn.sigmoid(r))).astype(BF16)
            u_h = lax.dot_general(v_h, k_end, TN_DIMS, preferred_element_type=F32)
            u = u + jnp.where(hmask[h], u_h, 0.0)
        st_ref[...] = st * decay + u


def _gla(gq, gk, gv, gr, la, gla_norm, *, B, S, ts):
    n_chunks = ts // GLA_CHUNK
    blk = lambda w: pl.BlockSpec((1, ts, w), lambda b, j: (b, j, 0))
    r3 = lambda a: a.reshape(B, S, a.shape[-1])
    return pl.pallas_call(
        functools.partial(_gla_kernel, n_chunks=n_chunks),
        grid=(B, S // ts),
        in_specs=[blk(GLA_QK), blk(GLA_QK), blk(GLA_V), blk(GLA_V), blk(GLA_QK),
                  pl.BlockSpec((1, GLA_DV), lambda b, j: (0, 0))],
        out_specs=blk(GLA_V),
        out_shape=jax.ShapeDtypeStruct((B, S, GLA_V), BF16),
        scratch_shapes=[pltpu.VMEM((GLA_DV, GLA_QK), F32)],
        compiler_params=_cparams(("parallel", "arbitrary")),
        name="gla",
    )(r3(gq), r3(gk), r3(gv), r3(gr), r3(la), gla_norm.reshape(1, GLA_DV))


def _diffattn_kernel(dq_ref, dk_ref, dv_ref, lq1_ref, lk1_ref, lq2_ref, lk2_ref, sub_ref,
                     ob_ref, m_ref, l_ref, acc_ref, *, tq, tk, lambda_init):
    S = dq_ref.shape[1]
    nq = S // tq
    R = 2 * tq
    lam = (jnp.exp(jnp.sum(lq1_ref[...] * lk1_ref[...], axis=-1, keepdims=True))
           - jnp.exp(jnp.sum(lq2_ref[...] * lk2_ref[...], axis=-1, keepdims=True))
           + lambda_init)
    first_comp = lax.broadcasted_iota(jnp.int32, (1, DIFF_DV), 1) < DIFF_DH
    sub = sub_ref[...]

    for h in range(DIFF_HEADS):
        cols = slice(h * DIFF_DV, (h + 1) * DIFF_DV)
        for qi in range(nq):
            qb = dq_ref[0, qi * tq:(qi + 1) * tq, cols]
            zero = jnp.zeros_like(qb)
            qs = jnp.concatenate([jnp.where(first_comp, qb, zero),
                                  jnp.where(first_comp, zero, qb)], axis=0)
            m_ref[...] = jnp.full(m_ref.shape, NEG_BIG, F32)
            l_ref[...] = jnp.zeros(l_ref.shape, F32)
            acc_ref[...] = jnp.zeros(acc_ref.shape, F32)

            def step(kstart, masked, qs=qs, cols=cols, qi=qi):
                kb = dk_ref[0, pl.ds(kstart, tk), cols]
                vb = dv_ref[0, pl.ds(kstart, tk), cols]
                s = lax.dot_general(kb, qs, NT_DIMS, preferred_element_type=F32)
                if masked:
                    kpos = kstart + lax.broadcasted_iota(jnp.int32, (tk, R), 0)
                    qpos = qi * tq + lax.broadcasted_iota(jnp.int32, (tk, R), 1) % tq
                    s = jnp.where(qpos >= kpos, s, NEG_BIG)
                m_prev = m_ref[...]
                m_new = jnp.maximum(m_prev, jnp.max(s, axis=0, keepdims=True))
                alpha = jnp.exp(m_prev - m_new)
                p = jnp.exp(s - m_new)
                l_ref[...] = alpha * l_ref[...] + jnp.sum(p, axis=0, keepdims=True)
                pv = lax.dot_general(vb, p.astype(BF16), TN_DIMS, preferred_element_type=F32)
                acc_ref[...] = alpha * acc_ref[...] + pv
                m_ref[...] = m_new

            n_full = (qi * tq) // tk

            def body(i, carry, step=step):
                step(pl.multiple_of(i * tk, tk), False)
                return carry

            if n_full > 0:
                lax.fori_loop(0, n_full, body, 0)
            for d in range(tq // tk):
                step(qi * tq + d * tk, True)

            o_all = acc_ref[...] / l_ref[...]
            o = o_all[:, :tq] - lam * o_all[:, tq:]
            y = o * lax.rsqrt(jnp.mean(o * o, axis=0, keepdims=True) + EPS) * sub
            y = y * (1.0 - lambda_init)
            ob_ref[0, qi * tq:(qi + 1) * tq, cols] = y.T.astype(BF16)


def _diffattn(dq, dk, dv, lq1, lk1, lq2, lk2, diff_subln, *, B, S, tq, tk, lambda_init):
    seq = pl.BlockSpec((1, S, DIFF_W), lambda b: (b, 0, 0))
    vec = lambda w: pl.BlockSpec((1, w), lambda b: (0, 0))
    r3 = lambda a: a.reshape(B, S, DIFF_W)
    return pl.pallas_call(
        functools.partial(_diffattn_kernel, tq=tq, tk=tk, lambda_init=lambda_init),
        grid=(B,),
        in_specs=[seq, seq, seq, vec(DIFF_DH), vec(DIFF_DH), vec(DIFF_DH), vec(DIFF_DH),
                  pl.BlockSpec((DIFF_DV, 1), lambda b: (0, 0))],
        out_specs=seq,
        out_shape=jax.ShapeDtypeStruct((B, S, DIFF_W), BF16),
        scratch_shapes=[pltpu.VMEM((1, 2 * tq), F32), pltpu.VMEM((1, 2 * tq), F32),
                        pltpu.VMEM((DIFF_DV, 2 * tq), F32)],
        compiler_params=_cparams(("parallel",)),
        name="diffattn",
    )(r3(dq), r3(dk), r3(dv), lq1.reshape(1, -1), lk1.reshape(1, -1), lq2.reshape(1, -1),
      lk2.reshape(1, -1), diff_subln.reshape(-1, 1))


ROUTE_W = 8


def _first_index_of(mask, lane):
    return jnp.min(jnp.where(mask, lane, LANES), axis=-1, keepdims=True)


def _postmix_kernel(x_ref, oa_ref, ob_ref, g_ref, wg_ref, wba_ref, wbb_ref, wo_ref, fn_ref,
                    wr_ref, br_ref, h_ref, n2_ref, route_ref, cnt_ref):
    D = x_ref.shape[1]
    x = x_ref[...]
    nb = _rms(x, g_ref[...]).astype(BF16)
    y_a = jnp.dot(oa_ref[...], wba_ref[...], preferred_element_type=F32)
    g_a = jnp.dot(nb, wg_ref[:, :D], preferred_element_type=F32)
    merged = jax.nn.sigmoid(g_a) * y_a
    y_b = jnp.dot(ob_ref[...], wbb_ref[...], preferred_element_type=F32)
    g_b = jnp.dot(nb, wg_ref[:, D:], preferred_element_type=F32)
    merged = merged + jax.nn.sigmoid(g_b) * y_b
    h = x + jnp.dot(merged.astype(BF16), wo_ref[...], preferred_element_type=F32)
    h_ref[...] = h
    n2 = _rms(h, fn_ref[...])
    n2_ref[...] = n2.astype(BF16)

    lg = jnp.dot(n2, wr_ref[...], precision=lax.Precision.HIGHEST,
                 preferred_element_type=F32) + br_ref[...]
    lane = lax.broadcasted_iota(jnp.int32, lg.shape, 1)
    is_g = lane < N_GROUPS
    g_max = jnp.max(jnp.where(is_g, lg, -jnp.inf), axis=-1, keepdims=True)
    g_exp = jnp.where(is_g, jnp.exp(lg - g_max), 0.0)
    g_prob = g_exp / jnp.sum(g_exp, axis=-1, keepdims=True)
    g_p = jnp.max(g_prob, axis=-1, keepdims=True)
    g_idx = _first_index_of(is_g & (g_prob == g_p), lane)

    e_lo = N_GROUPS + EXPERTS_PER_GROUP * g_idx
    is_e = (lane >= e_lo) & (lane < e_lo + EXPERTS_PER_GROUP)
    e_max = jnp.max(jnp.where(is_e, lg, -jnp.inf), axis=-1, keepdims=True)
    e_exp = jnp.where(is_e, jnp.exp(lg - e_max), 0.0)
    e_prob = e_exp / jnp.sum(e_exp, axis=-1, keepdims=True)
    p1 = jnp.max(jnp.where(is_e, e_prob, -1.0), axis=-1, keepdims=True)
    i1 = _first_index_of(is_e & (e_prob == p1), lane)
    rest = is_e & (lane != i1)
    p2 = jnp.max(jnp.where(rest, e_prob, -1.0), axis=-1, keepdims=True)
    i2 = _first_index_of(rest & (e_prob == p2), lane)
    den = p1 + p2
    w1 = g_p * (p1 / den)
    w2 = g_p * (p2 / den)
    rec = jnp.where(lane == 0, w1, 0.0)
    rec = jnp.where(lane == 1, w2, rec)
    rec = jnp.where(lane == 2, (i1 - N_GROUPS).astype(F32), rec)
    rec = jnp.where(lane == 3, (i2 - N_GROUPS).astype(F32), rec)
    route_ref[...] = rec[:, :ROUTE_W]
    chosen = (lane == i1 - N_GROUPS) | (lane == i2 - N_GROUPS)
    cnt_ref[0] = jnp.sum(chosen.astype(F32), axis=0, keepdims=True)


def _postmix(x2, oa, ob, attn_norm, wg, wba, wbb, wo, ffn_norm, wr, br, *, tm):
    T, D = x2.shape
    row = lambda i: (i, 0)
    fixed = lambda i: (0, 0)
    full = lambda a: pl.BlockSpec(a.shape, fixed)
    g = attn_norm.reshape(1, D)
    fn = ffn_norm.reshape(1, D)
    return pl.pallas_call(
        _postmix_kernel,
        grid=(T // tm,),
        in_specs=[pl.BlockSpec((tm, D), row), pl.BlockSpec((tm, GLA_V), row),
                  pl.BlockSpec((tm, DIFF_W), row), full(g), full(wg), full(wba), full(wbb),
                  full(wo), full(fn), full(wr), full(br)],
        out_specs=[pl.BlockSpec((tm, D), row), pl.BlockSpec((tm, D), row),
                   pl.BlockSpec((tm, ROUTE_W), row),
                   pl.BlockSpec((1, 1, LANES), lambda i: (i, 0, 0))],
        out_shape=(jax.ShapeDtypeStruct((T, D), F32), jax.ShapeDtypeStruct((T, D), BF16),
                   jax.ShapeDtypeStruct((T, ROUTE_W), F32),
                   jax.ShapeDtypeStruct((T // tm, 1, LANES), F32)),
        compiler_params=_cparams(("parallel",)),
        name="postmix",
    )(x2, oa, ob, g, wg, wba, wbb, wo, fn, wr, br)


DISPATCH_CHUNK = 8
POS_W = 8


def _token_copy(src, src_tok, dst, dst_tok, n_tok, sem):
    return pltpu.make_async_copy(
        src.at[pl.ds(pl.multiple_of(src_tok * SUBLANES, SUBLANES), n_tok * SUBLANES), :],
        dst.at[pl.ds(pl.multiple_of(dst_tok * SUBLANES, SUBLANES), n_tok * SUBLANES), :], sem)


def _for_each_piece(length, fn):
    n_full = lax.shift_right_logical(length, 3)

    def body(c, carry):
        fn(c * DISPATCH_CHUNK, DISPATCH_CHUNK)
        return carry

    lax.fori_loop(0, n_full, body, 0)
    off = n_full * DISPATCH_CHUNK
    for n in (4, 2, 1):
        has = (length & n) != 0

        @pl.when(has)
        def _(off=off, n=n):
            fn(off, n)

        off = off + jnp.where(has, n, 0)


def _dispatch_kernel(seg_ref, lo_ref, cnt_ref, fill_ref, fill_len_ref, n_tiles_ref,
                     n2_ref, route_ref, lo_row_ref, seg_row_ref,
                     xg_hbm, pos_ref, xs_buf, zero_buf, sem, fill_sem):
    tm = n2_ref.shape[0]
    n_slab = 2 * tm * SUBLANES
    i = pl.program_id(0)
    nt = pl.num_programs(0)
    slot = i % 2

    @pl.when(i == 0)
    def _():
        zero_buf[...] = jnp.zeros_like(zero_buf)
        te = zero_buf.shape[0] // SUBLANES
        for wait in (False, True):
            def fill(e, carry, wait=wait):
                def piece(off, n):
                    cp = _token_copy(zero_buf, 0, xg_hbm, fill_ref[e] + off, n, fill_sem)
                    cp.wait() if wait else cp.start()
                _for_each_piece(fill_len_ref[e], piece)
                return carry
            lax.fori_loop(0, N_EXPERTS, fill, 0)

            def fill_tile(t, carry, wait=wait):
                cp = _token_copy(zero_buf, 0, xg_hbm, t * te, te, fill_sem)
                cp.wait() if wait else cp.start()
                return carry
            lax.fori_loop(n_tiles_ref[0], xg_hbm.shape[0] // zero_buf.shape[0], fill_tile, 0)

    route = route_ref[...]
    lane = lax.broadcasted_iota(jnp.int32, (tm, LANES), 1)
    oh1 = lane == route[:, 2:3].astype(jnp.int32)
    oh2 = lane == route[:, 3:4].astype(jnp.int32)
    both = jnp.where(oh1 | oh2, 1.0, 0.0).astype(BF16)
    earlier = (lax.broadcasted_iota(jnp.int32, (tm, tm), 0)
               > lax.broadcasted_iota(jnp.int32, (tm, tm), 1))
    rank = jnp.dot(jnp.where(earlier, 1.0, 0.0).astype(BF16), both, preferred_element_type=F32)

    def pick(onehot, v):
        return jnp.sum(jnp.where(onehot, v, 0.0), axis=-1, keepdims=True)

    r1 = pick(oh1, rank)
    r2 = pick(oh2, rank)
    q1 = (pick(oh1, lo_row_ref[0]) + r1).astype(jnp.int32)
    q2 = (pick(oh2, lo_row_ref[0]) + r2).astype(jnp.int32)
    pos1 = (pick(oh1, seg_row_ref[0]) + r1).astype(jnp.int32)
    pos2 = (pick(oh2, seg_row_ref[0]) + r2).astype(jnp.int32)
    rec = jnp.where(lane == 0, pos1, jnp.where(lane == 1, pos2, 0))
    pos_ref[...] = rec[:, :POS_W]

    col = lax.broadcasted_iota(jnp.int32, (tm, 2 * tm), 1)
    perm_t = jnp.where((col == q1) | (col == q2), 1.0, 0.0).astype(BF16)
    xs = lax.dot_general(perm_t, n2_ref[...], TN_DIMS, preferred_element_type=F32)

    def wait_slot(s):
        pltpu.make_async_copy(xg_hbm.at[pl.ds(0, n_slab), :], xs_buf.at[s], sem.at[s]).wait()

    @pl.when(i >= 2)
    def _():
        wait_slot(slot)

    _store_slab(xs_buf.at[slot], xs)

    def send(e, carry):
        k = i * N_EXPERTS + e

        def piece(off, n):
            _token_copy(xs_buf.at[slot], lo_ref[k] + off, xg_hbm, seg_ref[k] + off, n,
                        sem.at[slot]).start()

        _for_each_piece(cnt_ref[k], piece)
        return carry

    lax.fori_loop(0, N_EXPERTS, send, 0)

    @pl.when(i == nt - 1)
    def _():
        wait_slot(slot)

        @pl.when(nt >= 2)
        def _():
            wait_slot(1 - slot)


def _dispatch(n2, route, cnt, *, tm, te):
    T, D = n2.shape
    nt = T // tm
    n_tiles_max = 2 * T // te + N_EXPERTS
    c = cnt[:, 0, :N_EXPERTS].astype(jnp.int32)
    count = jnp.sum(c, axis=0)
    padded = (count + te - 1) // te * te
    p_end = jnp.cumsum(padded)
    p_start = p_end - padded
    n_tiles = (p_end[-1] // te).astype(jnp.int32).reshape(1)
    tile_start = jnp.arange(n_tiles_max, dtype=jnp.int32) * te
    tile_e = jnp.sum(tile_start[:, None] >= p_end[None, :], axis=1, dtype=jnp.int32)
    tile_e = jnp.minimum(tile_e, jnp.sum(tile_start[n_tiles[0] - 1] >= p_end, dtype=jnp.int32))
    seg = p_start[None, :] + jnp.cumsum(c, axis=0) - c
    lo = jnp.cumsum(c, axis=1) - c
    lanes = lambda a: jnp.pad(a.astype(F32), ((0, 0), (0, LANES - N_EXPERTS))).reshape(nt, 1, LANES)
    flat = lambda a: a.reshape(-1).astype(jnp.int32)
    row = lambda i, *_: (i, 0)
    per_tile = pl.BlockSpec((1, 1, LANES), lambda i, *_: (i, 0, 0))
    grid_spec = pltpu.PrefetchScalarGridSpec(
        num_scalar_prefetch=6,
        grid=(nt,),
        in_specs=[pl.BlockSpec((tm, D), row), pl.BlockSpec((tm, ROUTE_W), row), per_tile, per_tile],
        out_specs=[pl.BlockSpec(memory_space=pl.ANY), pl.BlockSpec((tm, POS_W), row)],
        scratch_shapes=[pltpu.VMEM((2, 2 * tm * SUBLANES, LANES), F32),
                        pltpu.VMEM((te * SUBLANES, LANES), F32),
                        pltpu.SemaphoreType.DMA((2,)), pltpu.SemaphoreType.DMA(())],
    )
    xg, pos = pl.pallas_call(
        _dispatch_kernel,
        grid_spec=grid_spec,
        out_shape=(jax.ShapeDtypeStruct((n_tiles_max * te * SUBLANES, LANES), F32),
                   jax.ShapeDtypeStruct((T, POS_W), jnp.int32)),
        compiler_params=_cparams(("arbitrary",)),
        name="dispatch",
    )(flat(seg), flat(lo), flat(c), flat(p_start + count), flat(padded - count), n_tiles,
      n2, route, lanes(lo), lanes(seg))
    return xg, pos, tile_e, n_tiles


def _experts_kernel(tile_e_ref, n_tiles_ref, xg_ref, wg_ref, wu_ref, wd_ref, y_ref):
    tm = xg_ref.shape[0] // SUBLANES
    i = pl.program_id(0)

    @pl.when(i < n_tiles_ref[0])
    def _():
        x = _load_slab(xg_ref, tm).astype(BF16)
        g = jnp.dot(x, wg_ref[0], preferred_element_type=F32)
        u = jnp.dot(x, wu_ref[0], preferred_element_type=F32)
        he = (g * jax.nn.sigmoid(g)) * u
        _store_slab(y_ref, jnp.dot(he.astype(BF16), wd_ref[0], preferred_element_type=F32))

    @pl.when(i >= n_tiles_ref[0])
    def _():
        y_ref[...] = jnp.zeros_like(y_ref)


def _experts(tile_e, n_tiles, xg, wg, wu, wd, *, tm):
    D = wg.shape[1]
    n_tiles_max = xg.shape[0] // (tm * SUBLANES)
    slab = lambda f: pl.BlockSpec((tm * SUBLANES, LANES), f)
    grid_spec = pltpu.PrefetchScalarGridSpec(
        num_scalar_prefetch=2,
        grid=(n_tiles_max,),
        in_specs=[slab(lambda i, te, nt: (jnp.minimum(i, nt[0] - 1), 0)),
                  pl.BlockSpec((1, D, D_EXPERT), lambda i, te, nt: (te[i], 0, 0)),
                  pl.BlockSpec((1, D, D_EXPERT), lambda i, te, nt: (te[i], 0, 0)),
                  pl.BlockSpec((1, D_EXPERT, D), lambda i, te, nt: (te[i], 0, 0))],
        out_specs=slab(lambda i, te, nt: (i, 0)),
    )
    return pl.pallas_call(
        _experts_kernel,
        grid_spec=grid_spec,
        out_shape=jax.ShapeDtypeStruct(xg.shape, F32),
        compiler_params=_cparams(("arbitrary",)),
        name="experts",
    )(tile_e, n_tiles, xg, wg, wu, wd)


def _final_kernel(idx_ref, idx_next_ref, h_ref, ys_hbm, route_ref, p_ref, wple_ref, pn_ref, wpg_ref,
                  fn_ref, o_ref, ybuf, sem, *, apply_final_norm):
    tm = h_ref.shape[0]
    i = pl.program_id(0)
    slot = i % 2

    @pl.when(i == 0)
    def _():
        _start_token_gather(idx_ref, 2 * tm, ys_hbm, ybuf.at[0], sem.at[0])

    @pl.when(i + 1 < pl.num_programs(0))
    def _():
        _start_token_gather(idx_next_ref, 2 * tm, ys_hbm, ybuf.at[1 - slot], sem.at[1 - slot])

    _wait_token_gather(2 * tm, ys_hbm, ybuf.at[slot], sem.at[slot])
    route = route_ref[...]
    y = route[:, 0:1] * _load_slab(ybuf.at[slot, pl.ds(0, tm * SUBLANES), :], tm)
    y = y + route[:, 1:2] * _load_slab(ybuf.at[slot, pl.ds(tm * SUBLANES, tm * SUBLANES), :], tm)
    h = h_ref[...] + y
    e = _rms(jnp.dot(p_ref[...].astype(BF16), wple_ref[...], preferred_element_type=F32), pn_ref[...])
    gate = jax.nn.sigmoid(jnp.dot(h.astype(BF16), wpg_ref[...], preferred_element_type=F32))
    h = h + gate * e
    if apply_final_norm:
        h = _rms(h, fn_ref[...])
    o_ref[...] = h


def _final(h1, ys, pos, route, p2, wple, ple_norm, wpg, final_norm, *, tm, apply_final_norm):
    T, D = h1.shape
    nt = T // tm
    row = lambda i: (i, 0)
    fixed = lambda i: (0, 0)
    full = lambda a: pl.BlockSpec(a.shape, fixed)
    pn = ple_norm.reshape(1, D)
    fn = final_norm.reshape(1, D)
    idx = pos[:, :2].reshape(nt, tm, 2).transpose(0, 2, 1).reshape(nt, 1, 2 * tm)
    smem_rows = lambda f: pl.BlockSpec((1, 1, 2 * tm), f, memory_space=pltpu.SMEM)
    return pl.pallas_call(
        functools.partial(_final_kernel, apply_final_norm=apply_final_norm),
        grid=(nt,),
        in_specs=[smem_rows(lambda i: (i, 0, 0)),
                  smem_rows(lambda i: (jnp.minimum(i + 1, nt - 1), 0, 0)),
                  pl.BlockSpec((tm, D), row), pl.BlockSpec(memory_space=pl.ANY),
                  pl.BlockSpec((tm, ROUTE_W), row), pl.BlockSpec((tm, p2.shape[1]), row),
                  full(wple), full(pn), full(wpg), full(fn)],
        out_specs=pl.BlockSpec((tm, D), row),
        out_shape=jax.ShapeDtypeStruct((T, D), F32),
        scratch_shapes=[pltpu.VMEM((2, 2 * tm * SUBLANES, LANES), F32),
                        pltpu.SemaphoreType.DMA((2,))],
        compiler_params=_cparams(("arbitrary",)),
        name="final",
    )(idx, idx, h1, ys, route, p2, wple, pn, wpg, fn)


SPLIT_SIZES = (GLA_QK, GLA_QK, GLA_V, GLA_V, GLA_LOWRANK, DIFF_W, DIFF_W, DIFF_W)


def _layer(h2, p2, positions, B, S, lambda_init, apply_final_norm, attn_norm, w_in, w_a2, b_a,
           gla_norm, lq1, lk1, lq2, lk2, diff_subln, w_branch_a, w_branch_b, w_out, ffn_norm,
           w_rg, b_rg, w_re, b_re, w_gate, w_up, w_down, w_ple, ple_norm, w_ple_gate, final_norm,
           *, tm, ts, tq, tk, te):
    T, D = h2.shape
    assert D == SUBLANES * LANES, "token-slab gathers need one (8, 128) tile per token"
    n_mix = sum(SPLIT_SIZES)
    wgla = w_in[:, :C_GLA_END].astype(BF16)
    wdiff = w_in[:, C_GLA_END:n_mix].astype(BF16)
    wg = w_in[:, n_mix:].astype(BF16)

    gq, gk, gv, gr, la, dq, dk, dv = _inproj(h2, positions, attn_norm, wgla, wdiff,
                                             w_a2.astype(BF16), b_a, tm=tm)
    oa = _gla(gq, gk, gv, gr, la, gla_norm, B=B, S=S, ts=ts).reshape(T, GLA_V)
    ob = _diffattn(dq, dk, dv, lq1, lk1, lq2, lk2, diff_subln, B=B, S=S, tq=tq, tk=tk,
                   lambda_init=lambda_init).reshape(T, DIFF_W)

    wr = jnp.concatenate([w_rg, w_re, jnp.zeros((D, LANES - N_GROUPS - N_EXPERTS), F32)], axis=1)
    br = jnp.concatenate([b_rg, b_re, jnp.zeros((LANES - N_GROUPS - N_EXPERTS,), F32)]).reshape(1, LANES)
    h1, n2, route, cnt = _postmix(h2, oa, ob, attn_norm, wg, w_branch_a.astype(BF16),
                                  w_branch_b.astype(BF16), w_out.astype(BF16), ffn_norm, wr, br,
                                  tm=tm)

    xg, pos, tile_e, n_tiles = _dispatch(n2, route, cnt, tm=tm, te=te)
    ys = _experts(tile_e, n_tiles, xg, w_gate.astype(BF16), w_up.astype(BF16),
                  w_down.astype(BF16), tm=te)
    return _final(h1, ys, pos, route, p2, w_ple.astype(BF16), ple_norm, w_ple_gate.astype(BF16),
                  final_norm, tm=tm, apply_final_norm=apply_final_norm)


def _block(x, p, positions, attn_norm, w_in, w_a2, b_a, gla_norm, lambda_q1, lambda_k1, lambda_q2,
           lambda_k2, diff_subln, w_branch_a, w_branch_b, w_out, ffn_norm, w_router_group,
           b_router_group, w_router_expert, b_router_expert, w_gate, w_up, w_down, w_ple, ple_norm,
           w_ple_gate, final_norm, *, tm, ts, tq, tk, te):
    B, S, D = x.shape
    depth = w_in.shape[0]
    h = x.reshape(B * S, D)
    for i in range(depth):
        lambda_init = 0.8 - 0.6 * math.exp(-0.3 * i)
        h = _layer(h, p[i].reshape(B * S, -1), positions, B, S, lambda_init, i == depth - 1,
                   attn_norm[i], w_in[i], w_a2[i], b_a[i], gla_norm[i], lambda_q1[i], lambda_k1[i],
                   lambda_q2[i], lambda_k2[i], diff_subln[i], w_branch_a[i], w_branch_b[i], w_out[i],
                   ffn_norm[i], w_router_group[i], b_router_group[i], w_router_expert[i],
                   b_router_expert[i], w_gate[i], w_up[i], w_down[i], w_ple[i], ple_norm[i],
                   w_ple_gate[i], final_norm, tm=tm, ts=ts, tq=tq, tk=tk, te=te)
    return h.reshape(B, S, D)


def kernel(x, p, positions, attn_norm, w_in, w_a2, b_a, gla_norm, lambda_q1, lambda_k1, lambda_q2, lambda_k2, diff_subln, w_branch_a, w_branch_b, w_out, ffn_norm, w_router_group, b_router_group, w_router_expert, b_router_expert, w_gate, w_up, w_down, w_ple, ple_norm, w_ple_gate, final_norm):
    S = x.shape[1]
    return _block(x, p, positions, attn_norm, w_in, w_a2, b_a, gla_norm, lambda_q1, lambda_k1,
                  lambda_q2, lambda_k2, diff_subln, w_branch_a, w_branch_b, w_out, ffn_norm,
                  w_router_group, b_router_group, w_router_expert, b_router_expert, w_gate, w_up,
                  w_down, w_ple, ple_norm, w_ple_gate, final_norm,
                  tm=512, ts=min(512, S), tq=min(512, S), tk=min(512, S), te=512)
```

```python
import functools
import math

import jax
import jax.numpy as jnp
import numpy as np
from jax import lax
from jax.experimental import pallas as pl
from jax.experimental.pallas import tpu as pltpu

EPS = 1e-6

GLA_HEADS = 4
GLA_DK = 64
GLA_DV = 128
GLA_LOWRANK = 16
GLA_TAU = 16.0
GLA_CHUNK = 64
GLA_QK = GLA_HEADS * GLA_DK
GLA_V = GLA_HEADS * GLA_DV

DIFF_HEADS = 4
DIFF_DH = 64
DIFF_DV = 2 * DIFF_DH
DIFF_W = DIFF_HEADS * DIFF_DV
ROPE_THETA = 500000.0
ROPE_DIM = DIFF_DH // 4
ROPE_HALF = ROPE_DIM // 2

N_GROUPS = 4
EXPERTS_PER_GROUP = 8
N_EXPERTS = N_GROUPS * EXPERTS_PER_GROUP
D_EXPERT = 256

LANES = 128
MXU_N = 256
VMEM_LIMIT = 56 * 1024 * 1024

BF16 = jnp.bfloat16
F32 = jnp.float32
NT_DIMS = (((1,), (1,)), ((), ()))
TN_DIMS = (((0,), (0,)), ((), ()))
NEG_BIG = -1e30


def _rms(x, g):
    return x * lax.rsqrt(jnp.mean(x * x, axis=-1, keepdims=True) + EPS) * g


def _cparams(semantics):
    return pltpu.CompilerParams(dimension_semantics=semantics, vmem_limit_bytes=VMEM_LIMIT)


SUBLANES = 8


def _store_slab(ref, x):
    n = x.shape[0]
    for j in range(SUBLANES):
        ref[pl.ds(j, n, stride=SUBLANES), :] = x[:, j * LANES:(j + 1) * LANES]


def _load_slab(ref, n):
    return jnp.concatenate([ref[pl.ds(j, n, stride=SUBLANES), :] for j in range(SUBLANES)], axis=1)


def _start_token_gather(idx_ref, n_tokens, src_hbm, dst, sem):
    def body(r, carry):
        src_row = pl.multiple_of(idx_ref[0, 0, r] * SUBLANES, SUBLANES)
        dst_row = pl.multiple_of(r * SUBLANES, SUBLANES)
        pltpu.make_async_copy(src_hbm.at[pl.ds(src_row, SUBLANES), :],
                              dst.at[pl.ds(dst_row, SUBLANES), :], sem).start(priority=1)
        return carry
    lax.fori_loop(0, n_tokens, body, 0, unroll=8)


def _wait_token_gather(n_tokens, src_hbm, dst, sem):
    pltpu.make_async_copy(src_hbm.at[pl.ds(0, n_tokens * SUBLANES), :], dst, sem).wait()


C_GQ = 0
C_GK = C_GQ + GLA_QK
C_GV = C_GK + GLA_QK
C_GR = C_GV + GLA_V
C_AL = C_GR + GLA_V
C_GLA_END = C_AL + GLA_LOWRANK
C_DQ = 0
C_DK = C_DQ + DIFF_W
C_DV = C_DK + DIFF_W
C_END = C_DV + DIFF_W


def _inproj_kernel(x_ref, pos_ref, g_ref, wgla_ref, wdiff_ref, wa2_ref, ba_ref, inv_ref, sel_ref,
                   one_ref,
                   gq_ref, gk_ref, gv_ref, gr_ref, la_ref, dq_ref, dk_ref, dv_ref):
    nb = _rms(x_ref[...], g_ref[...]).astype(BF16)

    def proj_gla(c0, c1):
        return jnp.dot(nb, wgla_ref[:, c0:c1], preferred_element_type=F32)

    def proj(c0, c1):
        return jnp.dot(nb, wdiff_ref[:, c0:c1], preferred_element_type=F32)

    gq_ref[...] = proj_gla(C_GQ, C_GK) * (GLA_DK ** -0.5)
    gk_ref[...] = proj_gla(C_GK, C_GV)
    gv_ref[...] = proj_gla(C_GV, C_GR).astype(BF16)
    gr_ref[...] = proj_gla(C_GR, C_AL).astype(BF16)

    gal = proj_gla(C_AL, C_GLA_END).astype(BF16)
    a_logit = jnp.dot(gal, wa2_ref[...], preferred_element_type=F32) + ba_ref[...]
    la_ref[...] = (jnp.minimum(a_logit, 0.0) - jnp.log1p(jnp.exp(-jnp.abs(a_logit)))) / GLA_TAU

    ang = inv_ref[...] * pos_ref[0].astype(F32)
    cs = jnp.concatenate([jnp.cos(ang), jnp.sin(ang)], axis=0)

    tabs = None
    rest = cs
    for _ in range(3):
        piece = rest.astype(BF16)
        rest = rest - piece.astype(F32)
        t = lax.dot_general(piece, sel_ref[...], TN_DIMS, preferred_element_type=F32)
        tabs = t if tabs is None else tabs + t
    cosf = tabs[:, :LANES] + one_ref[...]
    sneg = tabs[:, LANES:2 * LANES]
    spos = tabs[:, 2 * LANES:]

    def rope_store(c0, out_ref, scale):
        for j in range(DIFF_W // MXU_N):
            pair = proj(c0 + j * MXU_N, c0 + (j + 1) * MXU_N)
            for i in range(MXU_N // LANES):
                blk = pair[:, i * LANES:(i + 1) * LANES]
                rot = (blk * cosf + pltpu.roll(blk, LANES - ROPE_HALF, 1) * sneg
                       + pltpu.roll(blk, ROPE_HALF, 1) * spos)
                c = j * MXU_N + i * LANES
                out_ref[:, c:c + LANES] = (rot * scale).astype(BF16)

    rope_store(C_DQ, dq_ref, DIFF_DH ** -0.5)
    rope_store(C_DK, dk_ref, 1.0)
    dv_ref[...] = proj(C_DV, C_END).astype(BF16)


def _rope_tables():
    lane = np.arange(LANES)
    r = lane % DIFF_DH
    selc = np.zeros((2 * ROPE_HALF, LANES), np.float32)
    seln = np.zeros_like(selc)
    selp = np.zeros_like(selc)
    one = np.zeros((1, LANES), np.float32)
    for l in range(LANES):
        if r[l] < ROPE_DIM:
            selc[r[l] % ROPE_HALF, l] = 1.0
            if r[l] < ROPE_HALF:
                seln[ROPE_HALF + r[l], l] = -1.0
            else:
                selp[ROPE_HALF + r[l] - ROPE_HALF, l] = 1.0
        else:
            one[0, l] = 1.0
    sel = np.concatenate([selc, seln, selp], axis=1)
    return jnp.asarray(sel, dtype=BF16), jnp.asarray(one)


def _inproj(x2, positions, attn_norm, wgla, wdiff, wa2, b_a, *, tm):
    T, D = x2.shape
    nt = T // tm
    pos3 = positions.reshape(nt, 1, tm)
    inv = (ROPE_THETA ** (-jnp.arange(0, ROPE_DIM, 2, dtype=F32) / ROPE_DIM)).reshape(ROPE_HALF, 1)
    sel, one = _rope_tables()
    row = lambda i: (i, 0)
    fixed = lambda i: (0, 0)
    out_shapes = (
        jax.ShapeDtypeStruct((T, GLA_QK), F32), jax.ShapeDtypeStruct((T, GLA_QK), F32),
        jax.ShapeDtypeStruct((T, GLA_V), BF16), jax.ShapeDtypeStruct((T, GLA_V), BF16),
        jax.ShapeDtypeStruct((T, GLA_QK), F32),
        jax.ShapeDtypeStruct((T, DIFF_W), BF16), jax.ShapeDtypeStruct((T, DIFF_W), BF16),
        jax.ShapeDtypeStruct((T, DIFF_W), BF16),
    )
    return pl.pallas_call(
        _inproj_kernel,
        grid=(nt,),
        in_specs=[
            pl.BlockSpec((tm, D), row),
            pl.BlockSpec((1, 1, tm), lambda i: (i, 0, 0)),
            pl.BlockSpec((1, D), fixed),
            pl.BlockSpec((D, C_GLA_END), fixed),
            pl.BlockSpec((D, C_END), fixed),
            pl.BlockSpec((GLA_LOWRANK, GLA_QK), fixed),
            pl.BlockSpec((1, GLA_QK), fixed),
            pl.BlockSpec((ROPE_HALF, 1), fixed),
            pl.BlockSpec((2 * ROPE_HALF, 3 * LANES), fixed),
            pl.BlockSpec((1, LANES), fixed),
        ],
        out_specs=[pl.BlockSpec((tm, s.shape[1]), row) for s in out_shapes],
        out_shape=out_shapes,
        compiler_params=_cparams(("parallel",)),
        name="inproj",
    )(x2, pos3, attn_norm.reshape(1, D), wgla, wdiff, wa2, b_a.reshape(1, GLA_QK), inv, sel, one)


def _gla_kernel(gq_ref, gk_ref, gv_ref, gr_ref, la_ref, gn_ref, oa_ref, st_ref, *, n_chunks):
    @pl.when(pl.program_id(1) == 0)
    def _():
        st_ref[...] = jnp.zeros_like(st_ref)

    C = GLA_CHUNK
    tril = lax.broadcasted_iota(jnp.int32, (C, C), 0) >= lax.broadcasted_iota(jnp.int32, (C, C), 1)
    trilf = tril.astype(F32)
    head_of_lane = lax.broadcasted_iota(jnp.int32, (1, GLA_QK), 1) // GLA_DK
    hmask = [head_of_lane == h for h in range(GLA_HEADS)]
    gn = gn_ref[...]

    for c in range(n_chunks):
        sl = pl.ds(c * C, C)
        q = gq_ref[0, sl, :]
        k = gk_ref[0, sl, :]
        la = la_ref[0, sl, :]
        v = gv_ref[0, sl, :]
        b = jnp.dot(trilf, la, precision=lax.Precision.HIGHEST, preferred_element_type=F32)
        b_last = b[C - 1:C, :]
        q_dec = (q * jnp.exp(b)).astype(BF16)
        k_inv = (k * jnp.exp(-b)).astype(BF16)
        k_end = (k * jnp.exp(b_last - b)).astype(BF16)
        decay = jnp.exp(b_last)

        q_heads = jnp.concatenate([jnp.where(hmask[h], q_dec, jnp.zeros_like(q_dec))
                                   for h in range(GLA_HEADS)], axis=0)
        att = lax.dot_general(q_heads, k_inv, NT_DIMS, preferred_element_type=F32)
        st = st_ref[...]
        inter = lax.dot_general(q_heads, st.astype(BF16), NT_DIMS,
                                preferred_element_type=F32)
        u = jnp.zeros_like(st)
        for h in range(GLA_HEADS):
            rows = slice(h * C, (h + 1) * C)
            cols = slice(h * GLA_DV, (h + 1) * GLA_DV)
            a_h = jnp.where(tril, att[rows], 0.0).astype(BF16)
            v_h = v[:, cols]
            o_h = jnp.dot(a_h, v_h, preferred_element_type=F32) + inter[rows]
            y = _rms(o_h, gn)
            r = gr_ref[0, sl, cols].astype(F32)
            oa_ref[0, sl, cols] = (y * (r * jax.nn.sigmoid(r))).astype(BF16)
            u_h = lax.dot_general(v_h, k_end, TN_DIMS, preferred_element_type=F32)
            u = u + jnp.where(hmask[h], u_h, 0.0)
        st_ref[...] = st * decay + u


def _gla(gq, gk, gv, gr, la, gla_norm, *, B, S, ts):
    n_chunks = ts // GLA_CHUNK
    blk = lambda w: pl.BlockSpec((1, ts, w), lambda b, j: (b, j, 0))
    r3 = lambda a: a.reshape(B, S, a.shape[-1])
    return pl.pallas_call(
        functools.partial(_gla_kernel, n_chunks=n_chunks),
        grid=(B, S // ts),
        in_specs=[blk(GLA_QK), blk(GLA_QK), blk(GLA_V), blk(GLA_V), blk(GLA_QK),
                  pl.BlockSpec((1, GLA_DV), lambda b, j: (0, 0))],
        out_specs=blk(GLA_V),
        out_shape=jax.ShapeDtypeStruct((B, S, GLA_V), BF16),
        scratch_shapes=[pltpu.VMEM((GLA_DV, GLA_QK), F32)],
        compiler_params=_cparams(("parallel", "arbitrary")),
        name="gla",
    )(r3(gq), r3(gk), r3(gv), r3(gr), r3(la), gla_norm.reshape(1, GLA_DV))


def _diffattn_kernel(dq_ref, dk_ref, dv_ref, lq1_ref, lk1_ref, lq2_ref, lk2_ref, sub_ref,
                     ob_ref, m_ref, l_ref, acc_ref, *, tq, tk, lambda_init):
    S = dq_ref.shape[1]
    nq = S // tq
    R = 2 * tq
    lam = (jnp.exp(jnp.sum(lq1_ref[...] * lk1_ref[...], axis=-1, keepdims=True))
           - jnp.exp(jnp.sum(lq2_ref[...] * lk2_ref[...], axis=-1, keepdims=True))
           + lambda_init)
    first_comp = lax.broadcasted_iota(jnp.int32, (1, DIFF_DV), 1) < DIFF_DH
    sub = sub_ref[...]

    for h in range(DIFF_HEADS):
        cols = slice(h * DIFF_DV, (h + 1) * DIFF_DV)
        for qi in range(nq):
            qb = dq_ref[0, qi * tq:(qi + 1) * tq, cols]
            zero = jnp.zeros_like(qb)
            qs = jnp.concatenate([jnp.where(first_comp, qb, zero),
                                  jnp.where(first_comp, zero, qb)], axis=0)
            m_ref[...] = jnp.full(m_ref.shape, NEG_BIG, F32)
            l_ref[...] = jnp.zeros(l_ref.shape, F32)
            acc_ref[...] = jnp.zeros(acc_ref.shape, F32)

            def step(kstart, masked, qs=qs, cols=cols, qi=qi):
                kb = dk_ref[0, pl.ds(kstart, tk), cols]
                vb = dv_ref[0, pl.ds(kstart, tk), cols]
                s = lax.dot_general(kb, qs, NT_DIMS, preferred_element_type=F32)
                if masked:
                    kpos = kstart + lax.broadcasted_iota(jnp.int32, (tk, R), 0)
                    qpos = qi * tq + lax.broadcasted_iota(jnp.int32, (tk, R), 1) % tq
                    s = jnp.where(qpos >= kpos, s, NEG_BIG)
                m_prev = m_ref[...]
                m_new = jnp.maximum(m_prev, jnp.max(s, axis=0, keepdims=True))
                alpha = jnp.exp(m_prev - m_new)
                p = jnp.exp(s - m_new)
                l_ref[...] = alpha * l_ref[...] + jnp.sum(p, axis=0, keepdims=True)
                pv = lax.dot_general(vb, p.astype(BF16), TN_DIMS, preferred_element_type=F32)
                acc_ref[...] = alpha * acc_ref[...] + pv
                m_ref[...] = m_new

            n_full = (qi * tq) // tk

            def body(i, carry, step=step):
                step(pl.multiple_of(i * tk, tk), False)
                return carry

            if n_full > 0:
                lax.fori_loop(0, n_full, body, 0)
            for d in range(tq // tk):
                step(qi * tq + d * tk, True)

            o_all = acc_ref[...] / l_ref[...]
            o = o_all[:, :tq] - lam * o_all[:, tq:]
            y = o * lax.rsqrt(jnp.mean(o * o, axis=0, keepdims=True) + EPS) * sub
            y = y * (1.0 - lambda_init)
            ob_ref[0, qi * tq:(qi + 1) * tq, cols] = y.T.astype(BF16)


def _diffattn(dq, dk, dv, lq1, lk1, lq2, lk2, diff_subln, *, B, S, tq, tk, lambda_init):
    seq = pl.BlockSpec((1, S, DIFF_W), lambda b: (b, 0, 0))
    vec = lambda w: pl.BlockSpec((1, w), lambda b: (0, 0))
    r3 = lambda a: a.reshape(B, S, DIFF_W)
    return pl.pallas_call(
        functools.partial(_diffattn_kernel, tq=tq, tk=tk, lambda_init=lambda_init),
        grid=(B,),
        in_specs=[seq, seq, seq, vec(DIFF_DH), vec(DIFF_DH), vec(DIFF_DH), vec(DIFF_DH),
                  pl.BlockSpec((DIFF_DV, 1), lambda b: (0, 0))],
        out_specs=seq,
        out_shape=jax.ShapeDtypeStruct((B, S, DIFF_W), BF16),
        scratch_shapes=[pltpu.VMEM((1, 2 * tq), F32), pltpu.VMEM((1, 2 * tq), F32),
                        pltpu.VMEM((DIFF_DV, 2 * tq), F32)],
        compiler_params=_cparams(("parallel",)),
        name="diffattn",
    )(r3(dq), r3(dk), r3(dv), lq1.reshape(1, -1), lk1.reshape(1, -1), lq2.reshape(1, -1),
      lk2.reshape(1, -1), diff_subln.reshape(-1, 1))


ROUTE_W = 8


def _first_index_of(mask, lane):
    return jnp.min(jnp.where(mask, lane, LANES), axis=-1, keepdims=True)


POSTMIX_SPLIT = 2


def _postmix_kernel(x_ref, oa_ref, ob_ref, g_ref, wg_ref, wba_ref, wbb_ref, wo_ref, fn_ref,
                    wr_ref, br_ref, h_ref, n2_ref, route_ref, cnt_ref):
    rows_per = x_ref.shape[0] // POSTMIX_SPLIT
    cnt = None
    for s in range(POSTMIX_SPLIT):
        rows = slice(s * rows_per, (s + 1) * rows_per)
        h, n2, rec, c = _postmix_rows(x_ref[rows, :], oa_ref[rows, :], ob_ref[rows, :], g_ref, wg_ref,
                                      wba_ref, wbb_ref, wo_ref, fn_ref, wr_ref, br_ref)
        h_ref[rows, :] = h
        n2_ref[rows, :] = n2
        route_ref[rows, :] = rec
        cnt = c if cnt is None else cnt + c
    cnt_ref[0] = cnt


def _postmix_rows(x, oa, ob, g_ref, wg_ref, wba_ref, wbb_ref, wo_ref, fn_ref, wr_ref, br_ref):
    D = x.shape[1]
    nb = _rms(x, g_ref[...]).astype(BF16)
    y_a = jnp.dot(oa, wba_ref[...], preferred_element_type=F32)
    g_a = jnp.dot(nb, wg_ref[:, :D], preferred_element_type=F32)
    merged = jax.nn.sigmoid(g_a) * y_a
    y_b = jnp.dot(ob, wbb_ref[...], preferred_element_type=F32)
    g_b = jnp.dot(nb, wg_ref[:, D:], preferred_element_type=F32)
    merged = merged + jax.nn.sigmoid(g_b) * y_b
    h = x + jnp.dot(merged.astype(BF16), wo_ref[...], preferred_element_type=F32)
    n2 = _rms(h, fn_ref[...])

    n2_hi = n2.astype(BF16)
    n2_lo = (n2 - n2_hi.astype(F32)).astype(BF16)
    hi_prod = jnp.dot(n2_hi, wr_ref[...], preferred_element_type=F32)
    lg = (hi_prod[:, :LANES] + hi_prod[:, LANES:]
          + jnp.dot(n2_lo, wr_ref[:, :LANES], preferred_element_type=F32)) + br_ref[...]
    lane = lax.broadcasted_iota(jnp.int32, lg.shape, 1)
    is_g = lane < N_GROUPS
    g_max = jnp.max(jnp.where(is_g, lg, -jnp.inf), axis=-1, keepdims=True)
    g_exp = jnp.where(is_g, jnp.exp(lg - g_max), 0.0)
    g_prob = g_exp / jnp.sum(g_exp, axis=-1, keepdims=True)
    g_p = jnp.max(g_prob, axis=-1, keepdims=True)
    g_idx = _first_index_of(is_g & (g_prob == g_p), lane)

    e_lo = N_GROUPS + EXPERTS_PER_GROUP * g_idx
    is_e = (lane >= e_lo) & (lane < e_lo + EXPERTS_PER_GROUP)
    e_max = jnp.max(jnp.where(is_e, lg, -jnp.inf), axis=-1, keepdims=True)
    e_exp = jnp.where(is_e, jnp.exp(lg - e_max), 0.0)
    e_prob = e_exp / jnp.sum(e_exp, axis=-1, keepdims=True)
    p1 = jnp.max(jnp.where(is_e, e_prob, -1.0), axis=-1, keepdims=True)
    i1 = _first_index_of(is_e & (e_prob == p1), lane)
    rest = is_e & (lane != i1)
    p2 = jnp.max(jnp.where(rest, e_prob, -1.0), axis=-1, keepdims=True)
    i2 = _first_index_of(rest & (e_prob == p2), lane)
    den = p1 + p2
    w1 = g_p * (p1 / den)
    w2 = g_p * (p2 / den)
    rec = jnp.where(lane == 0, w1, 0.0)
    rec = jnp.where(lane == 1, w2, rec)
    rec = jnp.where(lane == 2, (i1 - N_GROUPS).astype(F32), rec)
    rec = jnp.where(lane == 3, (i2 - N_GROUPS).astype(F32), rec)
    chosen = (lane == i1 - N_GROUPS) | (lane == i2 - N_GROUPS)
    cnt = jnp.sum(chosen.astype(F32), axis=0, keepdims=True)
    return h, n2.astype(BF16), rec[:, :ROUTE_W], cnt


def _postmix(x2, oa, ob, attn_norm, wg, wba, wbb, wo, ffn_norm, wr, br, *, tm):
    T, D = x2.shape
    row = lambda i: (i, 0)
    fixed = lambda i: (0, 0)
    full = lambda a: pl.BlockSpec(a.shape, fixed)
    g = attn_norm.reshape(1, D)
    fn = ffn_norm.reshape(1, D)
    return pl.pallas_call(
        _postmix_kernel,
        grid=(T // tm,),
        in_specs=[pl.BlockSpec((tm, D), row), pl.BlockSpec((tm, GLA_V), row),
                  pl.BlockSpec((tm, DIFF_W), row), full(g), full(wg), full(wba), full(wbb),
                  full(wo), full(fn), full(wr), full(br)],
        out_specs=[pl.BlockSpec((tm, D), row), pl.BlockSpec((tm, D), row),
                   pl.BlockSpec((tm, ROUTE_W), row),
                   pl.BlockSpec((1, 1, LANES), lambda i: (i, 0, 0))],
        out_shape=(jax.ShapeDtypeStruct((T, D), F32), jax.ShapeDtypeStruct((T, D), BF16),
                   jax.ShapeDtypeStruct((T, ROUTE_W), F32),
                   jax.ShapeDtypeStruct((T // tm, 1, LANES), F32)),
        compiler_params=_cparams(("parallel",)),
        name="postmix",
    )(x2, oa, ob, g, wg, wba, wbb, wo, fn, wr, br)


DISPATCH_CHUNK = 8
POS_W = 8


def _token_copy(src, src_tok, dst, dst_tok, n_tok, sem):
    return pltpu.make_async_copy(
        src.at[pl.ds(pl.multiple_of(src_tok * SUBLANES, SUBLANES), n_tok * SUBLANES), :],
        dst.at[pl.ds(pl.multiple_of(dst_tok * SUBLANES, SUBLANES), n_tok * SUBLANES), :], sem)


def _for_each_piece(length, fn):
    n_full = lax.shift_right_logical(length, 3)

    def body(c, carry):
        fn(c * DISPATCH_CHUNK, DISPATCH_CHUNK)
        return carry

    lax.fori_loop(0, n_full, body, 0)
    off = n_full * DISPATCH_CHUNK
    for n in (4, 2, 1):
        has = (length & n) != 0

        @pl.when(has)
        def _(off=off, n=n):
            fn(off, n)

        off = off + jnp.where(has, n, 0)


def _dispatch_kernel(seg_ref, lo_ref, cnt_ref, fill_ref, fill_len_ref, n_tiles_ref,
                     n2_ref, route_ref, lo_row_ref, seg_row_ref,
                     xg_hbm, pos_ref, xs_buf, zero_buf, sem, fill_sem):
    tm = n2_ref.shape[0]
    n_slab = 2 * tm * SUBLANES
    i = pl.program_id(0)
    nt = pl.num_programs(0)
    slot = i % 2

    @pl.when(i == 0)
    def _():
        zero_buf[...] = jnp.zeros_like(zero_buf)
        te = zero_buf.shape[0] // SUBLANES
        for wait in (False, True):
            def fill(e, carry, wait=wait):
                def piece(off, n):
                    cp = _token_copy(zero_buf, 0, xg_hbm, fill_ref[e] + off, n, fill_sem)
                    cp.wait() if wait else cp.start()
                _for_each_piece(fill_len_ref[e], piece)
                return carry
            lax.fori_loop(0, N_EXPERTS, fill, 0)

            def fill_tile(t, carry, wait=wait):
                cp = _token_copy(zero_buf, 0, xg_hbm, t * te, te, fill_sem)
                cp.wait() if wait else cp.start()
                return carry
            lax.fori_loop(n_tiles_ref[0], xg_hbm.shape[0] // zero_buf.shape[0], fill_tile, 0)

    route = route_ref[...]
    lane = lax.broadcasted_iota(jnp.int32, (tm, LANES), 1)
    oh1 = lane == route[:, 2:3].astype(jnp.int32)
    oh2 = lane == route[:, 3:4].astype(jnp.int32)
    both = jnp.where(oh1 | oh2, 1.0, 0.0).astype(BF16)
    earlier = (lax.broadcasted_iota(jnp.int32, (tm, tm), 0)
               > lax.broadcasted_iota(jnp.int32, (tm, tm), 1))
    rank = jnp.dot(jnp.where(earlier, 1.0, 0.0).astype(BF16), both, preferred_element_type=F32)

    def pick(onehot, v):
        return jnp.sum(jnp.where(onehot, v, 0.0), axis=-1, keepdims=True)

    r1 = pick(oh1, rank)
    r2 = pick(oh2, rank)
    q1 = (pick(oh1, lo_row_ref[0]) + r1).astype(jnp.int32)
    q2 = (pick(oh2, lo_row_ref[0]) + r2).astype(jnp.int32)
    pos1 = (pick(oh1, seg_row_ref[0]) + r1).astype(jnp.int32)
    pos2 = (pick(oh2, seg_row_ref[0]) + r2).astype(jnp.int32)
    rec = jnp.where(lane == 0, pos1, jnp.where(lane == 1, pos2, 0))
    pos_ref[...] = rec[:, :POS_W]

    col = lax.broadcasted_iota(jnp.int32, (tm, 2 * tm), 1)
    perm_t = jnp.where((col == q1) | (col == q2), 1.0, 0.0).astype(BF16)
    xs = lax.dot_general(perm_t, n2_ref[...], TN_DIMS, preferred_element_type=F32)

    def wait_slot(s):
        pltpu.make_async_copy(xg_hbm.at[pl.ds(0, n_slab), :], xs_buf.at[s], sem.at[s]).wait()

    @pl.when(i >= 2)
    def _():
        wait_slot(slot)

    _store_slab(xs_buf.at[slot], xs)

    def send(e, carry):
        k = i * N_EXPERTS + e

        def piece(off, n):
            _token_copy(xs_buf.at[slot], lo_ref[k] + off, xg_hbm, seg_ref[k] + off, n,
                        sem.at[slot]).start()

        _for_each_piece(cnt_ref[k], piece)
        return carry

    lax.fori_loop(0, N_EXPERTS, send, 0)

    @pl.when(i == nt - 1)
    def _():
        wait_slot(slot)

        @pl.when(nt >= 2)
        def _():
            wait_slot(1 - slot)


def _dispatch(n2, route, cnt, *, tm, te):
    T, D = n2.shape
    nt = T // tm
    n_tiles_max = 2 * T // te + N_EXPERTS
    c = cnt[:, 0, :N_EXPERTS].astype(jnp.int32)
    count = jnp.sum(c, axis=0)
    padded = (count + te - 1) // te * te
    p_end = jnp.cumsum(padded)
    p_start = p_end - padded
    n_tiles = (p_end[-1] // te).astype(jnp.int32).reshape(1)
    tile_start = jnp.arange(n_tiles_max, dtype=jnp.int32) * te
    tile_e = jnp.sum(tile_start[:, None] >= p_end[None, :], axis=1, dtype=jnp.int32)
    tile_e = jnp.minimum(tile_e, jnp.sum(tile_start[n_tiles[0] - 1] >= p_end, dtype=jnp.int32))
    seg = p_start[None, :] + jnp.cumsum(c, axis=0) - c
    lo = jnp.cumsum(c, axis=1) - c
    lanes = lambda a: jnp.pad(a.astype(F32), ((0, 0), (0, LANES - N_EXPERTS))).reshape(nt, 1, LANES)
    flat = lambda a: a.reshape(-1).astype(jnp.int32)
    row = lambda i, *_: (i, 0)
    per_tile = pl.BlockSpec((1, 1, LANES), lambda i, *_: (i, 0, 0))
    grid_spec = pltpu.PrefetchScalarGridSpec(
        num_scalar_prefetch=6,
        grid=(nt,),
        in_specs=[pl.BlockSpec((tm, D), row), pl.BlockSpec((tm, ROUTE_W), row), per_tile, per_tile],
        out_specs=[pl.BlockSpec(memory_space=pl.ANY), pl.BlockSpec((tm, POS_W), row)],
        scratch_shapes=[pltpu.VMEM((2, 2 * tm * SUBLANES, LANES), F32),
                        pltpu.VMEM((te * SUBLANES, LANES), F32),
                        pltpu.SemaphoreType.DMA((2,)), pltpu.SemaphoreType.DMA(())],
    )
    xg, pos = pl.pallas_call(
        _dispatch_kernel,
        grid_spec=grid_spec,
        out_shape=(jax.ShapeDtypeStruct((n_tiles_max * te * SUBLANES, LANES), F32),
                   jax.ShapeDtypeStruct((T, POS_W), jnp.int32)),
        compiler_params=_cparams(("arbitrary",)),
        name="dispatch",
    )(flat(seg), flat(lo), flat(c), flat(p_start + count), flat(padded - count), n_tiles,
      n2, route, lanes(lo), lanes(seg))
    return xg, pos, tile_e, n_tiles


def _experts_kernel(tile_e_ref, n_tiles_ref, xg_ref, wg_ref, wu_ref, wd_ref, y_ref):
    tm = xg_ref.shape[0] // SUBLANES
    i = pl.program_id(0)

    @pl.when(i < n_tiles_ref[0])
    def _():
        x = _load_slab(xg_ref, tm).astype(BF16)
        g = jnp.dot(x, wg_ref[0], preferred_element_type=F32)
        u = jnp.dot(x, wu_ref[0], preferred_element_type=F32)
        he = (g * jax.nn.sigmoid(g)) * u
        _store_slab(y_ref, jnp.dot(he.astype(BF16), wd_ref[0], preferred_element_type=F32))

    @pl.when(i >= n_tiles_ref[0])
    def _():
        y_ref[...] = jnp.zeros_like(y_ref)


def _experts(tile_e, n_tiles, xg, wg, wu, wd, *, tm):
    D = wg.shape[1]
    n_tiles_max = xg.shape[0] // (tm * SUBLANES)
    slab = lambda f: pl.BlockSpec((tm * SUBLANES, LANES), f)
    grid_spec = pltpu.PrefetchScalarGridSpec(
        num_scalar_prefetch=2,
        grid=(n_tiles_max,),
        in_specs=[slab(lambda i, te, nt: (jnp.minimum(i, nt[0] - 1), 0)),
                  pl.BlockSpec((1, D, D_EXPERT), lambda i, te, nt: (te[i], 0, 0)),
                  pl.BlockSpec((1, D, D_EXPERT), lambda i, te, nt: (te[i], 0, 0)),
                  pl.BlockSpec((1, D_EXPERT, D), lambda i, te, nt: (te[i], 0, 0))],
        out_specs=slab(lambda i, te, nt: (i, 0)),
    )
    return pl.pallas_call(
        _experts_kernel,
        grid_spec=grid_spec,
        out_shape=jax.ShapeDtypeStruct(xg.shape, F32),
        compiler_params=_cparams(("arbitrary",)),
        name="experts",
    )(tile_e, n_tiles, xg, wg, wu, wd)


def _final_kernel(idx_ref, idx_next_ref, h_ref, ys_hbm, route_ref, p_ref, wple_ref, pn_ref, wpg_ref,
                  fn_ref, o_ref, ybuf, sem, *, apply_final_norm):
    tm = h_ref.shape[0]
    i = pl.program_id(0)
    slot = i % 2

    @pl.when(i == 0)
    def _():
        _start_token_gather(idx_ref, 2 * tm, ys_hbm, ybuf.at[0], sem.at[0])

    @pl.when(i + 1 < pl.num_programs(0))
    def _():
        _start_token_gather(idx_next_ref, 2 * tm, ys_hbm, ybuf.at[1 - slot], sem.at[1 - slot])

    _wait_token_gather(2 * tm, ys_hbm, ybuf.at[slot], sem.at[slot])
    route = route_ref[...]
    y = route[:, 0:1] * _load_slab(ybuf.at[slot, pl.ds(0, tm * SUBLANES), :], tm)
    y = y + route[:, 1:2] * _load_slab(ybuf.at[slot, pl.ds(tm * SUBLANES, tm * SUBLANES), :], tm)
    h = h_ref[...] + y
    e = _rms(jnp.dot(p_ref[...].astype(BF16), wple_ref[...], preferred_element_type=F32), pn_ref[...])
    gate = jax.nn.sigmoid(jnp.dot(h.astype(BF16), wpg_ref[...], preferred_element_type=F32))
    h = h + gate * e
    if apply_final_norm:
        h = _rms(h, fn_ref[...])
    o_ref[...] = h


def _final(h1, ys, pos, route, p2, wple, ple_norm, wpg, final_norm, *, tm, apply_final_norm):
    T, D = h1.shape
    nt = T // tm
    row = lambda i: (i, 0)
    fixed = lambda i: (0, 0)
    full = lambda a: pl.BlockSpec(a.shape, fixed)
    pn = ple_norm.reshape(1, D)
    fn = final_norm.reshape(1, D)
    idx = pos[:, :2].reshape(nt, tm, 2).transpose(0, 2, 1).reshape(nt, 1, 2 * tm)
    smem_rows = lambda f: pl.BlockSpec((1, 1, 2 * tm), f, memory_space=pltpu.SMEM)
    return pl.pallas_call(
        functools.partial(_final_kernel, apply_final_norm=apply_final_norm),
        grid=(nt,),
        in_specs=[smem_rows(lambda i: (i, 0, 0)),
                  smem_rows(lambda i: (jnp.minimum(i + 1, nt - 1), 0, 0)),
                  pl.BlockSpec((tm, D), row), pl.BlockSpec(memory_space=pl.ANY),
                  pl.BlockSpec((tm, ROUTE_W), row), pl.BlockSpec((tm, p2.shape[1]), row),
                  full(wple), full(pn), full(wpg), full(fn)],
        out_specs=pl.BlockSpec((tm, D), row),
        out_shape=jax.ShapeDtypeStruct((T, D), F32),
        scratch_shapes=[pltpu.VMEM((2, 2 * tm * SUBLANES, LANES), F32),
                        pltpu.SemaphoreType.DMA((2,))],
        compiler_params=_cparams(("arbitrary",)),
        name="final",
    )(idx, idx, h1, ys, route, p2, wple, pn, wpg, fn)


SPLIT_SIZES = (GLA_QK, GLA_QK, GLA_V, GLA_V, GLA_LOWRANK, DIFF_W, DIFF_W, DIFF_W)


def _layer(h2, p2, positions, B, S, lambda_init, apply_final_norm, attn_norm, w_in, w_a2, b_a,
           gla_norm, lq1, lk1, lq2, lk2, diff_subln, w_branch_a, w_branch_b, w_out, ffn_norm,
           w_rg, b_rg, w_re, b_re, w_gate, w_up, w_down, w_ple, ple_norm, w_ple_gate, final_norm,
           *, tm, ts, tq, tk, te):
    T, D = h2.shape
    assert D == SUBLANES * LANES, "token-slab gathers need one (8, 128) tile per token"
    n_mix = sum(SPLIT_SIZES)
    wgla = w_in[:, :C_GLA_END].astype(BF16)
    wdiff = w_in[:, C_GLA_END:n_mix].astype(BF16)
    wg = w_in[:, n_mix:].astype(BF16)

    gq, gk, gv, gr, la, dq, dk, dv = _inproj(h2, positions, attn_norm, wgla, wdiff,
                                             w_a2.astype(BF16), b_a, tm=tm)
    oa = _gla(gq, gk, gv, gr, la, gla_norm, B=B, S=S, ts=ts).reshape(T, GLA_V)
    ob = _diffattn(dq, dk, dv, lq1, lk1, lq2, lk2, diff_subln, B=B, S=S, tq=tq, tk=tk,
                   lambda_init=lambda_init).reshape(T, DIFF_W)

    wr = jnp.concatenate([w_rg, w_re, jnp.zeros((D, LANES - N_GROUPS - N_EXPERTS), F32)], axis=1)
    wr_hi = wr.astype(BF16)
    wr = jnp.concatenate([wr_hi, (wr - wr_hi.astype(F32)).astype(BF16)], axis=1)
    br = jnp.concatenate([b_rg, b_re, jnp.zeros((LANES - N_GROUPS - N_EXPERTS,), F32)]).reshape(1, LANES)
    h1, n2, route, cnt = _postmix(h2, oa, ob, attn_norm, wg, w_branch_a.astype(BF16),
                                  w_branch_b.astype(BF16), w_out.astype(BF16), ffn_norm, wr, br,
                                  tm=tm)

    xg, pos, tile_e, n_tiles = _dispatch(n2, route, cnt, tm=tm, te=te)
    ys = _experts(tile_e, n_tiles, xg, w_gate.astype(BF16), w_up.astype(BF16),
                  w_down.astype(BF16), tm=te)
    return _final(h1, ys, pos, route, p2, w_ple.astype(BF16), ple_norm, w_ple_gate.astype(BF16),
                  final_norm, tm=tm, apply_final_norm=apply_final_norm)


def _block(x, p, positions, attn_norm, w_in, w_a2, b_a, gla_norm, lambda_q1, lambda_k1, lambda_q2,
           lambda_k2, diff_subln, w_branch_a, w_branch_b, w_out, ffn_norm, w_router_group,
           b_router_group, w_router_expert, b_router_expert, w_gate, w_up, w_down, w_ple, ple_norm,
           w_ple_gate, final_norm, *, tm, ts, tq, tk, te):
    B, S, D = x.shape
    depth = w_in.shape[0]
    h = x.reshape(B * S, D)
    for i in range(depth):
        lambda_init = 0.8 - 0.6 * math.exp(-0.3 * i)
        h = _layer(h, p[i].reshape(B * S, -1), positions, B, S, lambda_init, i == depth - 1,
                   attn_norm[i], w_in[i], w_a2[i], b_a[i], gla_norm[i], lambda_q1[i], lambda_k1[i],
                   lambda_q2[i], lambda_k2[i], diff_subln[i], w_branch_a[i], w_branch_b[i], w_out[i],
                   ffn_norm[i], w_router_group[i], b_router_group[i], w_router_expert[i],
                   b_router_expert[i], w_gate[i], w_up[i], w_down[i], w_ple[i], ple_norm[i],
                   w_ple_gate[i], final_norm, tm=tm, ts=ts, tq=tq, tk=tk, te=te)
    return h.reshape(B, S, D)


def kernel(x, p, positions, attn_norm, w_in, w_a2, b_a, gla_norm, lambda_q1, lambda_k1, lambda_q2, lambda_k2, diff_subln, w_branch_a, w_branch_b, w_out, ffn_norm, w_router_group, b_router_group, w_router_expert, b_router_expert, w_gate, w_up, w_down, w_ple, ple_norm, w_ple_gate, final_norm):
    S = x.shape[1]
    return _block(x, p, positions, attn_norm, w_in, w_a2, b_a, gla_norm, lambda_q1, lambda_k1,
                  lambda_q2, lambda_k2, diff_subln, w_branch_a, w_branch_b, w_out, ffn_norm,
                  w_router_group, b_router_group, w_router_expert, b_router_expert, w_gate, w_up,
                  w_down, w_ple, ple_norm, w_ple_gate, final_norm,
                  tm=512, ts=min(512, S), tq=min(512, S), tk=min(512, S), te=512)
```

```python
import functools
import math

import jax
import jax.numpy as jnp
import numpy as np
from jax import lax
from jax.experimental import pallas as pl
from jax.experimental.pallas import tpu as pltpu

EPS = 1e-6

GLA_HEADS = 4
GLA_DK = 64
GLA_DV = 128
GLA_LOWRANK = 16
GLA_TAU = 16.0
GLA_CHUNK = 64
GLA_QK = GLA_HEADS * GLA_DK
GLA_V = GLA_HEADS * GLA_DV

DIFF_HEADS = 4
DIFF_DH = 64
DIFF_DV = 2 * DIFF_DH
DIFF_W = DIFF_HEADS * DIFF_DV
ROPE_THETA = 500000.0
ROPE_DIM = DIFF_DH // 4
ROPE_HALF = ROPE_DIM // 2

N_GROUPS = 4
EXPERTS_PER_GROUP = 8
N_EXPERTS = N_GROUPS * EXPERTS_PER_GROUP
D_EXPERT = 256

LANES = 128
MXU_N = 256
VMEM_LIMIT = 56 * 1024 * 1024

BF16 = jnp.bfloat16
F32 = jnp.float32
NT_DIMS = (((1,), (1,)), ((), ()))
TN_DIMS = (((0,), (0,)), ((), ()))
NEG_BIG = -1e30


def _rms(x, g):
    return x * lax.rsqrt(jnp.mean(x * x, axis=-1, keepdims=True) + EPS) * g


def _cparams(semantics):
    return pltpu.CompilerParams(dimension_semantics=semantics, vmem_limit_bytes=VMEM_LIMIT)


SUBLANES = 8


def _store_slab(ref, x):
    n = x.shape[0]
    for j in range(SUBLANES):
        ref[pl.ds(j, n, stride=SUBLANES), :] = x[:, j * LANES:(j + 1) * LANES]


def _load_slab(ref, n):
    return jnp.concatenate([ref[pl.ds(j, n, stride=SUBLANES), :] for j in range(SUBLANES)], axis=1)


def _start_token_gather(idx_ref, n_tokens, src_hbm, dst, sem):
    def body(r, carry):
        src_row = pl.multiple_of(idx_ref[0, 0, r] * SUBLANES, SUBLANES)
        dst_row = pl.multiple_of(r * SUBLANES, SUBLANES)
        pltpu.make_async_copy(src_hbm.at[pl.ds(src_row, SUBLANES), :],
                              dst.at[pl.ds(dst_row, SUBLANES), :], sem).start(priority=1)
        return carry
    lax.fori_loop(0, n_tokens, body, 0, unroll=8)


def _wait_token_gather(n_tokens, src_hbm, dst, sem):
    pltpu.make_async_copy(src_hbm.at[pl.ds(0, n_tokens * SUBLANES), :], dst, sem).wait()


C_GQ = 0
C_GK = C_GQ + GLA_QK
C_GV = C_GK + GLA_QK
C_GR = C_GV + GLA_V
C_AL = C_GR + GLA_V
C_GLA_END = C_AL + GLA_LOWRANK
C_DQ = 0
C_DK = C_DQ + DIFF_W
C_DV = C_DK + DIFF_W
C_END = C_DV + DIFF_W


def _inproj_kernel(x_ref, pos_ref, g_ref, wgla_ref, wdiff_ref, wa2_ref, ba_ref, inv_ref, sel_ref,
                   one_ref,
                   gq_ref, gk_ref, gv_ref, gr_ref, la_ref, dq_ref, dk_ref, dv_ref):
    nb = _rms(x_ref[...], g_ref[...]).astype(BF16)

    def proj_gla(c0, c1):
        return jnp.dot(nb, wgla_ref[:, c0:c1], preferred_element_type=F32)

    def proj(c0, c1):
        return jnp.dot(nb, wdiff_ref[:, c0:c1], preferred_element_type=F32)

    gq_ref[...] = proj_gla(C_GQ, C_GK) * (GLA_DK ** -0.5)
    gk_ref[...] = proj_gla(C_GK, C_GV)
    gv_ref[...] = proj_gla(C_GV, C_GR).astype(BF16)
    gr_ref[...] = proj_gla(C_GR, C_AL).astype(BF16)

    gal = proj_gla(C_AL, C_GLA_END).astype(BF16)
    a_logit = jnp.dot(gal, wa2_ref[...], preferred_element_type=F32) + ba_ref[...]
    la_ref[...] = (jnp.minimum(a_logit, 0.0) - jnp.log1p(jnp.exp(-jnp.abs(a_logit)))) / GLA_TAU

    ang = inv_ref[...] * pos_ref[0].astype(F32)
    cs = jnp.concatenate([jnp.cos(ang), jnp.sin(ang)], axis=0)

    tabs = None
    rest = cs
    for _ in range(3):
        piece = rest.astype(BF16)
        rest = rest - piece.astype(F32)
        t = lax.dot_general(piece, sel_ref[...], TN_DIMS, preferred_element_type=F32)
        tabs = t if tabs is None else tabs + t
    cosf = tabs[:, :LANES] + one_ref[...]
    sneg = tabs[:, LANES:2 * LANES]
    spos = tabs[:, 2 * LANES:]

    def rope_store(c0, out_ref, scale):
        for j in range(DIFF_W // MXU_N):
            pair = proj(c0 + j * MXU_N, c0 + (j + 1) * MXU_N)
            for i in range(MXU_N // LANES):
                blk = pair[:, i * LANES:(i + 1) * LANES]
                rot = (blk * cosf + pltpu.roll(blk, LANES - ROPE_HALF, 1) * sneg
                       + pltpu.roll(blk, ROPE_HALF, 1) * spos)
                c = j * MXU_N + i * LANES
                out_ref[:, c:c + LANES] = (rot * scale).astype(BF16)

    rope_store(C_DQ, dq_ref, DIFF_DH ** -0.5)
    rope_store(C_DK, dk_ref, 1.0)
    dv_ref[...] = proj(C_DV, C_END).astype(BF16)


def _rope_tables():
    lane = np.arange(LANES)
    r = lane % DIFF_DH
    selc = np.zeros((2 * ROPE_HALF, LANES), np.float32)
    seln = np.zeros_like(selc)
    selp = np.zeros_like(selc)
    one = np.zeros((1, LANES), np.float32)
    for l in range(LANES):
        if r[l] < ROPE_DIM:
            selc[r[l] % ROPE_HALF, l] = 1.0
            if r[l] < ROPE_HALF:
                seln[ROPE_HALF + r[l], l] = -1.0
            else:
                selp[ROPE_HALF + r[l] - ROPE_HALF, l] = 1.0
        else:
            one[0, l] = 1.0
    sel = np.concatenate([selc, seln, selp], axis=1)
    return jnp.asarray(sel, dtype=BF16), jnp.asarray(one)


def _inproj(x2, positions, attn_norm, wgla, wdiff, wa2, b_a, *, tm):
    T, D = x2.shape
    nt = T // tm
    pos3 = positions.reshape(nt, 1, tm)
    inv = (ROPE_THETA ** (-jnp.arange(0, ROPE_DIM, 2, dtype=F32) / ROPE_DIM)).reshape(ROPE_HALF, 1)
    sel, one = _rope_tables()
    row = lambda i: (i, 0)
    fixed = lambda i: (0, 0)
    out_shapes = (
        jax.ShapeDtypeStruct((T, GLA_QK), F32), jax.ShapeDtypeStruct((T, GLA_QK), F32),
        jax.ShapeDtypeStruct((T, GLA_V), BF16), jax.ShapeDtypeStruct((T, GLA_V), BF16),
        jax.ShapeDtypeStruct((T, GLA_QK), F32),
        jax.ShapeDtypeStruct((T, DIFF_W), BF16), jax.ShapeDtypeStruct((T, DIFF_W), BF16),
        jax.ShapeDtypeStruct((T, DIFF_W), BF16),
    )
    return pl.pallas_call(
        _inproj_kernel,
        grid=(nt,),
        in_specs=[
            pl.BlockSpec((tm, D), row),
            pl.BlockSpec((1, 1, tm), lambda i: (i, 0, 0)),
            pl.BlockSpec((1, D), fixed),
            pl.BlockSpec((D, C_GLA_END), fixed),
            pl.BlockSpec((D, C_END), fixed),
            pl.BlockSpec((GLA_LOWRANK, GLA_QK), fixed),
            pl.BlockSpec((1, GLA_QK), fixed),
            pl.BlockSpec((ROPE_HALF, 1), fixed),
            pl.BlockSpec((2 * ROPE_HALF, 3 * LANES), fixed),
            pl.BlockSpec((1, LANES), fixed),
        ],
        out_specs=[pl.BlockSpec((tm, s.shape[1]), row) for s in out_shapes],
        out_shape=out_shapes,
        compiler_params=_cparams(("parallel",)),
        name="inproj",
    )(x2, pos3, attn_norm.reshape(1, D), wgla, wdiff, wa2, b_a.reshape(1, GLA_QK), inv, sel, one)


def _gla_kernel(gq_ref, gk_ref, gv_ref, gr_ref, la_ref, gn_ref, oa_ref, st_ref, *, n_chunks):
    @pl.when(pl.program_id(1) == 0)
    def _():
        st_ref[...] = jnp.zeros_like(st_ref)

    C = GLA_CHUNK
    tril = lax.broadcasted_iota(jnp.int32, (C, C), 0) >= lax.broadcasted_iota(jnp.int32, (C, C), 1)
    trilf = tril.astype(F32)
    head_of_lane = lax.broadcasted_iota(jnp.int32, (1, GLA_QK), 1) // GLA_DK
    hmask = [head_of_lane == h for h in range(GLA_HEADS)]
    gn = gn_ref[...]

    for c in range(n_chunks):
        sl = pl.ds(c * C, C)
        q = gq_ref[0, sl, :]
        k = gk_ref[0, sl, :]
        la = la_ref[0, sl, :]
        v = gv_ref[0, sl, :]
        b = jnp.dot(trilf, la, precision=lax.Precision.HIGHEST, preferred_element_type=F32)
        b_last = b[C - 1:C, :]
        q_dec = (q * jnp.exp(b)).astype(BF16)
        k_inv = (k * jnp.exp(-b)).astype(BF16)
        k_end = (k * jnp.exp(b_last - b)).astype(BF16)
        decay = jnp.exp(b_last)

        q_heads = jnp.concatenate([jnp.where(hmask[h], q_dec, jnp.zeros_like(q_dec))
                                   for h in range(GLA_HEADS)], axis=0)
        att = lax.dot_general(q_heads, k_inv, NT_DIMS, preferred_element_type=F32)
        st = st_ref[...]
        inter = lax.dot_general(q_heads, st.astype(BF16), NT_DIMS,
                                preferred_element_type=F32)
        u = jnp.zeros_like(st)
        for h in range(GLA_HEADS):
            rows = slice(h * C, (h + 1) * C)
            cols = slice(h * GLA_DV, (h + 1) * GLA_DV)
            a_h = jnp.where(tril, att[rows], 0.0).astype(BF16)
            v_h = v[:, cols]
            o_h = jnp.dot(a_h, v_h, preferred_element_type=F32) + inter[rows]
            y = _rms(o_h, gn)
            r = gr_ref[0, sl, cols].astype(F32)
            oa_ref[0, sl, cols] = (y * (r * jax.nn.sigmoid(r))).astype(BF16)
            u_h = lax.dot_general(v_h, k_end, TN_DIMS, preferred_element_type=F32)
            u = u + jnp.where(hmask[h], u_h, 0.0)
        st_ref[...] = st * decay + u


def _gla(gq, gk, gv, gr, la, gla_norm, *, B, S, ts):
    n_chunks = ts // GLA_CHUNK
    blk = lambda w: pl.BlockSpec((1, ts, w), lambda b, j: (b, j, 0))
    r3 = lambda a: a.reshape(B, S, a.shape[-1])
    return pl.pallas_call(
        functools.partial(_gla_kernel, n_chunks=n_chunks),
        grid=(B, S // ts),
        in_specs=[blk(GLA_QK), blk(GLA_QK), blk(GLA_V), blk(GLA_V), blk(GLA_QK),
                  pl.BlockSpec((1, GLA_DV), lambda b, j: (0, 0))],
        out_specs=blk(GLA_V),
        out_shape=jax.ShapeDtypeStruct((B, S, GLA_V), BF16),
        scratch_shapes=[pltpu.VMEM((GLA_DV, GLA_QK), F32)],
        compiler_params=_cparams(("parallel", "arbitrary")),
        name="gla",
    )(r3(gq), r3(gk), r3(gv), r3(gr), r3(la), gla_norm.reshape(1, GLA_DV))


def _diffattn_kernel(dq_ref, dk_ref, dv_ref, lq1_ref, lk1_ref, lq2_ref, lk2_ref, sub_ref,
                     ob_ref, m_ref, l_ref, acc_ref, *, tq, tk, lambda_init):
    S = dq_ref.shape[1]
    nq = S // tq
    R = 2 * tq
    lam = (jnp.exp(jnp.sum(lq1_ref[...] * lk1_ref[...], axis=-1, keepdims=True))
           - jnp.exp(jnp.sum(lq2_ref[...] * lk2_ref[...], axis=-1, keepdims=True))
           + lambda_init)
    first_comp = lax.broadcasted_iota(jnp.int32, (1, DIFF_DV), 1) < DIFF_DH
    sub = sub_ref[...]

    for h in range(DIFF_HEADS):
        cols = slice(h * DIFF_DV, (h + 1) * DIFF_DV)
        for qi in range(nq):
            qb = dq_ref[0, qi * tq:(qi + 1) * tq, cols]
            zero = jnp.zeros_like(qb)
            qs = jnp.concatenate([jnp.where(first_comp, qb, zero),
                                  jnp.where(first_comp, zero, qb)], axis=0)
            m_ref[...] = jnp.full(m_ref.shape, NEG_BIG, F32)
            l_ref[...] = jnp.zeros(l_ref.shape, F32)
            acc_ref[...] = jnp.zeros(acc_ref.shape, F32)

            def scores(kstart, qs=qs, cols=cols):
                kb = dk_ref[0, kstart:kstart + tk, cols]
                return lax.dot_general(kb, qs, NT_DIMS, preferred_element_type=F32)

            def update(s, kstart, masked, cols=cols, qi=qi):
                vb = dv_ref[0, kstart:kstart + tk, cols]
                if masked:
                    kpos = kstart + lax.broadcasted_iota(jnp.int32, (tk, R), 0)
                    qpos = qi * tq + lax.broadcasted_iota(jnp.int32, (tk, R), 1) % tq
                    s = jnp.where(qpos >= kpos, s, NEG_BIG)
                m_prev = m_ref[...]
                m_new = jnp.maximum(m_prev, jnp.max(s, axis=0, keepdims=True))
                alpha = jnp.exp(m_prev - m_new)
                p = jnp.exp(s - m_new)
                l_ref[...] = alpha * l_ref[...] + jnp.sum(p, axis=0, keepdims=True)
                pv = lax.dot_general(vb, p.astype(BF16), TN_DIMS, preferred_element_type=F32)
                acc_ref[...] = alpha * acc_ref[...] + pv
                m_ref[...] = m_new

            n_blocks = (qi + 1) * tq // tk
            diag_from = qi * tq // tk
            s_next = scores(0)
            for kb_i in range(n_blocks):
                s_cur = s_next
                if kb_i + 1 < n_blocks:
                    s_next = scores((kb_i + 1) * tk)
                update(s_cur, kb_i * tk, kb_i >= diag_from)

            o_all = acc_ref[...] / l_ref[...]
            o = o_all[:, :tq] - lam * o_all[:, tq:]
            y = o * lax.rsqrt(jnp.mean(o * o, axis=0, keepdims=True) + EPS) * sub
            y = y * (1.0 - lambda_init)
            ob_ref[0, qi * tq:(qi + 1) * tq, cols] = y.T.astype(BF16)


def _diffattn(dq, dk, dv, lq1, lk1, lq2, lk2, diff_subln, *, B, S, tq, tk, lambda_init):
    seq = pl.BlockSpec((1, S, DIFF_W), lambda b: (b, 0, 0))
    vec = lambda w: pl.BlockSpec((1, w), lambda b: (0, 0))
    r3 = lambda a: a.reshape(B, S, DIFF_W)
    return pl.pallas_call(
        functools.partial(_diffattn_kernel, tq=tq, tk=tk, lambda_init=lambda_init),
        grid=(B,),
        in_specs=[seq, seq, seq, vec(DIFF_DH), vec(DIFF_DH), vec(DIFF_DH), vec(DIFF_DH),
                  pl.BlockSpec((DIFF_DV, 1), lambda b: (0, 0))],
        out_specs=seq,
        out_shape=jax.ShapeDtypeStruct((B, S, DIFF_W), BF16),
        scratch_shapes=[pltpu.VMEM((1, 2 * tq), F32), pltpu.VMEM((1, 2 * tq), F32),
                        pltpu.VMEM((DIFF_DV, 2 * tq), F32)],
        compiler_params=_cparams(("parallel",)),
        name="diffattn",
    )(r3(dq), r3(dk), r3(dv), lq1.reshape(1, -1), lk1.reshape(1, -1), lq2.reshape(1, -1),
      lk2.reshape(1, -1), diff_subln.reshape(-1, 1))


ROUTE_W = 8


def _first_index_of(mask, lane):
    return jnp.min(jnp.where(mask, lane, LANES), axis=-1, keepdims=True)


POSTMIX_SPLIT = 2


def _postmix_kernel(x_ref, oa_ref, ob_ref, g_ref, wg_ref, wba_ref, wbb_ref, wo_ref, fn_ref,
                    wr_ref, br_ref, h_ref, n2_ref, route_ref, cnt_ref):
    rows_per = x_ref.shape[0] // POSTMIX_SPLIT
    cnt = None
    for s in range(POSTMIX_SPLIT):
        rows = slice(s * rows_per, (s + 1) * rows_per)
        h, n2, rec, c = _postmix_rows(x_ref[rows, :], oa_ref[rows, :], ob_ref[rows, :], g_ref, wg_ref,
                                      wba_ref, wbb_ref, wo_ref, fn_ref, wr_ref, br_ref)
        h_ref[rows, :] = h
        n2_ref[rows, :] = n2
        route_ref[rows, :] = rec
        cnt = c if cnt is None else cnt + c
    cnt_ref[0] = cnt


def _postmix_rows(x, oa, ob, g_ref, wg_ref, wba_ref, wbb_ref, wo_ref, fn_ref, wr_ref, br_ref):
    D = x.shape[1]
    nb = _rms(x, g_ref[...]).astype(BF16)
    y_a = jnp.dot(oa, wba_ref[...], preferred_element_type=F32)
    g_a = jnp.dot(nb, wg_ref[:, :D], preferred_element_type=F32)
    merged = jax.nn.sigmoid(g_a) * y_a
    y_b = jnp.dot(ob, wbb_ref[...], preferred_element_type=F32)
    g_b = jnp.dot(nb, wg_ref[:, D:], preferred_element_type=F32)
    merged = merged + jax.nn.sigmoid(g_b) * y_b
    h = x + jnp.dot(merged.astype(BF16), wo_ref[...], preferred_element_type=F32)
    n2 = _rms(h, fn_ref[...])

    n2_hi = n2.astype(BF16)
    n2_lo = (n2 - n2_hi.astype(F32)).astype(BF16)
    hi_prod = jnp.dot(n2_hi, wr_ref[...], preferred_element_type=F32)
    lg = (hi_prod[:, :LANES] + hi_prod[:, LANES:]
          + jnp.dot(n2_lo, wr_ref[:, :LANES], preferred_element_type=F32)) + br_ref[...]
    lane = lax.broadcasted_iota(jnp.int32, lg.shape, 1)
    is_g = lane < N_GROUPS
    g_max = jnp.max(jnp.where(is_g, lg, -jnp.inf), axis=-1, keepdims=True)
    g_exp = jnp.where(is_g, jnp.exp(lg - g_max), 0.0)
    g_prob = g_exp / jnp.sum(g_exp, axis=-1, keepdims=True)
    g_p = jnp.max(g_prob, axis=-1, keepdims=True)
    g_idx = _first_index_of(is_g & (g_prob == g_p), lane)

    e_lo = N_GROUPS + EXPERTS_PER_GROUP * g_idx
    is_e = (lane >= e_lo) & (lane < e_lo + EXPERTS_PER_GROUP)
    e_max = jnp.max(jnp.where(is_e, lg, -jnp.inf), axis=-1, keepdims=True)
    e_exp = jnp.where(is_e, jnp.exp(lg - e_max), 0.0)
    e_prob = e_exp / jnp.sum(e_exp, axis=-1, keepdims=True)
    p1 = jnp.max(jnp.where(is_e, e_prob, -1.0), axis=-1, keepdims=True)
    i1 = _first_index_of(is_e & (e_prob == p1), lane)
    rest = is_e & (lane != i1)
    p2 = jnp.max(jnp.where(rest, e_prob, -1.0), axis=-1, keepdims=True)
    i2 = _first_index_of(rest & (e_prob == p2), lane)
    den = p1 + p2
    w1 = g_p * (p1 / den)
    w2 = g_p * (p2 / den)
    rec = jnp.where(lane == 0, w1, 0.0)
    rec = jnp.where(lane == 1, w2, rec)
    rec = jnp.where(lane == 2, (i1 - N_GROUPS).astype(F32), rec)
    rec = jnp.where(lane == 3, (i2 - N_GROUPS).astype(F32), rec)
    chosen = (lane == i1 - N_GROUPS) | (lane == i2 - N_GROUPS)
    cnt = jnp.sum(chosen.astype(F32), axis=0, keepdims=True)
    return h, n2.astype(BF16), rec[:, :ROUTE_W], cnt


def _postmix(x2, oa, ob, attn_norm, wg, wba, wbb, wo, ffn_norm, wr, br, *, tm):
    T, D = x2.shape
    row = lambda i: (i, 0)
    fixed = lambda i: (0, 0)
    full = lambda a: pl.BlockSpec(a.shape, fixed)
    g = attn_norm.reshape(1, D)
    fn = ffn_norm.reshape(1, D)
    return pl.pallas_call(
        _postmix_kernel,
        grid=(T // tm,),
        in_specs=[pl.BlockSpec((tm, D), row), pl.BlockSpec((tm, GLA_V), row),
                  pl.BlockSpec((tm, DIFF_W), row), full(g), full(wg), full(wba), full(wbb),
                  full(wo), full(fn), full(wr), full(br)],
        out_specs=[pl.BlockSpec((tm, D), row), pl.BlockSpec((tm, D), row),
                   pl.BlockSpec((tm, ROUTE_W), row),
                   pl.BlockSpec((1, 1, LANES), lambda i: (i, 0, 0))],
        out_shape=(jax.ShapeDtypeStruct((T, D), F32), jax.ShapeDtypeStruct((T, D), BF16),
                   jax.ShapeDtypeStruct((T, ROUTE_W), F32),
                   jax.ShapeDtypeStruct((T // tm, 1, LANES), F32)),
        compiler_params=_cparams(("parallel",)),
        name="postmix",
    )(x2, oa, ob, g, wg, wba, wbb, wo, fn, wr, br)


DISPATCH_CHUNK = 8
POS_W = 8


def _token_copy(src, src_tok, dst, dst_tok, n_tok, sem):
    return pltpu.make_async_copy(
        src.at[pl.ds(pl.multiple_of(src_tok * SUBLANES, SUBLANES), n_tok * SUBLANES), :],
        dst.at[pl.ds(pl.multiple_of(dst_tok * SUBLANES, SUBLANES), n_tok * SUBLANES), :], sem)


def _for_each_piece(length, fn):
    n_full = lax.shift_right_logical(length, 3)

    def body(c, carry):
        fn(c * DISPATCH_CHUNK, DISPATCH_CHUNK)
        return carry

    lax.fori_loop(0, n_full, body, 0)
    off = n_full * DISPATCH_CHUNK
    for n in (4, 2, 1):
        has = (length & n) != 0

        @pl.when(has)
        def _(off=off, n=n):
            fn(off, n)

        off = off + jnp.where(has, n, 0)


def _dispatch_kernel(seg_ref, lo_ref, cnt_ref, fill_ref, fill_len_ref, n_tiles_ref,
                     n2_ref, route_ref, lo_row_ref, seg_row_ref,
                     xg_hbm, pos_ref, xs_buf, zero_buf, sem, fill_sem):
    tm = n2_ref.shape[0]
    n_slab = 2 * tm * SUBLANES
    i = pl.program_id(0)
    nt = pl.num_programs(0)
    slot = i % 2

    @pl.when(i == 0)
    def _():
        zero_buf[...] = jnp.zeros_like(zero_buf)
        te = zero_buf.shape[0] // SUBLANES
        for wait in (False, True):
            def fill(e, carry, wait=wait):
                def piece(off, n):
                    cp = _token_copy(zero_buf, 0, xg_hbm, fill_ref[e] + off, n, fill_sem)
                    cp.wait() if wait else cp.start()
                _for_each_piece(fill_len_ref[e], piece)
                return carry
            lax.fori_loop(0, N_EXPERTS, fill, 0)

            def fill_tile(t, carry, wait=wait):
                cp = _token_copy(zero_buf, 0, xg_hbm, t * te, te, fill_sem)
                cp.wait() if wait else cp.start()
                return carry
            lax.fori_loop(n_tiles_ref[0], xg_hbm.shape[0] // zero_buf.shape[0], fill_tile, 0)

    route = route_ref[...]
    lane = lax.broadcasted_iota(jnp.int32, (tm, LANES), 1)
    oh1 = lane == route[:, 2:3].astype(jnp.int32)
    oh2 = lane == route[:, 3:4].astype(jnp.int32)
    both = jnp.where(oh1 | oh2, 1.0, 0.0).astype(BF16)
    earlier = (lax.broadcasted_iota(jnp.int32, (tm, tm), 0)
               > lax.broadcasted_iota(jnp.int32, (tm, tm), 1))
    rank = jnp.dot(jnp.where(earlier, 1.0, 0.0).astype(BF16), both, preferred_element_type=F32)

    def pick(onehot, v):
        return jnp.sum(jnp.where(onehot, v, 0.0), axis=-1, keepdims=True)

    r1 = pick(oh1, rank)
    r2 = pick(oh2, rank)
    q1 = (pick(oh1, lo_row_ref[0]) + r1).astype(jnp.int32)
    q2 = (pick(oh2, lo_row_ref[0]) + r2).astype(jnp.int32)
    pos1 = (pick(oh1, seg_row_ref[0]) + r1).astype(jnp.int32)
    pos2 = (pick(oh2, seg_row_ref[0]) + r2).astype(jnp.int32)
    rec = jnp.where(lane == 0, pos1, jnp.where(lane == 1, pos2, 0))
    pos_ref[...] = rec[:, :POS_W]

    col = lax.broadcasted_iota(jnp.int32, (tm, 2 * tm), 1)
    perm_t = jnp.where((col == q1) | (col == q2), 1.0, 0.0).astype(BF16)
    xs = lax.dot_general(perm_t, n2_ref[...], TN_DIMS, preferred_element_type=F32)

    def wait_slot(s):
        pltpu.make_async_copy(xg_hbm.at[pl.ds(0, n_slab), :], xs_buf.at[s], sem.at[s]).wait()

    @pl.when(i >= 2)
    def _():
        wait_slot(slot)

    _store_slab(xs_buf.at[slot], xs)

    def send(e, carry):
        k = i * N_EXPERTS + e

        def piece(off, n):
            _token_copy(xs_buf.at[slot], lo_ref[k] + off, xg_hbm, seg_ref[k] + off, n,
                        sem.at[slot]).start()

        _for_each_piece(cnt_ref[k], piece)
        return carry

    lax.fori_loop(0, N_EXPERTS, send, 0)

    @pl.when(i == nt - 1)
    def _():
        wait_slot(slot)

        @pl.when(nt >= 2)
        def _():
            wait_slot(1 - slot)


def _dispatch(n2, route, cnt, *, tm, te):
    T, D = n2.shape
    nt = T // tm
    n_tiles_max = 2 * T // te + N_EXPERTS
    c = cnt[:, 0, :N_EXPERTS].astype(jnp.int32)
    count = jnp.sum(c, axis=0)
    padded = (count + te - 1) // te * te
    p_end = jnp.cumsum(padded)
    p_start = p_end - padded
    n_tiles = (p_end[-1] // te).astype(jnp.int32).reshape(1)
    tile_start = jnp.arange(n_tiles_max, dtype=jnp.int32) * te
    tile_e = jnp.sum(tile_start[:, None] >= p_end[None, :], axis=1, dtype=jnp.int32)
    tile_e = jnp.minimum(tile_e, jnp.sum(tile_start[n_tiles[0] - 1] >= p_end, dtype=jnp.int32))
    seg = p_start[None, :] + jnp.cumsum(c, axis=0) - c
    lo = jnp.cumsum(c, axis=1) - c
    lanes = lambda a: jnp.pad(a.astype(F32), ((0, 0), (0, LANES - N_EXPERTS))).reshape(nt, 1, LANES)
    flat = lambda a: a.reshape(-1).astype(jnp.int32)
    row = lambda i, *_: (i, 0)
    per_tile = pl.BlockSpec((1, 1, LANES), lambda i, *_: (i, 0, 0))
    grid_spec = pltpu.PrefetchScalarGridSpec(
        num_scalar_prefetch=6,
        grid=(nt,),
        in_specs=[pl.BlockSpec((tm, D), row), pl.BlockSpec((tm, ROUTE_W), row), per_tile, per_tile],
        out_specs=[pl.BlockSpec(memory_space=pl.ANY), pl.BlockSpec((tm, POS_W), row)],
        scratch_shapes=[pltpu.VMEM((2, 2 * tm * SUBLANES, LANES), F32),
                        pltpu.VMEM((te * SUBLANES, LANES), F32),
                        pltpu.SemaphoreType.DMA((2,)), pltpu.SemaphoreType.DMA(())],
    )
    xg, pos = pl.pallas_call(
        _dispatch_kernel,
        grid_spec=grid_spec,
        out_shape=(jax.ShapeDtypeStruct((n_tiles_max * te * SUBLANES, LANES), F32),
                   jax.ShapeDtypeStruct((T, POS_W), jnp.int32)),
        compiler_params=_cparams(("arbitrary",)),
        name="dispatch",
    )(flat(seg), flat(lo), flat(c), flat(p_start + count), flat(padded - count), n_tiles,
      n2, route, lanes(lo), lanes(seg))
    return xg, pos, tile_e, n_tiles


def _experts_kernel(tile_e_ref, n_tiles_ref, xg_ref, wg_ref, wu_ref, wd_ref, y_ref):
    tm = xg_ref.shape[0] // SUBLANES
    i = pl.program_id(0)

    @pl.when(i < n_tiles_ref[0])
    def _():
        x = _load_slab(xg_ref, tm).astype(BF16)
        g = jnp.dot(x, wg_ref[0], preferred_element_type=F32)
        u = jnp.dot(x, wu_ref[0], preferred_element_type=F32)
        he = (g * jax.nn.sigmoid(g)) * u
        _store_slab(y_ref, jnp.dot(he.astype(BF16), wd_ref[0], preferred_element_type=F32))

    @pl.when(i >= n_tiles_ref[0])
    def _():
        y_ref[...] = jnp.zeros_like(y_ref)


def _experts(tile_e, n_tiles, xg, wg, wu, wd, *, tm):
    D = wg.shape[1]
    n_tiles_max = xg.shape[0] // (tm * SUBLANES)
    slab = lambda f: pl.BlockSpec((tm * SUBLANES, LANES), f)
    grid_spec = pltpu.PrefetchScalarGridSpec(
        num_scalar_prefetch=2,
        grid=(n_tiles_max,),
        in_specs=[slab(lambda i, te, nt: (jnp.minimum(i, nt[0] - 1), 0)),
                  pl.BlockSpec((1, D, D_EXPERT), lambda i, te, nt: (te[i], 0, 0)),
                  pl.BlockSpec((1, D, D_EXPERT), lambda i, te, nt: (te[i], 0, 0)),
                  pl.BlockSpec((1, D_EXPERT, D), lambda i, te, nt: (te[i], 0, 0))],
        out_specs=slab(lambda i, te, nt: (i, 0)),
    )
    return pl.pallas_call(
        _experts_kernel,
        grid_spec=grid_spec,
        out_shape=jax.ShapeDtypeStruct(xg.shape, F32),
        compiler_params=_cparams(("arbitrary",)),
        name="experts",
    )(tile_e, n_tiles, xg, wg, wu, wd)


def _final_kernel(idx_ref, idx_next_ref, h_ref, ys_hbm, route_ref, p_ref, wple_ref, pn_ref, wpg_ref,
                  fn_ref, o_ref, ybuf, sem, *, apply_final_norm):
    tm = h_ref.shape[0]
    i = pl.program_id(0)
    slot = i % 2

    @pl.when(i == 0)
    def _():
        _start_token_gather(idx_ref, 2 * tm, ys_hbm, ybuf.at[0], sem.at[0])

    @pl.when(i + 1 < pl.num_programs(0))
    def _():
        _start_token_gather(idx_next_ref, 2 * tm, ys_hbm, ybuf.at[1 - slot], sem.at[1 - slot])

    _wait_token_gather(2 * tm, ys_hbm, ybuf.at[slot], sem.at[slot])
    route = route_ref[...]
    y = route[:, 0:1] * _load_slab(ybuf.at[slot, pl.ds(0, tm * SUBLANES), :], tm)
    y = y + route[:, 1:2] * _load_slab(ybuf.at[slot, pl.ds(tm * SUBLANES, tm * SUBLANES), :], tm)
    h = h_ref[...] + y
    e = _rms(jnp.dot(p_ref[...].astype(BF16), wple_ref[...], preferred_element_type=F32), pn_ref[...])
    gate = jax.nn.sigmoid(jnp.dot(h.astype(BF16), wpg_ref[...], preferred_element_type=F32))
    h = h + gate * e
    if apply_final_norm:
        h = _rms(h, fn_ref[...])
    o_ref[...] = h


def _final(h1, ys, pos, route, p2, wple, ple_norm, wpg, final_norm, *, tm, apply_final_norm):
    T, D = h1.shape
    nt = T // tm
    row = lambda i: (i, 0)
    fixed = lambda i: (0, 0)
    full = lambda a: pl.BlockSpec(a.shape, fixed)
    pn = ple_norm.reshape(1, D)
    fn = final_norm.reshape(1, D)
    idx = pos[:, :2].reshape(nt, tm, 2).transpose(0, 2, 1).reshape(nt, 1, 2 * tm)
    smem_rows = lambda f: pl.BlockSpec((1, 1, 2 * tm), f, memory_space=pltpu.SMEM)
    return pl.pallas_call(
        functools.partial(_final_kernel, apply_final_norm=apply_final_norm),
        grid=(nt,),
        in_specs=[smem_rows(lambda i: (i, 0, 0)),
                  smem_rows(lambda i: (jnp.minimum(i + 1, nt - 1), 0, 0)),
                  pl.BlockSpec((tm, D), row), pl.BlockSpec(memory_space=pl.ANY),
                  pl.BlockSpec((tm, ROUTE_W), row), pl.BlockSpec((tm, p2.shape[1]), row),
                  full(wple), full(pn), full(wpg), full(fn)],
        out_specs=pl.BlockSpec((tm, D), row),
        out_shape=jax.ShapeDtypeStruct((T, D), F32),
        scratch_shapes=[pltpu.VMEM((2, 2 * tm * SUBLANES, LANES), F32),
                        pltpu.SemaphoreType.DMA((2,))],
        compiler_params=_cparams(("arbitrary",)),
        name="final",
    )(idx, idx, h1, ys, route, p2, wple, pn, wpg, fn)


SPLIT_SIZES = (GLA_QK, GLA_QK, GLA_V, GLA_V, GLA_LOWRANK, DIFF_W, DIFF_W, DIFF_W)


def _layer(h2, p2, positions, B, S, lambda_init, apply_final_norm, attn_norm, w_in, w_a2, b_a,
           gla_norm, lq1, lk1, lq2, lk2, diff_subln, w_branch_a, w_branch_b, w_out, ffn_norm,
           w_rg, b_rg, w_re, b_re, w_gate, w_up, w_down, w_ple, ple_norm, w_ple_gate, final_norm,
           *, tm, ts, tq, tk, te):
    T, D = h2.shape
    assert D == SUBLANES * LANES, "token-slab gathers need one (8, 128) tile per token"
    n_mix = sum(SPLIT_SIZES)
    wgla = w_in[:, :C_GLA_END].astype(BF16)
    wdiff = w_in[:, C_GLA_END:n_mix].astype(BF16)
    wg = w_in[:, n_mix:].astype(BF16)

    gq, gk, gv, gr, la, dq, dk, dv = _inproj(h2, positions, attn_norm, wgla, wdiff,
                                             w_a2.astype(BF16), b_a, tm=tm)
    oa = _gla(gq, gk, gv, gr, la, gla_norm, B=B, S=S, ts=ts).reshape(T, GLA_V)
    ob = _diffattn(dq, dk, dv, lq1, lk1, lq2, lk2, diff_subln, B=B, S=S, tq=tq, tk=tk,
                   lambda_init=lambda_init).reshape(T, DIFF_W)

    wr = jnp.concatenate([w_rg, w_re, jnp.zeros((D, LANES - N_GROUPS - N_EXPERTS), F32)], axis=1)
    wr_hi = wr.astype(BF16)
    wr = jnp.concatenate([wr_hi, (wr - wr_hi.astype(F32)).astype(BF16)], axis=1)
    br = jnp.concatenate([b_rg, b_re, jnp.zeros((LANES - N_GROUPS - N_EXPERTS,), F32)]).reshape(1, LANES)
    h1, n2, route, cnt = _postmix(h2, oa, ob, attn_norm, wg, w_branch_a.astype(BF16),
                                  w_branch_b.astype(BF16), w_out.astype(BF16), ffn_norm, wr, br,
                                  tm=tm)

    xg, pos, tile_e, n_tiles = _dispatch(n2, route, cnt, tm=tm, te=te)
    ys = _experts(tile_e, n_tiles, xg, w_gate.astype(BF16), w_up.astype(BF16),
                  w_down.astype(BF16), tm=te)
    return _final(h1, ys, pos, route, p2, w_ple.astype(BF16), ple_norm, w_ple_gate.astype(BF16),
                  final_norm, tm=tm, apply_final_norm=apply_final_norm)


def _block(x, p, positions, attn_norm, w_in, w_a2, b_a, gla_norm, lambda_q1, lambda_k1, lambda_q2,
           lambda_k2, diff_subln, w_branch_a, w_branch_b, w_out, ffn_norm, w_router_group,
           b_router_group, w_router_expert, b_router_expert, w_gate, w_up, w_down, w_ple, ple_norm,
           w_ple_gate, final_norm, *, tm, ts, tq, tk, te):
    B, S, D = x.shape
    depth = w_in.shape[0]
    h = x.reshape(B * S, D)
    for i in range(depth):
        lambda_init = 0.8 - 0.6 * math.exp(-0.3 * i)
        h = _layer(h, p[i].reshape(B * S, -1), positions, B, S, lambda_init, i == depth - 1,
                   attn_norm[i], w_in[i], w_a2[i], b_a[i], gla_norm[i], lambda_q1[i], lambda_k1[i],
                   lambda_q2[i], lambda_k2[i], diff_subln[i], w_branch_a[i], w_branch_b[i], w_out[i],
                   ffn_norm[i], w_router_group[i], b_router_group[i], w_router_expert[i],
                   b_router_expert[i], w_gate[i], w_up[i], w_down[i], w_ple[i], ple_norm[i],
                   w_ple_gate[i], final_norm, tm=tm, ts=ts, tq=tq, tk=tk, te=te)
    return h.reshape(B, S, D)


def kernel(x, p, positions, attn_norm, w_in, w_a2, b_a, gla_norm, lambda_q1, lambda_k1, lambda_q2, lambda_k2, diff_subln, w_branch_a, w_branch_b, w_out, ffn_norm, w_router_group, b_router_group, w_router_expert, b_router_expert, w_gate, w_up, w_down, w_ple, ple_norm, w_ple_gate, final_norm):
    S = x.shape[1]
    return _block(x, p, positions, attn_norm, w_in, w_a2, b_a, gla_norm, lambda_q1, lambda_k1,
                  lambda_q2, lambda_k2, diff_subln, w_branch_a, w_branch_b, w_out, ffn_norm,
                  w_router_group, b_router_group, w_router_expert, b_router_expert, w_gate, w_up,
                  w_down, w_ple, ple_norm, w_ple_gate, final_norm,
                  tm=512, ts=min(512, S), tq=min(512, S), tk=min(512, S), te=512)
```

```python
import functools
import math

import jax
import jax.numpy as jnp
import numpy as np
from jax import lax
from jax.experimental import pallas as pl
from jax.experimental.pallas import tpu as pltpu

EPS = 1e-6

GLA_HEADS = 4
GLA_DK = 64
GLA_DV = 128
GLA_LOWRANK = 16
GLA_TAU = 16.0
GLA_CHUNK = 64
GLA_QK = GLA_HEADS * GLA_DK
GLA_V = GLA_HEADS * GLA_DV

DIFF_HEADS = 4
DIFF_DH = 64
DIFF_DV = 2 * DIFF_DH
DIFF_W = DIFF_HEADS * DIFF_DV
ROPE_THETA = 500000.0
ROPE_DIM = DIFF_DH // 4
ROPE_HALF = ROPE_DIM // 2

N_GROUPS = 4
EXPERTS_PER_GROUP = 8
N_EXPERTS = N_GROUPS * EXPERTS_PER_GROUP
D_EXPERT = 256

LANES = 128
MXU_N = 256
VMEM_LIMIT = 56 * 1024 * 1024

BF16 = jnp.bfloat16
F32 = jnp.float32
NT_DIMS = (((1,), (1,)), ((), ()))
TN_DIMS = (((0,), (0,)), ((), ()))
NEG_BIG = -1e30
LOG2_E = math.log2(math.e)


def _rms(x, g):
    return x * lax.rsqrt(jnp.mean(x * x, axis=-1, keepdims=True) + EPS) * g


def _cparams(semantics):
    return pltpu.CompilerParams(dimension_semantics=semantics, vmem_limit_bytes=VMEM_LIMIT)


SUBLANES = 8


def _store_slab(ref, x):
    n = x.shape[0]
    for j in range(SUBLANES):
        ref[pl.ds(j, n, stride=SUBLANES), :] = x[:, j * LANES:(j + 1) * LANES]


def _load_slab(ref, n):
    return jnp.concatenate([ref[pl.ds(j, n, stride=SUBLANES), :] for j in range(SUBLANES)], axis=1)


def _start_token_gather(idx_ref, n_tokens, src_hbm, dst, sem):
    def body(r, carry):
        src_row = pl.multiple_of(idx_ref[0, 0, r] * SUBLANES, SUBLANES)
        dst_row = pl.multiple_of(r * SUBLANES, SUBLANES)
        pltpu.make_async_copy(src_hbm.at[pl.ds(src_row, SUBLANES), :],
                              dst.at[pl.ds(dst_row, SUBLANES), :], sem).start(priority=1)
        return carry
    lax.fori_loop(0, n_tokens, body, 0, unroll=8)


def _wait_token_gather(n_tokens, src_hbm, dst, sem):
    pltpu.make_async_copy(src_hbm.at[pl.ds(0, n_tokens * SUBLANES), :], dst, sem).wait()


C_GQ = 0
C_GK = C_GQ + GLA_QK
C_GV = C_GK + GLA_QK
C_GR = C_GV + GLA_V
C_AL = C_GR + GLA_V
C_GLA_END = C_AL + GLA_LOWRANK
C_DQ = 0
C_DK = C_DQ + DIFF_W
C_DV = C_DK + DIFF_W
C_END = C_DV + DIFF_W


def _inproj_kernel(x_ref, pos_ref, g_ref, wgla_ref, wdiff_ref, wa2_ref, ba_ref, inv_ref, sel_ref,
                   one_ref,
                   gq_ref, gk_ref, gv_ref, gr_ref, la_ref, dq_ref, dk_ref, dv_ref):
    nb = _rms(x_ref[...], g_ref[...]).astype(BF16)

    def proj_gla(c0, c1):
        return jnp.dot(nb, wgla_ref[:, c0:c1], preferred_element_type=F32)

    def proj(c0, c1):
        return jnp.dot(nb, wdiff_ref[:, c0:c1], preferred_element_type=F32)

    gq_ref[...] = proj_gla(C_GQ, C_GK) * (GLA_DK ** -0.5)
    gk_ref[...] = proj_gla(C_GK, C_GV)
    gv_ref[...] = proj_gla(C_GV, C_GR).astype(BF16)
    gr_ref[...] = proj_gla(C_GR, C_AL).astype(BF16)

    gal = proj_gla(C_AL, C_GLA_END).astype(BF16)
    a_logit = jnp.dot(gal, wa2_ref[...], preferred_element_type=F32) + ba_ref[...]
    la_ref[...] = (jnp.minimum(a_logit, 0.0) - jnp.log1p(jnp.exp(-jnp.abs(a_logit)))) / GLA_TAU

    ang = inv_ref[...] * pos_ref[0].astype(F32)
    cs = jnp.concatenate([jnp.cos(ang), jnp.sin(ang)], axis=0)

    tabs = None
    rest = cs
    for _ in range(3):
        piece = rest.astype(BF16)
        rest = rest - piece.astype(F32)
        t = lax.dot_general(piece, sel_ref[...], TN_DIMS, preferred_element_type=F32)
        tabs = t if tabs is None else tabs + t
    cosf = tabs[:, :LANES] + one_ref[...]
    sneg = tabs[:, LANES:2 * LANES]
    spos = tabs[:, 2 * LANES:]

    def rope_store(c0, out_ref, scale):
        for j in range(DIFF_W // MXU_N):
            pair = proj(c0 + j * MXU_N, c0 + (j + 1) * MXU_N)
            for i in range(MXU_N // LANES):
                blk = pair[:, i * LANES:(i + 1) * LANES]
                rot = (blk * cosf + pltpu.roll(blk, LANES - ROPE_HALF, 1) * sneg
                       + pltpu.roll(blk, ROPE_HALF, 1) * spos)
                c = j * MXU_N + i * LANES
                out_ref[:, c:c + LANES] = (rot * scale).astype(BF16)

    rope_store(C_DQ, dq_ref, DIFF_DH ** -0.5 * LOG2_E)
    rope_store(C_DK, dk_ref, 1.0)
    dv_ref[...] = proj(C_DV, C_END).astype(BF16)


def _rope_tables():
    lane = np.arange(LANES)
    r = lane % DIFF_DH
    selc = np.zeros((2 * ROPE_HALF, LANES), np.float32)
    seln = np.zeros_like(selc)
    selp = np.zeros_like(selc)
    one = np.zeros((1, LANES), np.float32)
    for l in range(LANES):
        if r[l] < ROPE_DIM:
            selc[r[l] % ROPE_HALF, l] = 1.0
            if r[l] < ROPE_HALF:
                seln[ROPE_HALF + r[l], l] = -1.0
            else:
                selp[ROPE_HALF + r[l] - ROPE_HALF, l] = 1.0
        else:
            one[0, l] = 1.0
    sel = np.concatenate([selc, seln, selp], axis=1)
    return jnp.asarray(sel, dtype=BF16), jnp.asarray(one)


def _inproj(x2, positions, attn_norm, wgla, wdiff, wa2, b_a, *, tm):
    T, D = x2.shape
    nt = T // tm
    pos3 = positions.reshape(nt, 1, tm)
    inv = (ROPE_THETA ** (-jnp.arange(0, ROPE_DIM, 2, dtype=F32) / ROPE_DIM)).reshape(ROPE_HALF, 1)
    sel, one = _rope_tables()
    row = lambda i: (i, 0)
    fixed = lambda i: (0, 0)
    out_shapes = (
        jax.ShapeDtypeStruct((T, GLA_QK), F32), jax.ShapeDtypeStruct((T, GLA_QK), F32),
        jax.ShapeDtypeStruct((T, GLA_V), BF16), jax.ShapeDtypeStruct((T, GLA_V), BF16),
        jax.ShapeDtypeStruct((T, GLA_QK), F32),
        jax.ShapeDtypeStruct((T, DIFF_W), BF16), jax.ShapeDtypeStruct((T, DIFF_W), BF16),
        jax.ShapeDtypeStruct((T, DIFF_W), BF16),
    )
    return pl.pallas_call(
        _inproj_kernel,
        grid=(nt,),
        in_specs=[
            pl.BlockSpec((tm, D), row),
            pl.BlockSpec((1, 1, tm), lambda i: (i, 0, 0)),
            pl.BlockSpec((1, D), fixed),
            pl.BlockSpec((D, C_GLA_END), fixed),
            pl.BlockSpec((D, C_END), fixed),
            pl.BlockSpec((GLA_LOWRANK, GLA_QK), fixed),
            pl.BlockSpec((1, GLA_QK), fixed),
            pl.BlockSpec((ROPE_HALF, 1), fixed),
            pl.BlockSpec((2 * ROPE_HALF, 3 * LANES), fixed),
            pl.BlockSpec((1, LANES), fixed),
        ],
        out_specs=[pl.BlockSpec((tm, s.shape[1]), row) for s in out_shapes],
        out_shape=out_shapes,
        compiler_params=_cparams(("parallel",)),
        name="inproj",
    )(x2, pos3, attn_norm.reshape(1, D), wgla, wdiff, wa2, b_a.reshape(1, GLA_QK), inv, sel, one)


def _gla_kernel(gq_ref, gk_ref, gv_ref, gr_ref, la_ref, gn_ref, oa_ref, st_ref, *, n_chunks):
    @pl.when(pl.program_id(1) == 0)
    def _():
        st_ref[...] = jnp.zeros_like(st_ref)

    C = GLA_CHUNK
    tril = lax.broadcasted_iota(jnp.int32, (C, C), 0) >= lax.broadcasted_iota(jnp.int32, (C, C), 1)
    trilf = tril.astype(F32)
    head_of_lane = lax.broadcasted_iota(jnp.int32, (1, GLA_QK), 1) // GLA_DK
    hmask = [head_of_lane == h for h in range(GLA_HEADS)]
    gn = gn_ref[...]

    def by_head(x):
        return jnp.concatenate([jnp.where(hmask[h], x, jnp.zeros_like(x))
                                for h in range(GLA_HEADS)], axis=0)

    st = st_ref[...]
    for c in range(n_chunks):
        sl = pl.ds(c * C, C)
        q = gq_ref[0, sl, :]
        k = gk_ref[0, sl, :]
        la = la_ref[0, sl, :]
        v = gv_ref[0, sl, :]
        b = jnp.dot(trilf, la, precision=lax.Precision.HIGHEST, preferred_element_type=F32)
        b_last = b[C - 1:C, :]
        q_dec = (q * jnp.exp(b)).astype(BF16)
        k_inv = (k * jnp.exp(-b)).astype(BF16)
        k_end = (k * jnp.exp(b_last - b)).astype(BF16)
        decay = jnp.exp(b_last)

        q_heads = by_head(q_dec)
        att = lax.dot_general(q_heads, k_inv, NT_DIMS, preferred_element_type=F32)
        inter = lax.dot_general(q_heads, st.astype(BF16), NT_DIMS,
                                preferred_element_type=F32)
        for h in range(GLA_HEADS):
            rows = slice(h * C, (h + 1) * C)
            cols = slice(h * GLA_DV, (h + 1) * GLA_DV)
            a_h = jnp.where(tril, att[rows], 0.0).astype(BF16)
            v_h = v[:, cols]
            o_h = jnp.dot(a_h, v_h, preferred_element_type=F32) + inter[rows]
            y = _rms(o_h, gn)
            r = gr_ref[0, sl, cols].astype(F32)
            oa_ref[0, sl, cols] = (y * (r * jax.nn.sigmoid(r))).astype(BF16)
        v_heads = jnp.concatenate([v[:, h * GLA_DV:(h + 1) * GLA_DV] for h in range(GLA_HEADS)],
                                  axis=0)
        u = lax.dot_general(v_heads, by_head(k_end), TN_DIMS, preferred_element_type=F32)
        st = st * decay + u
    st_ref[...] = st


def _gla(gq, gk, gv, gr, la, gla_norm, *, B, S, ts):
    n_chunks = ts // GLA_CHUNK
    blk = lambda w: pl.BlockSpec((1, ts, w), lambda b, j: (b, j, 0))
    r3 = lambda a: a.reshape(B, S, a.shape[-1])
    return pl.pallas_call(
        functools.partial(_gla_kernel, n_chunks=n_chunks),
        grid=(B, S // ts),
        in_specs=[blk(GLA_QK), blk(GLA_QK), blk(GLA_V), blk(GLA_V), blk(GLA_QK),
                  pl.BlockSpec((1, GLA_DV), lambda b, j: (0, 0))],
        out_specs=blk(GLA_V),
        out_shape=jax.ShapeDtypeStruct((B, S, GLA_V), BF16),
        scratch_shapes=[pltpu.VMEM((GLA_DV, GLA_QK), F32)],
        compiler_params=_cparams(("parallel", "arbitrary")),
        name="gla",
    )(r3(gq), r3(gk), r3(gv), r3(gr), r3(la), gla_norm.reshape(1, GLA_DV))


def _diffattn_kernel(dq_ref, dk_ref, dv_ref, lq1_ref, lk1_ref, lq2_ref, lk2_ref, sub_ref,
                     ob_ref, m_ref, l_ref, acc_ref, *, tq, tk, lambda_init):
    S = dq_ref.shape[1]
    nq = S // tq
    R = 2 * tq
    lam = (jnp.exp(jnp.sum(lq1_ref[...] * lk1_ref[...], axis=-1, keepdims=True))
           - jnp.exp(jnp.sum(lq2_ref[...] * lk2_ref[...], axis=-1, keepdims=True))
           + lambda_init)
    first_comp = lax.broadcasted_iota(jnp.int32, (1, DIFF_DV), 1) < DIFF_DH
    sub = sub_ref[...]

    for h in range(DIFF_HEADS):
        cols = slice(h * DIFF_DV, (h + 1) * DIFF_DV)
        for qi in range(nq):
            qb = dq_ref[0, qi * tq:(qi + 1) * tq, cols]
            zero = jnp.zeros_like(qb)
            qs = jnp.concatenate([jnp.where(first_comp, qb, zero),
                                  jnp.where(first_comp, zero, qb)], axis=0)
            m_ref[...] = jnp.full(m_ref.shape, NEG_BIG, F32)
            l_ref[...] = jnp.zeros(l_ref.shape, F32)
            acc_ref[...] = jnp.zeros(acc_ref.shape, F32)

            def scores(kstart, qs=qs, cols=cols):
                kb = dk_ref[0, kstart:kstart + tk, cols]
                return lax.dot_general(kb, qs, NT_DIMS, preferred_element_type=F32)

            def update(s, kstart, masked, cols=cols, qi=qi):
                vb = dv_ref[0, kstart:kstart + tk, cols]
                if masked:
                    kpos = kstart + lax.broadcasted_iota(jnp.int32, (tk, R), 0)
                    qpos = qi * tq + lax.broadcasted_iota(jnp.int32, (tk, R), 1) % tq
                    s = jnp.where(qpos >= kpos, s, NEG_BIG)
                m_prev = m_ref[...]
                m_new = jnp.maximum(m_prev, jnp.max(s, axis=0, keepdims=True))
                alpha = jnp.exp2(m_prev - m_new)
                p = jnp.exp2(s - m_new)
                l_ref[...] = alpha * l_ref[...] + jnp.sum(p, axis=0, keepdims=True)
                pv = lax.dot_general(vb, p.astype(BF16), TN_DIMS, preferred_element_type=F32)
                acc_ref[...] = alpha * acc_ref[...] + pv
                m_ref[...] = m_new

            n_blocks = (qi + 1) * tq // tk
            diag_from = qi * tq // tk
            s_next = scores(0)
            for kb_i in range(n_blocks):
                s_cur = s_next
                if kb_i + 1 < n_blocks:
                    s_next = scores((kb_i + 1) * tk)
                update(s_cur, kb_i * tk, kb_i >= diag_from)

            o_all = acc_ref[...] / l_ref[...]
            o = o_all[:, :tq] - lam * o_all[:, tq:]
            y = o * lax.rsqrt(jnp.mean(o * o, axis=0, keepdims=True) + EPS) * sub
            y = y * (1.0 - lambda_init)
            ob_ref[0, qi * tq:(qi + 1) * tq, cols] = y.T.astype(BF16)


def _diffattn(dq, dk, dv, lq1, lk1, lq2, lk2, diff_subln, *, B, S, tq, tk, lambda_init):
    seq = pl.BlockSpec((1, S, DIFF_W), lambda b: (b, 0, 0))
    vec = lambda w: pl.BlockSpec((1, w), lambda b: (0, 0))
    r3 = lambda a: a.reshape(B, S, DIFF_W)
    return pl.pallas_call(
        functools.partial(_diffattn_kernel, tq=tq, tk=tk, lambda_init=lambda_init),
        grid=(B,),
        in_specs=[seq, seq, seq, vec(DIFF_DH), vec(DIFF_DH), vec(DIFF_DH), vec(DIFF_DH),
                  pl.BlockSpec((DIFF_DV, 1), lambda b: (0, 0))],
        out_specs=seq,
        out_shape=jax.ShapeDtypeStruct((B, S, DIFF_W), BF16),
        scratch_shapes=[pltpu.VMEM((1, 2 * tq), F32), pltpu.VMEM((1, 2 * tq), F32),
                        pltpu.VMEM((DIFF_DV, 2 * tq), F32)],
        compiler_params=_cparams(("parallel",)),
        name="diffattn",
    )(r3(dq), r3(dk), r3(dv), lq1.reshape(1, -1), lk1.reshape(1, -1), lq2.reshape(1, -1),
      lk2.reshape(1, -1), diff_subln.reshape(-1, 1))


ROUTE_W = 8


def _first_index_of(mask, lane):
    return jnp.min(jnp.where(mask, lane, LANES), axis=-1, keepdims=True)


POSTMIX_SPLIT = 2


def _postmix_kernel(x_ref, oa_ref, ob_ref, g_ref, wg_ref, wba_ref, wbb_ref, wo_ref, fn_ref,
                    wr_ref, br_ref, h_ref, n2_ref, route_ref, cnt_ref):
    rows_per = x_ref.shape[0] // POSTMIX_SPLIT
    cnt = None
    for s in range(POSTMIX_SPLIT):
        rows = slice(s * rows_per, (s + 1) * rows_per)
        h, n2, rec, c = _postmix_rows(x_ref[rows, :], oa_ref[rows, :], ob_ref[rows, :], g_ref, wg_ref,
                                      wba_ref, wbb_ref, wo_ref, fn_ref, wr_ref, br_ref)
        h_ref[rows, :] = h
        n2_ref[rows, :] = n2
        route_ref[rows, :] = rec
        cnt = c if cnt is None else cnt + c
    cnt_ref[0] = cnt


def _postmix_rows(x, oa, ob, g_ref, wg_ref, wba_ref, wbb_ref, wo_ref, fn_ref, wr_ref, br_ref):
    D = x.shape[1]
    nb = _rms(x, g_ref[...]).astype(BF16)
    y_a = jnp.dot(oa, wba_ref[...], preferred_element_type=F32)
    g_a = jnp.dot(nb, wg_ref[:, :D], preferred_element_type=F32)
    merged = jax.nn.sigmoid(g_a) * y_a
    y_b = jnp.dot(ob, wbb_ref[...], preferred_element_type=F32)
    g_b = jnp.dot(nb, wg_ref[:, D:], preferred_element_type=F32)
    merged = merged + jax.nn.sigmoid(g_b) * y_b
    h = x + jnp.dot(merged.astype(BF16), wo_ref[...], preferred_element_type=F32)
    n2 = _rms(h, fn_ref[...])

    n2_hi = n2.astype(BF16)
    n2_lo = (n2 - n2_hi.astype(F32)).astype(BF16)
    hi_prod = jnp.dot(n2_hi, wr_ref[...], preferred_element_type=F32)
    lg = (hi_prod[:, :LANES] + hi_prod[:, LANES:]
          + jnp.dot(n2_lo, wr_ref[:, :LANES], preferred_element_type=F32)) + br_ref[...]
    lane = lax.broadcasted_iota(jnp.int32, lg.shape, 1)
    is_g = lane < N_GROUPS
    g_max = jnp.max(jnp.where(is_g, lg, -jnp.inf), axis=-1, keepdims=True)
    g_exp = jnp.where(is_g, jnp.exp(lg - g_max), 0.0)
    g_prob = g_exp / jnp.sum(g_exp, axis=-1, keepdims=True)
    g_p = jnp.max(g_prob, axis=-1, keepdims=True)
    g_idx = _first_index_of(is_g & (g_prob == g_p), lane)

    e_lo = N_GROUPS + EXPERTS_PER_GROUP * g_idx
    is_e = (lane >= e_lo) & (lane < e_lo + EXPERTS_PER_GROUP)
    e_max = jnp.max(jnp.where(is_e, lg, -jnp.inf), axis=-1, keepdims=True)
    e_exp = jnp.where(is_e, jnp.exp(lg - e_max), 0.0)
    e_prob = e_exp / jnp.sum(e_exp, axis=-1, keepdims=True)
    p1 = jnp.max(jnp.where(is_e, e_prob, -1.0), axis=-1, keepdims=True)
    i1 = _first_index_of(is_e & (e_prob == p1), lane)
    rest = is_e & (lane != i1)
    p2 = jnp.max(jnp.where(rest, e_prob, -1.0), axis=-1, keepdims=True)
    i2 = _first_index_of(rest & (e_prob == p2), lane)
    den = p1 + p2
    w1 = g_p * (p1 / den)
    w2 = g_p * (p2 / den)
    rec = jnp.where(lane == 0, w1, 0.0)
    rec = jnp.where(lane == 1, w2, rec)
    rec = jnp.where(lane == 2, (i1 - N_GROUPS).astype(F32), rec)
    rec = jnp.where(lane == 3, (i2 - N_GROUPS).astype(F32), rec)
    chosen = (lane == i1 - N_GROUPS) | (lane == i2 - N_GROUPS)
    cnt = jnp.sum(chosen.astype(F32), axis=0, keepdims=True)
    return h, n2.astype(BF16), rec[:, :ROUTE_W], cnt


def _postmix(x2, oa, ob, attn_norm, wg, wba, wbb, wo, ffn_norm, wr, br, *, tm):
    T, D = x2.shape
    row = lambda i: (i, 0)
    fixed = lambda i: (0, 0)
    full = lambda a: pl.BlockSpec(a.shape, fixed)
    g = attn_norm.reshape(1, D)
    fn = ffn_norm.reshape(1, D)
    return pl.pallas_call(
        _postmix_kernel,
        grid=(T // tm,),
        in_specs=[pl.BlockSpec((tm, D), row), pl.BlockSpec((tm, GLA_V), row),
                  pl.BlockSpec((tm, DIFF_W), row), full(g), full(wg), full(wba), full(wbb),
                  full(wo), full(fn), full(wr), full(br)],
        out_specs=[pl.BlockSpec((tm, D), row), pl.BlockSpec((tm, D), row),
                   pl.BlockSpec((tm, ROUTE_W), row),
                   pl.BlockSpec((1, 1, LANES), lambda i: (i, 0, 0))],
        out_shape=(jax.ShapeDtypeStruct((T, D), F32), jax.ShapeDtypeStruct((T, D), BF16),
                   jax.ShapeDtypeStruct((T, ROUTE_W), F32),
                   jax.ShapeDtypeStruct((T // tm, 1, LANES), F32)),
        compiler_params=_cparams(("parallel",)),
        name="postmix",
    )(x2, oa, ob, g, wg, wba, wbb, wo, fn, wr, br)


DISPATCH_CHUNK = 8
POS_W = 8


def _token_copy(src, src_tok, dst, dst_tok, n_tok, sem):
    return pltpu.make_async_copy(
        src.at[pl.ds(pl.multiple_of(src_tok * SUBLANES, SUBLANES), n_tok * SUBLANES), :],
        dst.at[pl.ds(pl.multiple_of(dst_tok * SUBLANES, SUBLANES), n_tok * SUBLANES), :], sem)


def _for_each_piece(length, fn):
    n_full = lax.shift_right_logical(length, 3)

    def body(c, carry):
        fn(c * DISPATCH_CHUNK, DISPATCH_CHUNK)
        return carry

    lax.fori_loop(0, n_full, body, 0)
    off = n_full * DISPATCH_CHUNK
    for n in (4, 2, 1):
        has = (length & n) != 0

        @pl.when(has)
        def _(off=off, n=n):
            fn(off, n)

        off = off + jnp.where(has, n, 0)


def _dispatch_kernel(seg_ref, lo_ref, cnt_ref, fill_ref, fill_len_ref, n_tiles_ref,
                     n2_ref, route_ref, lo_row_ref, seg_row_ref,
                     xg_hbm, pos_ref, xs_buf, zero_buf, sem, fill_sem):
    tm = n2_ref.shape[0]
    n_slab = 2 * tm * SUBLANES
    i = pl.program_id(0)
    nt = pl.num_programs(0)
    slot = i % 2

    @pl.when(i == 0)
    def _():
        zero_buf[...] = jnp.zeros_like(zero_buf)
        te = zero_buf.shape[0] // SUBLANES
        for wait in (False, True):
            def fill(e, carry, wait=wait):
                def piece(off, n):
                    cp = _token_copy(zero_buf, 0, xg_hbm, fill_ref[e] + off, n, fill_sem)
                    cp.wait() if wait else cp.start()
                _for_each_piece(fill_len_ref[e], piece)
                return carry
            lax.fori_loop(0, N_EXPERTS, fill, 0)

            def fill_tile(t, carry, wait=wait):
                cp = _token_copy(zero_buf, 0, xg_hbm, t * te, te, fill_sem)
                cp.wait() if wait else cp.start()
                return carry
            lax.fori_loop(n_tiles_ref[0], xg_hbm.shape[0] // zero_buf.shape[0], fill_tile, 0)

    route = route_ref[...]
    lane = lax.broadcasted_iota(jnp.int32, (tm, LANES), 1)
    oh1 = lane == route[:, 2:3].astype(jnp.int32)
    oh2 = lane == route[:, 3:4].astype(jnp.int32)
    both = jnp.where(oh1 | oh2, 1.0, 0.0).astype(BF16)
    earlier = (lax.broadcasted_iota(jnp.int32, (tm, tm), 0)
               > lax.broadcasted_iota(jnp.int32, (tm, tm), 1))
    rank = jnp.dot(jnp.where(earlier, 1.0, 0.0).astype(BF16), both, preferred_element_type=F32)

    def pick(onehot, v):
        return jnp.sum(jnp.where(onehot, v, 0.0), axis=-1, keepdims=True)

    r1 = pick(oh1, rank)
    r2 = pick(oh2, rank)
    q1 = pick(oh1, lo_row_ref[0]) + r1
    q2 = pick(oh2, lo_row_ref[0]) + r2
    pos1 = (pick(oh1, seg_row_ref[0]) + r1).astype(jnp.int32)
    pos2 = (pick(oh2, seg_row_ref[0]) + r2).astype(jnp.int32)
    rec = jnp.where(lane == 0, pos1, jnp.where(lane == 1, pos2, 0))
    pos_ref[...] = rec[:, :POS_W]

    slots_t = jnp.where(lane == 0, q1, jnp.where(lane == 1, q2, 0.0)).T.astype(jnp.int32)
    slot_id = lax.broadcasted_iota(jnp.int32, (2 * tm, tm), 0)
    perm = jnp.where((slot_id == slots_t[0:1, :]) | (slot_id == slots_t[1:2, :]), 1.0, 0.0)
    xs = jnp.dot(perm.astype(BF16), n2_ref[...], preferred_element_type=F32)

    def wait_slot(s):
        pltpu.make_async_copy(xg_hbm.at[pl.ds(0, n_slab), :], xs_buf.at[s], sem.at[s]).wait()

    @pl.when(i >= 2)
    def _():
        wait_slot(slot)

    _store_slab(xs_buf.at[slot], xs)

    def send(e, carry):
        k = i * N_EXPERTS + e

        def piece(off, n):
            _token_copy(xs_buf.at[slot], lo_ref[k] + off, xg_hbm, seg_ref[k] + off, n,
                        sem.at[slot]).start()

        _for_each_piece(cnt_ref[k], piece)
        return carry

    lax.fori_loop(0, N_EXPERTS, send, 0)

    @pl.when(i == nt - 1)
    def _():
        wait_slot(slot)

        @pl.when(nt >= 2)
        def _():
            wait_slot(1 - slot)


def _dispatch(n2, route, cnt, *, tm, te):
    T, D = n2.shape
    nt = T // tm
    n_tiles_max = 2 * T // te + N_EXPERTS
    c = cnt[:, 0, :N_EXPERTS].astype(jnp.int32)
    count = jnp.sum(c, axis=0)
    padded = (count + te - 1) // te * te
    p_end = jnp.cumsum(padded)
    p_start = p_end - padded
    n_tiles = (p_end[-1] // te).astype(jnp.int32).reshape(1)
    tile_start = jnp.arange(n_tiles_max, dtype=jnp.int32) * te
    tile_e = jnp.sum(tile_start[:, None] >= p_end[None, :], axis=1, dtype=jnp.int32)
    tile_e = jnp.minimum(tile_e, jnp.sum(tile_start[n_tiles[0] - 1] >= p_end, dtype=jnp.int32))
    seg = p_start[None, :] + jnp.cumsum(c, axis=0) - c
    lo = jnp.cumsum(c, axis=1) - c
    lanes = lambda a: jnp.pad(a.astype(F32), ((0, 0), (0, LANES - N_EXPERTS))).reshape(nt, 1, LANES)
    flat = lambda a: a.reshape(-1).astype(jnp.int32)
    row = lambda i, *_: (i, 0)
    per_tile = pl.BlockSpec((1, 1, LANES), lambda i, *_: (i, 0, 0))
    grid_spec = pltpu.PrefetchScalarGridSpec(
        num_scalar_prefetch=6,
        grid=(nt,),
        in_specs=[pl.BlockSpec((tm, D), row), pl.BlockSpec((tm, ROUTE_W), row), per_tile, per_tile],
        out_specs=[pl.BlockSpec(memory_space=pl.ANY), pl.BlockSpec((tm, POS_W), row)],
        scratch_shapes=[pltpu.VMEM((2, 2 * tm * SUBLANES, LANES), F32),
                        pltpu.VMEM((te * SUBLANES, LANES), F32),
                        pltpu.SemaphoreType.DMA((2,)), pltpu.SemaphoreType.DMA(())],
    )
    xg, pos = pl.pallas_call(
        _dispatch_kernel,
        grid_spec=grid_spec,
        out_shape=(jax.ShapeDtypeStruct((n_tiles_max * te * SUBLANES, LANES), F32),
                   jax.ShapeDtypeStruct((T, POS_W), jnp.int32)),
        compiler_params=_cparams(("arbitrary",)),
        name="dispatch",
    )(flat(seg), flat(lo), flat(c), flat(p_start + count), flat(padded - count), n_tiles,
      n2, route, lanes(lo), lanes(seg))
    return xg, pos, tile_e, n_tiles


def _experts_kernel(tile_e_ref, n_tiles_ref, xg_ref, wg_ref, wu_ref, wd_ref, y_ref):
    tm = xg_ref.shape[0] // SUBLANES
    i = pl.program_id(0)

    @pl.when(i < n_tiles_ref[0])
    def _():
        x = _load_slab(xg_ref, tm).astype(BF16)
        g = jnp.dot(x, wg_ref[0], preferred_element_type=F32)
        u = jnp.dot(x, wu_ref[0], preferred_element_type=F32)
        he = (g * jax.nn.sigmoid(g)) * u
        _store_slab(y_ref, jnp.dot(he.astype(BF16), wd_ref[0], preferred_element_type=F32))

    @pl.when(i >= n_tiles_ref[0])
    def _():
        y_ref[...] = jnp.zeros_like(y_ref)


def _experts(tile_e, n_tiles, xg, wg, wu, wd, *, tm):
    D = wg.shape[1]
    n_tiles_max = xg.shape[0] // (tm * SUBLANES)
    slab = lambda f: pl.BlockSpec((tm * SUBLANES, LANES), f)
    grid_spec = pltpu.PrefetchScalarGridSpec(
        num_scalar_prefetch=2,
        grid=(n_tiles_max,),
        in_specs=[slab(lambda i, te, nt: (jnp.minimum(i, nt[0] - 1), 0)),
                  pl.BlockSpec((1, D, D_EXPERT), lambda i, te, nt: (te[i], 0, 0)),
                  pl.BlockSpec((1, D, D_EXPERT), lambda i, te, nt: (te[i], 0, 0)),
                  pl.BlockSpec((1, D_EXPERT, D), lambda i, te, nt: (te[i], 0, 0))],
        out_specs=slab(lambda i, te, nt: (i, 0)),
    )
    return pl.pallas_call(
        _experts_kernel,
        grid_spec=grid_spec,
        out_shape=jax.ShapeDtypeStruct(xg.shape, F32),
        compiler_params=_cparams(("arbitrary",)),
        name="experts",
    )(tile_e, n_tiles, xg, wg, wu, wd)


def _final_kernel(idx_ref, idx_next_ref, h_ref, ys_hbm, route_ref, p_ref, wple_ref, pn_ref, wpg_ref,
                  fn_ref, o_ref, ybuf, sem, *, apply_final_norm):
    tm = h_ref.shape[0]
    i = pl.program_id(0)
    slot = i % 2

    @pl.when(i == 0)
    def _():
        _start_token_gather(idx_ref, 2 * tm, ys_hbm, ybuf.at[0], sem.at[0])

    @pl.when(i + 1 < pl.num_programs(0))
    def _():
        _start_token_gather(idx_next_ref, 2 * tm, ys_hbm, ybuf.at[1 - slot], sem.at[1 - slot])

    _wait_token_gather(2 * tm, ys_hbm, ybuf.at[slot], sem.at[slot])
    route = route_ref[...]
    y = route[:, 0:1] * _load_slab(ybuf.at[slot, pl.ds(0, tm * SUBLANES), :], tm)
    y = y + route[:, 1:2] * _load_slab(ybuf.at[slot, pl.ds(tm * SUBLANES, tm * SUBLANES), :], tm)
    h = h_ref[...] + y
    e = _rms(jnp.dot(p_ref[...].astype(BF16), wple_ref[...], preferred_element_type=F32), pn_ref[...])
    gate = jax.nn.sigmoid(jnp.dot(h.astype(BF16), wpg_ref[...], preferred_element_type=F32))
    h = h + gate * e
    if apply_final_norm:
        h = _rms(h, fn_ref[...])
    o_ref[...] = h


def _final(h1, ys, pos, route, p2, wple, ple_norm, wpg, final_norm, *, tm, apply_final_norm):
    T, D = h1.shape
    nt = T // tm
    row = lambda i: (i, 0)
    fixed = lambda i: (0, 0)
    full = lambda a: pl.BlockSpec(a.shape, fixed)
    pn = ple_norm.reshape(1, D)
    fn = final_norm.reshape(1, D)
    idx = pos[:, :2].reshape(nt, tm, 2).transpose(0, 2, 1).reshape(nt, 1, 2 * tm)
    smem_rows = lambda f: pl.BlockSpec((1, 1, 2 * tm), f, memory_space=pltpu.SMEM)
    return pl.pallas_call(
        functools.partial(_final_kernel, apply_final_norm=apply_final_norm),
        grid=(nt,),
        in_specs=[smem_rows(lambda i: (i, 0, 0)),
                  smem_rows(lambda i: (jnp.minimum(i + 1, nt - 1), 0, 0)),
                  pl.BlockSpec((tm, D), row), pl.BlockSpec(memory_space=pl.ANY),
                  pl.BlockSpec((tm, ROUTE_W), row), pl.BlockSpec((tm, p2.shape[1]), row),
                  full(wple), full(pn), full(wpg), full(fn)],
        out_specs=pl.BlockSpec((tm, D), row),
        out_shape=jax.ShapeDtypeStruct((T, D), F32),
        scratch_shapes=[pltpu.VMEM((2, 2 * tm * SUBLANES, LANES), F32),
                        pltpu.SemaphoreType.DMA((2,))],
        compiler_params=_cparams(("arbitrary",)),
        name="final",
    )(idx, idx, h1, ys, route, p2, wple, pn, wpg, fn)


SPLIT_SIZES = (GLA_QK, GLA_QK, GLA_V, GLA_V, GLA_LOWRANK, DIFF_W, DIFF_W, DIFF_W)


def _layer(h2, p2, positions, B, S, lambda_init, apply_final_norm, attn_norm, w_in, w_a2, b_a,
           gla_norm, lq1, lk1, lq2, lk2, diff_subln, w_branch_a, w_branch_b, w_out, ffn_norm,
           w_rg, b_rg, w_re, b_re, w_gate, w_up, w_down, w_ple, ple_norm, w_ple_gate, final_norm,
           *, tm, ts, tq, tk, te):
    T, D = h2.shape
    assert D == SUBLANES * LANES, "token-slab gathers need one (8, 128) tile per token"
    n_mix = sum(SPLIT_SIZES)
    wgla = w_in[:, :C_GLA_END].astype(BF16)
    wdiff = w_in[:, C_GLA_END:n_mix].astype(BF16)
    wg = w_in[:, n_mix:].astype(BF16)

    gq, gk, gv, gr, la, dq, dk, dv = _inproj(h2, positions, attn_norm, wgla, wdiff,
                                             w_a2.astype(BF16), b_a, tm=tm)
    oa = _gla(gq, gk, gv, gr, la, gla_norm, B=B, S=S, ts=ts).reshape(T, GLA_V)
    ob = _diffattn(dq, dk, dv, lq1, lk1, lq2, lk2, diff_subln, B=B, S=S, tq=tq, tk=tk,
                   lambda_init=lambda_init).reshape(T, DIFF_W)

    wr = jnp.concatenate([w_rg, w_re, jnp.zeros((D, LANES - N_GROUPS - N_EXPERTS), F32)], axis=1)
    wr_hi = wr.astype(BF16)
    wr = jnp.concatenate([wr_hi, (wr - wr_hi.astype(F32)).astype(BF16)], axis=1)
    br = jnp.concatenate([b_rg, b_re, jnp.zeros((LANES - N_GROUPS - N_EXPERTS,), F32)]).reshape(1, LANES)
    h1, n2, route, cnt = _postmix(h2, oa, ob, attn_norm, wg, w_branch_a.astype(BF16),
                                  w_branch_b.astype(BF16), w_out.astype(BF16), ffn_norm, wr, br,
                                  tm=tm)

    xg, pos, tile_e, n_tiles = _dispatch(n2, route, cnt, tm=tm, te=te)
    ys = _experts(tile_e, n_tiles, xg, w_gate.astype(BF16), w_up.astype(BF16),
                  w_down.astype(BF16), tm=te)
    return _final(h1, ys, pos, route, p2, w_ple.astype(BF16), ple_norm, w_ple_gate.astype(BF16),
                  final_norm, tm=tm, apply_final_norm=apply_final_norm)


def _block(x, p, positions, attn_norm, w_in, w_a2, b_a, gla_norm, lambda_q1, lambda_k1, lambda_q2,
           lambda_k2, diff_subln, w_branch_a, w_branch_b, w_out, ffn_norm, w_router_group,
           b_router_group, w_router_expert, b_router_expert, w_gate, w_up, w_down, w_ple, ple_norm,
           w_ple_gate, final_norm, *, tm, ts, tq, tk, te):
    B, S, D = x.shape
    depth = w_in.shape[0]
    h = x.reshape(B * S, D)
    for i in range(depth):
        lambda_init = 0.8 - 0.6 * math.exp(-0.3 * i)
        h = _layer(h, p[i].reshape(B * S, -1), positions, B, S, lambda_init, i == depth - 1,
                   attn_norm[i], w_in[i], w_a2[i], b_a[i], gla_norm[i], lambda_q1[i], lambda_k1[i],
                   lambda_q2[i], lambda_k2[i], diff_subln[i], w_branch_a[i], w_branch_b[i], w_out[i],
                   ffn_norm[i], w_router_group[i], b_router_group[i], w_router_expert[i],
                   b_router_expert[i], w_gate[i], w_up[i], w_down[i], w_ple[i], ple_norm[i],
                   w_ple_gate[i], final_norm, tm=tm, ts=ts, tq=tq, tk=tk, te=te)
    return h.reshape(B, S, D)


def kernel(x, p, positions, attn_norm, w_in, w_a2, b_a, gla_norm, lambda_q1, lambda_k1, lambda_q2, lambda_k2, diff_subln, w_branch_a, w_branch_b, w_out, ffn_norm, w_router_group, b_router_group, w_router_expert, b_router_expert, w_gate, w_up, w_down, w_ple, ple_norm, w_ple_gate, final_norm):
    S = x.shape[1]
    return _block(x, p, positions, attn_norm, w_in, w_a2, b_a, gla_norm, lambda_q1, lambda_k1,
                  lambda_q2, lambda_k2, diff_subln, w_branch_a, w_branch_b, w_out, ffn_norm,
                  w_router_group, b_router_group, w_router_expert, b_router_expert, w_gate, w_up,
                  w_down, w_ple, ple_norm, w_ple_gate, final_norm,
                  tm=512, ts=min(512, S), tq=min(512, S), tk=min(512, S), te=512)
```

```python
import functools
import math

import jax
import jax.numpy as jnp
import numpy as np
from jax import lax
from jax.experimental import pallas as pl
from jax.experimental.pallas import tpu as pltpu

EPS = 1e-6

GLA_HEADS = 4
GLA_DK = 64
GLA_DV = 128
GLA_LOWRANK = 16
GLA_TAU = 16.0
GLA_CHUNK = 64
GLA_QK = GLA_HEADS * GLA_DK
GLA_V = GLA_HEADS * GLA_DV

DIFF_HEADS = 4
DIFF_DH = 64
DIFF_DV = 2 * DIFF_DH
DIFF_W = DIFF_HEADS * DIFF_DV
ROPE_THETA = 500000.0
ROPE_DIM = DIFF_DH // 4
ROPE_HALF = ROPE_DIM // 2

N_GROUPS = 4
EXPERTS_PER_GROUP = 8
N_EXPERTS = N_GROUPS * EXPERTS_PER_GROUP
D_EXPERT = 256

LANES = 128
MXU_N = 256
VMEM_LIMIT = 56 * 1024 * 1024

BF16 = jnp.bfloat16
F32 = jnp.float32
NT_DIMS = (((1,), (1,)), ((), ()))
TN_DIMS = (((0,), (0,)), ((), ()))
NEG_BIG = -1e30
LOG2_E = math.log2(math.e)


def _rms(x, g):
    return x * lax.rsqrt(jnp.mean(x * x, axis=-1, keepdims=True) + EPS) * g


def _cparams(semantics):
    return pltpu.CompilerParams(dimension_semantics=semantics, vmem_limit_bytes=VMEM_LIMIT)


SUBLANES = 8


def _store_slab(ref, x):
    n = x.shape[0]
    for j in range(SUBLANES):
        ref[pl.ds(j, n, stride=SUBLANES), :] = x[:, j * LANES:(j + 1) * LANES]


def _load_slab(ref, n):
    return jnp.concatenate([ref[pl.ds(j, n, stride=SUBLANES), :] for j in range(SUBLANES)], axis=1)


def _start_token_gather(idx_ref, idx_row, n_tokens, src_hbm, dst, dst_tok, sem):
    def body(r, carry):
        src_row = pl.multiple_of(idx_ref[0, idx_row, r] * SUBLANES, SUBLANES)
        dst_row = pl.multiple_of((dst_tok + r) * SUBLANES, SUBLANES)
        pltpu.make_async_copy(src_hbm.at[pl.ds(src_row, SUBLANES), :],
                              dst.at[pl.ds(dst_row, SUBLANES), :], sem).start(priority=1)
        return carry
    lax.fori_loop(0, n_tokens, body, 0, unroll=8)


def _wait_token_gather(n_tokens, src_hbm, dst, sem):
    pltpu.make_async_copy(src_hbm.at[pl.ds(0, n_tokens * SUBLANES), :], dst, sem).wait()


C_GQ = 0
C_GK = C_GQ + GLA_QK
C_GV = C_GK + GLA_QK
C_GR = C_GV + GLA_V
C_AL = C_GR + GLA_V
C_GLA_END = C_AL + GLA_LOWRANK
C_DQ = 0
C_DK = C_DQ + DIFF_W
C_DV = C_DK + DIFF_W
C_END = C_DV + DIFF_W


def _inproj_kernel(x_ref, pos_ref, g_ref, wgla_ref, wdiff_ref, wa2_ref, ba_ref, inv_ref, sel_ref,
                   one_ref,
                   gq_ref, gk_ref, gv_ref, gr_ref, la_ref, dq_ref, dk_ref, dv_ref):
    nb = _rms(x_ref[...], g_ref[...]).astype(BF16)

    def proj_gla(c0, c1):
        return jnp.dot(nb, wgla_ref[:, c0:c1], preferred_element_type=F32)

    def proj(c0, c1):
        return jnp.dot(nb, wdiff_ref[:, c0:c1], preferred_element_type=F32)

    gq_ref[...] = proj_gla(C_GQ, C_GK) * (GLA_DK ** -0.5)
    gk_ref[...] = proj_gla(C_GK, C_GV)
    gv_ref[...] = proj_gla(C_GV, C_GR).astype(BF16)
    gr_ref[...] = proj_gla(C_GR, C_AL).astype(BF16)

    gal = proj_gla(C_AL, C_GLA_END).astype(BF16)
    a_logit = jnp.dot(gal, wa2_ref[...], preferred_element_type=F32) + ba_ref[...]
    la_ref[...] = (jnp.minimum(a_logit, 0.0) - jnp.log1p(jnp.exp(-jnp.abs(a_logit)))) / GLA_TAU

    ang = inv_ref[...] * pos_ref[0].astype(F32)
    cs = jnp.concatenate([jnp.cos(ang), jnp.sin(ang)], axis=0)

    tabs = None
    rest = cs
    for _ in range(3):
        piece = rest.astype(BF16)
        rest = rest - piece.astype(F32)
        t = lax.dot_general(piece, sel_ref[...], TN_DIMS, preferred_element_type=F32)
        tabs = t if tabs is None else tabs + t
    cosf = tabs[:, :LANES] + one_ref[...]
    sneg = tabs[:, LANES:2 * LANES]
    spos = tabs[:, 2 * LANES:]

    def rope_store(c0, out_ref, scale):
        for j in range(DIFF_W // MXU_N):
            pair = proj(c0 + j * MXU_N, c0 + (j + 1) * MXU_N)
            for i in range(MXU_N // LANES):
                blk = pair[:, i * LANES:(i + 1) * LANES]
                rot = (blk * cosf + pltpu.roll(blk, LANES - ROPE_HALF, 1) * sneg
                       + pltpu.roll(blk, ROPE_HALF, 1) * spos)
                c = j * MXU_N + i * LANES
                out_ref[:, c:c + LANES] = (rot * scale).astype(BF16)

    rope_store(C_DQ, dq_ref, DIFF_DH ** -0.5 * LOG2_E)
    rope_store(C_DK, dk_ref, 1.0)
    dv_ref[...] = proj(C_DV, C_END).astype(BF16)


def _rope_tables():
    lane = np.arange(LANES)
    r = lane % DIFF_DH
    selc = np.zeros((2 * ROPE_HALF, LANES), np.float32)
    seln = np.zeros_like(selc)
    selp = np.zeros_like(selc)
    one = np.zeros((1, LANES), np.float32)
    for l in range(LANES):
        if r[l] < ROPE_DIM:
            selc[r[l] % ROPE_HALF, l] = 1.0
            if r[l] < ROPE_HALF:
                seln[ROPE_HALF + r[l], l] = -1.0
            else:
                selp[ROPE_HALF + r[l] - ROPE_HALF, l] = 1.0
        else:
            one[0, l] = 1.0
    sel = np.concatenate([selc, seln, selp], axis=1)
    return jnp.asarray(sel, dtype=BF16), jnp.asarray(one)


def _inproj(x2, positions, attn_norm, wgla, wdiff, wa2, b_a, *, tm):
    T, D = x2.shape
    nt = T // tm
    pos3 = positions.reshape(nt, 1, tm)
    inv = (ROPE_THETA ** (-jnp.arange(0, ROPE_DIM, 2, dtype=F32) / ROPE_DIM)).reshape(ROPE_HALF, 1)
    sel, one = _rope_tables()
    row = lambda i: (i, 0)
    fixed = lambda i: (0, 0)
    out_shapes = (
        jax.ShapeDtypeStruct((T, GLA_QK), F32), jax.ShapeDtypeStruct((T, GLA_QK), F32),
        jax.ShapeDtypeStruct((T, GLA_V), BF16), jax.ShapeDtypeStruct((T, GLA_V), BF16),
        jax.ShapeDtypeStruct((T, GLA_QK), F32),
        jax.ShapeDtypeStruct((T, DIFF_W), BF16), jax.ShapeDtypeStruct((T, DIFF_W), BF16),
        jax.ShapeDtypeStruct((T, DIFF_W), BF16),
    )
    return pl.pallas_call(
        _inproj_kernel,
        grid=(nt,),
        in_specs=[
            pl.BlockSpec((tm, D), row),
            pl.BlockSpec((1, 1, tm), lambda i: (i, 0, 0)),
            pl.BlockSpec((1, D), fixed),
            pl.BlockSpec((D, C_GLA_END), fixed),
            pl.BlockSpec((D, C_END), fixed),
            pl.BlockSpec((GLA_LOWRANK, GLA_QK), fixed),
            pl.BlockSpec((1, GLA_QK), fixed),
            pl.BlockSpec((ROPE_HALF, 1), fixed),
            pl.BlockSpec((2 * ROPE_HALF, 3 * LANES), fixed),
            pl.BlockSpec((1, LANES), fixed),
        ],
        out_specs=[pl.BlockSpec((tm, s.shape[1]), row) for s in out_shapes],
        out_shape=out_shapes,
        compiler_params=_cparams(("parallel",)),
        name="inproj",
    )(x2, pos3, attn_norm.reshape(1, D), wgla, wdiff, wa2, b_a.reshape(1, GLA_QK), inv, sel, one)


def _gla_kernel(gq_ref, gk_ref, gv_ref, gr_ref, la_ref, gn_ref, oa_ref, st_ref, *, n_chunks):
    @pl.when(pl.program_id(1) == 0)
    def _():
        st_ref[...] = jnp.zeros_like(st_ref)

    C = GLA_CHUNK
    tril = lax.broadcasted_iota(jnp.int32, (C, C), 0) >= lax.broadcasted_iota(jnp.int32, (C, C), 1)
    trilf = tril.astype(F32)
    head_of_lane = lax.broadcasted_iota(jnp.int32, (1, GLA_QK), 1) // GLA_DK
    hmask = [head_of_lane == h for h in range(GLA_HEADS)]
    gn = gn_ref[...]

    def by_head(x):
        return jnp.concatenate([jnp.where(hmask[h], x, jnp.zeros_like(x))
                                for h in range(GLA_HEADS)], axis=0)

    st = st_ref[...]
    for c in range(n_chunks):
        sl = pl.ds(c * C, C)
        q = gq_ref[0, sl, :]
        k = gk_ref[0, sl, :]
        la = la_ref[0, sl, :]
        v = gv_ref[0, sl, :]
        b = jnp.dot(trilf, la, precision=lax.Precision.HIGHEST, preferred_element_type=F32)
        b_last = b[C - 1:C, :]
        q_dec = (q * jnp.exp(b)).astype(BF16)
        k_inv = (k * jnp.exp(-b)).astype(BF16)
        k_end = (k * jnp.exp(b_last - b)).astype(BF16)
        decay = jnp.exp(b_last)

        q_heads = by_head(q_dec)
        att = lax.dot_general(q_heads, k_inv, NT_DIMS, preferred_element_type=F32)
        inter = lax.dot_general(q_heads, st.astype(BF16), NT_DIMS,
                                preferred_element_type=F32)
        for h in range(GLA_HEADS):
            rows = slice(h * C, (h + 1) * C)
            cols = slice(h * GLA_DV, (h + 1) * GLA_DV)
            a_h = jnp.where(tril, att[rows], 0.0).astype(BF16)
            v_h = v[:, cols]
            o_h = jnp.dot(a_h, v_h, preferred_element_type=F32) + inter[rows]
            y = _rms(o_h, gn)
            r = gr_ref[0, sl, cols].astype(F32)
            oa_ref[0, sl, cols] = (y * (r * jax.nn.sigmoid(r))).astype(BF16)
        v_heads = jnp.concatenate([v[:, h * GLA_DV:(h + 1) * GLA_DV] for h in range(GLA_HEADS)],
                                  axis=0)
        u = lax.dot_general(v_heads, by_head(k_end), TN_DIMS, preferred_element_type=F32)
        st = st * decay + u
    st_ref[...] = st


def _gla(gq, gk, gv, gr, la, gla_norm, *, B, S, ts):
    n_chunks = ts // GLA_CHUNK
    blk = lambda w: pl.BlockSpec((1, ts, w), lambda b, j: (b, j, 0))
    r3 = lambda a: a.reshape(B, S, a.shape[-1])
    return pl.pallas_call(
        functools.partial(_gla_kernel, n_chunks=n_chunks),
        grid=(B, S // ts),
        in_specs=[blk(GLA_QK), blk(GLA_QK), blk(GLA_V), blk(GLA_V), blk(GLA_QK),
                  pl.BlockSpec((1, GLA_DV), lambda b, j: (0, 0))],
        out_specs=blk(GLA_V),
        out_shape=jax.ShapeDtypeStruct((B, S, GLA_V), BF16),
        scratch_shapes=[pltpu.VMEM((GLA_DV, GLA_QK), F32)],
        compiler_params=_cparams(("parallel", "arbitrary")),
        name="gla",
    )(r3(gq), r3(gk), r3(gv), r3(gr), r3(la), gla_norm.reshape(1, GLA_DV))


def _diffattn_kernel(dq_ref, dk_ref, dv_ref, lq1_ref, lk1_ref, lq2_ref, lk2_ref, sub_ref,
                     ob_ref, m_ref, l_ref, acc_ref, *, tq, tk, lambda_init):
    S = dq_ref.shape[1]
    nq = S // tq
    R = 2 * tq
    lam = (jnp.exp(jnp.sum(lq1_ref[...] * lk1_ref[...], axis=-1, keepdims=True))
           - jnp.exp(jnp.sum(lq2_ref[...] * lk2_ref[...], axis=-1, keepdims=True))
           + lambda_init)
    first_comp = lax.broadcasted_iota(jnp.int32, (1, DIFF_DV), 1) < DIFF_DH
    sub = sub_ref[...]

    for h in range(DIFF_HEADS):
        cols = slice(h * DIFF_DV, (h + 1) * DIFF_DV)
        for qi in range(nq):
            qb = dq_ref[0, qi * tq:(qi + 1) * tq, cols]
            zero = jnp.zeros_like(qb)
            qs = jnp.concatenate([jnp.where(first_comp, qb, zero),
                                  jnp.where(first_comp, zero, qb)], axis=0)
            m_ref[...] = jnp.full(m_ref.shape, NEG_BIG, F32)
            l_ref[...] = jnp.zeros(l_ref.shape, F32)
            acc_ref[...] = jnp.zeros(acc_ref.shape, F32)

            def scores(kstart, qs=qs, cols=cols):
                kb = dk_ref[0, kstart:kstart + tk, cols]
                return lax.dot_general(kb, qs, NT_DIMS, preferred_element_type=F32)

            def update(s, kstart, masked, cols=cols, qi=qi):
                vb = dv_ref[0, kstart:kstart + tk, cols]
                if masked:
                    kpos = kstart + lax.broadcasted_iota(jnp.int32, (tk, R), 0)
                    qpos = qi * tq + lax.broadcasted_iota(jnp.int32, (tk, R), 1) % tq
                    s = jnp.where(qpos >= kpos, s, NEG_BIG)
                m_prev = m_ref[...]
                m_new = jnp.maximum(m_prev, jnp.max(s, axis=0, keepdims=True))
                alpha = jnp.exp2(m_prev - m_new)
                p = jnp.exp2(s - m_new)
                l_ref[...] = alpha * l_ref[...] + jnp.sum(p, axis=0, keepdims=True)
                pv = lax.dot_general(vb, p.astype(BF16), TN_DIMS, preferred_element_type=F32)
                acc_ref[...] = alpha * acc_ref[...] + pv
                m_ref[...] = m_new

            n_blocks = (qi + 1) * tq // tk
            diag_from = qi * tq // tk
            s_next = scores(0)
            for kb_i in range(n_blocks):
                s_cur = s_next
                if kb_i + 1 < n_blocks:
                    s_next = scores((kb_i + 1) * tk)
                update(s_cur, kb_i * tk, kb_i >= diag_from)

            o_all = acc_ref[...] / l_ref[...]
            o = o_all[:, :tq] - lam * o_all[:, tq:]
            y = o * lax.rsqrt(jnp.mean(o * o, axis=0, keepdims=True) + EPS) * sub
            y = y * (1.0 - lambda_init)
            ob_ref[0, qi * tq:(qi + 1) * tq, cols] = y.T.astype(BF16)


def _diffattn(dq, dk, dv, lq1, lk1, lq2, lk2, diff_subln, *, B, S, tq, tk, lambda_init):
    seq = pl.BlockSpec((1, S, DIFF_W), lambda b: (b, 0, 0))
    vec = lambda w: pl.BlockSpec((1, w), lambda b: (0, 0))
    r3 = lambda a: a.reshape(B, S, DIFF_W)
    return pl.pallas_call(
        functools.partial(_diffattn_kernel, tq=tq, tk=tk, lambda_init=lambda_init),
        grid=(B,),
        in_specs=[seq, seq, seq, vec(DIFF_DH), vec(DIFF_DH), vec(DIFF_DH), vec(DIFF_DH),
                  pl.BlockSpec((DIFF_DV, 1), lambda b: (0, 0))],
        out_specs=seq,
        out_shape=jax.ShapeDtypeStruct((B, S, DIFF_W), BF16),
        scratch_shapes=[pltpu.VMEM((1, 2 * tq), F32), pltpu.VMEM((1, 2 * tq), F32),
                        pltpu.VMEM((DIFF_DV, 2 * tq), F32)],
        compiler_params=_cparams(("parallel",)),
        name="diffattn",
    )(r3(dq), r3(dk), r3(dv), lq1.reshape(1, -1), lk1.reshape(1, -1), lq2.reshape(1, -1),
      lk2.reshape(1, -1), diff_subln.reshape(-1, 1))


ROUTE_W = 8


def _first_index_of(mask, lane):
    return jnp.min(jnp.where(mask, lane, LANES), axis=-1, keepdims=True)


POSTMIX_SPLIT = 2


def _postmix_kernel(x_ref, oa_ref, ob_ref, g_ref, wg_ref, wba_ref, wbb_ref, wo_ref, fn_ref,
                    wr_ref, br_ref, h_ref, n2_ref, route_ref, cnt_ref):
    rows_per = x_ref.shape[0] // POSTMIX_SPLIT
    cnt = None
    for s in range(POSTMIX_SPLIT):
        rows = slice(s * rows_per, (s + 1) * rows_per)
        h, n2, rec, c = _postmix_rows(x_ref[rows, :], oa_ref[rows, :], ob_ref[rows, :], g_ref, wg_ref,
                                      wba_ref, wbb_ref, wo_ref, fn_ref, wr_ref, br_ref)
        h_ref[rows, :] = h
        n2_ref[rows, :] = n2
        route_ref[rows, :] = rec
        cnt = c if cnt is None else cnt + c
    cnt_ref[0] = cnt


def _postmix_rows(x, oa, ob, g_ref, wg_ref, wba_ref, wbb_ref, wo_ref, fn_ref, wr_ref, br_ref):
    D = x.shape[1]
    nb = _rms(x, g_ref[...]).astype(BF16)
    y_a = jnp.dot(oa, wba_ref[...], preferred_element_type=F32)
    g_a = jnp.dot(nb, wg_ref[:, :D], preferred_element_type=F32)
    merged = jax.nn.sigmoid(g_a) * y_a
    y_b = jnp.dot(ob, wbb_ref[...], preferred_element_type=F32)
    g_b = jnp.dot(nb, wg_ref[:, D:], preferred_element_type=F32)
    merged = merged + jax.nn.sigmoid(g_b) * y_b
    h = x + jnp.dot(merged.astype(BF16), wo_ref[...], preferred_element_type=F32)
    n2 = _rms(h, fn_ref[...])

    n2_hi = n2.astype(BF16)
    n2_lo = (n2 - n2_hi.astype(F32)).astype(BF16)
    hi_prod = jnp.dot(n2_hi, wr_ref[...], preferred_element_type=F32)
    lg = (hi_prod[:, :LANES] + hi_prod[:, LANES:]
          + jnp.dot(n2_lo, wr_ref[:, :LANES], preferred_element_type=F32)) + br_ref[...]
    lane = lax.broadcasted_iota(jnp.int32, lg.shape, 1)
    is_g = lane < N_GROUPS
    g_max = jnp.max(jnp.where(is_g, lg, -jnp.inf), axis=-1, keepdims=True)
    g_exp = jnp.where(is_g, jnp.exp(lg - g_max), 0.0)
    g_prob = g_exp / jnp.sum(g_exp, axis=-1, keepdims=True)
    g_p = jnp.max(g_prob, axis=-1, keepdims=True)
    g_idx = _first_index_of(is_g & (g_prob == g_p), lane)

    e_lo = N_GROUPS + EXPERTS_PER_GROUP * g_idx
    is_e = (lane >= e_lo) & (lane < e_lo + EXPERTS_PER_GROUP)
    e_max = jnp.max(jnp.where(is_e, lg, -jnp.inf), axis=-1, keepdims=True)
    e_exp = jnp.where(is_e, jnp.exp(lg - e_max), 0.0)
    e_prob = e_exp / jnp.sum(e_exp, axis=-1, keepdims=True)
    p1 = jnp.max(jnp.where(is_e, e_prob, -1.0), axis=-1, keepdims=True)
    i1 = _first_index_of(is_e & (e_prob == p1), lane)
    rest = is_e & (lane != i1)
    p2 = jnp.max(jnp.where(rest, e_prob, -1.0), axis=-1, keepdims=True)
    i2 = _first_index_of(rest & (e_prob == p2), lane)
    den = p1 + p2
    w1 = g_p * (p1 / den)
    w2 = g_p * (p2 / den)
    rec = jnp.where(lane == 0, w1, 0.0)
    rec = jnp.where(lane == 1, w2, rec)
    rec = jnp.where(lane == 2, (i1 - N_GROUPS).astype(F32), rec)
    rec = jnp.where(lane == 3, (i2 - N_GROUPS).astype(F32), rec)
    chosen = (lane == i1 - N_GROUPS) | (lane == i2 - N_GROUPS)
    cnt = jnp.sum(chosen.astype(F32), axis=0, keepdims=True)
    return h, n2.astype(BF16), rec[:, :ROUTE_W], cnt


def _postmix(x2, oa, ob, attn_norm, wg, wba, wbb, wo, ffn_norm, wr, br, *, tm):
    T, D = x2.shape
    row = lambda i: (i, 0)
    fixed = lambda i: (0, 0)
    full = lambda a: pl.BlockSpec(a.shape, fixed)
    g = attn_norm.reshape(1, D)
    fn = ffn_norm.reshape(1, D)
    return pl.pallas_call(
        _postmix_kernel,
        grid=(T // tm,),
        in_specs=[pl.BlockSpec((tm, D), row), pl.BlockSpec((tm, GLA_V), row),
                  pl.BlockSpec((tm, DIFF_W), row), full(g), full(wg), full(wba), full(wbb),
                  full(wo), full(fn), full(wr), full(br)],
        out_specs=[pl.BlockSpec((tm, D), row), pl.BlockSpec((tm, D), row),
                   pl.BlockSpec((tm, ROUTE_W), row),
                   pl.BlockSpec((1, 1, LANES), lambda i: (i, 0, 0))],
        out_shape=(jax.ShapeDtypeStruct((T, D), F32), jax.ShapeDtypeStruct((T, D), BF16),
                   jax.ShapeDtypeStruct((T, ROUTE_W), F32),
                   jax.ShapeDtypeStruct((T // tm, 1, LANES), F32)),
        compiler_params=_cparams(("parallel",)),
        name="postmix",
    )(x2, oa, ob, g, wg, wba, wbb, wo, fn, wr, br)


DISPATCH_CHUNK = 8
POS_W = 8
POS_ROW = 2


def _token_copy(src, src_tok, dst, dst_tok, n_tok, sem):
    return pltpu.make_async_copy(
        src.at[pl.ds(pl.multiple_of(src_tok * SUBLANES, SUBLANES), n_tok * SUBLANES), :],
        dst.at[pl.ds(pl.multiple_of(dst_tok * SUBLANES, SUBLANES), n_tok * SUBLANES), :], sem)


def _for_each_piece(length, fn):
    n_full = lax.shift_right_logical(length, 3)

    def body(c, carry):
        fn(c * DISPATCH_CHUNK, DISPATCH_CHUNK)
        return carry

    lax.fori_loop(0, n_full, body, 0)
    off = n_full * DISPATCH_CHUNK
    for n in (4, 2, 1):
        has = (length & n) != 0

        @pl.when(has)
        def _(off=off, n=n):
            fn(off, n)

        off = off + jnp.where(has, n, 0)


def _dispatch_kernel(seg_ref, lo_ref, cnt_ref, fill_ref, fill_len_ref, n_tiles_ref,
                     n2_ref, route_ref, lo_row_ref, seg_row_ref,
                     xg_hbm, pos_ref, xs_buf, zero_buf, sem, fill_sem):
    tm = n2_ref.shape[0]
    n_slab = 2 * tm * SUBLANES
    i = pl.program_id(0)
    nt = pl.num_programs(0)
    slot = i % 2

    @pl.when(i == 0)
    def _():
        zero_buf[...] = jnp.zeros_like(zero_buf)
        te = zero_buf.shape[0] // SUBLANES
        for wait in (False, True):
            def fill(e, carry, wait=wait):
                def piece(off, n):
                    cp = _token_copy(zero_buf, 0, xg_hbm, fill_ref[e] + off, n, fill_sem)
                    cp.wait() if wait else cp.start()
                _for_each_piece(fill_len_ref[e], piece)
                return carry
            lax.fori_loop(0, N_EXPERTS, fill, 0)

            def fill_tile(t, carry, wait=wait):
                cp = _token_copy(zero_buf, 0, xg_hbm, t * te, te, fill_sem)
                cp.wait() if wait else cp.start()
                return carry
            lax.fori_loop(n_tiles_ref[0], xg_hbm.shape[0] // zero_buf.shape[0], fill_tile, 0)

    route = route_ref[...]
    lane = lax.broadcasted_iota(jnp.int32, (tm, LANES), 1)
    oh1 = lane == route[:, 2:3].astype(jnp.int32)
    oh2 = lane == route[:, 3:4].astype(jnp.int32)
    both = jnp.where(oh1 | oh2, 1.0, 0.0).astype(BF16)
    earlier = (lax.broadcasted_iota(jnp.int32, (tm, tm), 0)
               > lax.broadcasted_iota(jnp.int32, (tm, tm), 1))
    rank = jnp.dot(jnp.where(earlier, 1.0, 0.0).astype(BF16), both, preferred_element_type=F32)

    def pick(onehot, v):
        return jnp.sum(jnp.where(onehot, v, 0.0), axis=-1, keepdims=True)

    r1 = pick(oh1, rank)
    r2 = pick(oh2, rank)
    q1 = pick(oh1, lo_row_ref[0]) + r1
    q2 = pick(oh2, lo_row_ref[0]) + r2
    pos1 = pick(oh1, seg_row_ref[0]) + r1
    pos2 = pick(oh2, seg_row_ref[0]) + r2

    rec = jnp.where(lane == 0, q1, jnp.where(lane == 1, q2,
                    jnp.where(lane == POS_ROW, pos1, jnp.where(lane == POS_ROW + 1, pos2, 0.0))))
    rec_t = rec.T.astype(jnp.int32)
    pos_ref[0] = rec_t[:POS_W, :]
    slot_id = lax.broadcasted_iota(jnp.int32, (2 * tm, tm), 0)
    perm = jnp.where((slot_id == rec_t[0:1, :]) | (slot_id == rec_t[1:2, :]), 1.0, 0.0)
    xs = jnp.dot(perm.astype(BF16), n2_ref[...], preferred_element_type=F32)

    def wait_slot(s):
        pltpu.make_async_copy(xg_hbm.at[pl.ds(0, n_slab), :], xs_buf.at[s], sem.at[s]).wait()

    @pl.when(i >= 2)
    def _():
        wait_slot(slot)

    _store_slab(xs_buf.at[slot], xs)

    def send(e, carry):
        k = i * N_EXPERTS + e

        def piece(off, n):
            _token_copy(xs_buf.at[slot], lo_ref[k] + off, xg_hbm, seg_ref[k] + off, n,
                        sem.at[slot]).start()

        _for_each_piece(cnt_ref[k], piece)
        return carry

    lax.fori_loop(0, N_EXPERTS, send, 0)

    @pl.when(i == nt - 1)
    def _():
        wait_slot(slot)

        @pl.when(nt >= 2)
        def _():
            wait_slot(1 - slot)


def _dispatch(n2, route, cnt, *, tm, te):
    T, D = n2.shape
    nt = T // tm
    n_tiles_max = 2 * T // te + N_EXPERTS
    c = cnt[:, 0, :N_EXPERTS].astype(jnp.int32)
    count = jnp.sum(c, axis=0)
    padded = (count + te - 1) // te * te
    p_end = jnp.cumsum(padded)
    p_start = p_end - padded
    n_tiles = (p_end[-1] // te).astype(jnp.int32).reshape(1)
    tile_start = jnp.arange(n_tiles_max, dtype=jnp.int32) * te
    tile_e = jnp.sum(tile_start[:, None] >= p_end[None, :], axis=1, dtype=jnp.int32)
    tile_e = jnp.minimum(tile_e, jnp.sum(tile_start[n_tiles[0] - 1] >= p_end, dtype=jnp.int32))
    seg = p_start[None, :] + jnp.cumsum(c, axis=0) - c
    lo = jnp.cumsum(c, axis=1) - c
    lanes = lambda a: jnp.pad(a.astype(F32), ((0, 0), (0, LANES - N_EXPERTS))).reshape(nt, 1, LANES)
    flat = lambda a: a.reshape(-1).astype(jnp.int32)
    row = lambda i, *_: (i, 0)
    per_tile = pl.BlockSpec((1, 1, LANES), lambda i, *_: (i, 0, 0))
    grid_spec = pltpu.PrefetchScalarGridSpec(
        num_scalar_prefetch=6,
        grid=(nt,),
        in_specs=[pl.BlockSpec((tm, D), row), pl.BlockSpec((tm, ROUTE_W), row), per_tile, per_tile],
        out_specs=[pl.BlockSpec(memory_space=pl.ANY),
                   pl.BlockSpec((1, POS_W, tm), lambda i, *_: (i, 0, 0))],
        scratch_shapes=[pltpu.VMEM((2, 2 * tm * SUBLANES, LANES), F32),
                        pltpu.VMEM((te * SUBLANES, LANES), F32),
                        pltpu.SemaphoreType.DMA((2,)), pltpu.SemaphoreType.DMA(())],
    )
    xg, pos = pl.pallas_call(
        _dispatch_kernel,
        grid_spec=grid_spec,
        out_shape=(jax.ShapeDtypeStruct((n_tiles_max * te * SUBLANES, LANES), F32),
                   jax.ShapeDtypeStruct((nt, POS_W, tm), jnp.int32)),
        compiler_params=_cparams(("arbitrary",)),
        name="dispatch",
    )(flat(seg), flat(lo), flat(c), flat(p_start + count), flat(padded - count), n_tiles,
      n2, route, lanes(lo), lanes(seg))
    return xg, pos, tile_e, n_tiles


def _experts_kernel(tile_e_ref, n_tiles_ref, xg_ref, wg_ref, wu_ref, wd_ref, y_ref,
                    wg_bf, wu_bf, wd_bf):
    tm = xg_ref.shape[0] // SUBLANES
    i = pl.program_id(0)
    used = i < n_tiles_ref[0]

    @pl.when(used & ((i == 0) | (tile_e_ref[i] != tile_e_ref[jnp.maximum(i - 1, 0)])))
    def _():
        wg_bf[...] = wg_ref[0].astype(BF16)
        wu_bf[...] = wu_ref[0].astype(BF16)
        wd_bf[...] = wd_ref[0].astype(BF16)

    @pl.when(used)
    def _():
        x = _load_slab(xg_ref, tm).astype(BF16)
        g = jnp.dot(x, wg_bf[...], preferred_element_type=F32)
        u = jnp.dot(x, wu_bf[...], preferred_element_type=F32)
        he = (g * jax.nn.sigmoid(g)) * u
        _store_slab(y_ref, jnp.dot(he.astype(BF16), wd_bf[...], preferred_element_type=F32))

    @pl.when(i >= n_tiles_ref[0])
    def _():
        y_ref[...] = jnp.zeros_like(y_ref)


def _experts(tile_e, n_tiles, xg, wg, wu, wd, *, tm):
    D = wg.shape[1]
    n_tiles_max = xg.shape[0] // (tm * SUBLANES)
    slab = lambda f: pl.BlockSpec((tm * SUBLANES, LANES), f)
    grid_spec = pltpu.PrefetchScalarGridSpec(
        num_scalar_prefetch=2,
        grid=(n_tiles_max,),
        in_specs=[slab(lambda i, te, nt: (jnp.minimum(i, nt[0] - 1), 0)),
                  pl.BlockSpec((1, D, D_EXPERT), lambda i, te, nt: (te[i], 0, 0)),
                  pl.BlockSpec((1, D, D_EXPERT), lambda i, te, nt: (te[i], 0, 0)),
                  pl.BlockSpec((1, D_EXPERT, D), lambda i, te, nt: (te[i], 0, 0))],
        out_specs=slab(lambda i, te, nt: (i, 0)),
        scratch_shapes=[pltpu.VMEM((D, D_EXPERT), BF16), pltpu.VMEM((D, D_EXPERT), BF16),
                        pltpu.VMEM((D_EXPERT, D), BF16)],
    )
    return pl.pallas_call(
        _experts_kernel,
        grid_spec=grid_spec,
        out_shape=jax.ShapeDtypeStruct(xg.shape, F32),
        compiler_params=_cparams(("arbitrary",)),
        name="experts",
    )(tile_e, n_tiles, xg, wg, wu, wd)


def _final_kernel(idx_ref, idx_next_ref, h_ref, ys_hbm, route_ref, p_ref, wple_ref, pn_ref, wpg_ref,
                  fn_ref, o_ref, ybuf, sem, *, apply_final_norm):
    tm = h_ref.shape[0]
    i = pl.program_id(0)
    slot = i % 2

    def start(idx, s):
        for k in range(2):
            _start_token_gather(idx, POS_ROW + k, tm, ys_hbm, ybuf.at[s], k * tm, sem.at[s])

    @pl.when(i == 0)
    def _():
        start(idx_ref, 0)

    @pl.when(i + 1 < pl.num_programs(0))
    def _():
        start(idx_next_ref, 1 - slot)

    _wait_token_gather(2 * tm, ys_hbm, ybuf.at[slot], sem.at[slot])
    route = route_ref[...]
    y = route[:, 0:1] * _load_slab(ybuf.at[slot, pl.ds(0, tm * SUBLANES), :], tm)
    y = y + route[:, 1:2] * _load_slab(ybuf.at[slot, pl.ds(tm * SUBLANES, tm * SUBLANES), :], tm)
    h = h_ref[...] + y
    e = _rms(jnp.dot(p_ref[...].astype(BF16), wple_ref[...], preferred_element_type=F32), pn_ref[...])
    gate = jax.nn.sigmoid(jnp.dot(h.astype(BF16), wpg_ref[...], preferred_element_type=F32))
    h = h + gate * e
    if apply_final_norm:
        h = _rms(h, fn_ref[...])
    o_ref[...] = h


def _final(h1, ys, pos, route, p2, wple, ple_norm, wpg, final_norm, *, tm, apply_final_norm):
    T, D = h1.shape
    nt = T // tm
    row = lambda i: (i, 0)
    fixed = lambda i: (0, 0)
    full = lambda a: pl.BlockSpec(a.shape, fixed)
    pn = ple_norm.reshape(1, D)
    fn = final_norm.reshape(1, D)
    idx = pos
    smem_rows = lambda f: pl.BlockSpec((1, POS_W, tm), f, memory_space=pltpu.SMEM)
    return pl.pallas_call(
        functools.partial(_final_kernel, apply_final_norm=apply_final_norm),
        grid=(nt,),
        in_specs=[smem_rows(lambda i: (i, 0, 0)),
                  smem_rows(lambda i: (jnp.minimum(i + 1, nt - 1), 0, 0)),
                  pl.BlockSpec((tm, D), row), pl.BlockSpec(memory_space=pl.ANY),
                  pl.BlockSpec((tm, ROUTE_W), row), pl.BlockSpec((tm, p2.shape[1]), row),
                  full(wple), full(pn), full(wpg), full(fn)],
        out_specs=pl.BlockSpec((tm, D), row),
        out_shape=jax.ShapeDtypeStruct((T, D), F32),
        scratch_shapes=[pltpu.VMEM((2, 2 * tm * SUBLANES, LANES), F32),
                        pltpu.SemaphoreType.DMA((2,))],
        compiler_params=_cparams(("arbitrary",)),
        name="final",
    )(idx, idx, h1, ys, route, p2, wple, pn, wpg, fn)


SPLIT_SIZES = (GLA_QK, GLA_QK, GLA_V, GLA_V, GLA_LOWRANK, DIFF_W, DIFF_W, DIFF_W)


def _layer(h2, p2, positions, B, S, lambda_init, apply_final_norm, attn_norm, w_in, w_a2, b_a,
           gla_norm, lq1, lk1, lq2, lk2, diff_subln, w_branch_a, w_branch_b, w_out, ffn_norm,
           w_rg, b_rg, w_re, b_re, w_gate, w_up, w_down, w_ple, ple_norm, w_ple_gate, final_norm,
           *, tm, ts, tq, tk, te):
    T, D = h2.shape
    assert D == SUBLANES * LANES, "token-slab gathers need one (8, 128) tile per token"
    n_mix = sum(SPLIT_SIZES)
    wgla = w_in[:, :C_GLA_END].astype(BF16)
    wdiff = w_in[:, C_GLA_END:n_mix].astype(BF16)
    wg = w_in[:, n_mix:].astype(BF16)

    gq, gk, gv, gr, la, dq, dk, dv = _inproj(h2, positions, attn_norm, wgla, wdiff,
                                             w_a2.astype(BF16), b_a, tm=tm)
    oa = _gla(gq, gk, gv, gr, la, gla_norm, B=B, S=S, ts=ts).reshape(T, GLA_V)
    ob = _diffattn(dq, dk, dv, lq1, lk1, lq2, lk2, diff_subln, B=B, S=S, tq=tq, tk=tk,
                   lambda_init=lambda_init).reshape(T, DIFF_W)

    wr = jnp.concatenate([w_rg, w_re, jnp.zeros((D, LANES - N_GROUPS - N_EXPERTS), F32)], axis=1)
    wr_hi = wr.astype(BF16)
    wr = jnp.concatenate([wr_hi, (wr - wr_hi.astype(F32)).astype(BF16)], axis=1)
    br = jnp.concatenate([b_rg, b_re, jnp.zeros((LANES - N_GROUPS - N_EXPERTS,), F32)]).reshape(1, LANES)
    h1, n2, route, cnt = _postmix(h2, oa, ob, attn_norm, wg, w_branch_a.astype(BF16),
                                  w_branch_b.astype(BF16), w_out.astype(BF16), ffn_norm, wr, br,
                                  tm=tm)

    xg, pos, tile_e, n_tiles = _dispatch(n2, route, cnt, tm=tm, te=te)
    ys = _experts(tile_e, n_tiles, xg, w_gate, w_up, w_down, tm=te)
    return _final(h1, ys, pos, route, p2, w_ple.astype(BF16), ple_norm, w_ple_gate.astype(BF16),
                  final_norm, tm=tm, apply_final_norm=apply_final_norm)


def _block(x, p, positions, attn_norm, w_in, w_a2, b_a, gla_norm, lambda_q1, lambda_k1, lambda_q2,
           lambda_k2, diff_subln, w_branch_a, w_branch_b, w_out, ffn_norm, w_router_group,
           b_router_group, w_router_expert, b_router_expert, w_gate, w_up, w_down, w_ple, ple_norm,
           w_ple_gate, final_norm, *, tm, ts, tq, tk, te):
    B, S, D = x.shape
    depth = w_in.shape[0]
    h = x.reshape(B * S, D)
    for i in range(depth):
        lambda_init = 0.8 - 0.6 * math.exp(-0.3 * i)
        h = _layer(h, p[i].reshape(B * S, -1), positions, B, S, lambda_init, i == depth - 1,
                   attn_norm[i], w_in[i], w_a2[i], b_a[i], gla_norm[i], lambda_q1[i], lambda_k1[i],
                   lambda_q2[i], lambda_k2[i], diff_subln[i], w_branch_a[i], w_branch_b[i], w_out[i],
                   ffn_norm[i], w_router_group[i], b_router_group[i], w_router_expert[i],
                   b_router_expert[i], w_gate[i], w_up[i], w_down[i], w_ple[i], ple_norm[i],
                   w_ple_gate[i], final_norm, tm=tm, ts=ts, tq=tq, tk=tk, te=te)
    return h.reshape(B, S, D)


def kernel(x, p, positions, attn_norm, w_in, w_a2, b_a, gla_norm, lambda_q1, lambda_k1, lambda_q2, lambda_k2, diff_subln, w_branch_a, w_branch_b, w_out, ffn_norm, w_router_group, b_router_group, w_router_expert, b_router_expert, w_gate, w_up, w_down, w_ple, ple_norm, w_ple_gate, final_norm):
    S = x.shape[1]
    return _block(x, p, positions, attn_norm, w_in, w_a2, b_a, gla_norm, lambda_q1, lambda_k1,
                  lambda_q2, lambda_k2, diff_subln, w_branch_a, w_branch_b, w_out, ffn_norm,
                  w_router_group, b_router_group, w_router_expert, b_router_expert, w_gate, w_up,
                  w_down, w_ple, ple_norm, w_ple_gate, final_norm,
                  tm=512, ts=min(512, S), tq=min(512, S), tk=min(512, S), te=512)
```

```python
import functools
import math

import jax
import jax.numpy as jnp
import numpy as np
from jax import lax
from jax.experimental import pallas as pl
from jax.experimental.pallas import tpu as pltpu

EPS = 1e-6

GLA_HEADS = 4
GLA_DK = 64
GLA_DV = 128
GLA_LOWRANK = 16
GLA_TAU = 16.0
GLA_CHUNK = 64
GLA_QK = GLA_HEADS * GLA_DK
GLA_V = GLA_HEADS * GLA_DV

DIFF_HEADS = 4
DIFF_DH = 64
DIFF_DV = 2 * DIFF_DH
DIFF_W = DIFF_HEADS * DIFF_DV
ROPE_THETA = 500000.0
ROPE_DIM = DIFF_DH // 4
ROPE_HALF = ROPE_DIM // 2

N_GROUPS = 4
EXPERTS_PER_GROUP = 8
N_EXPERTS = N_GROUPS * EXPERTS_PER_GROUP
D_EXPERT = 256

LANES = 128
MXU_N = 256
VMEM_LIMIT = 56 * 1024 * 1024

BF16 = jnp.bfloat16
F32 = jnp.float32
NT_DIMS = (((1,), (1,)), ((), ()))
TN_DIMS = (((0,), (0,)), ((), ()))
NEG_BIG = -1e30
LOG2_E = math.log2(math.e)


def _rms(x, g):
    return x * lax.rsqrt(jnp.mean(x * x, axis=-1, keepdims=True) + EPS) * g


def _cparams(semantics):
    return pltpu.CompilerParams(dimension_semantics=semantics, vmem_limit_bytes=VMEM_LIMIT)


SUBLANES = 8


def _store_slab(ref, x):
    n = x.shape[0]
    for j in range(SUBLANES):
        ref[pl.ds(j, n, stride=SUBLANES), :] = x[:, j * LANES:(j + 1) * LANES]


def _load_slab(ref, n):
    return jnp.concatenate([ref[pl.ds(j, n, stride=SUBLANES), :] for j in range(SUBLANES)], axis=1)


C_GQ = 0
C_GK = C_GQ + GLA_QK
C_GV = C_GK + GLA_QK
C_GR = C_GV + GLA_V
C_AL = C_GR + GLA_V
C_GLA_END = C_AL + GLA_LOWRANK
C_DQ = 0
C_DK = C_DQ + DIFF_W
C_DV = C_DK + DIFF_W
C_END = C_DV + DIFF_W


def _inproj_kernel(x_ref, pos_ref, g_ref, wgla_ref, wdiff_ref, wa2_ref, ba_ref, inv_ref, sel_ref,
                   one_ref,
                   gq_ref, gk_ref, gv_ref, gr_ref, la_ref, dq_ref, dk_ref, dv_ref):
    nb = _rms(x_ref[...], g_ref[...]).astype(BF16)

    def proj_gla(c0, c1):
        return jnp.dot(nb, wgla_ref[:, c0:c1], preferred_element_type=F32)

    def proj(c0, c1):
        return jnp.dot(nb, wdiff_ref[:, c0:c1], preferred_element_type=F32)

    gq_ref[...] = proj_gla(C_GQ, C_GK) * (GLA_DK ** -0.5)
    gk_ref[...] = proj_gla(C_GK, C_GV)
    gv_ref[...] = proj_gla(C_GV, C_GR).astype(BF16)
    gr_ref[...] = proj_gla(C_GR, C_AL).astype(BF16)

    gal = proj_gla(C_AL, C_GLA_END).astype(BF16)
    a_logit = jnp.dot(gal, wa2_ref[...], preferred_element_type=F32) + ba_ref[...]
    la_ref[...] = (jnp.minimum(a_logit, 0.0) - jnp.log1p(jnp.exp(-jnp.abs(a_logit)))) / GLA_TAU

    ang = inv_ref[...] * pos_ref[0].astype(F32)
    cs = jnp.concatenate([jnp.cos(ang), jnp.sin(ang)], axis=0)

    tabs = None
    rest = cs
    for _ in range(3):
        piece = rest.astype(BF16)
        rest = rest - piece.astype(F32)
        t = lax.dot_general(piece, sel_ref[...], TN_DIMS, preferred_element_type=F32)
        tabs = t if tabs is None else tabs + t
    cosf = tabs[:, :LANES] + one_ref[...]
    sneg = tabs[:, LANES:2 * LANES]
    spos = tabs[:, 2 * LANES:]

    def rope_store(c0, out_ref, scale):
        for j in range(DIFF_W // MXU_N):
            pair = proj(c0 + j * MXU_N, c0 + (j + 1) * MXU_N)
            for i in range(MXU_N // LANES):
                blk = pair[:, i * LANES:(i + 1) * LANES]
                rot = (blk * cosf + pltpu.roll(blk, LANES - ROPE_HALF, 1) * sneg
                       + pltpu.roll(blk, ROPE_HALF, 1) * spos)
                c = j * MXU_N + i * LANES
                out_ref[:, c:c + LANES] = (rot * scale).astype(BF16)

    rope_store(C_DQ, dq_ref, DIFF_DH ** -0.5 * LOG2_E)
    rope_store(C_DK, dk_ref, 1.0)
    dv_ref[...] = proj(C_DV, C_END).astype(BF16)


def _rope_tables():
    lane = np.arange(LANES)
    r = lane % DIFF_DH
    selc = np.zeros((2 * ROPE_HALF, LANES), np.float32)
    seln = np.zeros_like(selc)
    selp = np.zeros_like(selc)
    one = np.zeros((1, LANES), np.float32)
    for l in range(LANES):
        if r[l] < ROPE_DIM:
            selc[r[l] % ROPE_HALF, l] = 1.0
            if r[l] < ROPE_HALF:
                seln[ROPE_HALF + r[l], l] = -1.0
            else:
                selp[ROPE_HALF + r[l] - ROPE_HALF, l] = 1.0
        else:
            one[0, l] = 1.0
    sel = np.concatenate([selc, seln, selp], axis=1)
    return jnp.asarray(sel, dtype=BF16), jnp.asarray(one)


def _inproj(x2, positions, attn_norm, wgla, wdiff, wa2, b_a, *, tm):
    T, D = x2.shape
    nt = T // tm
    pos3 = positions.reshape(nt, 1, tm)
    inv = (ROPE_THETA ** (-jnp.arange(0, ROPE_DIM, 2, dtype=F32) / ROPE_DIM)).reshape(ROPE_HALF, 1)
    sel, one = _rope_tables()
    row = lambda i: (i, 0)
    fixed = lambda i: (0, 0)
    out_shapes = (
        jax.ShapeDtypeStruct((T, GLA_QK), F32), jax.ShapeDtypeStruct((T, GLA_QK), F32),
        jax.ShapeDtypeStruct((T, GLA_V), BF16), jax.ShapeDtypeStruct((T, GLA_V), BF16),
        jax.ShapeDtypeStruct((T, GLA_QK), F32),
        jax.ShapeDtypeStruct((T, DIFF_W), BF16), jax.ShapeDtypeStruct((T, DIFF_W), BF16),
        jax.ShapeDtypeStruct((T, DIFF_W), BF16),
    )
    return pl.pallas_call(
        _inproj_kernel,
        grid=(nt,),
        in_specs=[
            pl.BlockSpec((tm, D), row),
            pl.BlockSpec((1, 1, tm), lambda i: (i, 0, 0)),
            pl.BlockSpec((1, D), fixed),
            pl.BlockSpec((D, C_GLA_END), fixed),
            pl.BlockSpec((D, C_END), fixed),
            pl.BlockSpec((GLA_LOWRANK, GLA_QK), fixed),
            pl.BlockSpec((1, GLA_QK), fixed),
            pl.BlockSpec((ROPE_HALF, 1), fixed),
            pl.BlockSpec((2 * ROPE_HALF, 3 * LANES), fixed),
            pl.BlockSpec((1, LANES), fixed),
        ],
        out_specs=[pl.BlockSpec((tm, s.shape[1]), row) for s in out_shapes],
        out_shape=out_shapes,
        compiler_params=_cparams(("parallel",)),
        name="inproj",
    )(x2, pos3, attn_norm.reshape(1, D), wgla, wdiff, wa2, b_a.reshape(1, GLA_QK), inv, sel, one)


def _gla_kernel(gq_ref, gk_ref, gv_ref, gr_ref, la_ref, gn_ref, oa_ref, st_ref, *, n_chunks):
    @pl.when(pl.program_id(1) == 0)
    def _():
        st_ref[...] = jnp.zeros_like(st_ref)

    C = GLA_CHUNK
    tril = lax.broadcasted_iota(jnp.int32, (C, C), 0) >= lax.broadcasted_iota(jnp.int32, (C, C), 1)
    trilf = tril.astype(F32)
    head_of_lane = lax.broadcasted_iota(jnp.int32, (1, GLA_QK), 1) // GLA_DK
    hmask = [head_of_lane == h for h in range(GLA_HEADS)]
    gn = gn_ref[...]

    def by_head(x):
        return jnp.concatenate([jnp.where(hmask[h], x, jnp.zeros_like(x))
                                for h in range(GLA_HEADS)], axis=0)

    st = st_ref[...]
    for c in range(n_chunks):
        sl = pl.ds(c * C, C)
        q = gq_ref[0, sl, :]
        k = gk_ref[0, sl, :]
        la = la_ref[0, sl, :]
        v = gv_ref[0, sl, :]
        b = jnp.dot(trilf, la, precision=lax.Precision.HIGHEST, preferred_element_type=F32)
        b_last = b[C - 1:C, :]
        q_dec = (q * jnp.exp(b)).astype(BF16)
        k_inv = (k * jnp.exp(-b)).astype(BF16)
        k_end = (k * jnp.exp(b_last - b)).astype(BF16)
        decay = jnp.exp(b_last)

        q_heads = by_head(q_dec)
        att = lax.dot_general(q_heads, k_inv, NT_DIMS, preferred_element_type=F32)
        inter = lax.dot_general(q_heads, st.astype(BF16), NT_DIMS,
                                preferred_element_type=F32)
        for h in range(GLA_HEADS):
            rows = slice(h * C, (h + 1) * C)
            cols = slice(h * GLA_DV, (h + 1) * GLA_DV)
            a_h = jnp.where(tril, att[rows], 0.0).astype(BF16)
            v_h = v[:, cols]
            o_h = jnp.dot(a_h, v_h, preferred_element_type=F32) + inter[rows]
            y = _rms(o_h, gn)
            r = gr_ref[0, sl, cols].astype(F32)
            oa_ref[0, sl, cols] = (y * (r * jax.nn.sigmoid(r))).astype(BF16)
        v_heads = jnp.concatenate([v[:, h * GLA_DV:(h + 1) * GLA_DV] for h in range(GLA_HEADS)],
                                  axis=0)
        u = lax.dot_general(v_heads, by_head(k_end), TN_DIMS, preferred_element_type=F32)
        st = st * decay + u
    st_ref[...] = st


def _gla(gq, gk, gv, gr, la, gla_norm, *, B, S, ts):
    n_chunks = ts // GLA_CHUNK
    blk = lambda w: pl.BlockSpec((1, ts, w), lambda b, j: (b, j, 0))
    r3 = lambda a: a.reshape(B, S, a.shape[-1])
    return pl.pallas_call(
        functools.partial(_gla_kernel, n_chunks=n_chunks),
        grid=(B, S // ts),
        in_specs=[blk(GLA_QK), blk(GLA_QK), blk(GLA_V), blk(GLA_V), blk(GLA_QK),
                  pl.BlockSpec((1, GLA_DV), lambda b, j: (0, 0))],
        out_specs=blk(GLA_V),
        out_shape=jax.ShapeDtypeStruct((B, S, GLA_V), BF16),
        scratch_shapes=[pltpu.VMEM((GLA_DV, GLA_QK), F32)],
        compiler_params=_cparams(("parallel", "arbitrary")),
        name="gla",
    )(r3(gq), r3(gk), r3(gv), r3(gr), r3(la), gla_norm.reshape(1, GLA_DV))


def _diffattn_kernel(dq_ref, dk_ref, dv_ref, lq1_ref, lk1_ref, lq2_ref, lk2_ref, sub_ref,
                     ob_ref, m_ref, l_ref, acc_ref, *, tq, tk, lambda_init):
    S = dq_ref.shape[1]
    nq = S // tq
    R = 2 * tq
    lam = (jnp.exp(jnp.sum(lq1_ref[...] * lk1_ref[...], axis=-1, keepdims=True))
           - jnp.exp(jnp.sum(lq2_ref[...] * lk2_ref[...], axis=-1, keepdims=True))
           + lambda_init)
    first_comp = lax.broadcasted_iota(jnp.int32, (1, DIFF_DV), 1) < DIFF_DH
    sub = sub_ref[...]

    for h in range(DIFF_HEADS):
        cols = slice(h * DIFF_DV, (h + 1) * DIFF_DV)
        for qi in range(nq):
            qb = dq_ref[0, qi * tq:(qi + 1) * tq, cols]
            zero = jnp.zeros_like(qb)
            qs = jnp.concatenate([jnp.where(first_comp, qb, zero),
                                  jnp.where(first_comp, zero, qb)], axis=0)
            m_ref[...] = jnp.full(m_ref.shape, NEG_BIG, F32)
            l_ref[...] = jnp.zeros(l_ref.shape, F32)
            acc_ref[...] = jnp.zeros(acc_ref.shape, F32)

            def scores(kstart, qs=qs, cols=cols):
                kb = dk_ref[0, kstart:kstart + tk, cols]
                return lax.dot_general(kb, qs, NT_DIMS, preferred_element_type=F32)

            def update(s, kstart, masked, cols=cols, qi=qi):
                vb = dv_ref[0, kstart:kstart + tk, cols]
                if masked:
                    kpos = kstart + lax.broadcasted_iota(jnp.int32, (tk, R), 0)
                    qpos = qi * tq + lax.broadcasted_iota(jnp.int32, (tk, R), 1) % tq
                    s = jnp.where(qpos >= kpos, s, NEG_BIG)
                m_prev = m_ref[...]
                m_new = jnp.maximum(m_prev, jnp.max(s, axis=0, keepdims=True))
                alpha = jnp.exp2(m_prev - m_new)
                p = jnp.exp2(s - m_new)
                l_ref[...] = alpha * l_ref[...] + jnp.sum(p, axis=0, keepdims=True)
                pv = lax.dot_general(vb, p.astype(BF16), TN_DIMS, preferred_element_type=F32)
                acc_ref[...] = alpha * acc_ref[...] + pv
                m_ref[...] = m_new

            n_blocks = (qi + 1) * tq // tk
            diag_from = qi * tq // tk
            s_next = scores(0)
            for kb_i in range(n_blocks):
                s_cur = s_next
                if kb_i + 1 < n_blocks:
                    s_next = scores((kb_i + 1) * tk)
                update(s_cur, kb_i * tk, kb_i >= diag_from)

            o_all = acc_ref[...] / l_ref[...]
            o = o_all[:, :tq] - lam * o_all[:, tq:]
            y = o * lax.rsqrt(jnp.mean(o * o, axis=0, keepdims=True) + EPS) * sub
            y = y * (1.0 - lambda_init)
            ob_ref[0, qi * tq:(qi + 1) * tq, cols] = y.T.astype(BF16)


def _diffattn(dq, dk, dv, lq1, lk1, lq2, lk2, diff_subln, *, B, S, tq, tk, lambda_init):
    seq = pl.BlockSpec((1, S, DIFF_W), lambda b: (b, 0, 0))
    vec = lambda w: pl.BlockSpec((1, w), lambda b: (0, 0))
    r3 = lambda a: a.reshape(B, S, DIFF_W)
    return pl.pallas_call(
        functools.partial(_diffattn_kernel, tq=tq, tk=tk, lambda_init=lambda_init),
        grid=(B,),
        in_specs=[seq, seq, seq, vec(DIFF_DH), vec(DIFF_DH), vec(DIFF_DH), vec(DIFF_DH),
                  pl.BlockSpec((DIFF_DV, 1), lambda b: (0, 0))],
        out_specs=seq,
        out_shape=jax.ShapeDtypeStruct((B, S, DIFF_W), BF16),
        scratch_shapes=[pltpu.VMEM((1, 2 * tq), F32), pltpu.VMEM((1, 2 * tq), F32),
                        pltpu.VMEM((DIFF_DV, 2 * tq), F32)],
        compiler_params=_cparams(("parallel",)),
        name="diffattn",
    )(r3(dq), r3(dk), r3(dv), lq1.reshape(1, -1), lk1.reshape(1, -1), lq2.reshape(1, -1),
      lk2.reshape(1, -1), diff_subln.reshape(-1, 1))


ROUTE_W = 8


def _first_index_of(mask, lane):
    return jnp.min(jnp.where(mask, lane, LANES), axis=-1, keepdims=True)


POSTMIX_SPLIT = 2


def _postmix_kernel(x_ref, oa_ref, ob_ref, g_ref, wg_ref, wba_ref, wbb_ref, wo_ref, fn_ref,
                    wr_ref, br_ref, h_ref, n2_ref, route_ref, cnt_ref):
    rows_per = x_ref.shape[0] // POSTMIX_SPLIT
    cnt = None
    for s in range(POSTMIX_SPLIT):
        rows = slice(s * rows_per, (s + 1) * rows_per)
        h, n2, rec, c = _postmix_rows(x_ref[rows, :], oa_ref[rows, :], ob_ref[rows, :], g_ref, wg_ref,
                                      wba_ref, wbb_ref, wo_ref, fn_ref, wr_ref, br_ref)
        h_ref[rows, :] = h
        n2_ref[rows, :] = n2
        route_ref[rows, :] = rec
        cnt = c if cnt is None else cnt + c
    cnt_ref[0] = cnt


def _postmix_rows(x, oa, ob, g_ref, wg_ref, wba_ref, wbb_ref, wo_ref, fn_ref, wr_ref, br_ref):
    D = x.shape[1]
    nb = _rms(x, g_ref[...]).astype(BF16)
    y_a = jnp.dot(oa, wba_ref[...], preferred_element_type=F32)
    g_a = jnp.dot(nb, wg_ref[:, :D], preferred_element_type=F32)
    merged = jax.nn.sigmoid(g_a) * y_a
    y_b = jnp.dot(ob, wbb_ref[...], preferred_element_type=F32)
    g_b = jnp.dot(nb, wg_ref[:, D:], preferred_element_type=F32)
    merged = merged + jax.nn.sigmoid(g_b) * y_b
    h = x + jnp.dot(merged.astype(BF16), wo_ref[...], preferred_element_type=F32)
    n2 = _rms(h, fn_ref[...])

    n2_hi = n2.astype(BF16)
    n2_lo = (n2 - n2_hi.astype(F32)).astype(BF16)
    hi_prod = jnp.dot(n2_hi, wr_ref[...], preferred_element_type=F32)
    lg = (hi_prod[:, :LANES] + hi_prod[:, LANES:]
          + jnp.dot(n2_lo, wr_ref[:, :LANES], preferred_element_type=F32)) + br_ref[...]
    lane = lax.broadcasted_iota(jnp.int32, lg.shape, 1)
    is_g = lane < N_GROUPS
    g_max = jnp.max(jnp.where(is_g, lg, -jnp.inf), axis=-1, keepdims=True)
    g_exp = jnp.where(is_g, jnp.exp(lg - g_max), 0.0)
    g_prob = g_exp / jnp.sum(g_exp, axis=-1, keepdims=True)
    g_p = jnp.max(g_prob, axis=-1, keepdims=True)
    g_idx = _first_index_of(is_g & (g_prob == g_p), lane)

    e_lo = N_GROUPS + EXPERTS_PER_GROUP * g_idx
    is_e = (lane >= e_lo) & (lane < e_lo + EXPERTS_PER_GROUP)
    e_max = jnp.max(jnp.where(is_e, lg, -jnp.inf), axis=-1, keepdims=True)
    e_exp = jnp.where(is_e, jnp.exp(lg - e_max), 0.0)
    e_prob = e_exp / jnp.sum(e_exp, axis=-1, keepdims=True)
    p1 = jnp.max(jnp.where(is_e, e_prob, -1.0), axis=-1, keepdims=True)
    i1 = _first_index_of(is_e & (e_prob == p1), lane)
    rest = is_e & (lane != i1)
    p2 = jnp.max(jnp.where(rest, e_prob, -1.0), axis=-1, keepdims=True)
    i2 = _first_index_of(rest & (e_prob == p2), lane)
    den = p1 + p2
    w1 = g_p * (p1 / den)
    w2 = g_p * (p2 / den)
    rec = jnp.where(lane == 0, w1, 0.0)
    rec = jnp.where(lane == 1, w2, rec)
    rec = jnp.where(lane == 2, (i1 - N_GROUPS).astype(F32), rec)
    rec = jnp.where(lane == 3, (i2 - N_GROUPS).astype(F32), rec)
    chosen = (lane == i1 - N_GROUPS) | (lane == i2 - N_GROUPS)
    cnt = jnp.sum(chosen.astype(F32), axis=0, keepdims=True)
    return h, n2.astype(BF16), rec[:, :ROUTE_W], cnt


def _postmix(x2, oa, ob, attn_norm, wg, wba, wbb, wo, ffn_norm, wr, br, *, tm):
    T, D = x2.shape
    row = lambda i: (i, 0)
    fixed = lambda i: (0, 0)
    full = lambda a: pl.BlockSpec(a.shape, fixed)
    g = attn_norm.reshape(1, D)
    fn = ffn_norm.reshape(1, D)
    return pl.pallas_call(
        _postmix_kernel,
        grid=(T // tm,),
        in_specs=[pl.BlockSpec((tm, D), row), pl.BlockSpec((tm, GLA_V), row),
                  pl.BlockSpec((tm, DIFF_W), row), full(g), full(wg), full(wba), full(wbb),
                  full(wo), full(fn), full(wr), full(br)],
        out_specs=[pl.BlockSpec((tm, D), row), pl.BlockSpec((tm, D), row),
                   pl.BlockSpec((tm, ROUTE_W), row),
                   pl.BlockSpec((1, 1, LANES), lambda i: (i, 0, 0))],
        out_shape=(jax.ShapeDtypeStruct((T, D), F32), jax.ShapeDtypeStruct((T, D), BF16),
                   jax.ShapeDtypeStruct((T, ROUTE_W), F32),
                   jax.ShapeDtypeStruct((T // tm, 1, LANES), F32)),
        compiler_params=_cparams(("parallel",)),
        name="postmix",
    )(x2, oa, ob, g, wg, wba, wbb, wo, fn, wr, br)


DISPATCH_CHUNK = 8
SLOT_W = 8


def _token_copy(src, src_tok, dst, dst_tok, n_tok, sem):
    return pltpu.make_async_copy(
        src.at[pl.ds(pl.multiple_of(src_tok * SUBLANES, SUBLANES), n_tok * SUBLANES), :],
        dst.at[pl.ds(pl.multiple_of(dst_tok * SUBLANES, SUBLANES), n_tok * SUBLANES), :], sem)


def _for_each_piece(length, fn):
    n_full = lax.shift_right_logical(length, 3)

    def body(c, carry):
        fn(c * DISPATCH_CHUNK, DISPATCH_CHUNK)
        return carry

    lax.fori_loop(0, n_full, body, 0)
    off = n_full * DISPATCH_CHUNK
    for n in (4, 2, 1):
        has = (length & n) != 0

        @pl.when(has)
        def _(off=off, n=n):
            fn(off, n)

        off = off + jnp.where(has, n, 0)


def _dispatch_kernel(seg_ref, lo_ref, cnt_ref, fill_ref, fill_len_ref, n_tiles_ref,
                     n2_ref, route_ref, lo_row_ref,
                     xg_hbm, slot_ref, xs_buf, zero_buf, sem, fill_sem):
    tm = n2_ref.shape[0]
    n_slab = 2 * tm * SUBLANES
    i = pl.program_id(0)
    nt = pl.num_programs(0)
    slot = i % 2

    @pl.when(i == 0)
    def _():
        zero_buf[...] = jnp.zeros_like(zero_buf)
        te = zero_buf.shape[0] // SUBLANES
        for wait in (False, True):
            def fill(e, carry, wait=wait):
                def piece(off, n):
                    cp = _token_copy(zero_buf, 0, xg_hbm, fill_ref[e] + off, n, fill_sem)
                    cp.wait() if wait else cp.start()
                _for_each_piece(fill_len_ref[e], piece)
                return carry
            lax.fori_loop(0, N_EXPERTS, fill, 0)

            def fill_tile(t, carry, wait=wait):
                cp = _token_copy(zero_buf, 0, xg_hbm, t * te, te, fill_sem)
                cp.wait() if wait else cp.start()
                return carry
            lax.fori_loop(n_tiles_ref[0], xg_hbm.shape[0] // zero_buf.shape[0], fill_tile, 0)

    route = route_ref[...]
    lane = lax.broadcasted_iota(jnp.int32, (tm, LANES), 1)
    oh1 = lane == route[:, 2:3].astype(jnp.int32)
    oh2 = lane == route[:, 3:4].astype(jnp.int32)
    both = jnp.where(oh1 | oh2, 1.0, 0.0).astype(BF16)
    earlier = (lax.broadcasted_iota(jnp.int32, (tm, tm), 0)
               > lax.broadcasted_iota(jnp.int32, (tm, tm), 1))
    rank = jnp.dot(jnp.where(earlier, 1.0, 0.0).astype(BF16), both, preferred_element_type=F32)

    def pick(onehot, v):
        return jnp.sum(jnp.where(onehot, v, 0.0), axis=-1, keepdims=True)

    r1 = pick(oh1, rank)
    r2 = pick(oh2, rank)
    q1 = pick(oh1, lo_row_ref[0]) + r1
    q2 = pick(oh2, lo_row_ref[0]) + r2
    rec = jnp.where(lane == 0, q1, jnp.where(lane == 1, q2, 0.0))
    slot_ref[...] = rec[:, :SLOT_W]

    rec_t = rec.T.astype(jnp.int32)
    slot_id = lax.broadcasted_iota(jnp.int32, (2 * tm, tm), 0)
    perm = jnp.where((slot_id == rec_t[0:1, :]) | (slot_id == rec_t[1:2, :]), 1.0, 0.0)
    xs = jnp.dot(perm.astype(BF16), n2_ref[...], preferred_element_type=F32)

    def wait_slot(s):
        pltpu.make_async_copy(xg_hbm.at[pl.ds(0, n_slab), :], xs_buf.at[s], sem.at[s]).wait()

    @pl.when(i >= 2)
    def _():
        wait_slot(slot)

    _store_slab(xs_buf.at[slot], xs)

    def send(e, carry):
        k = i * N_EXPERTS + e

        def piece(off, n):
            _token_copy(xs_buf.at[slot], lo_ref[k] + off, xg_hbm, seg_ref[k] + off, n,
                        sem.at[slot]).start()

        _for_each_piece(cnt_ref[k], piece)
        return carry

    lax.fori_loop(0, N_EXPERTS, send, 0)

    @pl.when(i == nt - 1)
    def _():
        wait_slot(slot)

        @pl.when(nt >= 2)
        def _():
            wait_slot(1 - slot)


def _dispatch(n2, route, cnt, *, tm, te):
    T, D = n2.shape
    nt = T // tm
    n_tiles_max = 2 * T // te + N_EXPERTS
    c = cnt[:, 0, :N_EXPERTS].astype(jnp.int32)
    count = jnp.sum(c, axis=0)
    padded = (count + te - 1) // te * te
    p_end = jnp.cumsum(padded)
    p_start = p_end - padded
    n_tiles = (p_end[-1] // te).astype(jnp.int32).reshape(1)
    tile_start = jnp.arange(n_tiles_max, dtype=jnp.int32) * te
    tile_e = jnp.sum(tile_start[:, None] >= p_end[None, :], axis=1, dtype=jnp.int32)
    tile_e = jnp.minimum(tile_e, jnp.sum(tile_start[n_tiles[0] - 1] >= p_end, dtype=jnp.int32))
    seg = p_start[None, :] + jnp.cumsum(c, axis=0) - c
    lo = jnp.cumsum(c, axis=1) - c
    lanes = lambda a: jnp.pad(a.astype(F32), ((0, 0), (0, LANES - N_EXPERTS))).reshape(nt, 1, LANES)
    flat = lambda a: a.reshape(-1).astype(jnp.int32)
    row = lambda i, *_: (i, 0)
    per_tile = pl.BlockSpec((1, 1, LANES), lambda i, *_: (i, 0, 0))
    grid_spec = pltpu.PrefetchScalarGridSpec(
        num_scalar_prefetch=6,
        grid=(nt,),
        in_specs=[pl.BlockSpec((tm, D), row), pl.BlockSpec((tm, ROUTE_W), row), per_tile],
        out_specs=[pl.BlockSpec(memory_space=pl.ANY), pl.BlockSpec((tm, SLOT_W), row)],
        scratch_shapes=[pltpu.VMEM((2, 2 * tm * SUBLANES, LANES), F32),
                        pltpu.VMEM((te * SUBLANES, LANES), F32),
                        pltpu.SemaphoreType.DMA((2,)), pltpu.SemaphoreType.DMA(())],
    )
    tables = (flat(seg), flat(lo), flat(c))
    xg, slots = pl.pallas_call(
        _dispatch_kernel,
        grid_spec=grid_spec,
        out_shape=(jax.ShapeDtypeStruct((n_tiles_max * te * SUBLANES, LANES), F32),
                   jax.ShapeDtypeStruct((T, SLOT_W), F32)),
        compiler_params=_cparams(("arbitrary",)),
        name="dispatch",
    )(*tables, flat(p_start + count), flat(padded - count), n_tiles, n2, route, lanes(lo))
    return xg, slots, tables, tile_e, n_tiles


def _experts_kernel(tile_e_ref, n_tiles_ref, xg_ref, wg_ref, wu_ref, wd_ref, y_ref,
                    wg_bf, wu_bf, wd_bf):
    tm = xg_ref.shape[0] // SUBLANES
    i = pl.program_id(0)
    used = i < n_tiles_ref[0]

    @pl.when(used & ((i == 0) | (tile_e_ref[i] != tile_e_ref[jnp.maximum(i - 1, 0)])))
    def _():
        wg_bf[...] = wg_ref[0].astype(BF16)
        wu_bf[...] = wu_ref[0].astype(BF16)
        wd_bf[...] = wd_ref[0].astype(BF16)

    @pl.when(used)
    def _():
        x = _load_slab(xg_ref, tm).astype(BF16)
        g = jnp.dot(x, wg_bf[...], preferred_element_type=F32)
        u = jnp.dot(x, wu_bf[...], preferred_element_type=F32)
        he = (g * jax.nn.sigmoid(g)) * u
        _store_slab(y_ref, jnp.dot(he.astype(BF16), wd_bf[...], preferred_element_type=F32))

    @pl.when(i >= n_tiles_ref[0])
    def _():
        y_ref[...] = jnp.zeros_like(y_ref)


def _experts(tile_e, n_tiles, xg, wg, wu, wd, *, tm):
    D = wg.shape[1]
    n_tiles_max = xg.shape[0] // (tm * SUBLANES)
    slab = lambda f: pl.BlockSpec((tm * SUBLANES, LANES), f)
    grid_spec = pltpu.PrefetchScalarGridSpec(
        num_scalar_prefetch=2,
        grid=(n_tiles_max,),
        in_specs=[slab(lambda i, te, nt: (jnp.minimum(i, nt[0] - 1), 0)),
                  pl.BlockSpec((1, D, D_EXPERT), lambda i, te, nt: (te[i], 0, 0)),
                  pl.BlockSpec((1, D, D_EXPERT), lambda i, te, nt: (te[i], 0, 0)),
                  pl.BlockSpec((1, D_EXPERT, D), lambda i, te, nt: (te[i], 0, 0))],
        out_specs=slab(lambda i, te, nt: (i, 0)),
        scratch_shapes=[pltpu.VMEM((D, D_EXPERT), BF16), pltpu.VMEM((D, D_EXPERT), BF16),
                        pltpu.VMEM((D_EXPERT, D), BF16)],
    )
    return pl.pallas_call(
        _experts_kernel,
        grid_spec=grid_spec,
        out_shape=jax.ShapeDtypeStruct(xg.shape, F32),
        compiler_params=_cparams(("arbitrary",)),
        name="experts",
    )(tile_e, n_tiles, xg, wg, wu, wd)


def _final_kernel(seg_ref, lo_ref, cnt_ref, h_ref, ys_hbm, route_ref, slot_ref, p_ref, wple_ref,
                  pn_ref, wpg_ref, fn_ref, o_ref, ybuf, sem, *, apply_final_norm):
    tm = h_ref.shape[0]
    i = pl.program_id(0)
    nt = pl.num_programs(0)
    slot = i % 2

    def fetch(tile, s):
        def run(e, carry):
            k = tile * N_EXPERTS + e

            def piece(off, n):
                _token_copy(ys_hbm, seg_ref[k] + off, ybuf.at[s], lo_ref[k] + off, n,
                            sem.at[s]).start()

            _for_each_piece(cnt_ref[k], piece)
            return carry

        lax.fori_loop(0, N_EXPERTS, run, 0)

    @pl.when(i == 0)
    def _():
        fetch(0, 0)

    @pl.when(i + 1 < nt)
    def _():
        fetch(i + 1, 1 - slot)

    pltpu.make_async_copy(ys_hbm.at[pl.ds(0, 2 * tm * SUBLANES), :], ybuf.at[slot],
                          sem.at[slot]).wait()
    y_sorted = _load_slab(ybuf.at[slot], 2 * tm).astype(BF16)
    route = route_ref[...]
    slots = slot_ref[...].astype(jnp.int32)
    col = lax.broadcasted_iota(jnp.int32, (tm, 2 * tm), 1)
    comb = jnp.where(col == slots[:, 0:1], route[:, 0:1],
                     jnp.where(col == slots[:, 1:2], route[:, 1:2], 0.0))
    h = h_ref[...] + jnp.dot(comb.astype(BF16), y_sorted, preferred_element_type=F32)
    e = _rms(jnp.dot(p_ref[...].astype(BF16), wple_ref[...], preferred_element_type=F32), pn_ref[...])
    gate = jax.nn.sigmoid(jnp.dot(h.astype(BF16), wpg_ref[...], preferred_element_type=F32))
    h = h + gate * e
    if apply_final_norm:
        h = _rms(h, fn_ref[...])
    o_ref[...] = h


def _final(h1, ys, slots, tables, route, p2, wple, ple_norm, wpg, final_norm, *, tm,
           apply_final_norm):
    T, D = h1.shape
    nt = T // tm
    row = lambda i, *_: (i, 0)
    fixed = lambda i, *_: (0, 0)
    full = lambda a: pl.BlockSpec(a.shape, fixed)
    pn = ple_norm.reshape(1, D)
    fn = final_norm.reshape(1, D)
    grid_spec = pltpu.PrefetchScalarGridSpec(
        num_scalar_prefetch=3,
        grid=(nt,),
        in_specs=[pl.BlockSpec((tm, D), row), pl.BlockSpec(memory_space=pl.ANY),
                  pl.BlockSpec((tm, ROUTE_W), row), pl.BlockSpec((tm, SLOT_W), row),
                  pl.BlockSpec((tm, p2.shape[1]), row),
                  full(wple), full(pn), full(wpg), full(fn)],
        out_specs=pl.BlockSpec((tm, D), row),
        scratch_shapes=[pltpu.VMEM((2, 2 * tm * SUBLANES, LANES), F32),
                        pltpu.SemaphoreType.DMA((2,))],
    )
    return pl.pallas_call(
        functools.partial(_final_kernel, apply_final_norm=apply_final_norm),
        grid_spec=grid_spec,
        out_shape=jax.ShapeDtypeStruct((T, D), F32),
        compiler_params=_cparams(("arbitrary",)),
        name="final",
    )(*tables, h1, ys, route, slots, p2, wple, pn, wpg, fn)


SPLIT_SIZES = (GLA_QK, GLA_QK, GLA_V, GLA_V, GLA_LOWRANK, DIFF_W, DIFF_W, DIFF_W)


def _layer(h2, p2, positions, B, S, lambda_init, apply_final_norm, attn_norm, w_in, w_a2, b_a,
           gla_norm, lq1, lk1, lq2, lk2, diff_subln, w_branch_a, w_branch_b, w_out, ffn_norm,
           w_rg, b_rg, w_re, b_re, w_gate, w_up, w_down, w_ple, ple_norm, w_ple_gate, final_norm,
           *, tm, ts, tq, tk, te):
    T, D = h2.shape
    assert D == SUBLANES * LANES, "token-slab gathers need one (8, 128) tile per token"
    n_mix = sum(SPLIT_SIZES)
    wgla = w_in[:, :C_GLA_END].astype(BF16)
    wdiff = w_in[:, C_GLA_END:n_mix].astype(BF16)
    wg = w_in[:, n_mix:].astype(BF16)

    gq, gk, gv, gr, la, dq, dk, dv = _inproj(h2, positions, attn_norm, wgla, wdiff,
                                             w_a2.astype(BF16), b_a, tm=tm)
    oa = _gla(gq, gk, gv, gr, la, gla_norm, B=B, S=S, ts=ts).reshape(T, GLA_V)
    ob = _diffattn(dq, dk, dv, lq1, lk1, lq2, lk2, diff_subln, B=B, S=S, tq=tq, tk=tk,
                   lambda_init=lambda_init).reshape(T, DIFF_W)

    wr = jnp.concatenate([w_rg, w_re, jnp.zeros((D, LANES - N_GROUPS - N_EXPERTS), F32)], axis=1)
    wr_hi = wr.astype(BF16)
    wr = jnp.concatenate([wr_hi, (wr - wr_hi.astype(F32)).astype(BF16)], axis=1)
    br = jnp.concatenate([b_rg, b_re, jnp.zeros((LANES - N_GROUPS - N_EXPERTS,), F32)]).reshape(1, LANES)
    h1, n2, route, cnt = _postmix(h2, oa, ob, attn_norm, wg, w_branch_a.astype(BF16),
                                  w_branch_b.astype(BF16), w_out.astype(BF16), ffn_norm, wr, br,
                                  tm=tm)

    xg, slots, tables, tile_e, n_tiles = _dispatch(n2, route, cnt, tm=tm, te=te)
    ys = _experts(tile_e, n_tiles, xg, w_gate, w_up, w_down, tm=te)
    return _final(h1, ys, slots, tables, route, p2, w_ple.astype(BF16), ple_norm,
                  w_ple_gate.astype(BF16), final_norm, tm=tm, apply_final_norm=apply_final_norm)


def _block(x, p, positions, attn_norm, w_in, w_a2, b_a, gla_norm, lambda_q1, lambda_k1, lambda_q2,
           lambda_k2, diff_subln, w_branch_a, w_branch_b, w_out, ffn_norm, w_router_group,
           b_router_group, w_router_expert, b_router_expert, w_gate, w_up, w_down, w_ple, ple_norm,
           w_ple_gate, final_norm, *, tm, ts, tq, tk, te):
    B, S, D = x.shape
    depth = w_in.shape[0]
    h = x.reshape(B * S, D)
    for i in range(depth):
        lambda_init = 0.8 - 0.6 * math.exp(-0.3 * i)
        h = _layer(h, p[i].reshape(B * S, -1), positions, B, S, lambda_init, i == depth - 1,
                   attn_norm[i], w_in[i], w_a2[i], b_a[i], gla_norm[i], lambda_q1[i], lambda_k1[i],
                   lambda_q2[i], lambda_k2[i], diff_subln[i], w_branch_a[i], w_branch_b[i], w_out[i],
                   ffn_norm[i], w_router_group[i], b_router_group[i], w_router_expert[i],
                   b_router_expert[i], w_gate[i], w_up[i], w_down[i], w_ple[i], ple_norm[i],
                   w_ple_gate[i], final_norm, tm=tm, ts=ts, tq=tq, tk=tk, te=te)
    return h.reshape(B, S, D)


def kernel(x, p, positions, attn_norm, w_in, w_a2, b_a, gla_norm, lambda_q1, lambda_k1, lambda_q2, lambda_k2, diff_subln, w_branch_a, w_branch_b, w_out, ffn_norm, w_router_group, b_router_group, w_router_expert, b_router_expert, w_gate, w_up, w_down, w_ple, ple_norm, w_ple_gate, final_norm):
    S = x.shape[1]
    return _block(x, p, positions, attn_norm, w_in, w_a2, b_a, gla_norm, lambda_q1, lambda_k1,
                  lambda_q2, lambda_k2, diff_subln, w_branch_a, w_branch_b, w_out, ffn_norm,
                  w_router_group, b_router_group, w_router_expert, b_router_expert, w_gate, w_up,
                  w_down, w_ple, ple_norm, w_ple_gate, final_norm,
                  tm=512, ts=min(512, S), tq=min(512, S), tk=min(512, S), te=512)
```

```python
import functools
import math

import jax
import jax.numpy as jnp
import numpy as np
from jax import lax
from jax.experimental import pallas as pl
from jax.experimental.pallas import tpu as pltpu

EPS = 1e-6

GLA_HEADS = 4
GLA_DK = 64
GLA_DV = 128
GLA_LOWRANK = 16
GLA_TAU = 16.0
GLA_CHUNK = 64
GLA_QK = GLA_HEADS * GLA_DK
GLA_V = GLA_HEADS * GLA_DV

DIFF_HEADS = 4
DIFF_DH = 64
DIFF_DV = 2 * DIFF_DH
DIFF_W = DIFF_HEADS * DIFF_DV
ROPE_THETA = 500000.0
ROPE_DIM = DIFF_DH // 4
ROPE_HALF = ROPE_DIM // 2

N_GROUPS = 4
EXPERTS_PER_GROUP = 8
N_EXPERTS = N_GROUPS * EXPERTS_PER_GROUP
D_EXPERT = 256

LANES = 128
MXU_N = 256
VMEM_LIMIT = 56 * 1024 * 1024

BF16 = jnp.bfloat16
F32 = jnp.float32
NT_DIMS = (((1,), (1,)), ((), ()))
TN_DIMS = (((0,), (0,)), ((), ()))
NEG_BIG = -1e30
LOG2_E = math.log2(math.e)


def _rms(x, g):
    return x * lax.rsqrt(jnp.mean(x * x, axis=-1, keepdims=True) + EPS) * g


def _cparams(semantics):
    return pltpu.CompilerParams(dimension_semantics=semantics, vmem_limit_bytes=VMEM_LIMIT)


SUBLANES = 8


def _store_slab(ref, x):
    n = x.shape[0]
    for j in range(SUBLANES):
        ref[pl.ds(j, n, stride=SUBLANES), :] = x[:, j * LANES:(j + 1) * LANES]


def _load_slab(ref, n):
    return jnp.concatenate([ref[pl.ds(j, n, stride=SUBLANES), :] for j in range(SUBLANES)], axis=1)


C_GQ = 0
C_GK = C_GQ + GLA_QK
C_GV = C_GK + GLA_QK
C_GR = C_GV + GLA_V
C_AL = C_GR + GLA_V
C_GLA_END = C_AL + GLA_LOWRANK
C_DQ = 0
C_DK = C_DQ + DIFF_W
C_DV = C_DK + DIFF_W
C_END = C_DV + DIFF_W


def _inproj_kernel(x_ref, pos_ref, g_ref, wgla_ref, wdiff_ref, wa2_ref, ba_ref, inv_ref, sel_ref,
                   one_ref,
                   gq_ref, gk_ref, gv_ref, gr_ref, la_ref, dq_ref, dk_ref, dv_ref):
    nb = _rms(x_ref[...], g_ref[...]).astype(BF16)

    def proj_gla(c0, c1):
        return jnp.dot(nb, wgla_ref[:, c0:c1], preferred_element_type=F32)

    def proj(c0, c1):
        return jnp.dot(nb, wdiff_ref[:, c0:c1], preferred_element_type=F32)

    gq_ref[...] = proj_gla(C_GQ, C_GK) * (GLA_DK ** -0.5)
    gk_ref[...] = proj_gla(C_GK, C_GV)
    gv_ref[...] = proj_gla(C_GV, C_GR).astype(BF16)
    gr_ref[...] = proj_gla(C_GR, C_AL).astype(BF16)

    gal = proj_gla(C_AL, C_GLA_END).astype(BF16)
    a_logit = jnp.dot(gal, wa2_ref[...], preferred_element_type=F32) + ba_ref[...]
    la_ref[...] = (jnp.minimum(a_logit, 0.0) - jnp.log1p(jnp.exp(-jnp.abs(a_logit)))) / GLA_TAU

    ang = inv_ref[...] * pos_ref[0].astype(F32)
    cs = jnp.concatenate([jnp.cos(ang), jnp.sin(ang)], axis=0)

    tabs = None
    rest = cs
    for _ in range(3):
        piece = rest.astype(BF16)
        rest = rest - piece.astype(F32)
        t = lax.dot_general(piece, sel_ref[...], TN_DIMS, preferred_element_type=F32)
        tabs = t if tabs is None else tabs + t
    cosf = tabs[:, :LANES] + one_ref[...]
    sneg = tabs[:, LANES:2 * LANES]
    spos = tabs[:, 2 * LANES:]

    def rope_store(c0, out_ref, scale):
        for j in range(DIFF_W // MXU_N):
            pair = proj(c0 + j * MXU_N, c0 + (j + 1) * MXU_N)
            for i in range(MXU_N // LANES):
                blk = pair[:, i * LANES:(i + 1) * LANES]
                rot = (blk * cosf + pltpu.roll(blk, LANES - ROPE_HALF, 1) * sneg
                       + pltpu.roll(blk, ROPE_HALF, 1) * spos)
                c = j * MXU_N + i * LANES
                out_ref[:, c:c + LANES] = (rot * scale).astype(BF16)

    rope_store(C_DQ, dq_ref, DIFF_DH ** -0.5 * LOG2_E)
    rope_store(C_DK, dk_ref, 1.0)
    dv_ref[...] = proj(C_DV, C_END).astype(BF16)


def _rope_tables():
    lane = np.arange(LANES)
    r = lane % DIFF_DH
    selc = np.zeros((2 * ROPE_HALF, LANES), np.float32)
    seln = np.zeros_like(selc)
    selp = np.zeros_like(selc)
    one = np.zeros((1, LANES), np.float32)
    for l in range(LANES):
        if r[l] < ROPE_DIM:
            selc[r[l] % ROPE_HALF, l] = 1.0
            if r[l] < ROPE_HALF:
                seln[ROPE_HALF + r[l], l] = -1.0
            else:
                selp[ROPE_HALF + r[l] - ROPE_HALF, l] = 1.0
        else:
            one[0, l] = 1.0
    sel = np.concatenate([selc, seln, selp], axis=1)
    return jnp.asarray(sel, dtype=BF16), jnp.asarray(one)


def _inproj(x2, positions, attn_norm, wgla, wdiff, wa2, b_a, *, tm):
    T, D = x2.shape
    nt = T // tm
    pos3 = positions.reshape(nt, 1, tm)
    inv = (ROPE_THETA ** (-jnp.arange(0, ROPE_DIM, 2, dtype=F32) / ROPE_DIM)).reshape(ROPE_HALF, 1)
    sel, one = _rope_tables()
    row = lambda i: (i, 0)
    fixed = lambda i: (0, 0)
    out_shapes = (
        jax.ShapeDtypeStruct((T, GLA_QK), F32), jax.ShapeDtypeStruct((T, GLA_QK), F32),
        jax.ShapeDtypeStruct((T, GLA_V), BF16), jax.ShapeDtypeStruct((T, GLA_V), BF16),
        jax.ShapeDtypeStruct((T, GLA_QK), F32),
        jax.ShapeDtypeStruct((T, DIFF_W), BF16), jax.ShapeDtypeStruct((T, DIFF_W), BF16),
        jax.ShapeDtypeStruct((T, DIFF_W), BF16),
    )
    return pl.pallas_call(
        _inproj_kernel,
        grid=(nt,),
        in_specs=[
            pl.BlockSpec((tm, D), row),
            pl.BlockSpec((1, 1, tm), lambda i: (i, 0, 0)),
            pl.BlockSpec((1, D), fixed),
            pl.BlockSpec((D, C_GLA_END), fixed),
            pl.BlockSpec((D, C_END), fixed),
            pl.BlockSpec((GLA_LOWRANK, GLA_QK), fixed),
            pl.BlockSpec((1, GLA_QK), fixed),
            pl.BlockSpec((ROPE_HALF, 1), fixed),
            pl.BlockSpec((2 * ROPE_HALF, 3 * LANES), fixed),
            pl.BlockSpec((1, LANES), fixed),
        ],
        out_specs=[pl.BlockSpec((tm, s.shape[1]), row) for s in out_shapes],
        out_shape=out_shapes,
        compiler_params=_cparams(("parallel",)),
        name="inproj",
    )(x2, pos3, attn_norm.reshape(1, D), wgla, wdiff, wa2, b_a.reshape(1, GLA_QK), inv, sel, one)


def _gla_kernel(gq_ref, gk_ref, gv_ref, gr_ref, la_ref, gn_ref, oa_ref, st_ref, *, n_chunks):
    @pl.when(pl.program_id(1) == 0)
    def _():
        st_ref[...] = jnp.zeros_like(st_ref)

    C = GLA_CHUNK
    tril = lax.broadcasted_iota(jnp.int32, (C, C), 0) >= lax.broadcasted_iota(jnp.int32, (C, C), 1)
    trilb = jnp.where(tril, 1.0, 0.0).astype(BF16)
    head_of_lane = lax.broadcasted_iota(jnp.int32, (1, GLA_QK), 1) // GLA_DK
    hmask = [head_of_lane == h for h in range(GLA_HEADS)]
    gn = gn_ref[...]

    def by_head(x):
        return jnp.concatenate([jnp.where(hmask[h], x, jnp.zeros_like(x))
                                for h in range(GLA_HEADS)], axis=0)

    chunks = range(n_chunks)
    sls = [pl.ds(c * C, C) for c in chunks]
    vs = [gv_ref[0, sl, :] for sl in sls]

    def cumsum(la):
        total, rest = None, la
        for _ in range(3):
            piece = rest.astype(BF16)
            rest = rest - piece.astype(F32)
            t = jnp.dot(trilb, piece, preferred_element_type=F32)
            total = t if total is None else total + t
        return total

    bs = [cumsum(la_ref[0, sl, :]) for sl in sls]
    b_lasts = [b[C - 1:C, :] for b in bs]
    q_heads = [by_head((gq_ref[0, sl, :] * jnp.exp(b)).astype(BF16)) for sl, b in zip(sls, bs)]
    k_invs = [(gk_ref[0, sl, :] * jnp.exp(-b)).astype(BF16) for sl, b in zip(sls, bs)]
    k_ends = [by_head((gk_ref[0, sl, :] * jnp.exp(bl - b)).astype(BF16))
              for sl, b, bl in zip(sls, bs, b_lasts)]
    decays = [jnp.exp(bl) for bl in b_lasts]
    atts = [lax.dot_general(qh, ki, NT_DIMS, preferred_element_type=F32)
            for qh, ki in zip(q_heads, k_invs)]
    v_heads = [jnp.concatenate([v[:, h * GLA_DV:(h + 1) * GLA_DV] for h in range(GLA_HEADS)], axis=0)
               for v in vs]
    us = [lax.dot_general(vh, ke, TN_DIMS, preferred_element_type=F32)
          for vh, ke in zip(v_heads, k_ends)]

    st = st_ref[...]
    states = []
    for c in chunks:
        states.append(st.astype(BF16))
        st = st * decays[c] + us[c]
    st_ref[...] = st

    inters = [lax.dot_general(qh, s_in, NT_DIMS, preferred_element_type=F32)
              for qh, s_in in zip(q_heads, states)]
    for c in chunks:
        for h in range(GLA_HEADS):
            rows = slice(h * C, (h + 1) * C)
            cols = slice(h * GLA_DV, (h + 1) * GLA_DV)
            a_h = jnp.where(tril, atts[c][rows], 0.0).astype(BF16)
            o_h = jnp.dot(a_h, vs[c][:, cols], preferred_element_type=F32) + inters[c][rows]
            y = _rms(o_h, gn)
            r = gr_ref[0, sls[c], cols].astype(F32)
            oa_ref[0, sls[c], cols] = (y * (r * jax.nn.sigmoid(r))).astype(BF16)


def _gla(gq, gk, gv, gr, la, gla_norm, *, B, S, ts):
    n_chunks = ts // GLA_CHUNK
    blk = lambda w: pl.BlockSpec((1, ts, w), lambda b, j: (b, j, 0))
    r3 = lambda a: a.reshape(B, S, a.shape[-1])
    return pl.pallas_call(
        functools.partial(_gla_kernel, n_chunks=n_chunks),
        grid=(B, S // ts),
        in_specs=[blk(GLA_QK), blk(GLA_QK), blk(GLA_V), blk(GLA_V), blk(GLA_QK),
                  pl.BlockSpec((1, GLA_DV), lambda b, j: (0, 0))],
        out_specs=blk(GLA_V),
        out_shape=jax.ShapeDtypeStruct((B, S, GLA_V), BF16),
        scratch_shapes=[pltpu.VMEM((GLA_DV, GLA_QK), F32)],
        compiler_params=_cparams(("parallel", "arbitrary")),
        name="gla",
    )(r3(gq), r3(gk), r3(gv), r3(gr), r3(la), gla_norm.reshape(1, GLA_DV))


def _diffattn_kernel(dq_ref, dk_ref, dv_ref, lq1_ref, lk1_ref, lq2_ref, lk2_ref, sub_ref,
                     ob_ref, m_ref, l_ref, acc_ref, *, tq, tk, lambda_init):
    S = dq_ref.shape[1]
    nq = S // tq
    R = 2 * tq
    lam = (jnp.exp(jnp.sum(lq1_ref[...] * lk1_ref[...], axis=-1, keepdims=True))
           - jnp.exp(jnp.sum(lq2_ref[...] * lk2_ref[...], axis=-1, keepdims=True))
           + lambda_init)
    first_comp = lax.broadcasted_iota(jnp.int32, (1, DIFF_DV), 1) < DIFF_DH
    sub = sub_ref[...]

    for h in range(DIFF_HEADS):
        cols = slice(h * DIFF_DV, (h + 1) * DIFF_DV)
        for qi in range(nq):
            qb = dq_ref[0, qi * tq:(qi + 1) * tq, cols]
            zero = jnp.zeros_like(qb)
            qs = jnp.concatenate([jnp.where(first_comp, qb, zero),
                                  jnp.where(first_comp, zero, qb)], axis=0)
            m_ref[...] = jnp.full(m_ref.shape, NEG_BIG, F32)
            l_ref[...] = jnp.zeros(l_ref.shape, F32)
            acc_ref[...] = jnp.zeros(acc_ref.shape, F32)

            def scores(kstart, qs=qs, cols=cols):
                kb = dk_ref[0, kstart:kstart + tk, cols]
                return lax.dot_general(kb, qs, NT_DIMS, preferred_element_type=F32)

            def update(s, kstart, masked, cols=cols, qi=qi):
                vb = dv_ref[0, kstart:kstart + tk, cols]
                if masked:
                    kpos = kstart + lax.broadcasted_iota(jnp.int32, (tk, R), 0)
                    qpos = qi * tq + lax.broadcasted_iota(jnp.int32, (tk, R), 1) % tq
                    s = jnp.where(qpos >= kpos, s, NEG_BIG)
                m_prev = m_ref[...]
                m_new = jnp.maximum(m_prev, jnp.max(s, axis=0, keepdims=True))
                alpha = jnp.exp2(m_prev - m_new)
                p = jnp.exp2(s - m_new)
                l_ref[...] = alpha * l_ref[...] + jnp.sum(p, axis=0, keepdims=True)
                pv = lax.dot_general(vb, p.astype(BF16), TN_DIMS, preferred_element_type=F32)
                acc_ref[...] = alpha * acc_ref[...] + pv
                m_ref[...] = m_new

            n_blocks = (qi + 1) * tq // tk
            diag_from = qi * tq // tk
            s_next = scores(0)
            for kb_i in range(n_blocks):
                s_cur = s_next
                if kb_i + 1 < n_blocks:
                    s_next = scores((kb_i + 1) * tk)
                update(s_cur, kb_i * tk, kb_i >= diag_from)

            o_all = acc_ref[...] / l_ref[...]
            o = o_all[:, :tq] - lam * o_all[:, tq:]
            y = o * lax.rsqrt(jnp.mean(o * o, axis=0, keepdims=True) + EPS) * sub
            y = y * (1.0 - lambda_init)
            ob_ref[0, qi * tq:(qi + 1) * tq, cols] = y.T.astype(BF16)


def _diffattn(dq, dk, dv, lq1, lk1, lq2, lk2, diff_subln, *, B, S, tq, tk, lambda_init):
    seq = pl.BlockSpec((1, S, DIFF_W), lambda b: (b, 0, 0))
    vec = lambda w: pl.BlockSpec((1, w), lambda b: (0, 0))
    r3 = lambda a: a.reshape(B, S, DIFF_W)
    return pl.pallas_call(
        functools.partial(_diffattn_kernel, tq=tq, tk=tk, lambda_init=lambda_init),
        grid=(B,),
        in_specs=[seq, seq, seq, vec(DIFF_DH), vec(DIFF_DH), vec(DIFF_DH), vec(DIFF_DH),
                  pl.BlockSpec((DIFF_DV, 1), lambda b: (0, 0))],
        out_specs=seq,
        out_shape=jax.ShapeDtypeStruct((B, S, DIFF_W), BF16),
        scratch_shapes=[pltpu.VMEM((1, 2 * tq), F32), pltpu.VMEM((1, 2 * tq), F32),
                        pltpu.VMEM((DIFF_DV, 2 * tq), F32)],
        compiler_params=_cparams(("parallel",)),
        name="diffattn",
    )(r3(dq), r3(dk), r3(dv), lq1.reshape(1, -1), lk1.reshape(1, -1), lq2.reshape(1, -1),
      lk2.reshape(1, -1), diff_subln.reshape(-1, 1))


ROUTE_W = 8


def _first_index_of(mask, lane):
    return jnp.min(jnp.where(mask, lane, LANES), axis=-1, keepdims=True)


POSTMIX_SPLIT = 2


def _postmix_kernel(x_ref, oa_ref, ob_ref, g_ref, wg_ref, wba_ref, wbb_ref, wo_ref, fn_ref,
                    wr_ref, br_ref, h_ref, n2_ref, route_ref, cnt_ref):
    rows_per = x_ref.shape[0] // POSTMIX_SPLIT
    cnt = None
    for s in range(POSTMIX_SPLIT):
        rows = slice(s * rows_per, (s + 1) * rows_per)
        h, n2, rec, c = _postmix_rows(x_ref[rows, :], oa_ref[rows, :], ob_ref[rows, :], g_ref, wg_ref,
                                      wba_ref, wbb_ref, wo_ref, fn_ref, wr_ref, br_ref)
        h_ref[rows, :] = h
        n2_ref[rows, :] = n2
        route_ref[rows, :] = rec
        cnt = c if cnt is None else cnt + c
    cnt_ref[0] = cnt


def _postmix_rows(x, oa, ob, g_ref, wg_ref, wba_ref, wbb_ref, wo_ref, fn_ref, wr_ref, br_ref):
    D = x.shape[1]
    nb = _rms(x, g_ref[...]).astype(BF16)
    y_a = jnp.dot(oa, wba_ref[...], preferred_element_type=F32)
    g_a = jnp.dot(nb, wg_ref[:, :D], preferred_element_type=F32)
    merged = jax.nn.sigmoid(g_a) * y_a
    y_b = jnp.dot(ob, wbb_ref[...], preferred_element_type=F32)
    g_b = jnp.dot(nb, wg_ref[:, D:], preferred_element_type=F32)
    merged = merged + jax.nn.sigmoid(g_b) * y_b
    h = x + jnp.dot(merged.astype(BF16), wo_ref[...], preferred_element_type=F32)
    n2 = _rms(h, fn_ref[...])

    n2_hi = n2.astype(BF16)
    n2_lo = (n2 - n2_hi.astype(F32)).astype(BF16)
    hi_prod = jnp.dot(n2_hi, wr_ref[...], preferred_element_type=F32)
    lg = (hi_prod[:, :LANES] + hi_prod[:, LANES:]
          + jnp.dot(n2_lo, wr_ref[:, :LANES], preferred_element_type=F32)) + br_ref[...]
    lane = lax.broadcasted_iota(jnp.int32, lg.shape, 1)
    is_g = lane < N_GROUPS
    g_max = jnp.max(jnp.where(is_g, lg, -jnp.inf), axis=-1, keepdims=True)
    g_exp = jnp.where(is_g, jnp.exp(lg - g_max), 0.0)
    g_prob = g_exp / jnp.sum(g_exp, axis=-1, keepdims=True)
    g_p = jnp.max(g_prob, axis=-1, keepdims=True)
    g_idx = _first_index_of(is_g & (g_prob == g_p), lane)

    e_lo = N_GROUPS + EXPERTS_PER_GROUP * g_idx
    is_e = (lane >= e_lo) & (lane < e_lo + EXPERTS_PER_GROUP)
    e_max = jnp.max(jnp.where(is_e, lg, -jnp.inf), axis=-1, keepdims=True)
    e_exp = jnp.where(is_e, jnp.exp(lg - e_max), 0.0)
    e_prob = e_exp / jnp.sum(e_exp, axis=-1, keepdims=True)
    p1 = jnp.max(jnp.where(is_e, e_prob, -1.0), axis=-1, keepdims=True)
    i1 = _first_index_of(is_e & (e_prob == p1), lane)
    rest = is_e & (lane != i1)
    p2 = jnp.max(jnp.where(rest, e_prob, -1.0), axis=-1, keepdims=True)
    i2 = _first_index_of(rest & (e_prob == p2), lane)
    den = p1 + p2
    w1 = g_p * (p1 / den)
    w2 = g_p * (p2 / den)
    rec = jnp.where(lane == 0, w1, 0.0)
    rec = jnp.where(lane == 1, w2, rec)
    rec = jnp.where(lane == 2, (i1 - N_GROUPS).astype(F32), rec)
    rec = jnp.where(lane == 3, (i2 - N_GROUPS).astype(F32), rec)
    chosen = (lane == i1 - N_GROUPS) | (lane == i2 - N_GROUPS)
    cnt = jnp.sum(chosen.astype(F32), axis=0, keepdims=True)
    return h, n2.astype(BF16), rec[:, :ROUTE_W], cnt


def _postmix(x2, oa, ob, attn_norm, wg, wba, wbb, wo, ffn_norm, wr, br, *, tm):
    T, D = x2.shape
    row = lambda i: (i, 0)
    fixed = lambda i: (0, 0)
    full = lambda a: pl.BlockSpec(a.shape, fixed)
    g = attn_norm.reshape(1, D)
    fn = ffn_norm.reshape(1, D)
    return pl.pallas_call(
        _postmix_kernel,
        grid=(T // tm,),
        in_specs=[pl.BlockSpec((tm, D), row), pl.BlockSpec((tm, GLA_V), row),
                  pl.BlockSpec((tm, DIFF_W), row), full(g), full(wg), full(wba), full(wbb),
                  full(wo), full(fn), full(wr), full(br)],
        out_specs=[pl.BlockSpec((tm, D), row), pl.BlockSpec((tm, D), row),
                   pl.BlockSpec((tm, ROUTE_W), row),
                   pl.BlockSpec((1, 1, LANES), lambda i: (i, 0, 0))],
        out_shape=(jax.ShapeDtypeStruct((T, D), F32), jax.ShapeDtypeStruct((T, D), BF16),
                   jax.ShapeDtypeStruct((T, ROUTE_W), F32),
                   jax.ShapeDtypeStruct((T // tm, 1, LANES), F32)),
        compiler_params=_cparams(("parallel",)),
        name="postmix",
    )(x2, oa, ob, g, wg, wba, wbb, wo, fn, wr, br)


DISPATCH_CHUNK = 8
SLOT_W = 8


def _token_copy(src, src_tok, dst, dst_tok, n_tok, sem):
    return pltpu.make_async_copy(
        src.at[pl.ds(pl.multiple_of(src_tok * SUBLANES, SUBLANES), n_tok * SUBLANES), :],
        dst.at[pl.ds(pl.multiple_of(dst_tok * SUBLANES, SUBLANES), n_tok * SUBLANES), :], sem)


def _for_each_piece(length, fn):
    n_full = lax.shift_right_logical(length, 3)

    def body(c, carry):
        fn(c * DISPATCH_CHUNK, DISPATCH_CHUNK)
        return carry

    lax.fori_loop(0, n_full, body, 0)
    off = n_full * DISPATCH_CHUNK
    for n in (4, 2, 1):
        has = (length & n) != 0

        @pl.when(has)
        def _(off=off, n=n):
            fn(off, n)

        off = off + jnp.where(has, n, 0)


def _dispatch_kernel(seg_ref, lo_ref, cnt_ref, fill_ref, fill_len_ref, n_tiles_ref,
                     n2_ref, route_ref, lo_row_ref,
                     xg_hbm, slot_ref, xs_buf, zero_buf, sem, fill_sem):
    tm = n2_ref.shape[0]
    n_slab = 2 * tm * SUBLANES
    i = pl.program_id(0)
    nt = pl.num_programs(0)
    slot = i % 2

    @pl.when(i == 0)
    def _():
        zero_buf[...] = jnp.zeros_like(zero_buf)
        te = zero_buf.shape[0] // SUBLANES
        for wait in (False, True):
            def fill(e, carry, wait=wait):
                def piece(off, n):
                    cp = _token_copy(zero_buf, 0, xg_hbm, fill_ref[e] + off, n, fill_sem)
                    cp.wait() if wait else cp.start()
                _for_each_piece(fill_len_ref[e], piece)
                return carry
            lax.fori_loop(0, N_EXPERTS, fill, 0)

            def fill_tile(t, carry, wait=wait):
                cp = _token_copy(zero_buf, 0, xg_hbm, t * te, te, fill_sem)
                cp.wait() if wait else cp.start()
                return carry
            lax.fori_loop(n_tiles_ref[0], xg_hbm.shape[0] // zero_buf.shape[0], fill_tile, 0)

    route = route_ref[...]
    lane = lax.broadcasted_iota(jnp.int32, (tm, LANES), 1)
    oh1 = lane == route[:, 2:3].astype(jnp.int32)
    oh2 = lane == route[:, 3:4].astype(jnp.int32)
    both = jnp.where(oh1 | oh2, 1.0, 0.0).astype(BF16)
    earlier = (lax.broadcasted_iota(jnp.int32, (tm, tm), 0)
               > lax.broadcasted_iota(jnp.int32, (tm, tm), 1))
    rank = jnp.dot(jnp.where(earlier, 1.0, 0.0).astype(BF16), both, preferred_element_type=F32)

    def pick(onehot, v):
        return jnp.sum(jnp.where(onehot, v, 0.0), axis=-1, keepdims=True)

    r1 = pick(oh1, rank)
    r2 = pick(oh2, rank)
    q1 = pick(oh1, lo_row_ref[0]) + r1
    q2 = pick(oh2, lo_row_ref[0]) + r2
    rec = jnp.where(lane == 0, q1, jnp.where(lane == 1, q2, 0.0))
    slot_ref[...] = rec[:, :SLOT_W]

    rec_t = rec.T.astype(jnp.int32)
    slot_id = lax.broadcasted_iota(jnp.int32, (2 * tm, tm), 0)
    perm = jnp.where((slot_id == rec_t[0:1, :]) | (slot_id == rec_t[1:2, :]), 1.0, 0.0)
    xs = jnp.dot(perm.astype(BF16), n2_ref[...], preferred_element_type=F32)

    def wait_slot(s):
        pltpu.make_async_copy(xg_hbm.at[pl.ds(0, n_slab), :], xs_buf.at[s], sem.at[s]).wait()

    @pl.when(i >= 2)
    def _():
        wait_slot(slot)

    _store_slab(xs_buf.at[slot], xs)

    def send(e, carry):
        k = i * N_EXPERTS + e

        def piece(off, n):
            _token_copy(xs_buf.at[slot], lo_ref[k] + off, xg_hbm, seg_ref[k] + off, n,
                        sem.at[slot]).start()

        _for_each_piece(cnt_ref[k], piece)
        return carry

    lax.fori_loop(0, N_EXPERTS, send, 0)

    @pl.when(i == nt - 1)
    def _():
        wait_slot(slot)

        @pl.when(nt >= 2)
        def _():
            wait_slot(1 - slot)


def _dispatch(n2, route, cnt, *, tm, te):
    T, D = n2.shape
    nt = T // tm
    n_tiles_max = 2 * T // te + N_EXPERTS
    c = cnt[:, 0, :N_EXPERTS].astype(jnp.int32)
    count = jnp.sum(c, axis=0)
    padded = (count + te - 1) // te * te
    p_end = jnp.cumsum(padded)
    p_start = p_end - padded
    n_tiles = (p_end[-1] // te).astype(jnp.int32).reshape(1)
    tile_start = jnp.arange(n_tiles_max, dtype=jnp.int32) * te
    tile_e = jnp.sum(tile_start[:, None] >= p_end[None, :], axis=1, dtype=jnp.int32)
    tile_e = jnp.minimum(tile_e, jnp.sum(tile_start[n_tiles[0] - 1] >= p_end, dtype=jnp.int32))
    seg = p_start[None, :] + jnp.cumsum(c, axis=0) - c
    lo = jnp.cumsum(c, axis=1) - c
    lanes = lambda a: jnp.pad(a.astype(F32), ((0, 0), (0, LANES - N_EXPERTS))).reshape(nt, 1, LANES)
    flat = lambda a: a.reshape(-1).astype(jnp.int32)
    row = lambda i, *_: (i, 0)
    per_tile = pl.BlockSpec((1, 1, LANES), lambda i, *_: (i, 0, 0))
    grid_spec = pltpu.PrefetchScalarGridSpec(
        num_scalar_prefetch=6,
        grid=(nt,),
        in_specs=[pl.BlockSpec((tm, D), row), pl.BlockSpec((tm, ROUTE_W), row), per_tile],
        out_specs=[pl.BlockSpec(memory_space=pl.ANY), pl.BlockSpec((tm, SLOT_W), row)],
        scratch_shapes=[pltpu.VMEM((2, 2 * tm * SUBLANES, LANES), F32),
                        pltpu.VMEM((te * SUBLANES, LANES), F32),
                        pltpu.SemaphoreType.DMA((2,)), pltpu.SemaphoreType.DMA(())],
    )
    tables = (flat(seg), flat(lo), flat(c))
    xg, slots = pl.pallas_call(
        _dispatch_kernel,
        grid_spec=grid_spec,
        out_shape=(jax.ShapeDtypeStruct((n_tiles_max * te * SUBLANES, LANES), F32),
                   jax.ShapeDtypeStruct((T, SLOT_W), F32)),
        compiler_params=_cparams(("arbitrary",)),
        name="dispatch",
    )(*tables, flat(p_start + count), flat(padded - count), n_tiles, n2, route, lanes(lo))
    return xg, slots, tables, tile_e, n_tiles


def _experts_kernel(tile_e_ref, n_tiles_ref, xg_ref, wg_ref, wu_ref, wd_ref, y_ref,
                    wg_bf, wu_bf, wd_bf):
    tm = xg_ref.shape[0] // SUBLANES
    i = pl.program_id(0)
    used = i < n_tiles_ref[0]

    @pl.when(used & ((i == 0) | (tile_e_ref[i] != tile_e_ref[jnp.maximum(i - 1, 0)])))
    def _():
        wg_bf[...] = wg_ref[0].astype(BF16)
        wu_bf[...] = wu_ref[0].astype(BF16)
        wd_bf[...] = wd_ref[0].astype(BF16)

    @pl.when(used)
    def _():
        x = _load_slab(xg_ref, tm).astype(BF16)
        g = jnp.dot(x, wg_bf[...], preferred_element_type=F32)
        u = jnp.dot(x, wu_bf[...], preferred_element_type=F32)
        he = (g * jax.nn.sigmoid(g)) * u
        _store_slab(y_ref, jnp.dot(he.astype(BF16), wd_bf[...], preferred_element_type=F32))

    @pl.when(i >= n_tiles_ref[0])
    def _():
        y_ref[...] = jnp.zeros_like(y_ref)


def _experts(tile_e, n_tiles, xg, wg, wu, wd, *, tm):
    D = wg.shape[1]
    n_tiles_max = xg.shape[0] // (tm * SUBLANES)
    slab = lambda f: pl.BlockSpec((tm * SUBLANES, LANES), f)
    grid_spec = pltpu.PrefetchScalarGridSpec(
        num_scalar_prefetch=2,
        grid=(n_tiles_max,),
        in_specs=[slab(lambda i, te, nt: (jnp.minimum(i, nt[0] - 1), 0)),
                  pl.BlockSpec((1, D, D_EXPERT), lambda i, te, nt: (te[i], 0, 0)),
                  pl.BlockSpec((1, D, D_EXPERT), lambda i, te, nt: (te[i], 0, 0)),
                  pl.BlockSpec((1, D_EXPERT, D), lambda i, te, nt: (te[i], 0, 0))],
        out_specs=slab(lambda i, te, nt: (i, 0)),
        scratch_shapes=[pltpu.VMEM((D, D_EXPERT), BF16), pltpu.VMEM((D, D_EXPERT), BF16),
                        pltpu.VMEM((D_EXPERT, D), BF16)],
    )
    return pl.pallas_call(
        _experts_kernel,
        grid_spec=grid_spec,
        out_shape=jax.ShapeDtypeStruct(xg.shape, F32),
        compiler_params=_cparams(("arbitrary",)),
        name="experts",
    )(tile_e, n_tiles, xg, wg, wu, wd)


def _final_kernel(seg_ref, lo_ref, cnt_ref, h_ref, ys_hbm, route_ref, slot_ref, p_ref, wple_ref,
                  pn_ref, wpg_ref, fn_ref, o_ref, ybuf, sem, *, apply_final_norm):
    tm = h_ref.shape[0]
    i = pl.program_id(0)
    nt = pl.num_programs(0)
    slot = i % 2

    def fetch(tile, s):
        def run(e, carry):
            k = tile * N_EXPERTS + e

            def piece(off, n):
                _token_copy(ys_hbm, seg_ref[k] + off, ybuf.at[s], lo_ref[k] + off, n,
                            sem.at[s]).start()

            _for_each_piece(cnt_ref[k], piece)
            return carry

        lax.fori_loop(0, N_EXPERTS, run, 0)

    @pl.when(i == 0)
    def _():
        fetch(0, 0)

    @pl.when(i + 1 < nt)
    def _():
        fetch(i + 1, 1 - slot)

    pltpu.make_async_copy(ys_hbm.at[pl.ds(0, 2 * tm * SUBLANES), :], ybuf.at[slot],
                          sem.at[slot]).wait()
    y_sorted = _load_slab(ybuf.at[slot], 2 * tm).astype(BF16)
    route = route_ref[...]
    slots = slot_ref[...].astype(jnp.int32)
    col = lax.broadcasted_iota(jnp.int32, (tm, 2 * tm), 1)
    comb = jnp.where(col == slots[:, 0:1], route[:, 0:1],
                     jnp.where(col == slots[:, 1:2], route[:, 1:2], 0.0))
    h = h_ref[...] + jnp.dot(comb.astype(BF16), y_sorted, preferred_element_type=F32)
    e = _rms(jnp.dot(p_ref[...].astype(BF16), wple_ref[...], preferred_element_type=F32), pn_ref[...])
    gate = jax.nn.sigmoid(jnp.dot(h.astype(BF16), wpg_ref[...], preferred_element_type=F32))
    h = h + gate * e
    if apply_final_norm:
        h = _rms(h, fn_ref[...])
    o_ref[...] = h


def _final(h1, ys, slots, tables, route, p2, wple, ple_norm, wpg, final_norm, *, tm,
           apply_final_norm):
    T, D = h1.shape
    nt = T // tm
    row = lambda i, *_: (i, 0)
    fixed = lambda i, *_: (0, 0)
    full = lambda a: pl.BlockSpec(a.shape, fixed)
    pn = ple_norm.reshape(1, D)
    fn = final_norm.reshape(1, D)
    grid_spec = pltpu.PrefetchScalarGridSpec(
        num_scalar_prefetch=3,
        grid=(nt,),
        in_specs=[pl.BlockSpec((tm, D), row), pl.BlockSpec(memory_space=pl.ANY),
                  pl.BlockSpec((tm, ROUTE_W), row), pl.BlockSpec((tm, SLOT_W), row),
                  pl.BlockSpec((tm, p2.shape[1]), row),
                  full(wple), full(pn), full(wpg), full(fn)],
        out_specs=pl.BlockSpec((tm, D), row),
        scratch_shapes=[pltpu.VMEM((2, 2 * tm * SUBLANES, LANES), F32),
                        pltpu.SemaphoreType.DMA((2,))],
    )
    return pl.pallas_call(
        functools.partial(_final_kernel, apply_final_norm=apply_final_norm),
        grid_spec=grid_spec,
        out_shape=jax.ShapeDtypeStruct((T, D), F32),
        compiler_params=_cparams(("arbitrary",)),
        name="final",
    )(*tables, h1, ys, route, slots, p2, wple, pn, wpg, fn)


SPLIT_SIZES = (GLA_QK, GLA_QK, GLA_V, GLA_V, GLA_LOWRANK, DIFF_W, DIFF_W, DIFF_W)


def _layer(h2, p2, positions, B, S, lambda_init, apply_final_norm, attn_norm, w_in, w_a2, b_a,
           gla_norm, lq1, lk1, lq2, lk2, diff_subln, w_branch_a, w_branch_b, w_out, ffn_norm,
           w_rg, b_rg, w_re, b_re, w_gate, w_up, w_down, w_ple, ple_norm, w_ple_gate, final_norm,
           *, tm, ts, tq, tk, te):
    T, D = h2.shape
    assert D == SUBLANES * LANES, "token-slab gathers need one (8, 128) tile per token"
    n_mix = sum(SPLIT_SIZES)
    wgla = w_in[:, :C_GLA_END].astype(BF16)
    wdiff = w_in[:, C_GLA_END:n_mix].astype(BF16)
    wg = w_in[:, n_mix:].astype(BF16)

    gq, gk, gv, gr, la, dq, dk, dv = _inproj(h2, positions, attn_norm, wgla, wdiff,
                                             w_a2.astype(BF16), b_a, tm=tm)
    oa = _gla(gq, gk, gv, gr, la, gla_norm, B=B, S=S, ts=ts).reshape(T, GLA_V)
    ob = _diffattn(dq, dk, dv, lq1, lk1, lq2, lk2, diff_subln, B=B, S=S, tq=tq, tk=tk,
                   lambda_init=lambda_init).reshape(T, DIFF_W)

    wr = jnp.concatenate([w_rg, w_re, jnp.zeros((D, LANES - N_GROUPS - N_EXPERTS), F32)], axis=1)
    wr_hi = wr.astype(BF16)
    wr = jnp.concatenate([wr_hi, (wr - wr_hi.astype(F32)).astype(BF16)], axis=1)
    br = jnp.concatenate([b_rg, b_re, jnp.zeros((LANES - N_GROUPS - N_EXPERTS,), F32)]).reshape(1, LANES)
    h1, n2, route, cnt = _postmix(h2, oa, ob, attn_norm, wg, w_branch_a.astype(BF16),
                                  w_branch_b.astype(BF16), w_out.astype(BF16), ffn_norm, wr, br,
                                  tm=tm)

    xg, slots, tables, tile_e, n_tiles = _dispatch(n2, route, cnt, tm=tm, te=te)
    ys = _experts(tile_e, n_tiles, xg, w_gate, w_up, w_down, tm=te)
    return _final(h1, ys, slots, tables, route, p2, w_ple.astype(BF16), ple_norm,
                  w_ple_gate.astype(BF16), final_norm, tm=tm, apply_final_norm=apply_final_norm)


def _block(x, p, positions, attn_norm, w_in, w_a2, b_a, gla_norm, lambda_q1, lambda_k1, lambda_q2,
           lambda_k2, diff_subln, w_branch_a, w_branch_b, w_out, ffn_norm, w_router_group,
           b_router_group, w_router_expert, b_router_expert, w_gate, w_up, w_down, w_ple, ple_norm,
           w_ple_gate, final_norm, *, tm, ts, tq, tk, te):
    B, S, D = x.shape
    depth = w_in.shape[0]
    h = x.reshape(B * S, D)
    for i in range(depth):
        lambda_init = 0.8 - 0.6 * math.exp(-0.3 * i)
        h = _layer(h, p[i].reshape(B * S, -1), positions, B, S, lambda_init, i == depth - 1,
                   attn_norm[i], w_in[i], w_a2[i], b_a[i], gla_norm[i], lambda_q1[i], lambda_k1[i],
                   lambda_q2[i], lambda_k2[i], diff_subln[i], w_branch_a[i], w_branch_b[i], w_out[i],
                   ffn_norm[i], w_router_group[i], b_router_group[i], w_router_expert[i],
                   b_router_expert[i], w_gate[i], w_up[i], w_down[i], w_ple[i], ple_norm[i],
                   w_ple_gate[i], final_norm, tm=tm, ts=ts, tq=tq, tk=tk, te=te)
    return h.reshape(B, S, D)


def kernel(x, p, positions, attn_norm, w_in, w_a2, b_a, gla_norm, lambda_q1, lambda_k1, lambda_q2, lambda_k2, diff_subln, w_branch_a, w_branch_b, w_out, ffn_norm, w_router_group, b_router_group, w_router_expert, b_router_expert, w_gate, w_up, w_down, w_ple, ple_norm, w_ple_gate, final_norm):
    S = x.shape[1]
    return _block(x, p, positions, attn_norm, w_in, w_a2, b_a, gla_norm, lambda_q1, lambda_k1,
                  lambda_q2, lambda_k2, diff_subln, w_branch_a, w_branch_b, w_out, ffn_norm,
                  w_router_group, b_router_group, w_router_expert, b_router_expert, w_gate, w_up,
                  w_down, w_ple, ple_norm, w_ple_gate, final_norm,
                  tm=512, ts=min(512, S), tq=min(512, S), tk=min(512, S), te=512)
```

```python
import functools
import math

import jax
import jax.numpy as jnp
import numpy as np
from jax import lax
from jax.experimental import pallas as pl
from jax.experimental.pallas import tpu as pltpu

EPS = 1e-6

GLA_HEADS = 4
GLA_DK = 64
GLA_DV = 128
GLA_LOWRANK = 16
GLA_TAU = 16.0
GLA_CHUNK = 64
GLA_QK = GLA_HEADS * GLA_DK
GLA_V = GLA_HEADS * GLA_DV

DIFF_HEADS = 4
DIFF_DH = 64
DIFF_DV = 2 * DIFF_DH
DIFF_W = DIFF_HEADS * DIFF_DV
ROPE_THETA = 500000.0
ROPE_DIM = DIFF_DH // 4
ROPE_HALF = ROPE_DIM // 2

N_GROUPS = 4
EXPERTS_PER_GROUP = 8
N_EXPERTS = N_GROUPS * EXPERTS_PER_GROUP
D_EXPERT = 256

LANES = 128
MXU_N = 256
VMEM_LIMIT = 56 * 1024 * 1024

BF16 = jnp.bfloat16
F32 = jnp.float32
NT_DIMS = (((1,), (1,)), ((), ()))
TN_DIMS = (((0,), (0,)), ((), ()))
NEG_BIG = -1e30
LOG2_E = math.log2(math.e)


def _rms(x, g):
    return x * lax.rsqrt(jnp.mean(x * x, axis=-1, keepdims=True) + EPS) * g


def _cparams(semantics):
    return pltpu.CompilerParams(dimension_semantics=semantics, vmem_limit_bytes=VMEM_LIMIT)


SUBLANES = 8


def _store_slab(ref, x):
    n = x.shape[0]
    for j in range(SUBLANES):
        ref[pl.ds(j, n, stride=SUBLANES), :] = x[:, j * LANES:(j + 1) * LANES]


def _load_slab(ref, n):
    return jnp.concatenate([ref[pl.ds(j, n, stride=SUBLANES), :] for j in range(SUBLANES)], axis=1)


C_GQ = 0
C_GK = C_GQ + GLA_QK
C_GV = C_GK + GLA_QK
C_GR = C_GV + GLA_V
C_AL = C_GR + GLA_V
C_GLA_END = C_AL + GLA_LOWRANK
C_DQ = 0
C_DK = C_DQ + DIFF_W
C_DV = C_DK + DIFF_W
C_END = C_DV + DIFF_W


def _inproj_kernel(x_ref, pos_ref, g_ref, wgla_ref, wdiff_ref, wa2_ref, ba_ref, inv_ref, sel_ref,
                   one_ref,
                   gq_ref, gk_ref, gv_ref, gr_ref, la_ref, dq_ref, dk_ref, dv_ref):
    nb = _rms(x_ref[...], g_ref[...]).astype(BF16)

    def proj_gla(c0, c1):
        return jnp.dot(nb, wgla_ref[:, c0:c1], preferred_element_type=F32)

    def proj(c0, c1):
        return jnp.dot(nb, wdiff_ref[:, c0:c1], preferred_element_type=F32)

    gq_ref[...] = proj_gla(C_GQ, C_GK) * (GLA_DK ** -0.5)
    gk_ref[...] = proj_gla(C_GK, C_GV)
    gv_ref[...] = proj_gla(C_GV, C_GR).astype(BF16)
    gr_ref[...] = proj_gla(C_GR, C_AL).astype(BF16)

    gal = proj_gla(C_AL, C_GLA_END).astype(BF16)
    a_logit = jnp.dot(gal, wa2_ref[...], preferred_element_type=F32) + ba_ref[...]
    la_ref[...] = (jnp.minimum(a_logit, 0.0) - jnp.log1p(jnp.exp(-jnp.abs(a_logit)))) / GLA_TAU

    ang = inv_ref[...] * pos_ref[0].astype(F32)
    cs = jnp.concatenate([jnp.cos(ang), jnp.sin(ang)], axis=0)

    tabs = None
    rest = cs
    for _ in range(3):
        piece = rest.astype(BF16)
        rest = rest - piece.astype(F32)
        t = lax.dot_general(piece, sel_ref[...], TN_DIMS, preferred_element_type=F32)
        tabs = t if tabs is None else tabs + t
    cosf = tabs[:, :LANES] + one_ref[...]
    sneg = tabs[:, LANES:2 * LANES]
    spos = tabs[:, 2 * LANES:]

    def rope_store(c0, out_ref, scale):
        for j in range(DIFF_W // MXU_N):
            pair = proj(c0 + j * MXU_N, c0 + (j + 1) * MXU_N)
            for i in range(MXU_N // LANES):
                blk = pair[:, i * LANES:(i + 1) * LANES]
                rot = (blk * cosf + pltpu.roll(blk, LANES - ROPE_HALF, 1) * sneg
                       + pltpu.roll(blk, ROPE_HALF, 1) * spos)
                c = j * MXU_N + i * LANES
                out_ref[:, c:c + LANES] = (rot * scale).astype(BF16)

    rope_store(C_DQ, dq_ref, DIFF_DH ** -0.5 * LOG2_E)
    rope_store(C_DK, dk_ref, 1.0)
    dv_ref[...] = proj(C_DV, C_END).astype(BF16)


def _rope_tables():
    lane = np.arange(LANES)
    r = lane % DIFF_DH
    selc = np.zeros((2 * ROPE_HALF, LANES), np.float32)
    seln = np.zeros_like(selc)
    selp = np.zeros_like(selc)
    one = np.zeros((1, LANES), np.float32)
    for l in range(LANES):
        if r[l] < ROPE_DIM:
            selc[r[l] % ROPE_HALF, l] = 1.0
            if r[l] < ROPE_HALF:
                seln[ROPE_HALF + r[l], l] = -1.0
            else:
                selp[ROPE_HALF + r[l] - ROPE_HALF, l] = 1.0
        else:
            one[0, l] = 1.0
    sel = np.concatenate([selc, seln, selp], axis=1)
    return jnp.asarray(sel, dtype=BF16), jnp.asarray(one)


def _inproj(x2, positions, attn_norm, wgla, wdiff, wa2, b_a, *, tm):
    T, D = x2.shape
    nt = T // tm
    pos3 = positions.reshape(nt, 1, tm)
    inv = (ROPE_THETA ** (-jnp.arange(0, ROPE_DIM, 2, dtype=F32) / ROPE_DIM)).reshape(ROPE_HALF, 1)
    sel, one = _rope_tables()
    row = lambda i: (i, 0)
    fixed = lambda i: (0, 0)
    out_shapes = (
        jax.ShapeDtypeStruct((T, GLA_QK), F32), jax.ShapeDtypeStruct((T, GLA_QK), F32),
        jax.ShapeDtypeStruct((T, GLA_V), BF16), jax.ShapeDtypeStruct((T, GLA_V), BF16),
        jax.ShapeDtypeStruct((T, GLA_QK), F32),
        jax.ShapeDtypeStruct((T, DIFF_W), BF16), jax.ShapeDtypeStruct((T, DIFF_W), BF16),
        jax.ShapeDtypeStruct((T, DIFF_W), BF16),
    )
    return pl.pallas_call(
        _inproj_kernel,
        grid=(nt,),
        in_specs=[
            pl.BlockSpec((tm, D), row),
            pl.BlockSpec((1, 1, tm), lambda i: (i, 0, 0)),
            pl.BlockSpec((1, D), fixed),
            pl.BlockSpec((D, C_GLA_END), fixed),
            pl.BlockSpec((D, C_END), fixed),
            pl.BlockSpec((GLA_LOWRANK, GLA_QK), fixed),
            pl.BlockSpec((1, GLA_QK), fixed),
            pl.BlockSpec((ROPE_HALF, 1), fixed),
            pl.BlockSpec((2 * ROPE_HALF, 3 * LANES), fixed),
            pl.BlockSpec((1, LANES), fixed),
        ],
        out_specs=[pl.BlockSpec((tm, s.shape[1]), row) for s in out_shapes],
        out_shape=out_shapes,
        compiler_params=_cparams(("parallel",)),
        name="inproj",
    )(x2, pos3, attn_norm.reshape(1, D), wgla, wdiff, wa2, b_a.reshape(1, GLA_QK), inv, sel, one)


def _gla_kernel(gq_ref, gk_ref, gv_ref, gr_ref, la_ref, gn_ref, oa_ref, st_ref, *, n_chunks):
    @pl.when(pl.program_id(1) == 0)
    def _():
        st_ref[...] = jnp.zeros_like(st_ref)

    C = GLA_CHUNK
    tril = lax.broadcasted_iota(jnp.int32, (C, C), 0) >= lax.broadcasted_iota(jnp.int32, (C, C), 1)
    trilb = jnp.where(tril, 1.0, 0.0).astype(BF16)
    head_of_lane = lax.broadcasted_iota(jnp.int32, (1, GLA_QK), 1) // GLA_DK
    hmask = [head_of_lane == h for h in range(GLA_HEADS)]
    gn = gn_ref[...]

    def by_head(x):
        return jnp.concatenate([jnp.where(hmask[h], x, jnp.zeros_like(x))
                                for h in range(GLA_HEADS)], axis=0)

    chunks = range(n_chunks)
    sls = [pl.ds(c * C, C) for c in chunks]
    vs = [gv_ref[0, sl, :] for sl in sls]

    def cumsum(la):
        total, rest = None, la
        for _ in range(3):
            piece = rest.astype(BF16)
            rest = rest - piece.astype(F32)
            t = jnp.dot(trilb, piece, preferred_element_type=F32)
            total = t if total is None else total + t
        return total

    bs = [cumsum(la_ref[0, sl, :]) for sl in sls]
    b_lasts = [b[C - 1:C, :] for b in bs]
    q_heads = [by_head((gq_ref[0, sl, :] * jnp.exp(b)).astype(BF16)) for sl, b in zip(sls, bs)]
    k_invs = [(gk_ref[0, sl, :] * jnp.exp(-b)).astype(BF16) for sl, b in zip(sls, bs)]
    k_ends = [by_head((gk_ref[0, sl, :] * jnp.exp(bl - b)).astype(BF16))
              for sl, b, bl in zip(sls, bs, b_lasts)]
    decays = [jnp.exp(bl) for bl in b_lasts]
    atts = [lax.dot_general(qh, ki, NT_DIMS, preferred_element_type=F32)
            for qh, ki in zip(q_heads, k_invs)]
    v_heads = [jnp.concatenate([v[:, h * GLA_DV:(h + 1) * GLA_DV] for h in range(GLA_HEADS)], axis=0)
               for v in vs]
    us = [lax.dot_general(vh, ke, TN_DIMS, preferred_element_type=F32)
          for vh, ke in zip(v_heads, k_ends)]

    st = st_ref[...]
    states = []
    for c in chunks:
        states.append(st.astype(BF16))
        st = st * decays[c] + us[c]
    st_ref[...] = st

    inters = [lax.dot_general(qh, s_in, NT_DIMS, preferred_element_type=F32)
              for qh, s_in in zip(q_heads, states)]
    for c in chunks:
        for h in range(GLA_HEADS):
            rows = slice(h * C, (h + 1) * C)
            cols = slice(h * GLA_DV, (h + 1) * GLA_DV)
            a_h = jnp.where(tril, atts[c][rows], 0.0).astype(BF16)
            o_h = jnp.dot(a_h, vs[c][:, cols], preferred_element_type=F32) + inters[c][rows]
            y = _rms(o_h, gn)
            r = gr_ref[0, sls[c], cols].astype(F32)
            oa_ref[0, sls[c], cols] = (y * (r * jax.nn.sigmoid(r))).astype(BF16)


def _gla(gq, gk, gv, gr, la, gla_norm, *, B, S, ts):
    n_chunks = ts // GLA_CHUNK
    blk = lambda w: pl.BlockSpec((1, ts, w), lambda b, j: (b, j, 0))
    r3 = lambda a: a.reshape(B, S, a.shape[-1])
    return pl.pallas_call(
        functools.partial(_gla_kernel, n_chunks=n_chunks),
        grid=(B, S // ts),
        in_specs=[blk(GLA_QK), blk(GLA_QK), blk(GLA_V), blk(GLA_V), blk(GLA_QK),
                  pl.BlockSpec((1, GLA_DV), lambda b, j: (0, 0))],
        out_specs=blk(GLA_V),
        out_shape=jax.ShapeDtypeStruct((B, S, GLA_V), BF16),
        scratch_shapes=[pltpu.VMEM((GLA_DV, GLA_QK), F32)],
        compiler_params=_cparams(("parallel", "arbitrary")),
        name="gla",
    )(r3(gq), r3(gk), r3(gv), r3(gr), r3(la), gla_norm.reshape(1, GLA_DV))


ATTN_PAIR = 2


def _diffattn_kernel(dq_ref, dk_ref, dv_ref, lq1_ref, lk1_ref, lq2_ref, lk2_ref, sub_ref,
                     ob_ref, m_ref, l_ref, acc_ref, *, tq, tk, lambda_init):
    S = dq_ref.shape[1]
    nq = S // tq
    R = 2 * tq
    lam = (jnp.exp(jnp.sum(lq1_ref[...] * lk1_ref[...], axis=-1, keepdims=True))
           - jnp.exp(jnp.sum(lq2_ref[...] * lk2_ref[...], axis=-1, keepdims=True))
           + lambda_init)
    first_comp = lax.broadcasted_iota(jnp.int32, (1, DIFF_DV), 1) < DIFF_DH
    sub = sub_ref[...]

    pair = range(ATTN_PAIR)
    for hp in range(DIFF_HEADS // ATTN_PAIR):
        colss = [slice((hp * ATTN_PAIR + t) * DIFF_DV, (hp * ATTN_PAIR + t + 1) * DIFF_DV)
                 for t in pair]
        for qi in range(nq):
            qss = []
            for t in pair:
                qb = dq_ref[0, qi * tq:(qi + 1) * tq, colss[t]]
                zero = jnp.zeros_like(qb)
                qss.append(jnp.concatenate([jnp.where(first_comp, qb, zero),
                                            jnp.where(first_comp, zero, qb)], axis=0))
            m_ref[...] = jnp.full(m_ref.shape, NEG_BIG, F32)
            l_ref[...] = jnp.zeros(l_ref.shape, F32)
            acc_ref[...] = jnp.zeros(acc_ref.shape, F32)

            def scores(kstart, qss=qss, colss=colss):
                return [lax.dot_general(dk_ref[0, kstart:kstart + tk, colss[t]], qss[t], NT_DIMS,
                                        preferred_element_type=F32) for t in pair]

            def update(ss, kstart, masked, colss=colss, qi=qi):
                if masked:
                    kpos = kstart + lax.broadcasted_iota(jnp.int32, (tk, R), 0)
                    qpos = qi * tq + lax.broadcasted_iota(jnp.int32, (tk, R), 1) % tq
                    ss = [jnp.where(qpos >= kpos, s, NEG_BIG) for s in ss]
                m_prevs = [m_ref[t] for t in pair]
                m_news = [jnp.maximum(mp, jnp.max(s, axis=0, keepdims=True))
                          for mp, s in zip(m_prevs, ss)]
                alphas = [jnp.exp2(mp - mn) for mp, mn in zip(m_prevs, m_news)]
                ps = [jnp.exp2(s - mn) for s, mn in zip(ss, m_news)]
                for t in pair:
                    l_ref[t] = alphas[t] * l_ref[t] + jnp.sum(ps[t], axis=0, keepdims=True)
                    m_ref[t] = m_news[t]
                pvs = [lax.dot_general(dv_ref[0, kstart:kstart + tk, colss[t]], ps[t].astype(BF16),
                                       TN_DIMS, preferred_element_type=F32) for t in pair]
                for t in pair:
                    acc_ref[t] = alphas[t] * acc_ref[t] + pvs[t]

            n_blocks = (qi + 1) * tq // tk
            diag_from = qi * tq // tk
            s_next = scores(0)
            for kb_i in range(n_blocks):
                s_cur = s_next
                if kb_i + 1 < n_blocks:
                    s_next = scores((kb_i + 1) * tk)
                update(s_cur, kb_i * tk, kb_i >= diag_from)

            for t in pair:
                o_all = acc_ref[t] / l_ref[t]
                o = o_all[:, :tq] - lam * o_all[:, tq:]
                y = o * lax.rsqrt(jnp.mean(o * o, axis=0, keepdims=True) + EPS) * sub
                y = y * (1.0 - lambda_init)
                ob_ref[0, qi * tq:(qi + 1) * tq, colss[t]] = y.T.astype(BF16)


def _diffattn(dq, dk, dv, lq1, lk1, lq2, lk2, diff_subln, *, B, S, tq, tk, lambda_init):
    seq = pl.BlockSpec((1, S, DIFF_W), lambda b: (b, 0, 0))
    vec = lambda w: pl.BlockSpec((1, w), lambda b: (0, 0))
    r3 = lambda a: a.reshape(B, S, DIFF_W)
    return pl.pallas_call(
        functools.partial(_diffattn_kernel, tq=tq, tk=tk, lambda_init=lambda_init),
        grid=(B,),
        in_specs=[seq, seq, seq, vec(DIFF_DH), vec(DIFF_DH), vec(DIFF_DH), vec(DIFF_DH),
                  pl.BlockSpec((DIFF_DV, 1), lambda b: (0, 0))],
        out_specs=seq,
        out_shape=jax.ShapeDtypeStruct((B, S, DIFF_W), BF16),
        scratch_shapes=[pltpu.VMEM((ATTN_PAIR, 1, 2 * tq), F32), pltpu.VMEM((ATTN_PAIR, 1, 2 * tq), F32),
                        pltpu.VMEM((ATTN_PAIR, DIFF_DV, 2 * tq), F32)],
        compiler_params=_cparams(("parallel",)),
        name="diffattn",
    )(r3(dq), r3(dk), r3(dv), lq1.reshape(1, -1), lk1.reshape(1, -1), lq2.reshape(1, -1),
      lk2.reshape(1, -1), diff_subln.reshape(-1, 1))


ROUTE_W = 8


def _first_index_of(mask, lane):
    return jnp.min(jnp.where(mask, lane, LANES), axis=-1, keepdims=True)


POSTMIX_SPLIT = 2


def _postmix_kernel(x_ref, oa_ref, ob_ref, g_ref, wg_ref, wba_ref, wbb_ref, wo_ref, fn_ref,
                    wr_ref, br_ref, h_ref, n2_ref, route_ref, cnt_ref):
    D = x_ref.shape[1]
    rows_per = x_ref.shape[0] // POSTMIX_SPLIT
    groups = [slice(s * rows_per, (s + 1) * rows_per) for s in range(POSTMIX_SPLIT)]
    dot = functools.partial(jnp.dot, preferred_element_type=F32)
    xs = [x_ref[r, :] for r in groups]
    nbs = [_rms(x, g_ref[...]).astype(BF16) for x in xs]
    y_as = [dot(oa_ref[r, :], wba_ref[...]) for r in groups]
    g_as = [dot(nb, wg_ref[:, :D]) for nb in nbs]
    mergeds = [jax.nn.sigmoid(g) * y for g, y in zip(g_as, y_as)]
    y_bs = [dot(ob_ref[r, :], wbb_ref[...]) for r in groups]
    g_bs = [dot(nb, wg_ref[:, D:]) for nb in nbs]
    mergeds = [m + jax.nn.sigmoid(g) * y for m, g, y in zip(mergeds, g_bs, y_bs)]
    hs = [x + dot(m.astype(BF16), wo_ref[...]) for x, m in zip(xs, mergeds)]
    n2s = [_rms(h, fn_ref[...]) for h in hs]
    n2_his = [n2.astype(BF16) for n2 in n2s]
    n2_los = [(n2 - hi.astype(F32)).astype(BF16) for n2, hi in zip(n2s, n2_his)]
    hi_prods = [dot(hi, wr_ref[...]) for hi in n2_his]
    lgs = [hp[:, :LANES] + hp[:, LANES:] + dot(lo, wr_ref[:, :LANES]) + br_ref[...]
           for hp, lo in zip(hi_prods, n2_los)]
    cnt = None
    for r, h, n2_hi, lg in zip(groups, hs, n2_his, lgs):
        h_ref[r, :] = h
        n2_ref[r, :] = n2_hi
        rec, c = _route_top2(lg)
        route_ref[r, :] = rec
        cnt = c if cnt is None else cnt + c
    cnt_ref[0] = cnt


def _route_top2(lg):
    lane = lax.broadcasted_iota(jnp.int32, lg.shape, 1)
    is_g = lane < N_GROUPS
    g_max = jnp.max(jnp.where(is_g, lg, -jnp.inf), axis=-1, keepdims=True)
    g_exp = jnp.where(is_g, jnp.exp(lg - g_max), 0.0)
    g_prob = g_exp / jnp.sum(g_exp, axis=-1, keepdims=True)
    g_p = jnp.max(g_prob, axis=-1, keepdims=True)
    g_idx = _first_index_of(is_g & (g_prob == g_p), lane)

    e_lo = N_GROUPS + EXPERTS_PER_GROUP * g_idx
    is_e = (lane >= e_lo) & (lane < e_lo + EXPERTS_PER_GROUP)
    e_max = jnp.max(jnp.where(is_e, lg, -jnp.inf), axis=-1, keepdims=True)
    e_exp = jnp.where(is_e, jnp.exp(lg - e_max), 0.0)
    e_prob = e_exp / jnp.sum(e_exp, axis=-1, keepdims=True)
    p1 = jnp.max(jnp.where(is_e, e_prob, -1.0), axis=-1, keepdims=True)
    i1 = _first_index_of(is_e & (e_prob == p1), lane)
    rest = is_e & (lane != i1)
    p2 = jnp.max(jnp.where(rest, e_prob, -1.0), axis=-1, keepdims=True)
    i2 = _first_index_of(rest & (e_prob == p2), lane)
    den = p1 + p2
    w1 = g_p * (p1 / den)
    w2 = g_p * (p2 / den)
    rec = jnp.where(lane == 0, w1, 0.0)
    rec = jnp.where(lane == 1, w2, rec)
    rec = jnp.where(lane == 2, (i1 - N_GROUPS).astype(F32), rec)
    rec = jnp.where(lane == 3, (i2 - N_GROUPS).astype(F32), rec)
    chosen = (lane == i1 - N_GROUPS) | (lane == i2 - N_GROUPS)
    cnt = jnp.sum(chosen.astype(F32), axis=0, keepdims=True)
    return rec[:, :ROUTE_W], cnt


def _postmix(x2, oa, ob, attn_norm, wg, wba, wbb, wo, ffn_norm, wr, br, *, tm):
    T, D = x2.shape
    row = lambda i: (i, 0)
    fixed = lambda i: (0, 0)
    full = lambda a: pl.BlockSpec(a.shape, fixed)
    g = attn_norm.reshape(1, D)
    fn = ffn_norm.reshape(1, D)
    return pl.pallas_call(
        _postmix_kernel,
        grid=(T // tm,),
        in_specs=[pl.BlockSpec((tm, D), row), pl.BlockSpec((tm, GLA_V), row),
                  pl.BlockSpec((tm, DIFF_W), row), full(g), full(wg), full(wba), full(wbb),
                  full(wo), full(fn), full(wr), full(br)],
        out_specs=[pl.BlockSpec((tm, D), row), pl.BlockSpec((tm, D), row),
                   pl.BlockSpec((tm, ROUTE_W), row),
                   pl.BlockSpec((1, 1, LANES), lambda i: (i, 0, 0))],
        out_shape=(jax.ShapeDtypeStruct((T, D), F32), jax.ShapeDtypeStruct((T, D), BF16),
                   jax.ShapeDtypeStruct((T, ROUTE_W), F32),
                   jax.ShapeDtypeStruct((T // tm, 1, LANES), F32)),
        compiler_params=_cparams(("parallel",)),
        name="postmix",
    )(x2, oa, ob, g, wg, wba, wbb, wo, fn, wr, br)


DISPATCH_CHUNK = 8
SLOT_W = 8


def _token_copy(src, src_tok, dst, dst_tok, n_tok, sem):
    return pltpu.make_async_copy(
        src.at[pl.ds(pl.multiple_of(src_tok * SUBLANES, SUBLANES), n_tok * SUBLANES), :],
        dst.at[pl.ds(pl.multiple_of(dst_tok * SUBLANES, SUBLANES), n_tok * SUBLANES), :], sem)


def _for_each_piece(length, fn):
    n_full = lax.shift_right_logical(length, 3)

    def body(c, carry):
        fn(c * DISPATCH_CHUNK, DISPATCH_CHUNK)
        return carry

    lax.fori_loop(0, n_full, body, 0)
    off = n_full * DISPATCH_CHUNK
    for n in (4, 2, 1):
        has = (length & n) != 0

        @pl.when(has)
        def _(off=off, n=n):
            fn(off, n)

        off = off + jnp.where(has, n, 0)


def _dispatch_kernel(seg_ref, lo_ref, cnt_ref, fill_ref, fill_len_ref, n_tiles_ref,
                     n2_ref, route_ref, lo_row_ref,
                     xg_hbm, slot_ref, xs_buf, zero_buf, sem, fill_sem):
    tm = n2_ref.shape[0]
    n_slab = 2 * tm * SUBLANES
    i = pl.program_id(0)
    nt = pl.num_programs(0)
    slot = i % 2

    @pl.when(i == 0)
    def _():
        zero_buf[...] = jnp.zeros_like(zero_buf)
        te = zero_buf.shape[0] // SUBLANES
        for wait in (False, True):
            def fill(e, carry, wait=wait):
                def piece(off, n):
                    cp = _token_copy(zero_buf, 0, xg_hbm, fill_ref[e] + off, n, fill_sem)
                    cp.wait() if wait else cp.start()
                _for_each_piece(fill_len_ref[e], piece)
                return carry
            lax.fori_loop(0, N_EXPERTS, fill, 0)

            def fill_tile(t, carry, wait=wait):
                cp = _token_copy(zero_buf, 0, xg_hbm, t * te, te, fill_sem)
                cp.wait() if wait else cp.start()
                return carry
            lax.fori_loop(n_tiles_ref[0], xg_hbm.shape[0] // zero_buf.shape[0], fill_tile, 0)

    route = route_ref[...]
    lane = lax.broadcasted_iota(jnp.int32, (tm, LANES), 1)
    oh1 = lane == route[:, 2:3].astype(jnp.int32)
    oh2 = lane == route[:, 3:4].astype(jnp.int32)
    both = jnp.where(oh1 | oh2, 1.0, 0.0).astype(BF16)
    earlier = (lax.broadcasted_iota(jnp.int32, (tm, tm), 0)
               > lax.broadcasted_iota(jnp.int32, (tm, tm), 1))
    rank = jnp.dot(jnp.where(earlier, 1.0, 0.0).astype(BF16), both, preferred_element_type=F32)

    def pick(onehot, v):
        return jnp.sum(jnp.where(onehot, v, 0.0), axis=-1, keepdims=True)

    r1 = pick(oh1, rank)
    r2 = pick(oh2, rank)
    q1 = pick(oh1, lo_row_ref[0]) + r1
    q2 = pick(oh2, lo_row_ref[0]) + r2
    rec = jnp.where(lane == 0, q1, jnp.where(lane == 1, q2, 0.0))
    slot_ref[...] = rec[:, :SLOT_W]

    rec_t = rec.T.astype(jnp.int32)
    slot_id = lax.broadcasted_iota(jnp.int32, (2 * tm, tm), 0)
    perm = jnp.where((slot_id == rec_t[0:1, :]) | (slot_id == rec_t[1:2, :]), 1.0, 0.0)
    xs = jnp.dot(perm.astype(BF16), n2_ref[...], preferred_element_type=F32)

    def wait_slot(s):
        pltpu.make_async_copy(xg_hbm.at[pl.ds(0, n_slab), :], xs_buf.at[s], sem.at[s]).wait()

    @pl.when(i >= 2)
    def _():
        wait_slot(slot)

    _store_slab(xs_buf.at[slot], xs)

    def send(e, carry):
        k = i * N_EXPERTS + e

        def piece(off, n):
            _token_copy(xs_buf.at[slot], lo_ref[k] + off, xg_hbm, seg_ref[k] + off, n,
                        sem.at[slot]).start()

        _for_each_piece(cnt_ref[k], piece)
        return carry

    lax.fori_loop(0, N_EXPERTS, send, 0)

    @pl.when(i == nt - 1)
    def _():
        wait_slot(slot)

        @pl.when(nt >= 2)
        def _():
            wait_slot(1 - slot)


def _dispatch(n2, route, cnt, *, tm, te):
    T, D = n2.shape
    nt = T // tm
    n_tiles_max = 2 * T // te + N_EXPERTS
    c = cnt[:, 0, :N_EXPERTS].astype(jnp.int32)
    count = jnp.sum(c, axis=0)
    padded = (count + te - 1) // te * te
    p_end = jnp.cumsum(padded)
    p_start = p_end - padded
    n_tiles = (p_end[-1] // te).astype(jnp.int32).reshape(1)
    tile_start = jnp.arange(n_tiles_max, dtype=jnp.int32) * te
    tile_e = jnp.sum(tile_start[:, None] >= p_end[None, :], axis=1, dtype=jnp.int32)
    tile_e = jnp.minimum(tile_e, jnp.sum(tile_start[n_tiles[0] - 1] >= p_end, dtype=jnp.int32))
    seg = p_start[None, :] + jnp.cumsum(c, axis=0) - c
    lo = jnp.cumsum(c, axis=1) - c
    lanes = lambda a: jnp.pad(a.astype(F32), ((0, 0), (0, LANES - N_EXPERTS))).reshape(nt, 1, LANES)
    flat = lambda a: a.reshape(-1).astype(jnp.int32)
    row = lambda i, *_: (i, 0)
    per_tile = pl.BlockSpec((1, 1, LANES), lambda i, *_: (i, 0, 0))
    grid_spec = pltpu.PrefetchScalarGridSpec(
        num_scalar_prefetch=6,
        grid=(nt,),
        in_specs=[pl.BlockSpec((tm, D), row), pl.BlockSpec((tm, ROUTE_W), row), per_tile],
        out_specs=[pl.BlockSpec(memory_space=pl.ANY), pl.BlockSpec((tm, SLOT_W), row)],
        scratch_shapes=[pltpu.VMEM((2, 2 * tm * SUBLANES, LANES), F32),
                        pltpu.VMEM((te * SUBLANES, LANES), F32),
                        pltpu.SemaphoreType.DMA((2,)), pltpu.SemaphoreType.DMA(())],
    )
    tables = (flat(seg), flat(lo), flat(c))
    xg, slots = pl.pallas_call(
        _dispatch_kernel,
        grid_spec=grid_spec,
        out_shape=(jax.ShapeDtypeStruct((n_tiles_max * te * SUBLANES, LANES), F32),
                   jax.ShapeDtypeStruct((T, SLOT_W), F32)),
        compiler_params=_cparams(("arbitrary",)),
        name="dispatch",
    )(*tables, flat(p_start + count), flat(padded - count), n_tiles, n2, route, lanes(lo))
    return xg, slots, tables, tile_e, n_tiles


def _experts_kernel(tile_e_ref, n_tiles_ref, xg_ref, wg_ref, wu_ref, wd_ref, y_ref,
                    wg_bf, wu_bf, wd_bf):
    tm = xg_ref.shape[0] // SUBLANES
    i = pl.program_id(0)
    used = i < n_tiles_ref[0]

    @pl.when(used & ((i == 0) | (tile_e_ref[i] != tile_e_ref[jnp.maximum(i - 1, 0)])))
    def _():
        wg_bf[...] = wg_ref[0].astype(BF16)
        wu_bf[...] = wu_ref[0].astype(BF16)
        wd_bf[...] = wd_ref[0].astype(BF16)

    @pl.when(used)
    def _():
        x = _load_slab(xg_ref, tm).astype(BF16)
        g = jnp.dot(x, wg_bf[...], preferred_element_type=F32)
        u = jnp.dot(x, wu_bf[...], preferred_element_type=F32)
        he = (g * jax.nn.sigmoid(g)) * u
        _store_slab(y_ref, jnp.dot(he.astype(BF16), wd_bf[...], preferred_element_type=F32))

    @pl.when(i >= n_tiles_ref[0])
    def _():
        y_ref[...] = jnp.zeros_like(y_ref)


def _experts(tile_e, n_tiles, xg, wg, wu, wd, *, tm):
    D = wg.shape[1]
    n_tiles_max = xg.shape[0] // (tm * SUBLANES)
    slab = lambda f: pl.BlockSpec((tm * SUBLANES, LANES), f)
    grid_spec = pltpu.PrefetchScalarGridSpec(
        num_scalar_prefetch=2,
        grid=(n_tiles_max,),
        in_specs=[slab(lambda i, te, nt: (jnp.minimum(i, nt[0] - 1), 0)),
                  pl.BlockSpec((1, D, D_EXPERT), lambda i, te, nt: (te[i], 0, 0)),
                  pl.BlockSpec((1, D, D_EXPERT), lambda i, te, nt: (te[i], 0, 0)),
                  pl.BlockSpec((1, D_EXPERT, D), lambda i, te, nt: (te[i], 0, 0))],
        out_specs=slab(lambda i, te, nt: (i, 0)),
        scratch_shapes=[pltpu.VMEM((D, D_EXPERT), BF16), pltpu.VMEM((D, D_EXPERT), BF16),
                        pltpu.VMEM((D_EXPERT, D), BF16)],
    )
    return pl.pallas_call(
        _experts_kernel,
        grid_spec=grid_spec,
        out_shape=jax.ShapeDtypeStruct(xg.shape, F32),
        compiler_params=_cparams(("arbitrary",)),
        name="experts",
    )(tile_e, n_tiles, xg, wg, wu, wd)


def _final_kernel(seg_ref, lo_ref, cnt_ref, h_ref, ys_hbm, route_ref, slot_ref, p_ref, wple_ref,
                  pn_ref, wpg_ref, fn_ref, o_ref, ybuf, sem, *, apply_final_norm):
    tm = h_ref.shape[0]
    i = pl.program_id(0)
    nt = pl.num_programs(0)
    slot = i % 2

    def fetch(tile, s):
        def run(e, carry):
            k = tile * N_EXPERTS + e

            def piece(off, n):
                _token_copy(ys_hbm, seg_ref[k] + off, ybuf.at[s], lo_ref[k] + off, n,
                            sem.at[s]).start()

            _for_each_piece(cnt_ref[k], piece)
            return carry

        lax.fori_loop(0, N_EXPERTS, run, 0)

    @pl.when(i == 0)
    def _():
        fetch(0, 0)

    @pl.when(i + 1 < nt)
    def _():
        fetch(i + 1, 1 - slot)

    pltpu.make_async_copy(ys_hbm.at[pl.ds(0, 2 * tm * SUBLANES), :], ybuf.at[slot],
                          sem.at[slot]).wait()
    y_sorted = _load_slab(ybuf.at[slot], 2 * tm).astype(BF16)
    route = route_ref[...]
    slots = slot_ref[...].astype(jnp.int32)
    col = lax.broadcasted_iota(jnp.int32, (tm, 2 * tm), 1)
    comb = jnp.where(col == slots[:, 0:1], route[:, 0:1],
                     jnp.where(col == slots[:, 1:2], route[:, 1:2], 0.0))
    h = h_ref[...] + jnp.dot(comb.astype(BF16), y_sorted, preferred_element_type=F32)
    e = _rms(jnp.dot(p_ref[...].astype(BF16), wple_ref[...], preferred_element_type=F32), pn_ref[...])
    gate = jax.nn.sigmoid(jnp.dot(h.astype(BF16), wpg_ref[...], preferred_element_type=F32))
    h = h + gate * e
    if apply_final_norm:
        h = _rms(h, fn_ref[...])
    o_ref[...] = h


def _final(h1, ys, slots, tables, route, p2, wple, ple_norm, wpg, final_norm, *, tm,
           apply_final_norm):
    T, D = h1.shape
    nt = T // tm
    row = lambda i, *_: (i, 0)
    fixed = lambda i, *_: (0, 0)
    full = lambda a: pl.BlockSpec(a.shape, fixed)
    pn = ple_norm.reshape(1, D)
    fn = final_norm.reshape(1, D)
    grid_spec = pltpu.PrefetchScalarGridSpec(
        num_scalar_prefetch=3,
        grid=(nt,),
        in_specs=[pl.BlockSpec((tm, D), row), pl.BlockSpec(memory_space=pl.ANY),
                  pl.BlockSpec((tm, ROUTE_W), row), pl.BlockSpec((tm, SLOT_W), row),
                  pl.BlockSpec((tm, p2.shape[1]), row),
                  full(wple), full(pn), full(wpg), full(fn)],
        out_specs=pl.BlockSpec((tm, D), row),
        scratch_shapes=[pltpu.VMEM((2, 2 * tm * SUBLANES, LANES), F32),
                        pltpu.SemaphoreType.DMA((2,))],
    )
    return pl.pallas_call(
        functools.partial(_final_kernel, apply_final_norm=apply_final_norm),
        grid_spec=grid_spec,
        out_shape=jax.ShapeDtypeStruct((T, D), F32),
        compiler_params=_cparams(("arbitrary",)),
        name="final",
    )(*tables, h1, ys, route, slots, p2, wple, pn, wpg, fn)


SPLIT_SIZES = (GLA_QK, GLA_QK, GLA_V, GLA_V, GLA_LOWRANK, DIFF_W, DIFF_W, DIFF_W)


def _layer(h2, p2, positions, B, S, lambda_init, apply_final_norm, attn_norm, w_in, w_a2, b_a,
           gla_norm, lq1, lk1, lq2, lk2, diff_subln, w_branch_a, w_branch_b, w_out, ffn_norm,
           w_rg, b_rg, w_re, b_re, w_gate, w_up, w_down, w_ple, ple_norm, w_ple_gate, final_norm,
           *, tm, ts, tq, tk, te):
    T, D = h2.shape
    assert D == SUBLANES * LANES, "token-slab gathers need one (8, 128) tile per token"
    n_mix = sum(SPLIT_SIZES)
    wgla = w_in[:, :C_GLA_END].astype(BF16)
    wdiff = w_in[:, C_GLA_END:n_mix].astype(BF16)
    wg = w_in[:, n_mix:].astype(BF16)

    gq, gk, gv, gr, la, dq, dk, dv = _inproj(h2, positions, attn_norm, wgla, wdiff,
                                             w_a2.astype(BF16), b_a, tm=tm)
    oa = _gla(gq, gk, gv, gr, la, gla_norm, B=B, S=S, ts=ts).reshape(T, GLA_V)
    ob = _diffattn(dq, dk, dv, lq1, lk1, lq2, lk2, diff_subln, B=B, S=S, tq=tq, tk=tk,
                   lambda_init=lambda_init).reshape(T, DIFF_W)

    wr = jnp.concatenate([w_rg, w_re, jnp.zeros((D, LANES - N_GROUPS - N_EXPERTS), F32)], axis=1)
    wr_hi = wr.astype(BF16)
    wr = jnp.concatenate([wr_hi, (wr - wr_hi.astype(F32)).astype(BF16)], axis=1)
    br = jnp.concatenate([b_rg, b_re, jnp.zeros((LANES - N_GROUPS - N_EXPERTS,), F32)]).reshape(1, LANES)
    h1, n2, route, cnt = _postmix(h2, oa, ob, attn_norm, wg, w_branch_a.astype(BF16),
                                  w_branch_b.astype(BF16), w_out.astype(BF16), ffn_norm, wr, br,
                                  tm=tm)

    xg, slots, tables, tile_e, n_tiles = _dispatch(n2, route, cnt, tm=tm, te=te)
    ys = _experts(tile_e, n_tiles, xg, w_gate, w_up, w_down, tm=te)
    return _final(h1, ys, slots, tables, route, p2, w_ple.astype(BF16), ple_norm,
                  w_ple_gate.astype(BF16), final_norm, tm=tm, apply_final_norm=apply_final_norm)


def _block(x, p, positions, attn_norm, w_in, w_a2, b_a, gla_norm, lambda_q1, lambda_k1, lambda_q2,
           lambda_k2, diff_subln, w_branch_a, w_branch_b, w_out, ffn_norm, w_router_group,
           b_router_group, w_router_expert, b_router_expert, w_gate, w_up, w_down, w_ple, ple_norm,
           w_ple_gate, final_norm, *, tm, ts, tq, tk, te):
    B, S, D = x.shape
    depth = w_in.shape[0]
    h = x.reshape(B * S, D)
    for i in range(depth):
        lambda_init = 0.8 - 0.6 * math.exp(-0.3 * i)
        h = _layer(h, p[i].reshape(B * S, -1), positions, B, S, lambda_init, i == depth - 1,
                   attn_norm[i], w_in[i], w_a2[i], b_a[i], gla_norm[i], lambda_q1[i], lambda_k1[i],
                   lambda_q2[i], lambda_k2[i], diff_subln[i], w_branch_a[i], w_branch_b[i], w_out[i],
                   ffn_norm[i], w_router_group[i], b_router_group[i], w_router_expert[i],
                   b_router_expert[i], w_gate[i], w_up[i], w_down[i], w_ple[i], ple_norm[i],
                   w_ple_gate[i], final_norm, tm=tm, ts=ts, tq=tq, tk=tk, te=te)
    return h.reshape(B, S, D)


def kernel(x, p, positions, attn_norm, w_in, w_a2, b_a, gla_norm, lambda_q1, lambda_k1, lambda_q2, lambda_k2, diff_subln, w_branch_a, w_branch_b, w_out, ffn_norm, w_router_group, b_router_group, w_router_expert, b_router_expert, w_gate, w_up, w_down, w_ple, ple_norm, w_ple_gate, final_norm):
    S = x.shape[1]
    return _block(x, p, positions, attn_norm, w_in, w_a2, b_a, gla_norm, lambda_q1, lambda_k1,
                  lambda_q2, lambda_k2, diff_subln, w_branch_a, w_branch_b, w_out, ffn_norm,
                  w_router_group, b_router_group, w_router_expert, b_router_expert, w_gate, w_up,
                  w_down, w_ple, ple_norm, w_ple_gate, final_norm,
                  tm=512, ts=min(512, S), tq=min(512, S), tk=min(512, S), te=512)
```

```python
import functools
import math

import jax
import jax.numpy as jnp
import numpy as np
from jax import lax
from jax.experimental import pallas as pl
from jax.experimental.pallas import tpu as pltpu

EPS = 1e-6

GLA_HEADS = 4
GLA_DK = 64
GLA_DV = 128
GLA_LOWRANK = 16
GLA_TAU = 16.0
GLA_CHUNK = 64
GLA_QK = GLA_HEADS * GLA_DK
GLA_V = GLA_HEADS * GLA_DV

DIFF_HEADS = 4
DIFF_DH = 64
DIFF_DV = 2 * DIFF_DH
DIFF_W = DIFF_HEADS * DIFF_DV
ROPE_THETA = 500000.0
ROPE_DIM = DIFF_DH // 4
ROPE_HALF = ROPE_DIM // 2

N_GROUPS = 4
EXPERTS_PER_GROUP = 8
N_EXPERTS = N_GROUPS * EXPERTS_PER_GROUP
D_EXPERT = 256

LANES = 128
MXU_N = 256
VMEM_LIMIT = 56 * 1024 * 1024

BF16 = jnp.bfloat16
F32 = jnp.float32
NT_DIMS = (((1,), (1,)), ((), ()))
TN_DIMS = (((0,), (0,)), ((), ()))
NEG_BIG = -1e30
LOG2_E = math.log2(math.e)


def _rms(x, g):
    return x * lax.rsqrt(jnp.mean(x * x, axis=-1, keepdims=True) + EPS) * g


def _cparams(semantics):
    return pltpu.CompilerParams(dimension_semantics=semantics, vmem_limit_bytes=VMEM_LIMIT)


SUBLANES = 8


def _store_slab(ref, x):
    n = x.shape[0]
    for j in range(SUBLANES):
        ref[pl.ds(j, n, stride=SUBLANES), :] = x[:, j * LANES:(j + 1) * LANES]


def _load_slab(ref, n):
    return jnp.concatenate([ref[pl.ds(j, n, stride=SUBLANES), :] for j in range(SUBLANES)], axis=1)


C_GQ = 0
C_GK = C_GQ + GLA_QK
C_GV = C_GK + GLA_QK
C_GR = C_GV + GLA_V
C_AL = C_GR + GLA_V
C_GLA_END = C_AL + GLA_LOWRANK
C_DQ = 0
C_DK = C_DQ + DIFF_W
C_DV = C_DK + DIFF_W
C_END = C_DV + DIFF_W


def _inproj_kernel(x_ref, pos_ref, g_ref, wgla_ref, wdiff_ref, wa2_ref, ba_ref, inv_ref, sel_ref,
                   one_ref,
                   gq_ref, gk_ref, gv_ref, gr_ref, la_ref, dq_ref, dk_ref, dv_ref):
    ang = inv_ref[...] * pos_ref[0].astype(F32)
    cs = jnp.concatenate([jnp.cos(ang), jnp.sin(ang)], axis=0)
    tabs = None
    rest = cs
    for _ in range(3):
        piece = rest.astype(BF16)
        rest = rest - piece.astype(F32)
        t = lax.dot_general(piece, sel_ref[...], TN_DIMS, preferred_element_type=F32)
        tabs = t if tabs is None else tabs + t
    cosf = tabs[:, :LANES] + one_ref[...]
    sneg = tabs[:, LANES:2 * LANES]
    spos = tabs[:, 2 * LANES:]

    nb = _rms(x_ref[...], g_ref[...]).astype(BF16)

    def proj_gla(c0, c1):
        return jnp.dot(nb, wgla_ref[:, c0:c1], preferred_element_type=F32)

    def proj(c0, c1):
        return jnp.dot(nb, wdiff_ref[:, c0:c1], preferred_element_type=F32)

    gq_ref[...] = proj_gla(C_GQ, C_GK) * (GLA_DK ** -0.5)
    gk_ref[...] = proj_gla(C_GK, C_GV)
    gv_ref[...] = proj_gla(C_GV, C_GR).astype(BF16)
    gr_ref[...] = proj_gla(C_GR, C_AL).astype(BF16)

    gal = proj_gla(C_AL, C_GLA_END).astype(BF16)
    a_logit = jnp.dot(gal, wa2_ref[...], preferred_element_type=F32) + ba_ref[...]
    la_ref[...] = (jnp.minimum(a_logit, 0.0) - jnp.log1p(jnp.exp(-jnp.abs(a_logit)))) / GLA_TAU

    def rope_store(c0, out_ref, scale):
        for j in range(DIFF_W // MXU_N):
            pair = proj(c0 + j * MXU_N, c0 + (j + 1) * MXU_N)
            for i in range(MXU_N // LANES):
                blk = pair[:, i * LANES:(i + 1) * LANES]
                rot = (blk * cosf + pltpu.roll(blk, LANES - ROPE_HALF, 1) * sneg
                       + pltpu.roll(blk, ROPE_HALF, 1) * spos)
                c = j * MXU_N + i * LANES
                out_ref[:, c:c + LANES] = (rot * scale).astype(BF16)

    rope_store(C_DQ, dq_ref, DIFF_DH ** -0.5 * LOG2_E)
    rope_store(C_DK, dk_ref, 1.0)
    dv_ref[...] = proj(C_DV, C_END).astype(BF16)


def _rope_tables():
    lane = np.arange(LANES)
    r = lane % DIFF_DH
    selc = np.zeros((2 * ROPE_HALF, LANES), np.float32)
    seln = np.zeros_like(selc)
    selp = np.zeros_like(selc)
    one = np.zeros((1, LANES), np.float32)
    for l in range(LANES):
        if r[l] < ROPE_DIM:
            selc[r[l] % ROPE_HALF, l] = 1.0
            if r[l] < ROPE_HALF:
                seln[ROPE_HALF + r[l], l] = -1.0
            else:
                selp[ROPE_HALF + r[l] - ROPE_HALF, l] = 1.0
        else:
            one[0, l] = 1.0
    sel = np.concatenate([selc, seln, selp], axis=1)
    return jnp.asarray(sel, dtype=BF16), jnp.asarray(one)


def _inproj(x2, positions, attn_norm, wgla, wdiff, wa2, b_a, *, tm):
    T, D = x2.shape
    nt = T // tm
    pos3 = positions.reshape(nt, 1, tm)
    inv = (ROPE_THETA ** (-jnp.arange(0, ROPE_DIM, 2, dtype=F32) / ROPE_DIM)).reshape(ROPE_HALF, 1)
    sel, one = _rope_tables()
    row = lambda i: (i, 0)
    fixed = lambda i: (0, 0)
    out_shapes = (
        jax.ShapeDtypeStruct((T, GLA_QK), F32), jax.ShapeDtypeStruct((T, GLA_QK), F32),
        jax.ShapeDtypeStruct((T, GLA_V), BF16), jax.ShapeDtypeStruct((T, GLA_V), BF16),
        jax.ShapeDtypeStruct((T, GLA_QK), F32),
        jax.ShapeDtypeStruct((T, DIFF_W), BF16), jax.ShapeDtypeStruct((T, DIFF_W), BF16),
        jax.ShapeDtypeStruct((T, DIFF_W), BF16),
    )
    return pl.pallas_call(
        _inproj_kernel,
        grid=(nt,),
        in_specs=[
            pl.BlockSpec((tm, D), row),
            pl.BlockSpec((1, 1, tm), lambda i: (i, 0, 0)),
            pl.BlockSpec((1, D), fixed),
            pl.BlockSpec((D, C_GLA_END), fixed),
            pl.BlockSpec((D, C_END), fixed),
            pl.BlockSpec((GLA_LOWRANK, GLA_QK), fixed),
            pl.BlockSpec((1, GLA_QK), fixed),
            pl.BlockSpec((ROPE_HALF, 1), fixed),
            pl.BlockSpec((2 * ROPE_HALF, 3 * LANES), fixed),
            pl.BlockSpec((1, LANES), fixed),
        ],
        out_specs=[pl.BlockSpec((tm, s.shape[1]), row) for s in out_shapes],
        out_shape=out_shapes,
        compiler_params=_cparams(("parallel",)),
        name="inproj",
    )(x2, pos3, attn_norm.reshape(1, D), wgla, wdiff, wa2, b_a.reshape(1, GLA_QK), inv, sel, one)


def _gla_kernel(gq_ref, gk_ref, gv_ref, gr_ref, la_ref, gn_ref, oa_ref, st_ref, *, n_chunks):
    @pl.when(pl.program_id(1) == 0)
    def _():
        st_ref[...] = jnp.zeros_like(st_ref)

    C = GLA_CHUNK
    tril = lax.broadcasted_iota(jnp.int32, (C, C), 0) >= lax.broadcasted_iota(jnp.int32, (C, C), 1)
    trilb = jnp.where(tril, 1.0, 0.0).astype(BF16)
    head_of_lane = lax.broadcasted_iota(jnp.int32, (1, GLA_QK), 1) // GLA_DK
    hmask = [head_of_lane == h for h in range(GLA_HEADS)]
    gn = gn_ref[...]

    def by_head(x):
        return jnp.concatenate([jnp.where(hmask[h], x, jnp.zeros_like(x))
                                for h in range(GLA_HEADS)], axis=0)

    chunks = range(n_chunks)
    sls = [pl.ds(c * C, C) for c in chunks]
    vs = [gv_ref[0, sl, :] for sl in sls]

    def cumsum(la):
        total, rest = None, la
        for _ in range(3):
            piece = rest.astype(BF16)
            rest = rest - piece.astype(F32)
            t = jnp.dot(trilb, piece, preferred_element_type=F32)
            total = t if total is None else total + t
        return total

    bs = [cumsum(la_ref[0, sl, :]) for sl in sls]
    b_lasts = [b[C - 1:C, :] for b in bs]
    q_heads = [by_head((gq_ref[0, sl, :] * jnp.exp(b)).astype(BF16)) for sl, b in zip(sls, bs)]
    k_invs = [(gk_ref[0, sl, :] * jnp.exp(-b)).astype(BF16) for sl, b in zip(sls, bs)]
    k_ends = [by_head((gk_ref[0, sl, :] * jnp.exp(bl - b)).astype(BF16))
              for sl, b, bl in zip(sls, bs, b_lasts)]
    decays = [jnp.exp(bl) for bl in b_lasts]
    atts = [lax.dot_general(qh, ki, NT_DIMS, preferred_element_type=F32)
            for qh, ki in zip(q_heads, k_invs)]
    v_heads = [jnp.concatenate([v[:, h * GLA_DV:(h + 1) * GLA_DV] for h in range(GLA_HEADS)], axis=0)
               for v in vs]
    us = [lax.dot_general(vh, ke, TN_DIMS, preferred_element_type=F32)
          for vh, ke in zip(v_heads, k_ends)]

    st = st_ref[...]
    states = []
    for c in chunks:
        states.append(st.astype(BF16))
        st = st * decays[c] + us[c]
    st_ref[...] = st

    inters = [lax.dot_general(qh, s_in, NT_DIMS, preferred_element_type=F32)
              for qh, s_in in zip(q_heads, states)]
    for c in chunks:
        for h in range(GLA_HEADS):
            rows = slice(h * C, (h + 1) * C)
            cols = slice(h * GLA_DV, (h + 1) * GLA_DV)
            a_h = jnp.where(tril, atts[c][rows], 0.0).astype(BF16)
            o_h = jnp.dot(a_h, vs[c][:, cols], preferred_element_type=F32) + inters[c][rows]
            y = _rms(o_h, gn)
            r = gr_ref[0, sls[c], cols].astype(F32)
            oa_ref[0, sls[c], cols] = (y * (r * jax.nn.sigmoid(r))).astype(BF16)


def _gla(gq, gk, gv, gr, la, gla_norm, *, B, S, ts):
    n_chunks = ts // GLA_CHUNK
    blk = lambda w: pl.BlockSpec((1, ts, w), lambda b, j: (b, j, 0))
    r3 = lambda a: a.reshape(B, S, a.shape[-1])
    return pl.pallas_call(
        functools.partial(_gla_kernel, n_chunks=n_chunks),
        grid=(B, S // ts),
        in_specs=[blk(GLA_QK), blk(GLA_QK), blk(GLA_V), blk(GLA_V), blk(GLA_QK),
                  pl.BlockSpec((1, GLA_DV), lambda b, j: (0, 0))],
        out_specs=blk(GLA_V),
        out_shape=jax.ShapeDtypeStruct((B, S, GLA_V), BF16),
        scratch_shapes=[pltpu.VMEM((GLA_DV, GLA_QK), F32)],
        compiler_params=_cparams(("parallel", "arbitrary")),
        name="gla",
    )(r3(gq), r3(gk), r3(gv), r3(gr), r3(la), gla_norm.reshape(1, GLA_DV))


ATTN_PAIR = 4


def _diffattn_kernel(dq_ref, dk_ref, dv_ref, lq1_ref, lk1_ref, lq2_ref, lk2_ref, sub_ref,
                     ob_ref, m_ref, l_ref, acc_ref, *, tq, tk, lambda_init):
    S = dq_ref.shape[1]
    nq = S // tq
    R = 2 * tq
    lam = (jnp.exp(jnp.sum(lq1_ref[...] * lk1_ref[...], axis=-1, keepdims=True))
           - jnp.exp(jnp.sum(lq2_ref[...] * lk2_ref[...], axis=-1, keepdims=True))
           + lambda_init)
    first_comp = lax.broadcasted_iota(jnp.int32, (1, DIFF_DV), 1) < DIFF_DH
    sub = sub_ref[...]

    pair = range(ATTN_PAIR)
    for hp in range(DIFF_HEADS // ATTN_PAIR):
        colss = [slice((hp * ATTN_PAIR + t) * DIFF_DV, (hp * ATTN_PAIR + t + 1) * DIFF_DV)
                 for t in pair]
        for qi in range(nq):
            qss = []
            for t in pair:
                qb = dq_ref[0, qi * tq:(qi + 1) * tq, colss[t]]
                zero = jnp.zeros_like(qb)
                qss.append(jnp.concatenate([jnp.where(first_comp, qb, zero),
                                            jnp.where(first_comp, zero, qb)], axis=0))
            m_ref[...] = jnp.full(m_ref.shape, NEG_BIG, F32)
            l_ref[...] = jnp.zeros(l_ref.shape, F32)
            acc_ref[...] = jnp.zeros(acc_ref.shape, F32)

            def scores(kstart, qss=qss, colss=colss):
                return [lax.dot_general(dk_ref[0, kstart:kstart + tk, colss[t]], qss[t], NT_DIMS,
                                        preferred_element_type=F32) for t in pair]

            def update(ss, kstart, masked, colss=colss, qi=qi):
                if masked:
                    kpos = kstart + lax.broadcasted_iota(jnp.int32, (tk, R), 0)
                    qpos = qi * tq + lax.broadcasted_iota(jnp.int32, (tk, R), 1) % tq
                    ss = [jnp.where(qpos >= kpos, s, NEG_BIG) for s in ss]
                m_prevs = [m_ref[t] for t in pair]
                m_news = [jnp.maximum(mp, jnp.max(s, axis=0, keepdims=True))
                          for mp, s in zip(m_prevs, ss)]
                alphas = [jnp.exp2(mp - mn) for mp, mn in zip(m_prevs, m_news)]
                ps = [jnp.exp2(s - mn) for s, mn in zip(ss, m_news)]
                for t in pair:
                    l_ref[t] = alphas[t] * l_ref[t] + jnp.sum(ps[t], axis=0, keepdims=True)
                    m_ref[t] = m_news[t]
                pvs = [lax.dot_general(dv_ref[0, kstart:kstart + tk, colss[t]], ps[t].astype(BF16),
                                       TN_DIMS, preferred_element_type=F32) for t in pair]
                for t in pair:
                    acc_ref[t] = alphas[t] * acc_ref[t] + pvs[t]

            n_blocks = (qi + 1) * tq // tk
            diag_from = qi * tq // tk
            s_next = scores(0)
            for kb_i in range(n_blocks):
                s_cur = s_next
                if kb_i + 1 < n_blocks:
                    s_next = scores((kb_i + 1) * tk)
                update(s_cur, kb_i * tk, kb_i >= diag_from)

            for t in pair:
                o_all = acc_ref[t] / l_ref[t]
                o = o_all[:, :tq] - lam * o_all[:, tq:]
                y = o * lax.rsqrt(jnp.mean(o * o, axis=0, keepdims=True) + EPS) * sub
                y = y * (1.0 - lambda_init)
                ob_ref[0, qi * tq:(qi + 1) * tq, colss[t]] = y.T.astype(BF16)


def _diffattn(dq, dk, dv, lq1, lk1, lq2, lk2, diff_subln, *, B, S, tq, tk, lambda_init):
    seq = pl.BlockSpec((1, S, DIFF_W), lambda b: (b, 0, 0))
    vec = lambda w: pl.BlockSpec((1, w), lambda b: (0, 0))
    r3 = lambda a: a.reshape(B, S, DIFF_W)
    return pl.pallas_call(
        functools.partial(_diffattn_kernel, tq=tq, tk=tk, lambda_init=lambda_init),
        grid=(B,),
        in_specs=[seq, seq, seq, vec(DIFF_DH), vec(DIFF_DH), vec(DIFF_DH), vec(DIFF_DH),
                  pl.BlockSpec((DIFF_DV, 1), lambda b: (0, 0))],
        out_specs=seq,
        out_shape=jax.ShapeDtypeStruct((B, S, DIFF_W), BF16),
        scratch_shapes=[pltpu.VMEM((ATTN_PAIR, 1, 2 * tq), F32), pltpu.VMEM((ATTN_PAIR, 1, 2 * tq), F32),
                        pltpu.VMEM((ATTN_PAIR, DIFF_DV, 2 * tq), F32)],
        compiler_params=_cparams(("parallel",)),
        name="diffattn",
    )(r3(dq), r3(dk), r3(dv), lq1.reshape(1, -1), lk1.reshape(1, -1), lq2.reshape(1, -1),
      lk2.reshape(1, -1), diff_subln.reshape(-1, 1))


ROUTE_W = 8


def _first_index_of(mask, lane):
    return jnp.min(jnp.where(mask, lane, LANES), axis=-1, keepdims=True)


POSTMIX_SPLIT = 4


def _postmix_kernel(x_ref, oa_ref, ob_ref, g_ref, wg_ref, wba_ref, wbb_ref, wo_ref, fn_ref,
                    wr_ref, br_ref, h_ref, n2_ref, route_ref, cnt_ref):
    D = x_ref.shape[1]
    rows_per = x_ref.shape[0] // POSTMIX_SPLIT
    groups = [slice(s * rows_per, (s + 1) * rows_per) for s in range(POSTMIX_SPLIT)]
    dot = functools.partial(jnp.dot, preferred_element_type=F32)
    xs = [x_ref[r, :] for r in groups]
    nbs = [_rms(x, g_ref[...]).astype(BF16) for x in xs]
    y_as = [dot(oa_ref[r, :], wba_ref[...]) for r in groups]
    g_as = [dot(nb, wg_ref[:, :D]) for nb in nbs]
    mergeds = [jax.nn.sigmoid(g) * y for g, y in zip(g_as, y_as)]
    y_bs = [dot(ob_ref[r, :], wbb_ref[...]) for r in groups]
    g_bs = [dot(nb, wg_ref[:, D:]) for nb in nbs]
    mergeds = [m + jax.nn.sigmoid(g) * y for m, g, y in zip(mergeds, g_bs, y_bs)]
    hs = [x + dot(m.astype(BF16), wo_ref[...]) for x, m in zip(xs, mergeds)]
    n2s = [_rms(h, fn_ref[...]) for h in hs]
    n2_his = [n2.astype(BF16) for n2 in n2s]
    n2_los = [(n2 - hi.astype(F32)).astype(BF16) for n2, hi in zip(n2s, n2_his)]
    hi_prods = [dot(hi, wr_ref[...]) for hi in n2_his]
    lgs = [hp[:, :LANES] + hp[:, LANES:] + dot(lo, wr_ref[:, :LANES]) + br_ref[...]
           for hp, lo in zip(hi_prods, n2_los)]
    cnt = None
    for r, h, n2_hi, lg in zip(groups, hs, n2_his, lgs):
        h_ref[r, :] = h
        n2_ref[r, :] = n2_hi
        rec, c = _route_top2(lg)
        route_ref[r, :] = rec
        cnt = c if cnt is None else cnt + c
    cnt_ref[0] = cnt


def _route_top2(lg):
    lane = lax.broadcasted_iota(jnp.int32, lg.shape, 1)
    is_g = lane < N_GROUPS
    g_max = jnp.max(jnp.where(is_g, lg, -jnp.inf), axis=-1, keepdims=True)
    g_exp = jnp.where(is_g, jnp.exp(lg - g_max), 0.0)
    g_prob = g_exp / jnp.sum(g_exp, axis=-1, keepdims=True)
    g_p = jnp.max(g_prob, axis=-1, keepdims=True)
    g_idx = _first_index_of(is_g & (g_prob == g_p), lane)

    e_lo = N_GROUPS + EXPERTS_PER_GROUP * g_idx
    is_e = (lane >= e_lo) & (lane < e_lo + EXPERTS_PER_GROUP)
    e_max = jnp.max(jnp.where(is_e, lg, -jnp.inf), axis=-1, keepdims=True)
    e_exp = jnp.where(is_e, jnp.exp(lg - e_max), 0.0)
    e_prob = e_exp / jnp.sum(e_exp, axis=-1, keepdims=True)
    p1 = jnp.max(jnp.where(is_e, e_prob, -1.0), axis=-1, keepdims=True)
    i1 = _first_index_of(is_e & (e_prob == p1), lane)
    rest = is_e & (lane != i1)
    p2 = jnp.max(jnp.where(rest, e_prob, -1.0), axis=-1, keepdims=True)
    i2 = _first_index_of(rest & (e_prob == p2), lane)
    den = p1 + p2
    w1 = g_p * (p1 / den)
    w2 = g_p * (p2 / den)
    rec = jnp.where(lane == 0, w1, 0.0)
    rec = jnp.where(lane == 1, w2, rec)
    rec = jnp.where(lane == 2, (i1 - N_GROUPS).astype(F32), rec)
    rec = jnp.where(lane == 3, (i2 - N_GROUPS).astype(F32), rec)
    chosen = (lane == i1 - N_GROUPS) | (lane == i2 - N_GROUPS)
    cnt = jnp.sum(chosen.astype(F32), axis=0, keepdims=True)
    return rec[:, :ROUTE_W], cnt


def _postmix(x2, oa, ob, attn_norm, wg, wba, wbb, wo, ffn_norm, wr, br, *, tm):
    T, D = x2.shape
    row = lambda i: (i, 0)
    fixed = lambda i: (0, 0)
    full = lambda a: pl.BlockSpec(a.shape, fixed)
    g = attn_norm.reshape(1, D)
    fn = ffn_norm.reshape(1, D)
    return pl.pallas_call(
        _postmix_kernel,
        grid=(T // tm,),
        in_specs=[pl.BlockSpec((tm, D), row), pl.BlockSpec((tm, GLA_V), row),
                  pl.BlockSpec((tm, DIFF_W), row), full(g), full(wg), full(wba), full(wbb),
                  full(wo), full(fn), full(wr), full(br)],
        out_specs=[pl.BlockSpec((tm, D), row), pl.BlockSpec((tm, D), row),
                   pl.BlockSpec((tm, ROUTE_W), row),
                   pl.BlockSpec((1, 1, LANES), lambda i: (i, 0, 0))],
        out_shape=(jax.ShapeDtypeStruct((T, D), F32), jax.ShapeDtypeStruct((T, D), BF16),
                   jax.ShapeDtypeStruct((T, ROUTE_W), F32),
                   jax.ShapeDtypeStruct((T // tm, 1, LANES), F32)),
        compiler_params=_cparams(("parallel",)),
        name="postmix",
    )(x2, oa, ob, g, wg, wba, wbb, wo, fn, wr, br)


DISPATCH_CHUNK = 8
SLOT_W = 8


def _token_copy(src, src_tok, dst, dst_tok, n_tok, sem):
    return pltpu.make_async_copy(
        src.at[pl.ds(pl.multiple_of(src_tok * SUBLANES, SUBLANES), n_tok * SUBLANES), :],
        dst.at[pl.ds(pl.multiple_of(dst_tok * SUBLANES, SUBLANES), n_tok * SUBLANES), :], sem)


def _for_each_piece(length, fn):
    n_full = lax.shift_right_logical(length, 3)

    def body(c, carry):
        fn(c * DISPATCH_CHUNK, DISPATCH_CHUNK)
        return carry

    lax.fori_loop(0, n_full, body, 0)
    off = n_full * DISPATCH_CHUNK
    for n in (4, 2, 1):
        has = (length & n) != 0

        @pl.when(has)
        def _(off=off, n=n):
            fn(off, n)

        off = off + jnp.where(has, n, 0)


def _dispatch_kernel(seg_ref, lo_ref, cnt_ref, fill_ref, fill_len_ref, n_tiles_ref,
                     n2_ref, route_ref, lo_row_ref,
                     xg_hbm, slot_ref, xs_buf, zero_buf, sem, fill_sem):
    tm = n2_ref.shape[0]
    n_slab = 2 * tm * SUBLANES
    i = pl.program_id(0)
    nt = pl.num_programs(0)
    slot = i % 2

    @pl.when(i == 0)
    def _():
        zero_buf[...] = jnp.zeros_like(zero_buf)
        te = zero_buf.shape[0] // SUBLANES
        for wait in (False, True):
            def fill(e, carry, wait=wait):
                def piece(off, n):
                    cp = _token_copy(zero_buf, 0, xg_hbm, fill_ref[e] + off, n, fill_sem)
                    cp.wait() if wait else cp.start()
                _for_each_piece(fill_len_ref[e], piece)
                return carry
            lax.fori_loop(0, N_EXPERTS, fill, 0)

            def fill_tile(t, carry, wait=wait):
                cp = _token_copy(zero_buf, 0, xg_hbm, t * te, te, fill_sem)
                cp.wait() if wait else cp.start()
                return carry
            lax.fori_loop(n_tiles_ref[0], xg_hbm.shape[0] // zero_buf.shape[0], fill_tile, 0)

    route = route_ref[...]
    lane = lax.broadcasted_iota(jnp.int32, (tm, LANES), 1)
    oh1 = lane == route[:, 2:3].astype(jnp.int32)
    oh2 = lane == route[:, 3:4].astype(jnp.int32)
    both = jnp.where(oh1 | oh2, 1.0, 0.0).astype(BF16)
    earlier = (lax.broadcasted_iota(jnp.int32, (tm, tm), 0)
               > lax.broadcasted_iota(jnp.int32, (tm, tm), 1))
    rank = jnp.dot(jnp.where(earlier, 1.0, 0.0).astype(BF16), both, preferred_element_type=F32)

    def pick(onehot, v):
        return jnp.sum(jnp.where(onehot, v, 0.0), axis=-1, keepdims=True)

    r1 = pick(oh1, rank)
    r2 = pick(oh2, rank)
    q1 = pick(oh1, lo_row_ref[0]) + r1
    q2 = pick(oh2, lo_row_ref[0]) + r2
    rec = jnp.where(lane == 0, q1, jnp.where(lane == 1, q2, 0.0))
    slot_ref[...] = rec[:, :SLOT_W]

    rec_t = rec.T.astype(jnp.int32)
    slot_id = lax.broadcasted_iota(jnp.int32, (2 * tm, tm), 0)
    perm = jnp.where((slot_id == rec_t[0:1, :]) | (slot_id == rec_t[1:2, :]), 1.0, 0.0)
    xs = jnp.dot(perm.astype(BF16), n2_ref[...], preferred_element_type=F32)

    def wait_slot(s):
        pltpu.make_async_copy(xg_hbm.at[pl.ds(0, n_slab), :], xs_buf.at[s], sem.at[s]).wait()

    @pl.when(i >= 2)
    def _():
        wait_slot(slot)

    _store_slab(xs_buf.at[slot], xs)

    def send(e, carry):
        k = i * N_EXPERTS + e

        def piece(off, n):
            _token_copy(xs_buf.at[slot], lo_ref[k] + off, xg_hbm, seg_ref[k] + off, n,
                        sem.at[slot]).start()

        _for_each_piece(cnt_ref[k], piece)
        return carry

    lax.fori_loop(0, N_EXPERTS, send, 0)

    @pl.when(i == nt - 1)
    def _():
        wait_slot(slot)

        @pl.when(nt >= 2)
        def _():
            wait_slot(1 - slot)


def _dispatch(n2, route, cnt, *, tm, te):
    T, D = n2.shape
    nt = T // tm
    n_tiles_max = 2 * T // te + N_EXPERTS
    c = cnt[:, 0, :N_EXPERTS].astype(jnp.int32)
    count = jnp.sum(c, axis=0)
    padded = (count + te - 1) // te * te
    p_end = jnp.cumsum(padded)
    p_start = p_end - padded
    n_tiles = (p_end[-1] // te).astype(jnp.int32).reshape(1)
    tile_start = jnp.arange(n_tiles_max, dtype=jnp.int32) * te
    tile_e = jnp.sum(tile_start[:, None] >= p_end[None, :], axis=1, dtype=jnp.int32)
    tile_e = jnp.minimum(tile_e, jnp.sum(tile_start[n_tiles[0] - 1] >= p_end, dtype=jnp.int32))
    seg = p_start[None, :] + jnp.cumsum(c, axis=0) - c
    lo = jnp.cumsum(c, axis=1) - c
    lanes = lambda a: jnp.pad(a.astype(F32), ((0, 0), (0, LANES - N_EXPERTS))).reshape(nt, 1, LANES)
    flat = lambda a: a.reshape(-1).astype(jnp.int32)
    row = lambda i, *_: (i, 0)
    per_tile = pl.BlockSpec((1, 1, LANES), lambda i, *_: (i, 0, 0))
    grid_spec = pltpu.PrefetchScalarGridSpec(
        num_scalar_prefetch=6,
        grid=(nt,),
        in_specs=[pl.BlockSpec((tm, D), row), pl.BlockSpec((tm, ROUTE_W), row), per_tile],
        out_specs=[pl.BlockSpec(memory_space=pl.ANY), pl.BlockSpec((tm, SLOT_W), row)],
        scratch_shapes=[pltpu.VMEM((2, 2 * tm * SUBLANES, LANES), F32),
                        pltpu.VMEM((te * SUBLANES, LANES), F32),
                        pltpu.SemaphoreType.DMA((2,)), pltpu.SemaphoreType.DMA(())],
    )
    tables = (flat(seg), flat(lo), flat(c))
    xg, slots = pl.pallas_call(
        _dispatch_kernel,
        grid_spec=grid_spec,
        out_shape=(jax.ShapeDtypeStruct((n_tiles_max * te * SUBLANES, LANES), F32),
                   jax.ShapeDtypeStruct((T, SLOT_W), F32)),
        compiler_params=_cparams(("arbitrary",)),
        name="dispatch",
    )(*tables, flat(p_start + count), flat(padded - count), n_tiles, n2, route, lanes(lo))
    return xg, slots, tables, tile_e, n_tiles


def _experts_kernel(tile_e_ref, n_tiles_ref, xg_ref, wg_ref, wu_ref, wd_ref, y_ref,
                    wg_bf, wu_bf, wd_bf):
    tm = xg_ref.shape[0] // SUBLANES
    i = pl.program_id(0)
    used = i < n_tiles_ref[0]

    @pl.when(used & ((i == 0) | (tile_e_ref[i] != tile_e_ref[jnp.maximum(i - 1, 0)])))
    def _():
        wg_bf[...] = wg_ref[0].astype(BF16)
        wu_bf[...] = wu_ref[0].astype(BF16)
        wd_bf[...] = wd_ref[0].astype(BF16)

    @pl.when(used)
    def _():
        x = _load_slab(xg_ref, tm).astype(BF16)
        g = jnp.dot(x, wg_bf[...], preferred_element_type=F32)
        u = jnp.dot(x, wu_bf[...], preferred_element_type=F32)
        he = (g * jax.nn.sigmoid(g)) * u
        _store_slab(y_ref, jnp.dot(he.astype(BF16), wd_bf[...], preferred_element_type=F32))

    @pl.when(i >= n_tiles_ref[0])
    def _():
        y_ref[...] = jnp.zeros_like(y_ref)


def _experts(tile_e, n_tiles, xg, wg, wu, wd, *, tm):
    D = wg.shape[1]
    n_tiles_max = xg.shape[0] // (tm * SUBLANES)
    slab = lambda f: pl.BlockSpec((tm * SUBLANES, LANES), f)
    grid_spec = pltpu.PrefetchScalarGridSpec(
        num_scalar_prefetch=2,
        grid=(n_tiles_max,),
        in_specs=[slab(lambda i, te, nt: (jnp.minimum(i, nt[0] - 1), 0)),
                  pl.BlockSpec((1, D, D_EXPERT), lambda i, te, nt: (te[i], 0, 0)),
                  pl.BlockSpec((1, D, D_EXPERT), lambda i, te, nt: (te[i], 0, 0)),
                  pl.BlockSpec((1, D_EXPERT, D), lambda i, te, nt: (te[i], 0, 0))],
        out_specs=slab(lambda i, te, nt: (i, 0)),
        scratch_shapes=[pltpu.VMEM((D, D_EXPERT), BF16), pltpu.VMEM((D, D_EXPERT), BF16),
                        pltpu.VMEM((D_EXPERT, D), BF16)],
    )
    return pl.pallas_call(
        _experts_kernel,
        grid_spec=grid_spec,
        out_shape=jax.ShapeDtypeStruct(xg.shape, F32),
        compiler_params=_cparams(("arbitrary",)),
        name="experts",
    )(tile_e, n_tiles, xg, wg, wu, wd)


def _final_kernel(seg_ref, lo_ref, cnt_ref, h_ref, ys_hbm, route_ref, slot_ref, p_ref, wple_ref,
                  pn_ref, wpg_ref, fn_ref, o_ref, ybuf, sem, *, apply_final_norm):
    tm = h_ref.shape[0]
    i = pl.program_id(0)
    nt = pl.num_programs(0)
    slot = i % 2

    def fetch(tile, s):
        def run(e, carry):
            k = tile * N_EXPERTS + e

            def piece(off, n):
                _token_copy(ys_hbm, seg_ref[k] + off, ybuf.at[s], lo_ref[k] + off, n,
                            sem.at[s]).start()

            _for_each_piece(cnt_ref[k], piece)
            return carry

        lax.fori_loop(0, N_EXPERTS, run, 0)

    @pl.when(i == 0)
    def _():
        fetch(0, 0)

    @pl.when(i + 1 < nt)
    def _():
        fetch(i + 1, 1 - slot)

    pltpu.make_async_copy(ys_hbm.at[pl.ds(0, 2 * tm * SUBLANES), :], ybuf.at[slot],
                          sem.at[slot]).wait()
    y_sorted = _load_slab(ybuf.at[slot], 2 * tm).astype(BF16)
    route = route_ref[...]
    slots = slot_ref[...].astype(jnp.int32)
    col = lax.broadcasted_iota(jnp.int32, (tm, 2 * tm), 1)
    comb = jnp.where(col == slots[:, 0:1], route[:, 0:1],
                     jnp.where(col == slots[:, 1:2], route[:, 1:2], 0.0))
    h = h_ref[...] + jnp.dot(comb.astype(BF16), y_sorted, preferred_element_type=F32)
    e = _rms(jnp.dot(p_ref[...].astype(BF16), wple_ref[...], preferred_element_type=F32), pn_ref[...])
    gate = jax.nn.sigmoid(jnp.dot(h.astype(BF16), wpg_ref[...], preferred_element_type=F32))
    h = h + gate * e
    if apply_final_norm:
        h = _rms(h, fn_ref[...])
    o_ref[...] = h


def _final(h1, ys, slots, tables, route, p2, wple, ple_norm, wpg, final_norm, *, tm,
           apply_final_norm):
    T, D = h1.shape
    nt = T // tm
    row = lambda i, *_: (i, 0)
    fixed = lambda i, *_: (0, 0)
    full = lambda a: pl.BlockSpec(a.shape, fixed)
    pn = ple_norm.reshape(1, D)
    fn = final_norm.reshape(1, D)
    grid_spec = pltpu.PrefetchScalarGridSpec(
        num_scalar_prefetch=3,
        grid=(nt,),
        in_specs=[pl.BlockSpec((tm, D), row), pl.BlockSpec(memory_space=pl.ANY),
                  pl.BlockSpec((tm, ROUTE_W), row), pl.BlockSpec((tm, SLOT_W), row),
                  pl.BlockSpec((tm, p2.shape[1]), row),
                  full(wple), full(pn), full(wpg), full(fn)],
        out_specs=pl.BlockSpec((tm, D), row),
        scratch_shapes=[pltpu.VMEM((2, 2 * tm * SUBLANES, LANES), F32),
                        pltpu.SemaphoreType.DMA((2,))],
    )
    return pl.pallas_call(
        functools.partial(_final_kernel, apply_final_norm=apply_final_norm),
        grid_spec=grid_spec,
        out_shape=jax.ShapeDtypeStruct((T, D), F32),
        compiler_params=_cparams(("arbitrary",)),
        name="final",
    )(*tables, h1, ys, route, slots, p2, wple, pn, wpg, fn)


SPLIT_SIZES = (GLA_QK, GLA_QK, GLA_V, GLA_V, GLA_LOWRANK, DIFF_W, DIFF_W, DIFF_W)


def _layer(h2, p2, positions, B, S, lambda_init, apply_final_norm, attn_norm, w_in, w_a2, b_a,
           gla_norm, lq1, lk1, lq2, lk2, diff_subln, w_branch_a, w_branch_b, w_out, ffn_norm,
           w_rg, b_rg, w_re, b_re, w_gate, w_up, w_down, w_ple, ple_norm, w_ple_gate, final_norm,
           *, tm, ts, tq, tk, te):
    T, D = h2.shape
    assert D == SUBLANES * LANES, "token-slab gathers need one (8, 128) tile per token"
    n_mix = sum(SPLIT_SIZES)
    wgla = w_in[:, :C_GLA_END].astype(BF16)
    wdiff = w_in[:, C_GLA_END:n_mix].astype(BF16)
    wg = w_in[:, n_mix:].astype(BF16)

    gq, gk, gv, gr, la, dq, dk, dv = _inproj(h2, positions, attn_norm, wgla, wdiff,
                                             w_a2.astype(BF16), b_a, tm=tm)
    oa = _gla(gq, gk, gv, gr, la, gla_norm, B=B, S=S, ts=ts).reshape(T, GLA_V)
    ob = _diffattn(dq, dk, dv, lq1, lk1, lq2, lk2, diff_subln, B=B, S=S, tq=tq, tk=tk,
                   lambda_init=lambda_init).reshape(T, DIFF_W)

    wr = jnp.concatenate([w_rg, w_re, jnp.zeros((D, LANES - N_GROUPS - N_EXPERTS), F32)], axis=1)
    wr_hi = wr.astype(BF16)
    wr = jnp.concatenate([wr_hi, (wr - wr_hi.astype(F32)).astype(BF16)], axis=1)
    br = jnp.concatenate([b_rg, b_re, jnp.zeros((LANES - N_GROUPS - N_EXPERTS,), F32)]).reshape(1, LANES)
    h1, n2, route, cnt = _postmix(h2, oa, ob, attn_norm, wg, w_branch_a.astype(BF16),
                                  w_branch_b.astype(BF16), w_out.astype(BF16), ffn_norm, wr, br,
                                  tm=tm)

    xg, slots, tables, tile_e, n_tiles = _dispatch(n2, route, cnt, tm=tm, te=te)
    ys = _experts(tile_e, n_tiles, xg, w_gate, w_up, w_down, tm=te)
    return _final(h1, ys, slots, tables, route, p2, w_ple.astype(BF16), ple_norm,
                  w_ple_gate.astype(BF16), final_norm, tm=tm, apply_final_norm=apply_final_norm)


def _block(x, p, positions, attn_norm, w_in, w_a2, b_a, gla_norm, lambda_q1, lambda_k1, lambda_q2,
           lambda_k2, diff_subln, w_branch_a, w_branch_b, w_out, ffn_norm, w_router_group,
           b_router_group, w_router_expert, b_router_expert, w_gate, w_up, w_down, w_ple, ple_norm,
           w_ple_gate, final_norm, *, tm, ts, tq, tk, te):
    B, S, D = x.shape
    depth = w_in.shape[0]
    h = x.reshape(B * S, D)
    for i in range(depth):
        lambda_init = 0.8 - 0.6 * math.exp(-0.3 * i)
        h = _layer(h, p[i].reshape(B * S, -1), positions, B, S, lambda_init, i == depth - 1,
                   attn_norm[i], w_in[i], w_a2[i], b_a[i], gla_norm[i], lambda_q1[i], lambda_k1[i],
                   lambda_q2[i], lambda_k2[i], diff_subln[i], w_branch_a[i], w_branch_b[i], w_out[i],
                   ffn_norm[i], w_router_group[i], b_router_group[i], w_router_expert[i],
                   b_router_expert[i], w_gate[i], w_up[i], w_down[i], w_ple[i], ple_norm[i],
                   w_ple_gate[i], final_norm, tm=tm, ts=ts, tq=tq, tk=tk, te=te)
    return h.reshape(B, S, D)


def kernel(x, p, positions, attn_norm, w_in, w_a2, b_a, gla_norm, lambda_q1, lambda_k1, lambda_q2, lambda_k2, diff_subln, w_branch_a, w_branch_b, w_out, ffn_norm, w_router_group, b_router_group, w_router_expert, b_router_expert, w_gate, w_up, w_down, w_ple, ple_norm, w_ple_gate, final_norm):
    S = x.shape[1]
    return _block(x, p, positions, attn_norm, w_in, w_a2, b_a, gla_norm, lambda_q1, lambda_k1,
                  lambda_q2, lambda_k2, diff_subln, w_branch_a, w_branch_b, w_out, ffn_norm,
                  w_router_group, b_router_group, w_router_expert, b_router_expert, w_gate, w_up,
                  w_down, w_ple, ple_norm, w_ple_gate, final_norm,
                  tm=512, ts=min(512, S), tq=min(512, S), tk=min(512, S), te=512)
```

```python
import functools
import math

import jax
import jax.numpy as jnp
import numpy as np
from jax import lax
from jax.experimental import pallas as pl
from jax.experimental.pallas import tpu as pltpu

EPS = 1e-6

GLA_HEADS = 4
GLA_DK = 64
GLA_DV = 128
GLA_LOWRANK = 16
GLA_TAU = 16.0
GLA_CHUNK = 64
GLA_QK = GLA_HEADS * GLA_DK
GLA_V = GLA_HEADS * GLA_DV

DIFF_HEADS = 4
DIFF_DH = 64
DIFF_DV = 2 * DIFF_DH
DIFF_W = DIFF_HEADS * DIFF_DV
ROPE_THETA = 500000.0
ROPE_DIM = DIFF_DH // 4
ROPE_HALF = ROPE_DIM // 2

N_GROUPS = 4
EXPERTS_PER_GROUP = 8
N_EXPERTS = N_GROUPS * EXPERTS_PER_GROUP
D_EXPERT = 256

LANES = 128
MXU_N = 256
VMEM_LIMIT = 56 * 1024 * 1024

BF16 = jnp.bfloat16
F32 = jnp.float32
NT_DIMS = (((1,), (1,)), ((), ()))
TN_DIMS = (((0,), (0,)), ((), ()))
NEG_BIG = -1e30
LOG2_E = math.log2(math.e)


def _rms(x, g):
    return x * lax.rsqrt(jnp.mean(x * x, axis=-1, keepdims=True) + EPS) * g


def _cparams(semantics):
    return pltpu.CompilerParams(dimension_semantics=semantics, vmem_limit_bytes=VMEM_LIMIT)


SUBLANES = 8


def _store_slab(ref, x):
    n = x.shape[0]
    for j in range(SUBLANES):
        ref[pl.ds(j, n, stride=SUBLANES), :] = x[:, j * LANES:(j + 1) * LANES]


def _load_slab(ref, n):
    return jnp.concatenate([ref[pl.ds(j, n, stride=SUBLANES), :] for j in range(SUBLANES)], axis=1)


C_GQ = 0
C_GK = C_GQ + GLA_QK
C_GV = C_GK + GLA_QK
C_GR = C_GV + GLA_V
C_AL = C_GR + GLA_V
C_GLA_END = C_AL + GLA_LOWRANK
C_DQ = 0
C_DK = C_DQ + DIFF_W
C_DV = C_DK + DIFF_W
C_END = C_DV + DIFF_W


def _inproj_kernel(x_ref, pos_ref, g_ref, wgla_ref, wdiff_ref, wa2_ref, ba_ref, inv_ref, sel_ref,
                   one_ref,
                   gq_ref, gk_ref, gv_ref, gr_ref, la_ref, dq_ref, dk_ref, dv_ref):
    ang = inv_ref[...] * pos_ref[0].astype(F32)
    cs = jnp.concatenate([jnp.cos(ang), jnp.sin(ang)], axis=0)
    tabs = None
    rest = cs
    for _ in range(3):
        piece = rest.astype(BF16)
        rest = rest - piece.astype(F32)
        t = lax.dot_general(piece, sel_ref[...], TN_DIMS, preferred_element_type=F32)
        tabs = t if tabs is None else tabs + t
    cosf = tabs[:, :LANES] + one_ref[...]
    sneg = tabs[:, LANES:2 * LANES]
    spos = tabs[:, 2 * LANES:]

    nb = _rms(x_ref[...], g_ref[...]).astype(BF16)

    def proj_gla(c0, c1):
        return jnp.dot(nb, wgla_ref[:, c0:c1], preferred_element_type=F32)

    def proj(c0, c1):
        return jnp.dot(nb, wdiff_ref[:, c0:c1], preferred_element_type=F32)

    gq_ref[...] = proj_gla(C_GQ, C_GK) * (GLA_DK ** -0.5)
    gk_ref[...] = proj_gla(C_GK, C_GV)
    gv_ref[...] = proj_gla(C_GV, C_GR).astype(BF16)
    gr_ref[...] = proj_gla(C_GR, C_AL).astype(BF16)

    gal = proj_gla(C_AL, C_GLA_END).astype(BF16)
    a_logit = jnp.dot(gal, wa2_ref[...], preferred_element_type=F32) + ba_ref[...]
    la_ref[...] = (jnp.minimum(a_logit, 0.0) - jnp.log1p(jnp.exp(-jnp.abs(a_logit)))) / GLA_TAU

    def rope_store(c0, out_ref, scale):
        for j in range(DIFF_W // MXU_N):
            pair = proj(c0 + j * MXU_N, c0 + (j + 1) * MXU_N)
            for i in range(MXU_N // LANES):
                blk = pair[:, i * LANES:(i + 1) * LANES]
                rot = (blk * cosf + pltpu.roll(blk, LANES - ROPE_HALF, 1) * sneg
                       + pltpu.roll(blk, ROPE_HALF, 1) * spos)
                c = j * MXU_N + i * LANES
                out_ref[:, c:c + LANES] = (rot * scale).astype(BF16)

    rope_store(C_DQ, dq_ref, DIFF_DH ** -0.5 * LOG2_E)
    rope_store(C_DK, dk_ref, 1.0)
    dv_ref[...] = proj(C_DV, C_END).astype(BF16)


def _rope_tables():
    lane = np.arange(LANES)
    r = lane % DIFF_DH
    selc = np.zeros((2 * ROPE_HALF, LANES), np.float32)
    seln = np.zeros_like(selc)
    selp = np.zeros_like(selc)
    one = np.zeros((1, LANES), np.float32)
    for l in range(LANES):
        if r[l] < ROPE_DIM:
            selc[r[l] % ROPE_HALF, l] = 1.0
            if r[l] < ROPE_HALF:
                seln[ROPE_HALF + r[l], l] = -1.0
            else:
                selp[ROPE_HALF + r[l] - ROPE_HALF, l] = 1.0
        else:
            one[0, l] = 1.0
    sel = np.concatenate([selc, seln, selp], axis=1)
    return jnp.asarray(sel, dtype=BF16), jnp.asarray(one)


def _inproj(x2, positions, attn_norm, wgla, wdiff, wa2, b_a, *, tm):
    T, D = x2.shape
    nt = T // tm
    pos3 = positions.reshape(nt, 1, tm)
    inv = (ROPE_THETA ** (-jnp.arange(0, ROPE_DIM, 2, dtype=F32) / ROPE_DIM)).reshape(ROPE_HALF, 1)
    sel, one = _rope_tables()
    row = lambda i: (i, 0)
    fixed = lambda i: (0, 0)
    out_shapes = (
        jax.ShapeDtypeStruct((T, GLA_QK), F32), jax.ShapeDtypeStruct((T, GLA_QK), F32),
        jax.ShapeDtypeStruct((T, GLA_V), BF16), jax.ShapeDtypeStruct((T, GLA_V), BF16),
        jax.ShapeDtypeStruct((T, GLA_QK), F32),
        jax.ShapeDtypeStruct((T, DIFF_W), BF16), jax.ShapeDtypeStruct((T, DIFF_W), BF16),
        jax.ShapeDtypeStruct((T, DIFF_W), BF16),
    )
    return pl.pallas_call(
        _inproj_kernel,
        grid=(nt,),
        in_specs=[
            pl.BlockSpec((tm, D), row),
            pl.BlockSpec((1, 1, tm), lambda i: (i, 0, 0)),
            pl.BlockSpec((1, D), fixed),
            pl.BlockSpec((D, C_GLA_END), fixed),
            pl.BlockSpec((D, C_END), fixed),
            pl.BlockSpec((GLA_LOWRANK, GLA_QK), fixed),
            pl.BlockSpec((1, GLA_QK), fixed),
            pl.BlockSpec((ROPE_HALF, 1), fixed),
            pl.BlockSpec((2 * ROPE_HALF, 3 * LANES), fixed),
            pl.BlockSpec((1, LANES), fixed),
        ],
        out_specs=[pl.BlockSpec((tm, s.shape[1]), row) for s in out_shapes],
        out_shape=out_shapes,
        compiler_params=_cparams(("parallel",)),
        name="inproj",
    )(x2, pos3, attn_norm.reshape(1, D), wgla, wdiff, wa2, b_a.reshape(1, GLA_QK), inv, sel, one)


def _gla_kernel(gq_ref, gk_ref, gv_ref, gr_ref, la_ref, gn_ref, oa_ref, st_ref, *, n_chunks):
    @pl.when(pl.program_id(1) == 0)
    def _():
        st_ref[...] = jnp.zeros_like(st_ref)

    C = GLA_CHUNK
    tril = lax.broadcasted_iota(jnp.int32, (C, C), 0) >= lax.broadcasted_iota(jnp.int32, (C, C), 1)
    trilb = jnp.where(tril, 1.0, 0.0).astype(BF16)
    head_of_lane = lax.broadcasted_iota(jnp.int32, (1, GLA_QK), 1) // GLA_DK
    hmask = [head_of_lane == h for h in range(GLA_HEADS)]
    gn = gn_ref[...]

    def by_head(x):
        return jnp.concatenate([jnp.where(hmask[h], x, jnp.zeros_like(x))
                                for h in range(GLA_HEADS)], axis=0)

    chunks = range(n_chunks)
    sls = [pl.ds(c * C, C) for c in chunks]
    vs = [gv_ref[0, sl, :] for sl in sls]

    def cumsum(la):
        total, rest = None, la
        for _ in range(3):
            piece = rest.astype(BF16)
            rest = rest - piece.astype(F32)
            t = jnp.dot(trilb, piece, preferred_element_type=F32)
            total = t if total is None else total + t
        return total

    bs = [cumsum(la_ref[0, sl, :]) for sl in sls]
    b_lasts = [b[C - 1:C, :] for b in bs]
    q_heads = [by_head((gq_ref[0, sl, :] * jnp.exp(b)).astype(BF16)) for sl, b in zip(sls, bs)]
    k_invs = [(gk_ref[0, sl, :] * jnp.exp(-b)).astype(BF16) for sl, b in zip(sls, bs)]
    k_ends = [by_head((gk_ref[0, sl, :] * jnp.exp(bl - b)).astype(BF16))
              for sl, b, bl in zip(sls, bs, b_lasts)]
    decays = [jnp.exp(bl) for bl in b_lasts]
    atts = [lax.dot_general(qh, ki, NT_DIMS, preferred_element_type=F32)
            for qh, ki in zip(q_heads, k_invs)]
    v_heads = [jnp.concatenate([v[:, h * GLA_DV:(h + 1) * GLA_DV] for h in range(GLA_HEADS)], axis=0)
               for v in vs]
    us = [lax.dot_general(vh, ke, TN_DIMS, preferred_element_type=F32)
          for vh, ke in zip(v_heads, k_ends)]

    st = st_ref[...]
    states = []
    for c in chunks:
        states.append(st.astype(BF16))
        st = st * decays[c] + us[c]
    st_ref[...] = st

    inters = [lax.dot_general(qh, s_in, NT_DIMS, preferred_element_type=F32)
              for qh, s_in in zip(q_heads, states)]
    for c in chunks:
        for h in range(GLA_HEADS):
            rows = slice(h * C, (h + 1) * C)
            cols = slice(h * GLA_DV, (h + 1) * GLA_DV)
            a_h = jnp.where(tril, atts[c][rows], 0.0).astype(BF16)
            o_h = jnp.dot(a_h, vs[c][:, cols], preferred_element_type=F32) + inters[c][rows]
            y = _rms(o_h, gn)
            r = gr_ref[0, sls[c], cols].astype(F32)
            oa_ref[0, sls[c], cols] = (y * (r * jax.nn.sigmoid(r))).astype(BF16)


def _gla(gq, gk, gv, gr, la, gla_norm, *, B, S, ts):
    n_chunks = ts // GLA_CHUNK
    blk = lambda w: pl.BlockSpec((1, ts, w), lambda b, j: (b, j, 0))
    r3 = lambda a: a.reshape(B, S, a.shape[-1])
    return pl.pallas_call(
        functools.partial(_gla_kernel, n_chunks=n_chunks),
        grid=(B, S // ts),
        in_specs=[blk(GLA_QK), blk(GLA_QK), blk(GLA_V), blk(GLA_V), blk(GLA_QK),
                  pl.BlockSpec((1, GLA_DV), lambda b, j: (0, 0))],
        out_specs=blk(GLA_V),
        out_shape=jax.ShapeDtypeStruct((B, S, GLA_V), BF16),
        scratch_shapes=[pltpu.VMEM((GLA_DV, GLA_QK), F32)],
        compiler_params=_cparams(("parallel", "arbitrary")),
        name="gla",
    )(r3(gq), r3(gk), r3(gv), r3(gr), r3(la), gla_norm.reshape(1, GLA_DV))


ATTN_PAIR = 4


def _diffattn_kernel(dq_ref, dk_ref, dv_ref, lq1_ref, lk1_ref, lq2_ref, lk2_ref, sub_ref,
                     ob_ref, m_ref, l_ref, acc_ref, *, tq, tk, lambda_init):
    S = dq_ref.shape[1]
    nq = S // tq
    R = 2 * tq
    lam = (jnp.exp(jnp.sum(lq1_ref[...] * lk1_ref[...], axis=-1, keepdims=True))
           - jnp.exp(jnp.sum(lq2_ref[...] * lk2_ref[...], axis=-1, keepdims=True))
           + lambda_init)
    first_comp = lax.broadcasted_iota(jnp.int32, (1, DIFF_DV), 1) < DIFF_DH
    sub = sub_ref[...]

    pair = range(ATTN_PAIR)
    for hp in range(DIFF_HEADS // ATTN_PAIR):
        colss = [slice((hp * ATTN_PAIR + t) * DIFF_DV, (hp * ATTN_PAIR + t + 1) * DIFF_DV)
                 for t in pair]
        for qi in range(nq):
            qss = []
            for t in pair:
                qb = dq_ref[0, qi * tq:(qi + 1) * tq, colss[t]]
                zero = jnp.zeros_like(qb)
                qss.append(jnp.concatenate([jnp.where(first_comp, qb, zero),
                                            jnp.where(first_comp, zero, qb)], axis=0))
            m_ref[...] = jnp.full(m_ref.shape, NEG_BIG, F32)
            l_ref[...] = jnp.zeros(l_ref.shape, F32)
            acc_ref[...] = jnp.zeros(acc_ref.shape, F32)

            def scores(kstart, qss=qss, colss=colss):
                return [lax.dot_general(dk_ref[0, kstart:kstart + tk, colss[t]], qss[t], NT_DIMS,
                                        preferred_element_type=F32) for t in pair]

            def update(ss, kstart, masked, colss=colss, qi=qi):
                if masked:
                    kpos = kstart + lax.broadcasted_iota(jnp.int32, (tk, R), 0)
                    qpos = qi * tq + lax.broadcasted_iota(jnp.int32, (tk, R), 1) % tq
                    ss = [jnp.where(qpos >= kpos, s, NEG_BIG) for s in ss]
                m_prevs = [m_ref[t] for t in pair]
                m_news = [jnp.maximum(mp, jnp.max(s, axis=0, keepdims=True))
                          for mp, s in zip(m_prevs, ss)]
                alphas = [jnp.exp2(mp - mn) for mp, mn in zip(m_prevs, m_news)]
                ps = [jnp.exp2(s - mn) for s, mn in zip(ss, m_news)]
                for t in pair:
                    l_ref[t] = alphas[t] * l_ref[t] + jnp.sum(ps[t], axis=0, keepdims=True)
                    m_ref[t] = m_news[t]
                pvs = [lax.dot_general(dv_ref[0, kstart:kstart + tk, colss[t]], ps[t].astype(BF16),
                                       TN_DIMS, preferred_element_type=F32) for t in pair]
                for t in pair:
                    acc_ref[t] = alphas[t] * acc_ref[t] + pvs[t]

            n_blocks = (qi + 1) * tq // tk
            diag_from = qi * tq // tk
            s_next = scores(0)
            for kb_i in range(n_blocks):
                s_cur = s_next
                if kb_i + 1 < n_blocks:
                    s_next = scores((kb_i + 1) * tk)
                update(s_cur, kb_i * tk, kb_i >= diag_from)

            for t in pair:
                o_all = acc_ref[t] / l_ref[t]
                o = o_all[:, :tq] - lam * o_all[:, tq:]
                y = o * lax.rsqrt(jnp.mean(o * o, axis=0, keepdims=True) + EPS) * sub
                y = y * (1.0 - lambda_init)
                ob_ref[0, qi * tq:(qi + 1) * tq, colss[t]] = y.T.astype(BF16)


def _diffattn(dq, dk, dv, lq1, lk1, lq2, lk2, diff_subln, *, B, S, tq, tk, lambda_init):
    seq = pl.BlockSpec((1, S, DIFF_W), lambda b: (b, 0, 0))
    vec = lambda w: pl.BlockSpec((1, w), lambda b: (0, 0))
    r3 = lambda a: a.reshape(B, S, DIFF_W)
    return pl.pallas_call(
        functools.partial(_diffattn_kernel, tq=tq, tk=tk, lambda_init=lambda_init),
        grid=(B,),
        in_specs=[seq, seq, seq, vec(DIFF_DH), vec(DIFF_DH), vec(DIFF_DH), vec(DIFF_DH),
                  pl.BlockSpec((DIFF_DV, 1), lambda b: (0, 0))],
        out_specs=seq,
        out_shape=jax.ShapeDtypeStruct((B, S, DIFF_W), BF16),
        scratch_shapes=[pltpu.VMEM((ATTN_PAIR, 1, 2 * tq), F32), pltpu.VMEM((ATTN_PAIR, 1, 2 * tq), F32),
                        pltpu.VMEM((ATTN_PAIR, DIFF_DV, 2 * tq), F32)],
        compiler_params=_cparams(("parallel",)),
        name="diffattn",
    )(r3(dq), r3(dk), r3(dv), lq1.reshape(1, -1), lk1.reshape(1, -1), lq2.reshape(1, -1),
      lk2.reshape(1, -1), diff_subln.reshape(-1, 1))


ROUTE_W = 8


def _first_index_of(mask, lane):
    return jnp.min(jnp.where(mask, lane, LANES), axis=-1, keepdims=True)


POSTMIX_SPLIT = 2


def _postmix_kernel(x_ref, oa_ref, ob_ref, g_ref, wg_ref, wba_ref, wbb_ref, wo_ref, fn_ref,
                    wr_ref, br_ref, h_ref, n2_ref, route_ref, cnt_ref):
    D = x_ref.shape[1]
    rows_per = x_ref.shape[0] // POSTMIX_SPLIT
    groups = [slice(s * rows_per, (s + 1) * rows_per) for s in range(POSTMIX_SPLIT)]
    dot = functools.partial(jnp.dot, preferred_element_type=F32)
    xs = [x_ref[r, :] for r in groups]
    nbs = [_rms(x, g_ref[...]).astype(BF16) for x in xs]
    y_as = [dot(oa_ref[r, :], wba_ref[...]) for r in groups]
    g_as = [dot(nb, wg_ref[:, :D]) for nb in nbs]
    mergeds = [jax.nn.sigmoid(g) * y for g, y in zip(g_as, y_as)]
    y_bs = [dot(ob_ref[r, :], wbb_ref[...]) for r in groups]
    g_bs = [dot(nb, wg_ref[:, D:]) for nb in nbs]
    mergeds = [m + jax.nn.sigmoid(g) * y for m, g, y in zip(mergeds, g_bs, y_bs)]
    hs = [x + dot(m.astype(BF16), wo_ref[...]) for x, m in zip(xs, mergeds)]
    n2s = [_rms(h, fn_ref[...]) for h in hs]
    n2_his = [n2.astype(BF16) for n2 in n2s]
    n2_los = [(n2 - hi.astype(F32)).astype(BF16) for n2, hi in zip(n2s, n2_his)]
    hi_prods = [dot(hi, wr_ref[...]) for hi in n2_his]
    lgs = [hp[:, :LANES] + hp[:, LANES:] + dot(lo, wr_ref[:, :LANES]) + br_ref[...]
           for hp, lo in zip(hi_prods, n2_los)]
    cnt = None
    for r, h, n2_hi, lg in zip(groups, hs, n2_his, lgs):
        h_ref[r, :] = h
        n2_ref[r, :] = n2_hi
        rec, c = _route_top2(lg)
        route_ref[r, :] = rec
        cnt = c if cnt is None else cnt + c
    cnt_ref[0] = cnt


def _route_top2(lg):
    lane = lax.broadcasted_iota(jnp.int32, lg.shape, 1)
    is_g = lane < N_GROUPS
    g_max = jnp.max(jnp.where(is_g, lg, -jnp.inf), axis=-1, keepdims=True)
    g_exp = jnp.where(is_g, jnp.exp(lg - g_max), 0.0)
    g_prob = g_exp / jnp.sum(g_exp, axis=-1, keepdims=True)
    g_p = jnp.max(g_prob, axis=-1, keepdims=True)
    g_idx = _first_index_of(is_g & (g_prob == g_p), lane)

    e_lo = N_GROUPS + EXPERTS_PER_GROUP * g_idx
    is_e = (lane >= e_lo) & (lane < e_lo + EXPERTS_PER_GROUP)
    e_max = jnp.max(jnp.where(is_e, lg, -jnp.inf), axis=-1, keepdims=True)
    e_exp = jnp.where(is_e, jnp.exp(lg - e_max), 0.0)
    e_prob = e_exp / jnp.sum(e_exp, axis=-1, keepdims=True)
    p1 = jnp.max(jnp.where(is_e, e_prob, -1.0), axis=-1, keepdims=True)
    i1 = _first_index_of(is_e & (e_prob == p1), lane)
    rest = is_e & (lane != i1)
    p2 = jnp.max(jnp.where(rest, e_prob, -1.0), axis=-1, keepdims=True)
    i2 = _first_index_of(rest & (e_prob == p2), lane)
    den = p1 + p2
    w1 = g_p * (p1 / den)
    w2 = g_p * (p2 / den)
    rec = jnp.where(lane == 0, w1, 0.0)
    rec = jnp.where(lane == 1, w2, rec)
    rec = jnp.where(lane == 2, (i1 - N_GROUPS).astype(F32), rec)
    rec = jnp.where(lane == 3, (i2 - N_GROUPS).astype(F32), rec)
    chosen = (lane == i1 - N_GROUPS) | (lane == i2 - N_GROUPS)
    cnt = jnp.sum(chosen.astype(F32), axis=0, keepdims=True)
    return rec[:, :ROUTE_W], cnt


def _postmix(x2, oa, ob, attn_norm, wg, wba, wbb, wo, ffn_norm, wr, br, *, tm):
    T, D = x2.shape
    row = lambda i: (i, 0)
    fixed = lambda i: (0, 0)
    full = lambda a: pl.BlockSpec(a.shape, fixed)
    g = attn_norm.reshape(1, D)
    fn = ffn_norm.reshape(1, D)
    return pl.pallas_call(
        _postmix_kernel,
        grid=(T // tm,),
        in_specs=[pl.BlockSpec((tm, D), row), pl.BlockSpec((tm, GLA_V), row),
                  pl.BlockSpec((tm, DIFF_W), row), full(g), full(wg), full(wba), full(wbb),
                  full(wo), full(fn), full(wr), full(br)],
        out_specs=[pl.BlockSpec((tm, D), row), pl.BlockSpec((tm, D), row),
                   pl.BlockSpec((tm, ROUTE_W), row),
                   pl.BlockSpec((1, 1, LANES), lambda i: (i, 0, 0))],
        out_shape=(jax.ShapeDtypeStruct((T, D), F32), jax.ShapeDtypeStruct((T, D), BF16),
                   jax.ShapeDtypeStruct((T, ROUTE_W), F32),
                   jax.ShapeDtypeStruct((T // tm, 1, LANES), F32)),
        compiler_params=_cparams(("parallel",)),
        name="postmix",
    )(x2, oa, ob, g, wg, wba, wbb, wo, fn, wr, br)


DISPATCH_CHUNK = 8
SLOT_W = 8


def _token_copy(src, src_tok, dst, dst_tok, n_tok, sem):
    return pltpu.make_async_copy(
        src.at[pl.ds(pl.multiple_of(src_tok * SUBLANES, SUBLANES), n_tok * SUBLANES), :],
        dst.at[pl.ds(pl.multiple_of(dst_tok * SUBLANES, SUBLANES), n_tok * SUBLANES), :], sem)


def _for_each_piece(length, fn):
    n_full = lax.shift_right_logical(length, 3)

    def body(c, carry):
        fn(c * DISPATCH_CHUNK, DISPATCH_CHUNK)
        return carry

    lax.fori_loop(0, n_full, body, 0)
    off = n_full * DISPATCH_CHUNK
    for n in (4, 2, 1):
        has = (length & n) != 0

        @pl.when(has)
        def _(off=off, n=n):
            fn(off, n)

        off = off + jnp.where(has, n, 0)


def _dispatch_kernel(seg_ref, lo_ref, cnt_ref, fill_ref, fill_len_ref, n_tiles_ref,
                     n2_ref, route_ref, lo_row_ref,
                     xg_hbm, slot_ref, xs_buf, zero_buf, sem, fill_sem):
    tm = n2_ref.shape[0]
    n_slab = 2 * tm * SUBLANES
    i = pl.program_id(0)
    nt = pl.num_programs(0)
    slot = i % 2

    @pl.when(i == 0)
    def _():
        zero_buf[...] = jnp.zeros_like(zero_buf)
        te = zero_buf.shape[0] // SUBLANES
        for wait in (False, True):
            def fill(e, carry, wait=wait):
                def piece(off, n):
                    cp = _token_copy(zero_buf, 0, xg_hbm, fill_ref[e] + off, n, fill_sem)
                    cp.wait() if wait else cp.start()
                _for_each_piece(fill_len_ref[e], piece)
                return carry
            lax.fori_loop(0, N_EXPERTS, fill, 0)

            def fill_tile(t, carry, wait=wait):
                cp = _token_copy(zero_buf, 0, xg_hbm, t * te, te, fill_sem)
                cp.wait() if wait else cp.start()
                return carry
            lax.fori_loop(n_tiles_ref[0], xg_hbm.shape[0] // zero_buf.shape[0], fill_tile, 0)

    route = route_ref[...]
    lane = lax.broadcasted_iota(jnp.int32, (tm, LANES), 1)
    oh1 = lane == route[:, 2:3].astype(jnp.int32)
    oh2 = lane == route[:, 3:4].astype(jnp.int32)
    both = jnp.where(oh1 | oh2, 1.0, 0.0).astype(BF16)
    earlier = (lax.broadcasted_iota(jnp.int32, (tm, tm), 0)
               > lax.broadcasted_iota(jnp.int32, (tm, tm), 1))
    rank = jnp.dot(jnp.where(earlier, 1.0, 0.0).astype(BF16), both, preferred_element_type=F32)

    def pick(onehot, v):
        return jnp.sum(jnp.where(onehot, v, 0.0), axis=-1, keepdims=True)

    r1 = pick(oh1, rank)
    r2 = pick(oh2, rank)
    q1 = pick(oh1, lo_row_ref[0]) + r1
    q2 = pick(oh2, lo_row_ref[0]) + r2
    rec = jnp.where(lane == 0, q1, jnp.where(lane == 1, q2, 0.0))
    slot_ref[...] = rec[:, :SLOT_W]

    rec_t = rec.T.astype(jnp.int32)
    slot_ids = [g * tm + lax.broadcasted_iota(jnp.int32, (tm, tm), 0) for g in range(2)]
    perms = [jnp.where((sid == rec_t[0:1, :]) | (sid == rec_t[1:2, :]), 1.0, 0.0).astype(BF16)
             for sid in slot_ids]
    xss = [jnp.dot(perm, n2_ref[...], preferred_element_type=F32) for perm in perms]

    def wait_slot(s):
        pltpu.make_async_copy(xg_hbm.at[pl.ds(0, n_slab), :], xs_buf.at[s], sem.at[s]).wait()

    @pl.when(i >= 2)
    def _():
        wait_slot(slot)

    for g in range(2):
        _store_slab(xs_buf.at[slot, pl.ds(g * tm * SUBLANES, tm * SUBLANES), :], xss[g])

    def send(e, carry):
        k = i * N_EXPERTS + e

        def piece(off, n):
            _token_copy(xs_buf.at[slot], lo_ref[k] + off, xg_hbm, seg_ref[k] + off, n,
                        sem.at[slot]).start()

        _for_each_piece(cnt_ref[k], piece)
        return carry

    lax.fori_loop(0, N_EXPERTS, send, 0)

    @pl.when(i == nt - 1)
    def _():
        wait_slot(slot)

        @pl.when(nt >= 2)
        def _():
            wait_slot(1 - slot)


def _dispatch(n2, route, cnt, *, tm, te):
    T, D = n2.shape
    nt = T // tm
    n_tiles_max = 2 * T // te + N_EXPERTS
    c = cnt[:, 0, :N_EXPERTS].astype(jnp.int32)
    count = jnp.sum(c, axis=0)
    padded = (count + te - 1) // te * te
    p_end = jnp.cumsum(padded)
    p_start = p_end - padded
    n_tiles = (p_end[-1] // te).astype(jnp.int32).reshape(1)
    tile_start = jnp.arange(n_tiles_max, dtype=jnp.int32) * te
    tile_e = jnp.sum(tile_start[:, None] >= p_end[None, :], axis=1, dtype=jnp.int32)
    tile_e = jnp.minimum(tile_e, jnp.sum(tile_start[n_tiles[0] - 1] >= p_end, dtype=jnp.int32))
    seg = p_start[None, :] + jnp.cumsum(c, axis=0) - c
    lo = jnp.cumsum(c, axis=1) - c
    lanes = lambda a: jnp.pad(a.astype(F32), ((0, 0), (0, LANES - N_EXPERTS))).reshape(nt, 1, LANES)
    flat = lambda a: a.reshape(-1).astype(jnp.int32)
    row = lambda i, *_: (i, 0)
    per_tile = pl.BlockSpec((1, 1, LANES), lambda i, *_: (i, 0, 0))
    grid_spec = pltpu.PrefetchScalarGridSpec(
        num_scalar_prefetch=6,
        grid=(nt,),
        in_specs=[pl.BlockSpec((tm, D), row), pl.BlockSpec((tm, ROUTE_W), row), per_tile],
        out_specs=[pl.BlockSpec(memory_space=pl.ANY), pl.BlockSpec((tm, SLOT_W), row)],
        scratch_shapes=[pltpu.VMEM((2, 2 * tm * SUBLANES, LANES), F32),
                        pltpu.VMEM((te * SUBLANES, LANES), F32),
                        pltpu.SemaphoreType.DMA((2,)), pltpu.SemaphoreType.DMA(())],
    )
    tables = (flat(seg), flat(lo), flat(c))
    xg, slots = pl.pallas_call(
        _dispatch_kernel,
        grid_spec=grid_spec,
        out_shape=(jax.ShapeDtypeStruct((n_tiles_max * te * SUBLANES, LANES), F32),
                   jax.ShapeDtypeStruct((T, SLOT_W), F32)),
        compiler_params=_cparams(("arbitrary",)),
        name="dispatch",
    )(*tables, flat(p_start + count), flat(padded - count), n_tiles, n2, route, lanes(lo))
    return xg, slots, tables, tile_e, n_tiles


def _experts_kernel(tile_e_ref, n_tiles_ref, xg_ref, wg_ref, wu_ref, wd_ref, y_ref,
                    wg_bf, wu_bf, wd_bf):
    tm = xg_ref.shape[0] // SUBLANES
    i = pl.program_id(0)
    used = i < n_tiles_ref[0]

    @pl.when(used & ((i == 0) | (tile_e_ref[i] != tile_e_ref[jnp.maximum(i - 1, 0)])))
    def _():
        wg_bf[...] = wg_ref[0].astype(BF16)
        wu_bf[...] = wu_ref[0].astype(BF16)
        wd_bf[...] = wd_ref[0].astype(BF16)

    @pl.when(used)
    def _():
        x = _load_slab(xg_ref, tm).astype(BF16)
        g = jnp.dot(x, wg_bf[...], preferred_element_type=F32)
        u = jnp.dot(x, wu_bf[...], preferred_element_type=F32)
        he = (g * jax.nn.sigmoid(g)) * u
        _store_slab(y_ref, jnp.dot(he.astype(BF16), wd_bf[...], preferred_element_type=F32))

    @pl.when(i >= n_tiles_ref[0])
    def _():
        y_ref[...] = jnp.zeros_like(y_ref)


def _experts(tile_e, n_tiles, xg, wg, wu, wd, *, tm):
    D = wg.shape[1]
    n_tiles_max = xg.shape[0] // (tm * SUBLANES)
    slab = lambda f: pl.BlockSpec((tm * SUBLANES, LANES), f)
    grid_spec = pltpu.PrefetchScalarGridSpec(
        num_scalar_prefetch=2,
        grid=(n_tiles_max,),
        in_specs=[slab(lambda i, te, nt: (jnp.minimum(i, nt[0] - 1), 0)),
                  pl.BlockSpec((1, D, D_EXPERT), lambda i, te, nt: (te[i], 0, 0)),
                  pl.BlockSpec((1, D, D_EXPERT), lambda i, te, nt: (te[i], 0, 0)),
                  pl.BlockSpec((1, D_EXPERT, D), lambda i, te, nt: (te[i], 0, 0))],
        out_specs=slab(lambda i, te, nt: (i, 0)),
        scratch_shapes=[pltpu.VMEM((D, D_EXPERT), BF16), pltpu.VMEM((D, D_EXPERT), BF16),
                        pltpu.VMEM((D_EXPERT, D), BF16)],
    )
    return pl.pallas_call(
        _experts_kernel,
        grid_spec=grid_spec,
        out_shape=jax.ShapeDtypeStruct(xg.shape, F32),
        compiler_params=_cparams(("arbitrary",)),
        name="experts",
    )(tile_e, n_tiles, xg, wg, wu, wd)


def _final_kernel(seg_ref, lo_ref, cnt_ref, h_ref, ys_hbm, route_ref, slot_ref, p_ref, wple_ref,
                  pn_ref, wpg_ref, fn_ref, o_ref, ybuf, sem, *, apply_final_norm):
    tm = h_ref.shape[0]
    i = pl.program_id(0)
    nt = pl.num_programs(0)
    slot = i % 2

    def fetch(tile, s):
        def run(e, carry):
            k = tile * N_EXPERTS + e

            def piece(off, n):
                _token_copy(ys_hbm, seg_ref[k] + off, ybuf.at[s], lo_ref[k] + off, n,
                            sem.at[s]).start()

            _for_each_piece(cnt_ref[k], piece)
            return carry

        lax.fori_loop(0, N_EXPERTS, run, 0)

    @pl.when(i == 0)
    def _():
        fetch(0, 0)

    @pl.when(i + 1 < nt)
    def _():
        fetch(i + 1, 1 - slot)

    pltpu.make_async_copy(ys_hbm.at[pl.ds(0, 2 * tm * SUBLANES), :], ybuf.at[slot],
                          sem.at[slot]).wait()
    y_sorted = _load_slab(ybuf.at[slot], 2 * tm).astype(BF16)
    half = tm // 2
    groups = [slice(g * half, (g + 1) * half) for g in range(2)]
    dot = functools.partial(jnp.dot, preferred_element_type=F32)
    col = lax.broadcasted_iota(jnp.int32, (half, 2 * tm), 1)
    combs = []
    for r in groups:
        route = route_ref[r, :]
        slots = slot_ref[r, :].astype(jnp.int32)
        combs.append(jnp.where(col == slots[:, 0:1], route[:, 0:1],
                               jnp.where(col == slots[:, 1:2], route[:, 1:2], 0.0))
                     .astype(BF16))
    es = [_rms(dot(p_ref[0, r, :].astype(BF16), wple_ref[...]), pn_ref[...]) for r in groups]
    hs = [h_ref[r, :] + dot(comb, y_sorted) for r, comb in zip(groups, combs)]
    gates = [jax.nn.sigmoid(dot(h.astype(BF16), wpg_ref[...])) for h in hs]
    for r, h, gate, e in zip(groups, hs, gates, es):
        h = h + gate * e
        if apply_final_norm:
            h = _rms(h, fn_ref[...])
        o_ref[r, :] = h


def _final(h1, ys, slots, tables, route, p_all, layer, wple, ple_norm, wpg, final_norm, *, tm,
           apply_final_norm):
    T, D = h1.shape
    nt = T // tm
    row = lambda i, *_: (i, 0)
    fixed = lambda i, *_: (0, 0)
    full = lambda a: pl.BlockSpec(a.shape, fixed)
    pn = ple_norm.reshape(1, D)
    fn = final_norm.reshape(1, D)
    grid_spec = pltpu.PrefetchScalarGridSpec(
        num_scalar_prefetch=3,
        grid=(nt,),
        in_specs=[pl.BlockSpec((tm, D), row), pl.BlockSpec(memory_space=pl.ANY),
                  pl.BlockSpec((tm, ROUTE_W), row), pl.BlockSpec((tm, SLOT_W), row),
                  pl.BlockSpec((1, tm, p_all.shape[2]), lambda i, *_: (layer, i, 0)),
                  full(wple), full(pn), full(wpg), full(fn)],
        out_specs=pl.BlockSpec((tm, D), row),
        scratch_shapes=[pltpu.VMEM((2, 2 * tm * SUBLANES, LANES), F32),
                        pltpu.SemaphoreType.DMA((2,))],
    )
    return pl.pallas_call(
        functools.partial(_final_kernel, apply_final_norm=apply_final_norm),
        grid_spec=grid_spec,
        out_shape=jax.ShapeDtypeStruct((T, D), F32),
        compiler_params=_cparams(("arbitrary",)),
        name="final",
    )(*tables, h1, ys, route, slots, p_all, wple, pn, wpg, fn)


SPLIT_SIZES = (GLA_QK, GLA_QK, GLA_V, GLA_V, GLA_LOWRANK, DIFF_W, DIFF_W, DIFF_W)


def _layer(h2, p_all, layer, positions, B, S, lambda_init, apply_final_norm, attn_norm, w_in, w_a2, b_a,
           gla_norm, lq1, lk1, lq2, lk2, diff_subln, w_branch_a, w_branch_b, w_out, ffn_norm,
           w_rg, b_rg, w_re, b_re, w_gate, w_up, w_down, w_ple, ple_norm, w_ple_gate, final_norm,
           *, tm, ts, tq, tk, te):
    T, D = h2.shape
    assert D == SUBLANES * LANES, "token-slab gathers need one (8, 128) tile per token"
    n_mix = sum(SPLIT_SIZES)
    wgla = w_in[:, :C_GLA_END].astype(BF16)
    wdiff = w_in[:, C_GLA_END:n_mix].astype(BF16)
    wg = w_in[:, n_mix:].astype(BF16)

    gq, gk, gv, gr, la, dq, dk, dv = _inproj(h2, positions, attn_norm, wgla, wdiff,
                                             w_a2.astype(BF16), b_a, tm=tm)
    oa = _gla(gq, gk, gv, gr, la, gla_norm, B=B, S=S, ts=ts).reshape(T, GLA_V)
    ob = _diffattn(dq, dk, dv, lq1, lk1, lq2, lk2, diff_subln, B=B, S=S, tq=tq, tk=tk,
                   lambda_init=lambda_init).reshape(T, DIFF_W)

    wr = jnp.concatenate([w_rg, w_re, jnp.zeros((D, LANES - N_GROUPS - N_EXPERTS), F32)], axis=1)
    wr_hi = wr.astype(BF16)
    wr = jnp.concatenate([wr_hi, (wr - wr_hi.astype(F32)).astype(BF16)], axis=1)
    br = jnp.concatenate([b_rg, b_re, jnp.zeros((LANES - N_GROUPS - N_EXPERTS,), F32)]).reshape(1, LANES)
    h1, n2, route, cnt = _postmix(h2, oa, ob, attn_norm, wg, w_branch_a.astype(BF16),
                                  w_branch_b.astype(BF16), w_out.astype(BF16), ffn_norm, wr, br,
                                  tm=tm)

    xg, slots, tables, tile_e, n_tiles = _dispatch(n2, route, cnt, tm=tm, te=te)
    ys = _experts(tile_e, n_tiles, xg, w_gate, w_up, w_down, tm=te)
    return _final(h1, ys, slots, tables, route, p_all, layer, w_ple.astype(BF16), ple_norm,
                  w_ple_gate.astype(BF16), final_norm, tm=tm, apply_final_norm=apply_final_norm)


def _block(x, p, positions, attn_norm, w_in, w_a2, b_a, gla_norm, lambda_q1, lambda_k1, lambda_q2,
           lambda_k2, diff_subln, w_branch_a, w_branch_b, w_out, ffn_norm, w_router_group,
           b_router_group, w_router_expert, b_router_expert, w_gate, w_up, w_down, w_ple, ple_norm,
           w_ple_gate, final_norm, *, tm, ts, tq, tk, te):
    B, S, D = x.shape
    depth = w_in.shape[0]
    h = x.reshape(B * S, D)
    for i in range(depth):
        lambda_init = 0.8 - 0.6 * math.exp(-0.3 * i)
        h = _layer(h, p.reshape(depth, B * S, -1), i, positions, B, S, lambda_init, i == depth - 1,
                   attn_norm[i], w_in[i], w_a2[i], b_a[i], gla_norm[i], lambda_q1[i], lambda_k1[i],
                   lambda_q2[i], lambda_k2[i], diff_subln[i], w_branch_a[i], w_branch_b[i], w_out[i],
                   ffn_norm[i], w_router_group[i], b_router_group[i], w_router_expert[i],
                   b_router_expert[i], w_gate[i], w_up[i], w_down[i], w_ple[i], ple_norm[i],
                   w_ple_gate[i], final_norm, tm=tm, ts=ts, tq=tq, tk=tk, te=te)
    return h.reshape(B, S, D)


def kernel(x, p, positions, attn_norm, w_in, w_a2, b_a, gla_norm, lambda_q1, lambda_k1, lambda_q2, lambda_k2, diff_subln, w_branch_a, w_branch_b, w_out, ffn_norm, w_router_group, b_router_group, w_router_expert, b_router_expert, w_gate, w_up, w_down, w_ple, ple_norm, w_ple_gate, final_norm):
    S = x.shape[1]
    return _block(x, p, positions, attn_norm, w_in, w_a2, b_a, gla_norm, lambda_q1, lambda_k1,
                  lambda_q2, lambda_k2, diff_subln, w_branch_a, w_branch_b, w_out, ffn_norm,
                  w_router_group, b_router_group, w_router_expert, b_router_expert, w_gate, w_up,
                  w_down, w_ple, ple_norm, w_ple_gate, final_norm,
                  tm=512, ts=min(512, S), tq=min(512, S), tk=min(512, S), te=512)
```

```python
import functools
import math

import jax
import jax.numpy as jnp
import numpy as np
from jax import lax
from jax.experimental import pallas as pl
from jax.experimental.pallas import tpu as pltpu

EPS = 1e-6

GLA_HEADS = 4
GLA_DK = 64
GLA_DV = 128
GLA_LOWRANK = 16
GLA_TAU = 16.0
GLA_CHUNK = 64
GLA_QK = GLA_HEADS * GLA_DK
GLA_V = GLA_HEADS * GLA_DV

DIFF_HEADS = 4
DIFF_DH = 64
DIFF_DV = 2 * DIFF_DH
DIFF_W = DIFF_HEADS * DIFF_DV
ROPE_THETA = 500000.0
ROPE_DIM = DIFF_DH // 4
ROPE_HALF = ROPE_DIM // 2

N_GROUPS = 4
EXPERTS_PER_GROUP = 8
N_EXPERTS = N_GROUPS * EXPERTS_PER_GROUP
D_EXPERT = 256

LANES = 128
MXU_N = 256
VMEM_LIMIT = 56 * 1024 * 1024

BF16 = jnp.bfloat16
F32 = jnp.float32
NT_DIMS = (((1,), (1,)), ((), ()))
TN_DIMS = (((0,), (0,)), ((), ()))
NEG_BIG = -1e30
LOG2_E = math.log2(math.e)


def _rms(x, g):
    return x * lax.rsqrt(jnp.mean(x * x, axis=-1, keepdims=True) + EPS) * g


def _cparams(semantics):
    return pltpu.CompilerParams(dimension_semantics=semantics, vmem_limit_bytes=VMEM_LIMIT)


SUBLANES = 8


def _store_slab(ref, x):
    n = x.shape[0]
    for j in range(SUBLANES):
        ref[pl.ds(j, n, stride=SUBLANES), :] = x[:, j * LANES:(j + 1) * LANES]


def _load_slab(ref, n):
    return jnp.concatenate([ref[pl.ds(j, n, stride=SUBLANES), :] for j in range(SUBLANES)], axis=1)


C_GQ = 0
C_GK = C_GQ + GLA_QK
C_GV = C_GK + GLA_QK
C_GR = C_GV + GLA_V
C_AL = C_GR + GLA_V
C_GLA_END = C_AL + GLA_LOWRANK
C_DQ = 0
C_DK = C_DQ + DIFF_W
C_DV = C_DK + DIFF_W
C_END = C_DV + DIFF_W


def _inproj_kernel(x_ref, pos_ref, g_ref, wgla_ref, wdiff_ref, wa2_ref, ba_ref, inv_ref, sel_ref,
                   one_ref,
                   gq_ref, gk_ref, gv_ref, gr_ref, la_ref, dq_ref, dk_ref, dv_ref):
    ang = inv_ref[...] * pos_ref[0].astype(F32)
    cs = jnp.concatenate([jnp.cos(ang), jnp.sin(ang)], axis=0)
    tabs = None
    rest = cs
    for _ in range(3):
        piece = rest.astype(BF16)
        rest = rest - piece.astype(F32)
        t = lax.dot_general(piece, sel_ref[...], TN_DIMS, preferred_element_type=F32)
        tabs = t if tabs is None else tabs + t
    cosf = tabs[:, :LANES] + one_ref[...]
    sneg = tabs[:, LANES:2 * LANES]
    spos = tabs[:, 2 * LANES:]

    nb = _rms(x_ref[...], g_ref[...]).astype(BF16)

    def proj_gla(c0, c1):
        return jnp.dot(nb, wgla_ref[:, c0:c1], preferred_element_type=F32)

    def proj(c0, c1):
        return jnp.dot(nb, wdiff_ref[:, c0:c1], preferred_element_type=F32)

    gq_ref[...] = proj_gla(C_GQ, C_GK) * (GLA_DK ** -0.5)
    gk_ref[...] = proj_gla(C_GK, C_GV)
    gv_ref[...] = proj_gla(C_GV, C_GR).astype(BF16)
    gr_ref[...] = proj_gla(C_GR, C_AL).astype(BF16)

    gal = proj_gla(C_AL, C_GLA_END).astype(BF16)
    a_logit = jnp.dot(gal, wa2_ref[...], preferred_element_type=F32) + ba_ref[...]
    la_ref[...] = (jnp.minimum(a_logit, 0.0) - jnp.log1p(jnp.exp(-jnp.abs(a_logit)))) / GLA_TAU

    def rope_store(c0, out_ref, scale):
        for j in range(DIFF_W // MXU_N):
            pair = proj(c0 + j * MXU_N, c0 + (j + 1) * MXU_N)
            for i in range(MXU_N // LANES):
                blk = pair[:, i * LANES:(i + 1) * LANES]
                rot = (blk * cosf + pltpu.roll(blk, LANES - ROPE_HALF, 1) * sneg
                       + pltpu.roll(blk, ROPE_HALF, 1) * spos)
                c = j * MXU_N + i * LANES
                out_ref[:, c:c + LANES] = (rot * scale).astype(BF16)

    rope_store(C_DQ, dq_ref, DIFF_DH ** -0.5 * LOG2_E)
    rope_store(C_DK, dk_ref, 1.0)
    dv_ref[...] = proj(C_DV, C_END).astype(BF16)


def _rope_tables():
    lane = np.arange(LANES)
    r = lane % DIFF_DH
    selc = np.zeros((2 * ROPE_HALF, LANES), np.float32)
    seln = np.zeros_like(selc)
    selp = np.zeros_like(selc)
    one = np.zeros((1, LANES), np.float32)
    for l in range(LANES):
        if r[l] < ROPE_DIM:
            selc[r[l] % ROPE_HALF, l] = 1.0
            if r[l] < ROPE_HALF:
                seln[ROPE_HALF + r[l], l] = -1.0
            else:
                selp[ROPE_HALF + r[l] - ROPE_HALF, l] = 1.0
        else:
            one[0, l] = 1.0
    sel = np.concatenate([selc, seln, selp], axis=1)
    return jnp.asarray(sel, dtype=BF16), jnp.asarray(one)


def _inproj(x2, positions, attn_norm, wgla, wdiff, wa2, b_a, *, tm):
    T, D = x2.shape
    nt = T // tm
    pos3 = positions.reshape(nt, 1, tm)
    inv = (ROPE_THETA ** (-jnp.arange(0, ROPE_DIM, 2, dtype=F32) / ROPE_DIM)).reshape(ROPE_HALF, 1)
    sel, one = _rope_tables()
    row = lambda i: (i, 0)
    fixed = lambda i: (0, 0)
    out_shapes = (
        jax.ShapeDtypeStruct((T, GLA_QK), F32), jax.ShapeDtypeStruct((T, GLA_QK), F32),
        jax.ShapeDtypeStruct((T, GLA_V), BF16), jax.ShapeDtypeStruct((T, GLA_V), BF16),
        jax.ShapeDtypeStruct((T, GLA_QK), F32),
        jax.ShapeDtypeStruct((T, DIFF_W), BF16), jax.ShapeDtypeStruct((T, DIFF_W), BF16),
        jax.ShapeDtypeStruct((T, DIFF_W), BF16),
    )
    return pl.pallas_call(
        _inproj_kernel,
        grid=(nt,),
        in_specs=[
            pl.BlockSpec((tm, D), row),
            pl.BlockSpec((1, 1, tm), lambda i: (i, 0, 0)),
            pl.BlockSpec((1, D), fixed),
            pl.BlockSpec((D, C_GLA_END), fixed),
            pl.BlockSpec((D, C_END), fixed),
            pl.BlockSpec((GLA_LOWRANK, GLA_QK), fixed),
            pl.BlockSpec((1, GLA_QK), fixed),
            pl.BlockSpec((ROPE_HALF, 1), fixed),
            pl.BlockSpec((2 * ROPE_HALF, 3 * LANES), fixed),
            pl.BlockSpec((1, LANES), fixed),
        ],
        out_specs=[pl.BlockSpec((tm, s.shape[1]), row) for s in out_shapes],
        out_shape=out_shapes,
        compiler_params=_cparams(("parallel",)),
        name="inproj",
    )(x2, pos3, attn_norm.reshape(1, D), wgla, wdiff, wa2, b_a.reshape(1, GLA_QK), inv, sel, one)


def _gla_kernel(gq_ref, gk_ref, gv_ref, gr_ref, la_ref, gn_ref, oa_ref, st_ref, *, n_chunks):
    @pl.when(pl.program_id(1) == 0)
    def _():
        st_ref[...] = jnp.zeros_like(st_ref)

    C = GLA_CHUNK
    tril = lax.broadcasted_iota(jnp.int32, (C, C), 0) >= lax.broadcasted_iota(jnp.int32, (C, C), 1)
    trilb = jnp.where(tril, 1.0, 0.0).astype(BF16)
    head_of_lane = lax.broadcasted_iota(jnp.int32, (1, GLA_QK), 1) // GLA_DK
    hmask = [head_of_lane == h for h in range(GLA_HEADS)]
    gn = gn_ref[...]

    def by_head(x):
        return jnp.concatenate([jnp.where(hmask[h], x, jnp.zeros_like(x))
                                for h in range(GLA_HEADS)], axis=0)

    chunks = range(n_chunks)
    sls = [pl.ds(c * C, C) for c in chunks]
    vs = [gv_ref[0, sl, :] for sl in sls]

    def cumsum(la):
        total, rest = None, la
        for _ in range(3):
            piece = rest.astype(BF16)
            rest = rest - piece.astype(F32)
            t = jnp.dot(trilb, piece, preferred_element_type=F32)
            total = t if total is None else total + t
        return total

    bs = [cumsum(la_ref[0, sl, :]) for sl in sls]
    b_lasts = [b[C - 1:C, :] for b in bs]
    q_heads = [by_head((gq_ref[0, sl, :] * jnp.exp(b)).astype(BF16)) for sl, b in zip(sls, bs)]
    k_invs = [(gk_ref[0, sl, :] * jnp.exp(-b)).astype(BF16) for sl, b in zip(sls, bs)]
    k_ends = [by_head((gk_ref[0, sl, :] * jnp.exp(bl - b)).astype(BF16))
              for sl, b, bl in zip(sls, bs, b_lasts)]
    decays = [jnp.exp(bl) for bl in b_lasts]
    atts = [lax.dot_general(qh, ki, NT_DIMS, preferred_element_type=F32)
            for qh, ki in zip(q_heads, k_invs)]
    v_heads = [jnp.concatenate([v[:, h * GLA_DV:(h + 1) * GLA_DV] for h in range(GLA_HEADS)], axis=0)
               for v in vs]
    us = [lax.dot_general(vh, ke, TN_DIMS, preferred_element_type=F32)
          for vh, ke in zip(v_heads, k_ends)]

    st = st_ref[...]
    states = []
    for c in chunks:
        states.append(st.astype(BF16))
        st = st * decays[c] + us[c]
    st_ref[...] = st

    inters = [lax.dot_general(qh, s_in, NT_DIMS, preferred_element_type=F32)
              for qh, s_in in zip(q_heads, states)]
    for c in chunks:
        for h in range(GLA_HEADS):
            rows = slice(h * C, (h + 1) * C)
            cols = slice(h * GLA_DV, (h + 1) * GLA_DV)
            a_h = jnp.where(tril, atts[c][rows], 0.0).astype(BF16)
            o_h = jnp.dot(a_h, vs[c][:, cols], preferred_element_type=F32) + inters[c][rows]
            y = _rms(o_h, gn)
            r = gr_ref[0, sls[c], cols].astype(F32)
            oa_ref[0, sls[c], cols] = (y * (r * jax.nn.sigmoid(r))).astype(BF16)


def _gla(gq, gk, gv, gr, la, gla_norm, *, B, S, ts):
    n_chunks = ts // GLA_CHUNK
    blk = lambda w: pl.BlockSpec((1, ts, w), lambda b, j: (b, j, 0))
    r3 = lambda a: a.reshape(B, S, a.shape[-1])
    return pl.pallas_call(
        functools.partial(_gla_kernel, n_chunks=n_chunks),
        grid=(B, S // ts),
        in_specs=[blk(GLA_QK), blk(GLA_QK), blk(GLA_V), blk(GLA_V), blk(GLA_QK),
                  pl.BlockSpec((1, GLA_DV), lambda b, j: (0, 0))],
        out_specs=blk(GLA_V),
        out_shape=jax.ShapeDtypeStruct((B, S, GLA_V), BF16),
        scratch_shapes=[pltpu.VMEM((GLA_DV, GLA_QK), F32)],
        compiler_params=_cparams(("parallel", "arbitrary")),
        name="gla",
    )(r3(gq), r3(gk), r3(gv), r3(gr), r3(la), gla_norm.reshape(1, GLA_DV))


ATTN_PAIR = 4


def _diffattn_kernel(dq_ref, dk_ref, dv_ref, lq1_ref, lk1_ref, lq2_ref, lk2_ref, sub_ref,
                     ob_ref, m_ref, l_ref, acc_ref, *, tq, tk, lambda_init):
    S = dq_ref.shape[1]
    nq = S // tq
    R = 2 * tq
    lam = (jnp.exp(jnp.sum(lq1_ref[...] * lk1_ref[...], axis=-1, keepdims=True))
           - jnp.exp(jnp.sum(lq2_ref[...] * lk2_ref[...], axis=-1, keepdims=True))
           + lambda_init)
    first_comp = lax.broadcasted_iota(jnp.int32, (1, DIFF_DV), 1) < DIFF_DH
    sub = sub_ref[...]

    pair = range(ATTN_PAIR)
    for hp in range(DIFF_HEADS // ATTN_PAIR):
        colss = [slice((hp * ATTN_PAIR + t) * DIFF_DV, (hp * ATTN_PAIR + t + 1) * DIFF_DV)
                 for t in pair]
        for qi in range(nq):
            qss = []
            for t in pair:
                qb = dq_ref[0, qi * tq:(qi + 1) * tq, colss[t]]
                zero = jnp.zeros_like(qb)
                qss.append(jnp.concatenate([jnp.where(first_comp, qb, zero),
                                            jnp.where(first_comp, zero, qb)], axis=0))
            m_ref[...] = jnp.full(m_ref.shape, NEG_BIG, F32)
            l_ref[...] = jnp.zeros(l_ref.shape, F32)
            acc_ref[...] = jnp.zeros(acc_ref.shape, F32)

            def scores(kstart, qss=qss, colss=colss):
                return [lax.dot_general(dk_ref[0, kstart:kstart + tk, colss[t]], qss[t], NT_DIMS,
                                        preferred_element_type=F32) for t in pair]

            def update(ss, kstart, masked, colss=colss, qi=qi):
                if masked:
                    kpos = kstart + lax.broadcasted_iota(jnp.int32, (tk, R), 0)
                    qpos = qi * tq + lax.broadcasted_iota(jnp.int32, (tk, R), 1) % tq
                    ss = [jnp.where(qpos >= kpos, s, NEG_BIG) for s in ss]
                m_prevs = [m_ref[t] for t in pair]
                m_news = [jnp.maximum(mp, jnp.max(s, axis=0, keepdims=True))
                          for mp, s in zip(m_prevs, ss)]
                alphas = [jnp.exp2(mp - mn) for mp, mn in zip(m_prevs, m_news)]
                ps = [jnp.exp2(s - mn) for s, mn in zip(ss, m_news)]
                for t in pair:
                    l_ref[t] = alphas[t] * l_ref[t] + jnp.sum(ps[t], axis=0, keepdims=True)
                    m_ref[t] = m_news[t]
                pvs = [lax.dot_general(dv_ref[0, kstart:kstart + tk, colss[t]], ps[t].astype(BF16),
                                       TN_DIMS, preferred_element_type=F32) for t in pair]
                for t in pair:
                    acc_ref[t] = alphas[t] * acc_ref[t] + pvs[t]

            n_blocks = (qi + 1) * tq // tk
            diag_from = qi * tq // tk
            s_next = scores(0)
            for kb_i in range(n_blocks):
                s_cur = s_next
                if kb_i + 1 < n_blocks:
                    s_next = scores((kb_i + 1) * tk)
                update(s_cur, kb_i * tk, kb_i >= diag_from)

            for t in pair:
                o_all = acc_ref[t] / l_ref[t]
                o = o_all[:, :tq] - lam * o_all[:, tq:]
                y = o * lax.rsqrt(jnp.mean(o * o, axis=0, keepdims=True) + EPS) * sub
                y = y * (1.0 - lambda_init)
                ob_ref[0, qi * tq:(qi + 1) * tq, colss[t]] = y.T.astype(BF16)


def _diffattn(dq, dk, dv, lq1, lk1, lq2, lk2, diff_subln, *, B, S, tq, tk, lambda_init):
    seq = pl.BlockSpec((1, S, DIFF_W), lambda b: (b, 0, 0))
    vec = lambda w: pl.BlockSpec((1, w), lambda b: (0, 0))
    r3 = lambda a: a.reshape(B, S, DIFF_W)
    return pl.pallas_call(
        functools.partial(_diffattn_kernel, tq=tq, tk=tk, lambda_init=lambda_init),
        grid=(B,),
        in_specs=[seq, seq, seq, vec(DIFF_DH), vec(DIFF_DH), vec(DIFF_DH), vec(DIFF_DH),
                  pl.BlockSpec((DIFF_DV, 1), lambda b: (0, 0))],
        out_specs=seq,
        out_shape=jax.ShapeDtypeStruct((B, S, DIFF_W), BF16),
        scratch_shapes=[pltpu.VMEM((ATTN_PAIR, 1, 2 * tq), F32), pltpu.VMEM((ATTN_PAIR, 1, 2 * tq), F32),
                        pltpu.VMEM((ATTN_PAIR, DIFF_DV, 2 * tq), F32)],
        compiler_params=_cparams(("parallel",)),
        name="diffattn",
    )(r3(dq), r3(dk), r3(dv), lq1.reshape(1, -1), lk1.reshape(1, -1), lq2.reshape(1, -1),
      lk2.reshape(1, -1), diff_subln.reshape(-1, 1))


ROUTE_W = 8


def _first_index_of(mask, lane):
    return jnp.min(jnp.where(mask, lane, LANES), axis=-1, keepdims=True)


POSTMIX_SPLIT = 2


def _postmix_kernel(x_ref, oa_ref, ob_ref, g_ref, wg_ref, wba_ref, wbb_ref, wo_ref, fn_ref,
                    wr_ref, br_ref, h_ref, n2_ref, route_ref, cnt_ref):
    D = x_ref.shape[1]
    rows_per = x_ref.shape[0] // POSTMIX_SPLIT
    groups = [slice(s * rows_per, (s + 1) * rows_per) for s in range(POSTMIX_SPLIT)]
    dot = functools.partial(jnp.dot, preferred_element_type=F32)
    xs = [x_ref[r, :] for r in groups]
    nbs = [_rms(x, g_ref[...]).astype(BF16) for x in xs]
    y_as = [dot(oa_ref[r, :], wba_ref[...]) for r in groups]
    g_as = [dot(nb, wg_ref[:, :D]) for nb in nbs]
    mergeds = [jax.nn.sigmoid(g) * y for g, y in zip(g_as, y_as)]
    y_bs = [dot(ob_ref[r, :], wbb_ref[...]) for r in groups]
    g_bs = [dot(nb, wg_ref[:, D:]) for nb in nbs]
    mergeds = [m + jax.nn.sigmoid(g) * y for m, g, y in zip(mergeds, g_bs, y_bs)]
    hs = [x + dot(m.astype(BF16), wo_ref[...]) for x, m in zip(xs, mergeds)]
    n2s = [_rms(h, fn_ref[...]) for h in hs]
    n2_his = [n2.astype(BF16) for n2 in n2s]
    n2_los = [(n2 - hi.astype(F32)).astype(BF16) for n2, hi in zip(n2s, n2_his)]
    hi_prods = [dot(hi, wr_ref[...]) for hi in n2_his]
    lgs = [hp[:, :LANES] + hp[:, LANES:] + dot(lo, wr_ref[:, :LANES]) + br_ref[...]
           for hp, lo in zip(hi_prods, n2_los)]
    cnt = None
    for r, h, n2_hi, lg in zip(groups, hs, n2_his, lgs):
        h_ref[r, :] = h
        n2_ref[r, :] = n2_hi
        rec, c = _route_top2(lg)
        route_ref[r, :] = rec
        cnt = c if cnt is None else cnt + c
    cnt_ref[0] = cnt


def _route_top2(lg):
    lane = lax.broadcasted_iota(jnp.int32, lg.shape, 1)
    is_g = lane < N_GROUPS
    g_max = jnp.max(jnp.where(is_g, lg, -jnp.inf), axis=-1, keepdims=True)
    g_exp = jnp.where(is_g, jnp.exp(lg - g_max), 0.0)
    g_prob = g_exp / jnp.sum(g_exp, axis=-1, keepdims=True)
    g_p = jnp.max(g_prob, axis=-1, keepdims=True)
    g_idx = _first_index_of(is_g & (g_prob == g_p), lane)

    e_lo = N_GROUPS + EXPERTS_PER_GROUP * g_idx
    is_e = (lane >= e_lo) & (lane < e_lo + EXPERTS_PER_GROUP)
    e_max = jnp.max(jnp.where(is_e, lg, -jnp.inf), axis=-1, keepdims=True)
    e_exp = jnp.where(is_e, jnp.exp(lg - e_max), 0.0)
    e_prob = e_exp / jnp.sum(e_exp, axis=-1, keepdims=True)
    p1 = jnp.max(jnp.where(is_e, e_prob, -1.0), axis=-1, keepdims=True)
    i1 = _first_index_of(is_e & (e_prob == p1), lane)
    rest = is_e & (lane != i1)
    p2 = jnp.max(jnp.where(rest, e_prob, -1.0), axis=-1, keepdims=True)
    i2 = _first_index_of(rest & (e_prob == p2), lane)
    den = p1 + p2
    w1 = g_p * (p1 / den)
    w2 = g_p * (p2 / den)
    rec = jnp.where(lane == 0, w1, 0.0)
    rec = jnp.where(lane == 1, w2, rec)
    rec = jnp.where(lane == 2, (i1 - N_GROUPS).astype(F32), rec)
    rec = jnp.where(lane == 3, (i2 - N_GROUPS).astype(F32), rec)
    chosen = (lane == i1 - N_GROUPS) | (lane == i2 - N_GROUPS)
    cnt = jnp.sum(chosen.astype(F32), axis=0, keepdims=True)
    return rec[:, :ROUTE_W], cnt


def _postmix(x2, oa, ob, attn_norm, wg, wba, wbb, wo, ffn_norm, wr, br, *, tm):
    T, D = x2.shape
    row = lambda i: (i, 0)
    fixed = lambda i: (0, 0)
    full = lambda a: pl.BlockSpec(a.shape, fixed)
    g = attn_norm.reshape(1, D)
    fn = ffn_norm.reshape(1, D)
    return pl.pallas_call(
        _postmix_kernel,
        grid=(T // tm,),
        in_specs=[pl.BlockSpec((tm, D), row), pl.BlockSpec((tm, GLA_V), row),
                  pl.BlockSpec((tm, DIFF_W), row), full(g), full(wg), full(wba), full(wbb),
                  full(wo), full(fn), full(wr), full(br)],
        out_specs=[pl.BlockSpec((tm, D), row), pl.BlockSpec((tm, D), row),
                   pl.BlockSpec((tm, ROUTE_W), row),
                   pl.BlockSpec((1, 1, LANES), lambda i: (i, 0, 0))],
        out_shape=(jax.ShapeDtypeStruct((T, D), F32), jax.ShapeDtypeStruct((T, D), BF16),
                   jax.ShapeDtypeStruct((T, ROUTE_W), F32),
                   jax.ShapeDtypeStruct((T // tm, 1, LANES), F32)),
        compiler_params=_cparams(("parallel",)),
        name="postmix",
    )(x2, oa, ob, g, wg, wba, wbb, wo, fn, wr, br)


DISPATCH_CHUNK = 16
DISPATCH_TAIL = (8, 4, 2, 1)
SLOT_W = 8


def _token_copy(src, src_tok, dst, dst_tok, n_tok, sem):
    return pltpu.make_async_copy(
        src.at[pl.ds(pl.multiple_of(src_tok * SUBLANES, SUBLANES), n_tok * SUBLANES), :],
        dst.at[pl.ds(pl.multiple_of(dst_tok * SUBLANES, SUBLANES), n_tok * SUBLANES), :], sem)


def _for_each_piece(length, fn):
    n_full = lax.shift_right_logical(length, DISPATCH_CHUNK.bit_length() - 1)

    def body(c, carry):
        fn(c * DISPATCH_CHUNK, DISPATCH_CHUNK)
        return carry

    lax.fori_loop(0, n_full, body, 0)
    off = n_full * DISPATCH_CHUNK
    for n in DISPATCH_TAIL:
        has = (length & n) != 0

        @pl.when(has)
        def _(off=off, n=n):
            fn(off, n)

        off = off + jnp.where(has, n, 0)


def _dispatch_kernel(seg_ref, lo_ref, cnt_ref, fill_ref, fill_len_ref, n_tiles_ref,
                     n2_ref, route_ref, lo_row_ref,
                     xg_hbm, slot_ref, xs_buf, zero_buf, sem, fill_sem):
    tm = n2_ref.shape[0]
    n_slab = 2 * tm * SUBLANES
    i = pl.program_id(0)
    nt = pl.num_programs(0)
    slot = i % 2

    @pl.when(i == 0)
    def _():
        zero_buf[...] = jnp.zeros_like(zero_buf)
        te = zero_buf.shape[0] // SUBLANES
        for wait in (False, True):
            def fill(e, carry, wait=wait):
                def piece(off, n):
                    cp = _token_copy(zero_buf, 0, xg_hbm, fill_ref[e] + off, n, fill_sem)
                    cp.wait() if wait else cp.start()
                _for_each_piece(fill_len_ref[e], piece)
                return carry
            lax.fori_loop(0, N_EXPERTS, fill, 0)

            def fill_tile(t, carry, wait=wait):
                cp = _token_copy(zero_buf, 0, xg_hbm, t * te, te, fill_sem)
                cp.wait() if wait else cp.start()
                return carry
            lax.fori_loop(n_tiles_ref[0], xg_hbm.shape[0] // zero_buf.shape[0], fill_tile, 0)

    route = route_ref[...]
    lane = lax.broadcasted_iota(jnp.int32, (tm, LANES), 1)
    oh1 = lane == route[:, 2:3].astype(jnp.int32)
    oh2 = lane == route[:, 3:4].astype(jnp.int32)
    both = jnp.where(oh1 | oh2, 1.0, 0.0).astype(BF16)
    earlier = (lax.broadcasted_iota(jnp.int32, (tm, tm), 0)
               > lax.broadcasted_iota(jnp.int32, (tm, tm), 1))
    rank = jnp.dot(jnp.where(earlier, 1.0, 0.0).astype(BF16), both, preferred_element_type=F32)

    def pick(onehot, v):
        return jnp.sum(jnp.where(onehot, v, 0.0), axis=-1, keepdims=True)

    r1 = pick(oh1, rank)
    r2 = pick(oh2, rank)
    q1 = pick(oh1, lo_row_ref[0]) + r1
    q2 = pick(oh2, lo_row_ref[0]) + r2
    rec = jnp.where(lane == 0, q1, jnp.where(lane == 1, q2, 0.0))
    slot_ref[...] = rec[:, :SLOT_W]

    rec_t = rec.T.astype(jnp.int32)
    slot_id = lax.broadcasted_iota(jnp.int32, (2 * tm, tm), 0)
    perm = jnp.where((slot_id == rec_t[0:1, :]) | (slot_id == rec_t[1:2, :]), 1.0, 0.0)
    xs = jnp.dot(perm.astype(BF16), n2_ref[...], preferred_element_type=F32)

    def wait_slot(s):
        pltpu.make_async_copy(xg_hbm.at[pl.ds(0, n_slab), :], xs_buf.at[s], sem.at[s]).wait()

    @pl.when(i >= 2)
    def _():
        wait_slot(slot)

    _store_slab(xs_buf.at[slot], xs)

    def send(e, carry):
        k = i * N_EXPERTS + e

        def piece(off, n):
            _token_copy(xs_buf.at[slot], lo_ref[k] + off, xg_hbm, seg_ref[k] + off, n,
                        sem.at[slot]).start()

        _for_each_piece(cnt_ref[k], piece)
        return carry

    lax.fori_loop(0, N_EXPERTS, send, 0)

    @pl.when(i == nt - 1)
    def _():
        wait_slot(slot)

        @pl.when(nt >= 2)
        def _():
            wait_slot(1 - slot)


def _dispatch(n2, route, cnt, *, tm, te):
    T, D = n2.shape
    nt = T // tm
    n_tiles_max = 2 * T // te + N_EXPERTS
    c = cnt[:, 0, :N_EXPERTS].astype(jnp.int32)
    count = jnp.sum(c, axis=0)
    padded = (count + te - 1) // te * te
    p_end = jnp.cumsum(padded)
    p_start = p_end - padded
    n_tiles = (p_end[-1] // te).astype(jnp.int32).reshape(1)
    tile_start = jnp.arange(n_tiles_max, dtype=jnp.int32) * te
    tile_e = jnp.sum(tile_start[:, None] >= p_end[None, :], axis=1, dtype=jnp.int32)
    tile_e = jnp.minimum(tile_e, jnp.sum(tile_start[n_tiles[0] - 1] >= p_end, dtype=jnp.int32))
    seg = p_start[None, :] + jnp.cumsum(c, axis=0) - c
    lo = jnp.cumsum(c, axis=1) - c
    lanes = lambda a: jnp.pad(a.astype(F32), ((0, 0), (0, LANES - N_EXPERTS))).reshape(nt, 1, LANES)
    flat = lambda a: a.reshape(-1).astype(jnp.int32)
    row = lambda i, *_: (i, 0)
    per_tile = pl.BlockSpec((1, 1, LANES), lambda i, *_: (i, 0, 0))
    grid_spec = pltpu.PrefetchScalarGridSpec(
        num_scalar_prefetch=6,
        grid=(nt,),
        in_specs=[pl.BlockSpec((tm, D), row), pl.BlockSpec((tm, ROUTE_W), row), per_tile],
        out_specs=[pl.BlockSpec(memory_space=pl.ANY), pl.BlockSpec((tm, SLOT_W), row)],
        scratch_shapes=[pltpu.VMEM((2, 2 * tm * SUBLANES, LANES), F32),
                        pltpu.VMEM((te * SUBLANES, LANES), F32),
                        pltpu.SemaphoreType.DMA((2,)), pltpu.SemaphoreType.DMA(())],
    )
    tables = (flat(seg), flat(lo), flat(c))
    xg, slots = pl.pallas_call(
        _dispatch_kernel,
        grid_spec=grid_spec,
        out_shape=(jax.ShapeDtypeStruct((n_tiles_max * te * SUBLANES, LANES), F32),
                   jax.ShapeDtypeStruct((T, SLOT_W), F32)),
        compiler_params=_cparams(("arbitrary",)),
        name="dispatch",
    )(*tables, flat(p_start + count), flat(padded - count), n_tiles, n2, route, lanes(lo))
    return xg, slots, tables, tile_e, n_tiles


def _experts_kernel(tile_e_ref, n_tiles_ref, xg_ref, wg_ref, wu_ref, wd_ref, y_ref,
                    wg_bf, wu_bf, wd_bf):
    tm = xg_ref.shape[0] // SUBLANES
    i = pl.program_id(0)
    used = i < n_tiles_ref[0]

    @pl.when(used & ((i == 0) | (tile_e_ref[i] != tile_e_ref[jnp.maximum(i - 1, 0)])))
    def _():
        wg_bf[...] = wg_ref[0].astype(BF16)
        wu_bf[...] = wu_ref[0].astype(BF16)
        wd_bf[...] = wd_ref[0].astype(BF16)

    @pl.when(used)
    def _():
        x = _load_slab(xg_ref, tm).astype(BF16)
        g = jnp.dot(x, wg_bf[...], preferred_element_type=F32)
        u = jnp.dot(x, wu_bf[...], preferred_element_type=F32)
        he = (g * jax.nn.sigmoid(g)) * u
        _store_slab(y_ref, jnp.dot(he.astype(BF16), wd_bf[...], preferred_element_type=F32))

    @pl.when(i >= n_tiles_ref[0])
    def _():
        y_ref[...] = jnp.zeros_like(y_ref)


def _experts(tile_e, n_tiles, xg, wg, wu, wd, *, tm):
    D = wg.shape[1]
    n_tiles_max = xg.shape[0] // (tm * SUBLANES)
    slab = lambda f: pl.BlockSpec((tm * SUBLANES, LANES), f)
    grid_spec = pltpu.PrefetchScalarGridSpec(
        num_scalar_prefetch=2,
        grid=(n_tiles_max,),
        in_specs=[slab(lambda i, te, nt: (jnp.minimum(i, nt[0] - 1), 0)),
                  pl.BlockSpec((1, D, D_EXPERT), lambda i, te, nt: (te[i], 0, 0)),
                  pl.BlockSpec((1, D, D_EXPERT), lambda i, te, nt: (te[i], 0, 0)),
                  pl.BlockSpec((1, D_EXPERT, D), lambda i, te, nt: (te[i], 0, 0))],
        out_specs=slab(lambda i, te, nt: (i, 0)),
        scratch_shapes=[pltpu.VMEM((D, D_EXPERT), BF16), pltpu.VMEM((D, D_EXPERT), BF16),
                        pltpu.VMEM((D_EXPERT, D), BF16)],
    )
    return pl.pallas_call(
        _experts_kernel,
        grid_spec=grid_spec,
        out_shape=jax.ShapeDtypeStruct(xg.shape, F32),
        compiler_params=_cparams(("arbitrary",)),
        name="experts",
    )(tile_e, n_tiles, xg, wg, wu, wd)


def _final_kernel(seg_ref, lo_ref, cnt_ref, h_ref, ys_hbm, route_ref, slot_ref, p_ref, wple_ref,
                  pn_ref, wpg_ref, fn_ref, o_ref, ybuf, sem, *, apply_final_norm):
    tm = h_ref.shape[0]
    i = pl.program_id(0)
    nt = pl.num_programs(0)
    slot = i % 2

    def fetch(tile, s):
        def run(e, carry):
            k = tile * N_EXPERTS + e

            def piece(off, n):
                _token_copy(ys_hbm, seg_ref[k] + off, ybuf.at[s], lo_ref[k] + off, n,
                            sem.at[s]).start()

            _for_each_piece(cnt_ref[k], piece)
            return carry

        lax.fori_loop(0, N_EXPERTS, run, 0)

    @pl.when(i == 0)
    def _():
        fetch(0, 0)

    @pl.when(i + 1 < nt)
    def _():
        fetch(i + 1, 1 - slot)

    pltpu.make_async_copy(ys_hbm.at[pl.ds(0, 2 * tm * SUBLANES), :], ybuf.at[slot],
                          sem.at[slot]).wait()
    y_sorted = _load_slab(ybuf.at[slot], 2 * tm).astype(BF16)
    route = route_ref[...]
    slots = slot_ref[...].astype(jnp.int32)
    col = lax.broadcasted_iota(jnp.int32, (tm, 2 * tm), 1)
    comb = jnp.where(col == slots[:, 0:1], route[:, 0:1],
                     jnp.where(col == slots[:, 1:2], route[:, 1:2], 0.0))
    h = h_ref[...] + jnp.dot(comb.astype(BF16), y_sorted, preferred_element_type=F32)
    e = _rms(jnp.dot(p_ref[0].astype(BF16), wple_ref[...], preferred_element_type=F32), pn_ref[...])
    gate = jax.nn.sigmoid(jnp.dot(h.astype(BF16), wpg_ref[...], preferred_element_type=F32))
    h = h + gate * e
    if apply_final_norm:
        h = _rms(h, fn_ref[...])
    o_ref[...] = h


def _final(h1, ys, slots, tables, route, p_all, layer, wple, ple_norm, wpg, final_norm, *, tm,
           apply_final_norm):
    T, D = h1.shape
    nt = T // tm
    row = lambda i, *_: (i, 0)
    fixed = lambda i, *_: (0, 0)
    full = lambda a: pl.BlockSpec(a.shape, fixed)
    pn = ple_norm.reshape(1, D)
    fn = final_norm.reshape(1, D)
    grid_spec = pltpu.PrefetchScalarGridSpec(
        num_scalar_prefetch=3,
        grid=(nt,),
        in_specs=[pl.BlockSpec((tm, D), row), pl.BlockSpec(memory_space=pl.ANY),
                  pl.BlockSpec((tm, ROUTE_W), row), pl.BlockSpec((tm, SLOT_W), row),
                  pl.BlockSpec((1, tm, p_all.shape[2]), lambda i, *_: (layer, i, 0)),
                  full(wple), full(pn), full(wpg), full(fn)],
        out_specs=pl.BlockSpec((tm, D), row),
        scratch_shapes=[pltpu.VMEM((2, 2 * tm * SUBLANES, LANES), F32),
                        pltpu.SemaphoreType.DMA((2,))],
    )
    return pl.pallas_call(
        functools.partial(_final_kernel, apply_final_norm=apply_final_norm),
        grid_spec=grid_spec,
        out_shape=jax.ShapeDtypeStruct((T, D), F32),
        compiler_params=_cparams(("arbitrary",)),
        name="final",
    )(*tables, h1, ys, route, slots, p_all, wple, pn, wpg, fn)


SPLIT_SIZES = (GLA_QK, GLA_QK, GLA_V, GLA_V, GLA_LOWRANK, DIFF_W, DIFF_W, DIFF_W)


def _layer(h2, p_all, layer, positions, B, S, lambda_init, apply_final_norm, attn_norm, w_in, w_a2, b_a,
           gla_norm, lq1, lk1, lq2, lk2, diff_subln, w_branch_a, w_branch_b, w_out, ffn_norm,
           w_rg, b_rg, w_re, b_re, w_gate, w_up, w_down, w_ple, ple_norm, w_ple_gate, final_norm,
           *, tm, ts, tq, tk, te):
    T, D = h2.shape
    assert D == SUBLANES * LANES, "token-slab gathers need one (8, 128) tile per token"
    n_mix = sum(SPLIT_SIZES)
    wgla = w_in[:, :C_GLA_END].astype(BF16)
    wdiff = w_in[:, C_GLA_END:n_mix].astype(BF16)
    wg = w_in[:, n_mix:].astype(BF16)

    gq, gk, gv, gr, la, dq, dk, dv = _inproj(h2, positions, attn_norm, wgla, wdiff,
                                             w_a2.astype(BF16), b_a, tm=tm)
    oa = _gla(gq, gk, gv, gr, la, gla_norm, B=B, S=S, ts=ts).reshape(T, GLA_V)
    ob = _diffattn(dq, dk, dv, lq1, lk1, lq2, lk2, diff_subln, B=B, S=S, tq=tq, tk=tk,
                   lambda_init=lambda_init).reshape(T, DIFF_W)

    wr = jnp.concatenate([w_rg, w_re, jnp.zeros((D, LANES - N_GROUPS - N_EXPERTS), F32)], axis=1)
    wr_hi = wr.astype(BF16)
    wr = jnp.concatenate([wr_hi, (wr - wr_hi.astype(F32)).astype(BF16)], axis=1)
    br = jnp.concatenate([b_rg, b_re, jnp.zeros((LANES - N_GROUPS - N_EXPERTS,), F32)]).reshape(1, LANES)
    h1, n2, route, cnt = _postmix(h2, oa, ob, attn_norm, wg, w_branch_a.astype(BF16),
                                  w_branch_b.astype(BF16), w_out.astype(BF16), ffn_norm, wr, br,
                                  tm=tm)

    xg, slots, tables, tile_e, n_tiles = _dispatch(n2, route, cnt, tm=tm, te=te)
    ys = _experts(tile_e, n_tiles, xg, w_gate, w_up, w_down, tm=te)
    return _final(h1, ys, slots, tables, route, p_all, layer, w_ple.astype(BF16), ple_norm,
                  w_ple_gate.astype(BF16), final_norm, tm=tm, apply_final_norm=apply_final_norm)


def _block(x, p, positions, attn_norm, w_in, w_a2, b_a, gla_norm, lambda_q1, lambda_k1, lambda_q2,
           lambda_k2, diff_subln, w_branch_a, w_branch_b, w_out, ffn_norm, w_router_group,
           b_router_group, w_router_expert, b_router_expert, w_gate, w_up, w_down, w_ple, ple_norm,
           w_ple_gate, final_norm, *, tm, ts, tq, tk, te):
    B, S, D = x.shape
    depth = w_in.shape[0]
    h = x.reshape(B * S, D)
    for i in range(depth):
        lambda_init = 0.8 - 0.6 * math.exp(-0.3 * i)
        h = _layer(h, p.reshape(depth, B * S, -1), i, positions, B, S, lambda_init, i == depth - 1,
                   attn_norm[i], w_in[i], w_a2[i], b_a[i], gla_norm[i], lambda_q1[i], lambda_k1[i],
                   lambda_q2[i], lambda_k2[i], diff_subln[i], w_branch_a[i], w_branch_b[i], w_out[i],
                   ffn_norm[i], w_router_group[i], b_router_group[i], w_router_expert[i],
                   b_router_expert[i], w_gate[i], w_up[i], w_down[i], w_ple[i], ple_norm[i],
                   w_ple_gate[i], final_norm, tm=tm, ts=ts, tq=tq, tk=tk, te=te)
    return h.reshape(B, S, D)


def kernel(x, p, positions, attn_norm, w_in, w_a2, b_a, gla_norm, lambda_q1, lambda_k1, lambda_q2, lambda_k2, diff_subln, w_branch_a, w_branch_b, w_out, ffn_norm, w_router_group, b_router_group, w_router_expert, b_router_expert, w_gate, w_up, w_down, w_ple, ple_norm, w_ple_gate, final_norm):
    S = x.shape[1]
    return _block(x, p, positions, attn_norm, w_in, w_a2, b_a, gla_norm, lambda_q1, lambda_k1,
                  lambda_q2, lambda_k2, diff_subln, w_branch_a, w_branch_b, w_out, ffn_norm,
                  w_router_group, b_router_group, w_router_expert, b_router_expert, w_gate, w_up,
                  w_down, w_ple, ple_norm, w_ple_gate, final_norm,
                  tm=512, ts=min(1024, S), tq=min(512, S), tk=min(512, S), te=512)
```

```python
import functools
import math

import jax
import jax.numpy as jnp
import numpy as np
from jax import lax
from jax.experimental import pallas as pl
from jax.experimental.pallas import tpu as pltpu

EPS = 1e-6

GLA_HEADS = 4
GLA_DK = 64
GLA_DV = 128
GLA_LOWRANK = 16
GLA_TAU = 16.0
GLA_CHUNK = 64
GLA_QK = GLA_HEADS * GLA_DK
GLA_V = GLA_HEADS * GLA_DV

DIFF_HEADS = 4
DIFF_DH = 64
DIFF_DV = 2 * DIFF_DH
DIFF_W = DIFF_HEADS * DIFF_DV
ROPE_THETA = 500000.0
ROPE_DIM = DIFF_DH // 4
ROPE_HALF = ROPE_DIM // 2

N_GROUPS = 4
EXPERTS_PER_GROUP = 8
N_EXPERTS = N_GROUPS * EXPERTS_PER_GROUP
D_EXPERT = 256

LANES = 128
MXU_N = 256
VMEM_LIMIT = 56 * 1024 * 1024

BF16 = jnp.bfloat16
F32 = jnp.float32
NT_DIMS = (((1,), (1,)), ((), ()))
TN_DIMS = (((0,), (0,)), ((), ()))
NEG_BIG = -1e30
LOG2_E = math.log2(math.e)


def _rms(x, g):
    return x * lax.rsqrt(jnp.mean(x * x, axis=-1, keepdims=True) + EPS) * g


def _cparams(semantics):
    return pltpu.CompilerParams(dimension_semantics=semantics, vmem_limit_bytes=VMEM_LIMIT)


SUBLANES = 8


def _store_slab(ref, x):
    n = x.shape[0]
    for j in range(SUBLANES):
        ref[pl.ds(j, n, stride=SUBLANES), :] = x[:, j * LANES:(j + 1) * LANES]


def _load_slab(ref, n):
    return jnp.concatenate([ref[pl.ds(j, n, stride=SUBLANES), :] for j in range(SUBLANES)], axis=1)


C_GQ = 0
C_GK = C_GQ + GLA_QK
C_GV = C_GK + GLA_QK
C_GR = C_GV + GLA_V
C_AL = C_GR + GLA_V
C_GLA_END = C_AL + GLA_LOWRANK
C_DQ = 0
C_DK = C_DQ + DIFF_W
C_DV = C_DK + DIFF_W
C_END = C_DV + DIFF_W


def _inproj_kernel(x_ref, pos_ref, g_ref, wgla_ref, wdiff_ref, wa2_ref, ba_ref, inv_ref, sel_ref,
                   one_ref,
                   gq_ref, gk_ref, gv_ref, gr_ref, la_ref, dq_ref, dk_ref, dv_ref):
    ang = inv_ref[...] * pos_ref[0].astype(F32)
    cs = jnp.concatenate([jnp.cos(ang), jnp.sin(ang)], axis=0)
    tabs = None
    rest = cs
    for _ in range(3):
        piece = rest.astype(BF16)
        rest = rest - piece.astype(F32)
        t = lax.dot_general(piece, sel_ref[...], TN_DIMS, preferred_element_type=F32)
        tabs = t if tabs is None else tabs + t
    cosf = tabs[:, :LANES] + one_ref[...]
    sneg = tabs[:, LANES:2 * LANES]
    spos = tabs[:, 2 * LANES:]

    nb = _rms(x_ref[...], g_ref[...]).astype(BF16)

    def proj_gla(c0, c1):
        return jnp.dot(nb, wgla_ref[:, c0:c1], preferred_element_type=F32)

    def proj(c0, c1):
        return jnp.dot(nb, wdiff_ref[:, c0:c1], preferred_element_type=F32)

    gq_ref[...] = proj_gla(C_GQ, C_GK) * (GLA_DK ** -0.5)
    gk_ref[...] = proj_gla(C_GK, C_GV)
    gv_ref[...] = proj_gla(C_GV, C_GR).astype(BF16)
    gr_ref[...] = proj_gla(C_GR, C_AL).astype(BF16)

    gal = proj_gla(C_AL, C_GLA_END).astype(BF16)
    a_logit = jnp.dot(gal, wa2_ref[...], preferred_element_type=F32) + ba_ref[...]
    la_ref[...] = (jnp.minimum(a_logit, 0.0) - jnp.log1p(jnp.exp(-jnp.abs(a_logit)))) / GLA_TAU

    def rope_store(c0, out_ref, scale):
        for j in range(DIFF_W // MXU_N):
            pair = proj(c0 + j * MXU_N, c0 + (j + 1) * MXU_N)
            for i in range(MXU_N // LANES):
                blk = pair[:, i * LANES:(i + 1) * LANES]
                rot = (blk * cosf + pltpu.roll(blk, LANES - ROPE_HALF, 1) * sneg
                       + pltpu.roll(blk, ROPE_HALF, 1) * spos)
                c = j * MXU_N + i * LANES
                out_ref[:, c:c + LANES] = (rot * scale).astype(BF16)

    rope_store(C_DQ, dq_ref, DIFF_DH ** -0.5 * LOG2_E)
    rope_store(C_DK, dk_ref, 1.0)
    dv_ref[...] = proj(C_DV, C_END).astype(BF16)


def _rope_tables():
    lane = np.arange(LANES)
    r = lane % DIFF_DH
    selc = np.zeros((2 * ROPE_HALF, LANES), np.float32)
    seln = np.zeros_like(selc)
    selp = np.zeros_like(selc)
    one = np.zeros((1, LANES), np.float32)
    for l in range(LANES):
        if r[l] < ROPE_DIM:
            selc[r[l] % ROPE_HALF, l] = 1.0
            if r[l] < ROPE_HALF:
                seln[ROPE_HALF + r[l], l] = -1.0
            else:
                selp[ROPE_HALF + r[l] - ROPE_HALF, l] = 1.0
        else:
            one[0, l] = 1.0
    sel = np.concatenate([selc, seln, selp], axis=1)
    return jnp.asarray(sel, dtype=BF16), jnp.asarray(one)


def _inproj(x2, positions, attn_norm, wgla, wdiff, wa2, b_a, *, tm):
    T, D = x2.shape
    nt = T // tm
    pos3 = positions.reshape(nt, 1, tm)
    inv = (ROPE_THETA ** (-jnp.arange(0, ROPE_DIM, 2, dtype=F32) / ROPE_DIM)).reshape(ROPE_HALF, 1)
    sel, one = _rope_tables()
    row = lambda i: (i, 0)
    fixed = lambda i: (0, 0)
    out_shapes = (
        jax.ShapeDtypeStruct((T, GLA_QK), F32), jax.ShapeDtypeStruct((T, GLA_QK), F32),
        jax.ShapeDtypeStruct((T, GLA_V), BF16), jax.ShapeDtypeStruct((T, GLA_V), BF16),
        jax.ShapeDtypeStruct((T, GLA_QK), F32),
        jax.ShapeDtypeStruct((T, DIFF_W), BF16), jax.ShapeDtypeStruct((T, DIFF_W), BF16),
        jax.ShapeDtypeStruct((T, DIFF_W), BF16),
    )
    return pl.pallas_call(
        _inproj_kernel,
        grid=(nt,),
        in_specs=[
            pl.BlockSpec((tm, D), row),
            pl.BlockSpec((1, 1, tm), lambda i: (i, 0, 0)),
            pl.BlockSpec((1, D), fixed),
            pl.BlockSpec((D, C_GLA_END), fixed),
            pl.BlockSpec((D, C_END), fixed),
            pl.BlockSpec((GLA_LOWRANK, GLA_QK), fixed),
            pl.BlockSpec((1, GLA_QK), fixed),
            pl.BlockSpec((ROPE_HALF, 1), fixed),
            pl.BlockSpec((2 * ROPE_HALF, 3 * LANES), fixed),
            pl.BlockSpec((1, LANES), fixed),
        ],
        out_specs=[pl.BlockSpec((tm, s.shape[1]), row) for s in out_shapes],
        out_shape=out_shapes,
        compiler_params=_cparams(("parallel",)),
        name="inproj",
    )(x2, pos3, attn_norm.reshape(1, D), wgla, wdiff, wa2, b_a.reshape(1, GLA_QK), inv, sel, one)


def _gla_kernel(gq_ref, gk_ref, gv_ref, gr_ref, la_ref, gn_ref, oa_ref, st_ref, *, n_chunks):
    @pl.when(pl.program_id(1) == 0)
    def _():
        st_ref[...] = jnp.zeros_like(st_ref)

    C = GLA_CHUNK
    tril = lax.broadcasted_iota(jnp.int32, (C, C), 0) >= lax.broadcasted_iota(jnp.int32, (C, C), 1)
    trilb = jnp.where(tril, 1.0, 0.0).astype(BF16)
    head_of_lane = lax.broadcasted_iota(jnp.int32, (1, GLA_QK), 1) // GLA_DK
    hmask = [head_of_lane == h for h in range(GLA_HEADS)]
    gn = gn_ref[...]

    def by_head(x):
        return jnp.concatenate([jnp.where(hmask[h], x, jnp.zeros_like(x))
                                for h in range(GLA_HEADS)], axis=0)

    chunks = range(n_chunks)
    sls = [pl.ds(c * C, C) for c in chunks]
    vs = [gv_ref[0, sl, :] for sl in sls]

    def cumsum(la):
        total, rest = None, la
        for _ in range(3):
            piece = rest.astype(BF16)
            rest = rest - piece.astype(F32)
            t = jnp.dot(trilb, piece, preferred_element_type=F32)
            total = t if total is None else total + t
        return total

    bs = [cumsum(la_ref[0, sl, :]) for sl in sls]
    b_lasts = [b[C - 1:C, :] for b in bs]
    q_heads = [by_head((gq_ref[0, sl, :] * jnp.exp(b)).astype(BF16)) for sl, b in zip(sls, bs)]
    k_invs = [(gk_ref[0, sl, :] * jnp.exp(-b)).astype(BF16) for sl, b in zip(sls, bs)]
    k_ends = [by_head((gk_ref[0, sl, :] * jnp.exp(bl - b)).astype(BF16))
              for sl, b, bl in zip(sls, bs, b_lasts)]
    decays = [jnp.exp(bl) for bl in b_lasts]
    atts = [lax.dot_general(qh, ki, NT_DIMS, preferred_element_type=F32)
            for qh, ki in zip(q_heads, k_invs)]
    v_heads = [jnp.concatenate([v[:, h * GLA_DV:(h + 1) * GLA_DV] for h in range(GLA_HEADS)], axis=0)
               for v in vs]
    us = [lax.dot_general(vh, ke, TN_DIMS, preferred_element_type=F32)
          for vh, ke in zip(v_heads, k_ends)]

    st = st_ref[...]
    states = []
    for c in chunks:
        states.append(st.astype(BF16))
        st = st * decays[c] + us[c]
    st_ref[...] = st

    inters = [lax.dot_general(qh, s_in, NT_DIMS, preferred_element_type=F32)
              for qh, s_in in zip(q_heads, states)]
    for c in chunks:
        for h in range(GLA_HEADS):
            rows = slice(h * C, (h + 1) * C)
            cols = slice(h * GLA_DV, (h + 1) * GLA_DV)
            a_h = jnp.where(tril, atts[c][rows], 0.0).astype(BF16)
            o_h = jnp.dot(a_h, vs[c][:, cols], preferred_element_type=F32) + inters[c][rows]
            y = _rms(o_h, gn)
            r = gr_ref[0, sls[c], cols].astype(F32)
            oa_ref[0, sls[c], cols] = (y * (r * jax.nn.sigmoid(r))).astype(BF16)


def _gla(gq, gk, gv, gr, la, gla_norm, *, B, S, ts):
    n_chunks = ts // GLA_CHUNK
    blk = lambda w: pl.BlockSpec((1, ts, w), lambda b, j: (b, j, 0))
    r3 = lambda a: a.reshape(B, S, a.shape[-1])
    return pl.pallas_call(
        functools.partial(_gla_kernel, n_chunks=n_chunks),
        grid=(B, S // ts),
        in_specs=[blk(GLA_QK), blk(GLA_QK), blk(GLA_V), blk(GLA_V), blk(GLA_QK),
                  pl.BlockSpec((1, GLA_DV), lambda b, j: (0, 0))],
        out_specs=blk(GLA_V),
        out_shape=jax.ShapeDtypeStruct((B, S, GLA_V), BF16),
        scratch_shapes=[pltpu.VMEM((GLA_DV, GLA_QK), F32)],
        compiler_params=_cparams(("parallel", "arbitrary")),
        name="gla",
    )(r3(gq), r3(gk), r3(gv), r3(gr), r3(la), gla_norm.reshape(1, GLA_DV))


ATTN_PAIR = 4


def _diffattn_kernel(dq_ref, dk_ref, dv_ref, lq1_ref, lk1_ref, lq2_ref, lk2_ref, sub_ref,
                     ob_ref, m_ref, l_ref, acc_ref, *, tq, tk, lambda_init):
    S = dq_ref.shape[1]
    nq = S // tq
    R = 2 * tq
    lam = (jnp.exp(jnp.sum(lq1_ref[...] * lk1_ref[...], axis=-1, keepdims=True))
           - jnp.exp(jnp.sum(lq2_ref[...] * lk2_ref[...], axis=-1, keepdims=True))
           + lambda_init)
    first_comp = lax.broadcasted_iota(jnp.int32, (1, DIFF_DV), 1) < DIFF_DH
    sub = sub_ref[...]

    pair = range(ATTN_PAIR)
    for hp in range(DIFF_HEADS // ATTN_PAIR):
        colss = [slice((hp * ATTN_PAIR + t) * DIFF_DV, (hp * ATTN_PAIR + t + 1) * DIFF_DV)
                 for t in pair]
        for qi in range(nq):
            qss = []
            for t in pair:
                qb = dq_ref[0, qi * tq:(qi + 1) * tq, colss[t]]
                zero = jnp.zeros_like(qb)
                qss.append(jnp.concatenate([jnp.where(first_comp, qb, zero),
                                            jnp.where(first_comp, zero, qb)], axis=0))
            m_ref[...] = jnp.full(m_ref.shape, NEG_BIG, F32)
            l_ref[...] = jnp.zeros(l_ref.shape, F32)
            acc_ref[...] = jnp.zeros(acc_ref.shape, F32)

            def scores(kstart, qss=qss, colss=colss):
                return [lax.dot_general(dk_ref[0, kstart:kstart + tk, colss[t]], qss[t], NT_DIMS,
                                        preferred_element_type=F32) for t in pair]

            def update(ss, kstart, masked, colss=colss, qi=qi):
                if masked:
                    kpos = kstart + lax.broadcasted_iota(jnp.int32, (tk, R), 0)
                    qpos = qi * tq + lax.broadcasted_iota(jnp.int32, (tk, R), 1) % tq
                    ss = [jnp.where(qpos >= kpos, s, NEG_BIG) for s in ss]
                m_prevs = [m_ref[t] for t in pair]
                m_news = [jnp.maximum(mp, jnp.max(s, axis=0, keepdims=True))
                          for mp, s in zip(m_prevs, ss)]
                alphas = [jnp.exp2(mp - mn) for mp, mn in zip(m_prevs, m_news)]
                ps = [jnp.exp2(s - mn) for s, mn in zip(ss, m_news)]
                for t in pair:
                    l_ref[t] = alphas[t] * l_ref[t] + jnp.sum(ps[t], axis=0, keepdims=True)
                    m_ref[t] = m_news[t]
                pvs = [lax.dot_general(dv_ref[0, kstart:kstart + tk, colss[t]], ps[t].astype(BF16),
                                       TN_DIMS, preferred_element_type=F32) for t in pair]
                for t in pair:
                    acc_ref[t] = alphas[t] * acc_ref[t] + pvs[t]

            n_blocks = (qi + 1) * tq // tk
            diag_from = qi * tq // tk
            s_next = scores(0)
            for kb_i in range(n_blocks):
                s_cur = s_next
                if kb_i + 1 < n_blocks:
                    s_next = scores((kb_i + 1) * tk)
                update(s_cur, kb_i * tk, kb_i >= diag_from)

            for t in pair:
                o_all = acc_ref[t] / l_ref[t]
                o = o_all[:, :tq] - lam * o_all[:, tq:]
                y = o * lax.rsqrt(jnp.mean(o * o, axis=0, keepdims=True) + EPS) * sub
                y = y * (1.0 - lambda_init)
                ob_ref[0, qi * tq:(qi + 1) * tq, colss[t]] = y.T.astype(BF16)


def _diffattn(dq, dk, dv, lq1, lk1, lq2, lk2, diff_subln, *, B, S, tq, tk, lambda_init):
    seq = pl.BlockSpec((1, S, DIFF_W), lambda b: (b, 0, 0))
    vec = lambda w: pl.BlockSpec((1, w), lambda b: (0, 0))
    r3 = lambda a: a.reshape(B, S, DIFF_W)
    return pl.pallas_call(
        functools.partial(_diffattn_kernel, tq=tq, tk=tk, lambda_init=lambda_init),
        grid=(B,),
        in_specs=[seq, seq, seq, vec(DIFF_DH), vec(DIFF_DH), vec(DIFF_DH), vec(DIFF_DH),
                  pl.BlockSpec((DIFF_DV, 1), lambda b: (0, 0))],
        out_specs=seq,
        out_shape=jax.ShapeDtypeStruct((B, S, DIFF_W), BF16),
        scratch_shapes=[pltpu.VMEM((ATTN_PAIR, 1, 2 * tq), F32), pltpu.VMEM((ATTN_PAIR, 1, 2 * tq), F32),
                        pltpu.VMEM((ATTN_PAIR, DIFF_DV, 2 * tq), F32)],
        compiler_params=_cparams(("parallel",)),
        name="diffattn",
    )(r3(dq), r3(dk), r3(dv), lq1.reshape(1, -1), lk1.reshape(1, -1), lq2.reshape(1, -1),
      lk2.reshape(1, -1), diff_subln.reshape(-1, 1))


ROUTE_W = 8


def _first_index_of(mask, lane):
    return jnp.min(jnp.where(mask, lane, LANES), axis=-1, keepdims=True)


POSTMIX_SPLIT = 2


def _postmix_kernel(x_ref, oa_ref, ob_ref, g_ref, wg_ref, wba_ref, wbb_ref, wo_ref, fn_ref,
                    wr_ref, br_ref, h_ref, n2_ref, route_ref, cnt_ref):
    D = x_ref.shape[1]
    rows_per = x_ref.shape[0] // POSTMIX_SPLIT
    groups = [slice(s * rows_per, (s + 1) * rows_per) for s in range(POSTMIX_SPLIT)]
    dot = functools.partial(jnp.dot, preferred_element_type=F32)
    xs = [x_ref[r, :] for r in groups]
    nbs = [_rms(x, g_ref[...]).astype(BF16) for x in xs]
    y_as = [dot(oa_ref[r, :], wba_ref[...]) for r in groups]
    g_as = [dot(nb, wg_ref[:, :D]) for nb in nbs]
    mergeds = [jax.nn.sigmoid(g) * y for g, y in zip(g_as, y_as)]
    y_bs = [dot(ob_ref[r, :], wbb_ref[...]) for r in groups]
    g_bs = [dot(nb, wg_ref[:, D:]) for nb in nbs]
    mergeds = [m + jax.nn.sigmoid(g) * y for m, g, y in zip(mergeds, g_bs, y_bs)]
    hs = [x + dot(m.astype(BF16), wo_ref[...]) for x, m in zip(xs, mergeds)]
    n2s = [_rms(h, fn_ref[...]) for h in hs]
    n2_his = [n2.astype(BF16) for n2 in n2s]
    n2_los = [(n2 - hi.astype(F32)).astype(BF16) for n2, hi in zip(n2s, n2_his)]
    hi_prods = [dot(hi, wr_ref[...]) for hi in n2_his]
    lgs = [hp[:, :LANES] + hp[:, LANES:] + dot(lo, wr_ref[:, :LANES]) + br_ref[...]
           for hp, lo in zip(hi_prods, n2_los)]
    cnt = None
    for r, h, n2_hi, lg in zip(groups, hs, n2_his, lgs):
        h_ref[r, :] = h
        n2_ref[r, :] = n2_hi
        rec, c = _route_top2(lg)
        route_ref[r, :] = rec
        cnt = c if cnt is None else cnt + c
    cnt_ref[0] = cnt


def _route_top2(lg):
    lane = lax.broadcasted_iota(jnp.int32, lg.shape, 1)
    is_g = lane < N_GROUPS
    g_max = jnp.max(jnp.where(is_g, lg, -jnp.inf), axis=-1, keepdims=True)
    g_exp = jnp.where(is_g, jnp.exp(lg - g_max), 0.0)
    g_prob = g_exp / jnp.sum(g_exp, axis=-1, keepdims=True)
    g_p = jnp.max(g_prob, axis=-1, keepdims=True)
    g_idx = _first_index_of(is_g & (g_prob == g_p), lane)

    e_lo = N_GROUPS + EXPERTS_PER_GROUP * g_idx
    is_e = (lane >= e_lo) & (lane < e_lo + EXPERTS_PER_GROUP)
    e_max = jnp.max(jnp.where(is_e, lg, -jnp.inf), axis=-1, keepdims=True)
    e_exp = jnp.where(is_e, jnp.exp(lg - e_max), 0.0)
    e_prob = e_exp / jnp.sum(e_exp, axis=-1, keepdims=True)
    p1 = jnp.max(jnp.where(is_e, e_prob, -1.0), axis=-1, keepdims=True)
    i1 = _first_index_of(is_e & (e_prob == p1), lane)
    rest = is_e & (lane != i1)
    p2 = jnp.max(jnp.where(rest, e_prob, -1.0), axis=-1, keepdims=True)
    i2 = _first_index_of(rest & (e_prob == p2), lane)
    den = p1 + p2
    w1 = g_p * (p1 / den)
    w2 = g_p * (p2 / den)
    rec = jnp.where(lane == 0, w1, 0.0)
    rec = jnp.where(lane == 1, w2, rec)
    rec = jnp.where(lane == 2, (i1 - N_GROUPS).astype(F32), rec)
    rec = jnp.where(lane == 3, (i2 - N_GROUPS).astype(F32), rec)
    chosen = (lane == i1 - N_GROUPS) | (lane == i2 - N_GROUPS)
    cnt = jnp.sum(chosen.astype(F32), axis=0, keepdims=True)
    return rec[:, :ROUTE_W], cnt


def _postmix(x2, oa, ob, attn_norm, wg, wba, wbb, wo, ffn_norm, wr, br, *, tm):
    T, D = x2.shape
    row = lambda i: (i, 0)
    fixed = lambda i: (0, 0)
    full = lambda a: pl.BlockSpec(a.shape, fixed)
    g = attn_norm.reshape(1, D)
    fn = ffn_norm.reshape(1, D)
    return pl.pallas_call(
        _postmix_kernel,
        grid=(T // tm,),
        in_specs=[pl.BlockSpec((tm, D), row), pl.BlockSpec((tm, GLA_V), row),
                  pl.BlockSpec((tm, DIFF_W), row), full(g), full(wg), full(wba), full(wbb),
                  full(wo), full(fn), full(wr), full(br)],
        out_specs=[pl.BlockSpec((tm, D), row), pl.BlockSpec((tm, D), row),
                   pl.BlockSpec((tm, ROUTE_W), row),
                   pl.BlockSpec((1, 1, LANES), lambda i: (i, 0, 0))],
        out_shape=(jax.ShapeDtypeStruct((T, D), F32), jax.ShapeDtypeStruct((T, D), BF16),
                   jax.ShapeDtypeStruct((T, ROUTE_W), F32),
                   jax.ShapeDtypeStruct((T // tm, 1, LANES), F32)),
        compiler_params=_cparams(("parallel",)),
        name="postmix",
    )(x2, oa, ob, g, wg, wba, wbb, wo, fn, wr, br)


DISPATCH_CHUNK = 16
DISPATCH_TAIL = (8, 4, 2, 1)
SLOT_W = 8


def _token_copy(src, src_tok, dst, dst_tok, n_tok, sem):
    return pltpu.make_async_copy(
        src.at[pl.ds(pl.multiple_of(src_tok * SUBLANES, SUBLANES), n_tok * SUBLANES), :],
        dst.at[pl.ds(pl.multiple_of(dst_tok * SUBLANES, SUBLANES), n_tok * SUBLANES), :], sem)


def _for_each_piece(length, fn):
    n_full = lax.shift_right_logical(length, DISPATCH_CHUNK.bit_length() - 1)

    def body(c, carry):
        fn(c * DISPATCH_CHUNK, DISPATCH_CHUNK)
        return carry

    lax.fori_loop(0, n_full, body, 0)
    off = n_full * DISPATCH_CHUNK
    for n in DISPATCH_TAIL:
        has = (length & n) != 0

        @pl.when(has)
        def _(off=off, n=n):
            fn(off, n)

        off = off + jnp.where(has, n, 0)


def _dispatch_kernel(seg_ref, lo_ref, cnt_ref, fill_ref, fill_len_ref, n_tiles_ref,
                     n2_ref, route_ref, lo_row_ref,
                     xg_hbm, slot_ref, xs_buf, zero_buf, sem, fill_sem):
    tm = n2_ref.shape[0]
    n_slab = 2 * tm * SUBLANES
    i = pl.program_id(0)
    nt = pl.num_programs(0)
    slot = i % 2

    @pl.when(i == 0)
    def _():
        zero_buf[...] = jnp.zeros_like(zero_buf)
        te = zero_buf.shape[0] // SUBLANES
        for wait in (False, True):
            def fill(e, carry, wait=wait):
                def piece(off, n):
                    cp = _token_copy(zero_buf, 0, xg_hbm, fill_ref[e] + off, n, fill_sem)
                    cp.wait() if wait else cp.start()
                _for_each_piece(fill_len_ref[e], piece)
                return carry
            lax.fori_loop(0, N_EXPERTS, fill, 0)

            def fill_tile(t, carry, wait=wait):
                cp = _token_copy(zero_buf, 0, xg_hbm, t * te, te, fill_sem)
                cp.wait() if wait else cp.start()
                return carry
            lax.fori_loop(n_tiles_ref[0], xg_hbm.shape[0] // zero_buf.shape[0], fill_tile, 0)

    route = route_ref[...]
    lane = lax.broadcasted_iota(jnp.int32, (tm, LANES), 1)
    oh1 = lane == route[:, 2:3].astype(jnp.int32)
    oh2 = lane == route[:, 3:4].astype(jnp.int32)
    both = jnp.where(oh1 | oh2, 1.0, 0.0).astype(BF16)
    earlier = (lax.broadcasted_iota(jnp.int32, (tm, tm), 0)
               > lax.broadcasted_iota(jnp.int32, (tm, tm), 1))
    rank = jnp.dot(jnp.where(earlier, 1.0, 0.0).astype(BF16), both, preferred_element_type=F32)

    def pick(onehot, v):
        return jnp.sum(jnp.where(onehot, v, 0.0), axis=-1, keepdims=True)

    r1 = pick(oh1, rank)
    r2 = pick(oh2, rank)
    q1 = pick(oh1, lo_row_ref[0]) + r1
    q2 = pick(oh2, lo_row_ref[0]) + r2
    rec = jnp.where(lane == 0, q1, jnp.where(lane == 1, q2, 0.0))
    slot_ref[...] = rec[:, :SLOT_W]

    rec_t = rec.T.astype(jnp.int32)
    slot_id = lax.broadcasted_iota(jnp.int32, (2 * tm, tm), 0)
    perm = jnp.where((slot_id == rec_t[0:1, :]) | (slot_id == rec_t[1:2, :]), 1.0, 0.0)
    xs = jnp.dot(perm.astype(BF16), n2_ref[...], preferred_element_type=F32)

    def wait_slot(s):
        pltpu.make_async_copy(xg_hbm.at[pl.ds(0, n_slab), :], xs_buf.at[s], sem.at[s]).wait()

    @pl.when(i >= 2)
    def _():
        wait_slot(slot)

    _store_slab(xs_buf.at[slot], xs)

    def send(e, carry):
        k = i * N_EXPERTS + e

        def piece(off, n):
            _token_copy(xs_buf.at[slot], lo_ref[k] + off, xg_hbm, seg_ref[k] + off, n,
                        sem.at[slot]).start()

        _for_each_piece(cnt_ref[k], piece)
        return carry

    lax.fori_loop(0, N_EXPERTS, send, 0)

    @pl.when(i == nt - 1)
    def _():
        wait_slot(slot)

        @pl.when(nt >= 2)
        def _():
            wait_slot(1 - slot)


def _dispatch(n2, route, cnt, *, tm, te):
    T, D = n2.shape
    nt = T // tm
    n_tiles_max = 2 * T // te + N_EXPERTS
    c = cnt[:, 0, :N_EXPERTS].astype(jnp.int32)
    count = jnp.sum(c, axis=0)
    padded = (count + te - 1) // te * te
    p_end = jnp.cumsum(padded)
    p_start = p_end - padded
    n_tiles = (p_end[-1] // te).astype(jnp.int32).reshape(1)
    tile_start = jnp.arange(n_tiles_max, dtype=jnp.int32) * te
    tile_e = jnp.sum(tile_start[:, None] >= p_end[None, :], axis=1, dtype=jnp.int32)
    tile_e = jnp.minimum(tile_e, jnp.sum(tile_start[n_tiles[0] - 1] >= p_end, dtype=jnp.int32))
    seg = p_start[None, :] + jnp.cumsum(c, axis=0) - c
    lo = jnp.cumsum(c, axis=1) - c
    lanes = lambda a: jnp.pad(a.astype(F32), ((0, 0), (0, LANES - N_EXPERTS))).reshape(nt, 1, LANES)
    flat = lambda a: a.reshape(-1).astype(jnp.int32)
    row = lambda i, *_: (i, 0)
    per_tile = pl.BlockSpec((1, 1, LANES), lambda i, *_: (i, 0, 0))
    grid_spec = pltpu.PrefetchScalarGridSpec(
        num_scalar_prefetch=6,
        grid=(nt,),
        in_specs=[pl.BlockSpec((tm, D), row), pl.BlockSpec((tm, ROUTE_W), row), per_tile],
        out_specs=[pl.BlockSpec(memory_space=pl.ANY), pl.BlockSpec((tm, SLOT_W), row)],
        scratch_shapes=[pltpu.VMEM((2, 2 * tm * SUBLANES, LANES), F32),
                        pltpu.VMEM((te * SUBLANES, LANES), F32),
                        pltpu.SemaphoreType.DMA((2,)), pltpu.SemaphoreType.DMA(())],
    )
    tables = (flat(seg), flat(lo), flat(c))
    xg, slots = pl.pallas_call(
        _dispatch_kernel,
        grid_spec=grid_spec,
        out_shape=(jax.ShapeDtypeStruct((n_tiles_max * te * SUBLANES, LANES), F32),
                   jax.ShapeDtypeStruct((T, SLOT_W), F32)),
        compiler_params=_cparams(("arbitrary",)),
        name="dispatch",
    )(*tables, flat(p_start + count), flat(padded - count), n_tiles, n2, route, lanes(lo))
    return xg, slots, tables, tile_e, n_tiles


def _experts_kernel(tile_e_ref, n_tiles_ref, xg_ref, wg_ref, wu_ref, wd_ref, y_ref,
                    wg_bf, wu_bf, wd_bf):
    tm = xg_ref.shape[0] // SUBLANES
    i = pl.program_id(0)
    used = i < n_tiles_ref[0]

    @pl.when(used & ((i == 0) | (tile_e_ref[i] != tile_e_ref[jnp.maximum(i - 1, 0)])))
    def _():
        wg_bf[...] = wg_ref[0].astype(BF16)
        wu_bf[...] = wu_ref[0].astype(BF16)
        wd_bf[...] = wd_ref[0].astype(BF16)

    @pl.when(used)
    def _():
        x = _load_slab(xg_ref, tm).astype(BF16)
        g = jnp.dot(x, wg_bf[...], preferred_element_type=F32)
        u = jnp.dot(x, wu_bf[...], preferred_element_type=F32)
        he = (g * jax.nn.sigmoid(g)) * u
        _store_slab(y_ref, jnp.dot(he.astype(BF16), wd_bf[...], preferred_element_type=F32))

    @pl.when(i >= n_tiles_ref[0])
    def _():
        y_ref[...] = jnp.zeros_like(y_ref)


def _experts(tile_e, n_tiles, xg, wg, wu, wd, *, tm):
    D = wg.shape[1]
    n_tiles_max = xg.shape[0] // (tm * SUBLANES)
    slab = lambda f: pl.BlockSpec((tm * SUBLANES, LANES), f)
    grid_spec = pltpu.PrefetchScalarGridSpec(
        num_scalar_prefetch=2,
        grid=(n_tiles_max,),
        in_specs=[slab(lambda i, te, nt: (jnp.minimum(i, nt[0] - 1), 0)),
                  pl.BlockSpec((1, D, D_EXPERT), lambda i, te, nt: (te[i], 0, 0)),
                  pl.BlockSpec((1, D, D_EXPERT), lambda i, te, nt: (te[i], 0, 0)),
                  pl.BlockSpec((1, D_EXPERT, D), lambda i, te, nt: (te[i], 0, 0))],
        out_specs=slab(lambda i, te, nt: (i, 0)),
        scratch_shapes=[pltpu.VMEM((D, D_EXPERT), BF16), pltpu.VMEM((D, D_EXPERT), BF16),
                        pltpu.VMEM((D_EXPERT, D), BF16)],
    )
    return pl.pallas_call(
        _experts_kernel,
        grid_spec=grid_spec,
        out_shape=jax.ShapeDtypeStruct(xg.shape, F32),
        compiler_params=_cparams(("arbitrary",)),
        name="experts",
    )(tile_e, n_tiles, xg, wg, wu, wd)


def _final_kernel(seg_ref, lo_ref, cnt_ref, h_ref, ys_hbm, route_ref, slot_ref, p_ref, wple_ref,
                  pn_ref, wpg_ref, fn_ref, o_ref, ybuf, sem, *, apply_final_norm):
    tm = h_ref.shape[0]
    i = pl.program_id(0)
    nt = pl.num_programs(0)
    slot = i % 2

    def fetch(tile, s):
        def run(e, carry):
            k = tile * N_EXPERTS + e

            def piece(off, n):
                _token_copy(ys_hbm, seg_ref[k] + off, ybuf.at[s], lo_ref[k] + off, n,
                            sem.at[s]).start()

            _for_each_piece(cnt_ref[k], piece)
            return carry

        lax.fori_loop(0, N_EXPERTS, run, 0)

    @pl.when(i == 0)
    def _():
        fetch(0, 0)

    @pl.when(i + 1 < nt)
    def _():
        fetch(i + 1, 1 - slot)

    pltpu.make_async_copy(ys_hbm.at[pl.ds(0, 2 * tm * SUBLANES), :], ybuf.at[slot],
                          sem.at[slot]).wait()
    y_sorted = _load_slab(ybuf.at[slot], 2 * tm).astype(BF16)
    route = route_ref[...]
    slots = slot_ref[...].astype(jnp.int32)
    col = lax.broadcasted_iota(jnp.int32, (tm, 2 * tm), 1)
    comb = jnp.where(col == slots[:, 0:1], route[:, 0:1],
                     jnp.where(col == slots[:, 1:2], route[:, 1:2], 0.0))
    h = h_ref[...] + jnp.dot(comb.astype(BF16), y_sorted, preferred_element_type=F32)
    e = _rms(jnp.dot(p_ref[0].astype(BF16), wple_ref[...], preferred_element_type=F32), pn_ref[...])
    gate = jax.nn.sigmoid(jnp.dot(h.astype(BF16), wpg_ref[...], preferred_element_type=F32))
    h = h + gate * e
    if apply_final_norm:
        h = _rms(h, fn_ref[...])
    o_ref[...] = h


def _final(h1, ys, slots, tables, route, p_all, layer, wple, ple_norm, wpg, final_norm, *, tm,
           apply_final_norm):
    T, D = h1.shape
    nt = T // tm
    row = lambda i, *_: (i, 0)
    fixed = lambda i, *_: (0, 0)
    full = lambda a: pl.BlockSpec(a.shape, fixed)
    pn = ple_norm.reshape(1, D)
    fn = final_norm.reshape(1, D)
    grid_spec = pltpu.PrefetchScalarGridSpec(
        num_scalar_prefetch=3,
        grid=(nt,),
        in_specs=[pl.BlockSpec((tm, D), row), pl.BlockSpec(memory_space=pl.ANY),
                  pl.BlockSpec((tm, ROUTE_W), row), pl.BlockSpec((tm, SLOT_W), row),
                  pl.BlockSpec((1, tm, p_all.shape[2]), lambda i, *_: (layer, i, 0)),
                  full(wple), full(pn), full(wpg), full(fn)],
        out_specs=pl.BlockSpec((tm, D), row),
        scratch_shapes=[pltpu.VMEM((2, 2 * tm * SUBLANES, LANES), F32),
                        pltpu.SemaphoreType.DMA((2,))],
    )
    return pl.pallas_call(
        functools.partial(_final_kernel, apply_final_norm=apply_final_norm),
        grid_spec=grid_spec,
        out_shape=jax.ShapeDtypeStruct((T, D), F32),
        compiler_params=_cparams(("arbitrary",)),
        name="final",
    )(*tables, h1, ys, route, slots, p_all, wple, pn, wpg, fn)


SPLIT_SIZES = (GLA_QK, GLA_QK, GLA_V, GLA_V, GLA_LOWRANK, DIFF_W, DIFF_W, DIFF_W)


def _layer(h2, p_all, layer, positions, B, S, lambda_init, apply_final_norm, attn_norm, w_in, w_a2, b_a,
           gla_norm, lq1, lk1, lq2, lk2, diff_subln, w_branch_a, w_branch_b, w_out, ffn_norm,
           w_rg, b_rg, w_re, b_re, w_gate, w_up, w_down, w_ple, ple_norm, w_ple_gate, final_norm,
           *, tm, ts, tq, tk, te):
    T, D = h2.shape
    assert D == SUBLANES * LANES, "token-slab gathers need one (8, 128) tile per token"
    n_mix = sum(SPLIT_SIZES)
    wgla = w_in[:, :C_GLA_END].astype(BF16)
    wdiff = w_in[:, C_GLA_END:n_mix].astype(BF16)
    wg = w_in[:, n_mix:].astype(BF16)

    gq, gk, gv, gr, la, dq, dk, dv = _inproj(h2, positions, attn_norm, wgla, wdiff,
                                             w_a2.astype(BF16), b_a, tm=tm)
    oa = _gla(gq, gk, gv, gr, la, gla_norm, B=B, S=S, ts=ts).reshape(T, GLA_V)
    ob = _diffattn(dq, dk, dv, lq1, lk1, lq2, lk2, diff_subln, B=B, S=S, tq=tq, tk=tk,
                   lambda_init=lambda_init).reshape(T, DIFF_W)

    wr = jnp.concatenate([w_rg, w_re, jnp.zeros((D, LANES - N_GROUPS - N_EXPERTS), F32)], axis=1)
    wr_hi = wr.astype(BF16)
    wr = jnp.concatenate([wr_hi, (wr - wr_hi.astype(F32)).astype(BF16)], axis=1)
    br = jnp.concatenate([b_rg, b_re, jnp.zeros((LANES - N_GROUPS - N_EXPERTS,), F32)]).reshape(1, LANES)
    h1, n2, route, cnt = _postmix(h2, oa, ob, attn_norm, wg, w_branch_a.astype(BF16),
                                  w_branch_b.astype(BF16), w_out.astype(BF16), ffn_norm, wr, br,
                                  tm=tm)

    xg, slots, tables, tile_e, n_tiles = _dispatch(n2, route, cnt, tm=tm, te=te)
    ys = _experts(tile_e, n_tiles, xg, w_gate, w_up, w_down, tm=te)
    return _final(h1, ys, slots, tables, route, p_all, layer, w_ple.astype(BF16), ple_norm,
                  w_ple_gate.astype(BF16), final_norm, tm=tm, apply_final_norm=apply_final_norm)


def _block(x, p, positions, attn_norm, w_in, w_a2, b_a, gla_norm, lambda_q1, lambda_k1, lambda_q2,
           lambda_k2, diff_subln, w_branch_a, w_branch_b, w_out, ffn_norm, w_router_group,
           b_router_group, w_router_expert, b_router_expert, w_gate, w_up, w_down, w_ple, ple_norm,
           w_ple_gate, final_norm, *, tm, ts, tq, tk, te):
    B, S, D = x.shape
    depth = w_in.shape[0]
    h = x.reshape(B * S, D)
    for i in range(depth):
        lambda_init = 0.8 - 0.6 * math.exp(-0.3 * i)
        h = _layer(h, p.reshape(depth, B * S, -1), i, positions, B, S, lambda_init, i == depth - 1,
                   attn_norm[i], w_in[i], w_a2[i], b_a[i], gla_norm[i], lambda_q1[i], lambda_k1[i],
                   lambda_q2[i], lambda_k2[i], diff_subln[i], w_branch_a[i], w_branch_b[i], w_out[i],
                   ffn_norm[i], w_router_group[i], b_router_group[i], w_router_expert[i],
                   b_router_expert[i], w_gate[i], w_up[i], w_down[i], w_ple[i], ple_norm[i],
                   w_ple_gate[i], final_norm, tm=tm, ts=ts, tq=tq, tk=tk, te=te)
    return h.reshape(B, S, D)


def kernel(x, p, positions, attn_norm, w_in, w_a2, b_a, gla_norm, lambda_q1, lambda_k1, lambda_q2, lambda_k2, diff_subln, w_branch_a, w_branch_b, w_out, ffn_norm, w_router_group, b_router_group, w_router_expert, b_router_expert, w_gate, w_up, w_down, w_ple, ple_norm, w_ple_gate, final_norm):
    return _block(x, p, positions, attn_norm, w_in, w_a2, b_a, gla_norm, lambda_q1, lambda_k1,
                  lambda_q2, lambda_k2, diff_subln, w_branch_a, w_branch_b, w_out, ffn_norm,
                  w_router_group, b_router_group, w_router_expert, b_router_expert, w_gate, w_up,
                  w_down, w_ple, ple_norm, w_ple_gate, final_norm, **_tile_sizes(x.shape[1]))


def _tile_sizes(seq):
    return dict(
        tm=512,
        ts=min(1024, seq),
        tq=min(512, seq),
        tk=min(512, seq),
        te=512,
    )
```

```python
import functools
import math

import jax
import jax.numpy as jnp
import numpy as np
from jax import lax
from jax.experimental import pallas as pl
from jax.experimental.pallas import tpu as pltpu

EPS = 1e-6

GLA_HEADS = 4
GLA_DK = 64
GLA_DV = 128
GLA_LOWRANK = 16
GLA_TAU = 16.0
GLA_CHUNK = 64
GLA_QK = GLA_HEADS * GLA_DK
GLA_V = GLA_HEADS * GLA_DV

DIFF_HEADS = 4
DIFF_DH = 64
DIFF_DV = 2 * DIFF_DH
DIFF_W = DIFF_HEADS * DIFF_DV
ROPE_THETA = 500000.0
ROPE_DIM = DIFF_DH // 4
ROPE_HALF = ROPE_DIM // 2

N_GROUPS = 4
EXPERTS_PER_GROUP = 8
N_EXPERTS = N_GROUPS * EXPERTS_PER_GROUP
D_EXPERT = 256

LANES = 128
MXU_N = 256
VMEM_LIMIT = 56 * 1024 * 1024

BF16 = jnp.bfloat16
F32 = jnp.float32
NT_DIMS = (((1,), (1,)), ((), ()))
TN_DIMS = (((0,), (0,)), ((), ()))
NEG_BIG = -1e30
LOG2_E = math.log2(math.e)


def _rms(x, g):
    return x * lax.rsqrt(jnp.mean(x * x, axis=-1, keepdims=True) + EPS) * g


def _cparams(semantics):
    return pltpu.CompilerParams(dimension_semantics=semantics, vmem_limit_bytes=VMEM_LIMIT)


SUBLANES = 8


def _store_slab(ref, x):
    n = x.shape[0]
    for j in range(SUBLANES):
        ref[pl.ds(j, n, stride=SUBLANES), :] = x[:, j * LANES:(j + 1) * LANES]


def _load_slab(ref, n):
    return jnp.concatenate([ref[pl.ds(j, n, stride=SUBLANES), :] for j in range(SUBLANES)], axis=1)


C_GQ = 0
C_GK = C_GQ + GLA_QK
C_GV = C_GK + GLA_QK
C_GR = C_GV + GLA_V
C_AL = C_GR + GLA_V
C_GLA_END = C_AL + GLA_LOWRANK
C_DQ = 0
C_DK = C_DQ + DIFF_W
C_DV = C_DK + DIFF_W
C_END = C_DV + DIFF_W


def _inproj_kernel(x_ref, pos_ref, g_ref, wgla_ref, wdiff_ref, wa2_ref, ba_ref, inv_ref, sel_ref,
                   one_ref,
                   gq_ref, gk_ref, gv_ref, gr_ref, la_ref, dq_ref, dk_ref, dv_ref):
    ang = inv_ref[...] * pos_ref[0].astype(F32)
    cs = jnp.concatenate([jnp.cos(ang), jnp.sin(ang)], axis=0)
    tabs = None
    rest = cs
    for _ in range(3):
        piece = rest.astype(BF16)
        rest = rest - piece.astype(F32)
        t = lax.dot_general(piece, sel_ref[...], TN_DIMS, preferred_element_type=F32)
        tabs = t if tabs is None else tabs + t
    cosf = tabs[:, :LANES] + one_ref[...]
    sneg = tabs[:, LANES:2 * LANES]
    spos = tabs[:, 2 * LANES:]

    nb = _rms(x_ref[...], g_ref[...]).astype(BF16)

    def proj_gla(c0, c1):
        return jnp.dot(nb, wgla_ref[:, c0:c1], preferred_element_type=F32)

    def proj(c0, c1):
        return jnp.dot(nb, wdiff_ref[:, c0:c1], preferred_element_type=F32)

    gq_ref[...] = proj_gla(C_GQ, C_GK) * (GLA_DK ** -0.5)
    gk_ref[...] = proj_gla(C_GK, C_GV)
    gv_ref[...] = proj_gla(C_GV, C_GR).astype(BF16)
    gr_ref[...] = proj_gla(C_GR, C_AL).astype(BF16)

    gal = proj_gla(C_AL, C_GLA_END).astype(BF16)
    a_logit = jnp.dot(gal, wa2_ref[...], preferred_element_type=F32) + ba_ref[...]
    la_ref[...] = (jnp.minimum(a_logit, 0.0) - jnp.log1p(jnp.exp(-jnp.abs(a_logit)))) / GLA_TAU

    def rope_store(c0, out_ref, scale):
        for j in range(DIFF_W // MXU_N):
            pair = proj(c0 + j * MXU_N, c0 + (j + 1) * MXU_N)
            for i in range(MXU_N // LANES):
                blk = pair[:, i * LANES:(i + 1) * LANES]
                rot = (blk * cosf + pltpu.roll(blk, LANES - ROPE_HALF, 1) * sneg
                       + pltpu.roll(blk, ROPE_HALF, 1) * spos)
                c = j * MXU_N + i * LANES
                out_ref[:, c:c + LANES] = (rot * scale).astype(BF16)

    rope_store(C_DQ, dq_ref, DIFF_DH ** -0.5 * LOG2_E)
    rope_store(C_DK, dk_ref, 1.0)
    dv_ref[...] = proj(C_DV, C_END).astype(BF16)


def _rope_tables():
    lane = np.arange(LANES)
    r = lane % DIFF_DH
    selc = np.zeros((2 * ROPE_HALF, LANES), np.float32)
    seln = np.zeros_like(selc)
    selp = np.zeros_like(selc)
    one = np.zeros((1, LANES), np.float32)
    for l in range(LANES):
        if r[l] < ROPE_DIM:
            selc[r[l] % ROPE_HALF, l] = 1.0
            if r[l] < ROPE_HALF:
                seln[ROPE_HALF + r[l], l] = -1.0
            else:
                selp[ROPE_HALF + r[l] - ROPE_HALF, l] = 1.0
        else:
            one[0, l] = 1.0
    sel = np.concatenate([selc, seln, selp], axis=1)
    return jnp.asarray(sel, dtype=BF16), jnp.asarray(one)


def _inproj(x2, positions, attn_norm, wgla, wdiff, wa2, b_a, *, tm):
    T, D = x2.shape
    nt = T // tm
    pos3 = positions.reshape(nt, 1, tm)
    inv = (ROPE_THETA ** (-jnp.arange(0, ROPE_DIM, 2, dtype=F32) / ROPE_DIM)).reshape(ROPE_HALF, 1)
    sel, one = _rope_tables()
    row = lambda i: (i, 0)
    fixed = lambda i: (0, 0)
    out_shapes = (
        jax.ShapeDtypeStruct((T, GLA_QK), F32), jax.ShapeDtypeStruct((T, GLA_QK), F32),
        jax.ShapeDtypeStruct((T, GLA_V), BF16), jax.ShapeDtypeStruct((T, GLA_V), BF16),
        jax.ShapeDtypeStruct((T, GLA_QK), F32),
        jax.ShapeDtypeStruct((T, DIFF_W), BF16), jax.ShapeDtypeStruct((T, DIFF_W), BF16),
        jax.ShapeDtypeStruct((T, DIFF_W), BF16),
    )
    return pl.pallas_call(
        _inproj_kernel,
        grid=(nt,),
        in_specs=[
            pl.BlockSpec((tm, D), row),
            pl.BlockSpec((1, 1, tm), lambda i: (i, 0, 0)),
            pl.BlockSpec((1, D), fixed),
            pl.BlockSpec((D, C_GLA_END), fixed),
            pl.BlockSpec((D, C_END), fixed),
            pl.BlockSpec((GLA_LOWRANK, GLA_QK), fixed),
            pl.BlockSpec((1, GLA_QK), fixed),
            pl.BlockSpec((ROPE_HALF, 1), fixed),
            pl.BlockSpec((2 * ROPE_HALF, 3 * LANES), fixed),
            pl.BlockSpec((1, LANES), fixed),
        ],
        out_specs=[pl.BlockSpec((tm, s.shape[1]), row) for s in out_shapes],
        out_shape=out_shapes,
        compiler_params=_cparams(("parallel",)),
        name="inproj",
    )(x2, pos3, attn_norm.reshape(1, D), wgla, wdiff, wa2, b_a.reshape(1, GLA_QK), inv, sel, one)


def _gla_kernel(gq_ref, gk_ref, gv_ref, gr_ref, la_ref, gn_ref, oa_ref, st_ref, *, n_chunks):
    @pl.when(pl.program_id(1) == 0)
    def _():
        st_ref[...] = jnp.zeros_like(st_ref)

    C = GLA_CHUNK
    tril = lax.broadcasted_iota(jnp.int32, (C, C), 0) >= lax.broadcasted_iota(jnp.int32, (C, C), 1)
    trilb = jnp.where(tril, 1.0, 0.0).astype(BF16)
    head_of_lane = lax.broadcasted_iota(jnp.int32, (1, GLA_QK), 1) // GLA_DK
    hmask = [head_of_lane == h for h in range(GLA_HEADS)]
    gn = gn_ref[...]

    def by_head(x):
        return jnp.concatenate([jnp.where(hmask[h], x, jnp.zeros_like(x))
                                for h in range(GLA_HEADS)], axis=0)

    chunks = range(n_chunks)
    sls = [pl.ds(c * C, C) for c in chunks]
    vs = [gv_ref[0, sl, :] for sl in sls]

    def cumsum(la):
        total, rest = None, la
        for _ in range(3):
            piece = rest.astype(BF16)
            rest = rest - piece.astype(F32)
            t = jnp.dot(trilb, piece, preferred_element_type=F32)
            total = t if total is None else total + t
        return total

    bs = [cumsum(la_ref[0, sl, :]) for sl in sls]
    b_lasts = [b[C - 1:C, :] for b in bs]
    q_heads = [by_head((gq_ref[0, sl, :] * jnp.exp(b)).astype(BF16)) for sl, b in zip(sls, bs)]
    k_invs = [(gk_ref[0, sl, :] * jnp.exp(-b)).astype(BF16) for sl, b in zip(sls, bs)]
    k_ends = [by_head((gk_ref[0, sl, :] * jnp.exp(bl - b)).astype(BF16))
              for sl, b, bl in zip(sls, bs, b_lasts)]
    decays = [jnp.exp(bl) for bl in b_lasts]
    atts = [lax.dot_general(qh, ki, NT_DIMS, preferred_element_type=F32)
            for qh, ki in zip(q_heads, k_invs)]
    v_heads = [jnp.concatenate([v[:, h * GLA_DV:(h + 1) * GLA_DV] for h in range(GLA_HEADS)], axis=0)
               for v in vs]
    us = [lax.dot_general(vh, ke, TN_DIMS, preferred_element_type=F32)
          for vh, ke in zip(v_heads, k_ends)]

    st = st_ref[...]
    states = []
    for c in chunks:
        states.append(st.astype(BF16))
        st = st * decays[c] + us[c]
    st_ref[...] = st

    inters = [lax.dot_general(qh, s_in, NT_DIMS, preferred_element_type=F32)
              for qh, s_in in zip(q_heads, states)]
    for c in chunks:
        for h in range(GLA_HEADS):
            rows = slice(h * C, (h + 1) * C)
            cols = slice(h * GLA_DV, (h + 1) * GLA_DV)
            a_h = jnp.where(tril, atts[c][rows], 0.0).astype(BF16)
            o_h = jnp.dot(a_h, vs[c][:, cols], preferred_element_type=F32) + inters[c][rows]
            y = _rms(o_h, gn)
            r = gr_ref[0, sls[c], cols].astype(F32)
            oa_ref[0, sls[c], cols] = (y * (r * jax.nn.sigmoid(r))).astype(BF16)


def _gla(gq, gk, gv, gr, la, gla_norm, *, B, S, ts):
    n_chunks = ts // GLA_CHUNK
    blk = lambda w: pl.BlockSpec((1, ts, w), lambda b, j: (b, j, 0))
    r3 = lambda a: a.reshape(B, S, a.shape[-1])
    return pl.pallas_call(
        functools.partial(_gla_kernel, n_chunks=n_chunks),
        grid=(B, S // ts),
        in_specs=[blk(GLA_QK), blk(GLA_QK), blk(GLA_V), blk(GLA_V), blk(GLA_QK),
                  pl.BlockSpec((1, GLA_DV), lambda b, j: (0, 0))],
        out_specs=blk(GLA_V),
        out_shape=jax.ShapeDtypeStruct((B, S, GLA_V), BF16),
        scratch_shapes=[pltpu.VMEM((GLA_DV, GLA_QK), F32)],
        compiler_params=_cparams(("parallel", "arbitrary")),
        name="gla",
    )(r3(gq), r3(gk), r3(gv), r3(gr), r3(la), gla_norm.reshape(1, GLA_DV))


ATTN_PAIR = 4


def _diffattn_kernel(dq_ref, dk_ref, dv_ref, lq1_ref, lk1_ref, lq2_ref, lk2_ref, sub_ref,
                     ob_ref, m_ref, l_ref, acc_ref, *, tq, tk, lambda_init):
    S = dq_ref.shape[1]
    nq = S // tq
    R = 2 * tq
    lam = (jnp.exp(jnp.sum(lq1_ref[...] * lk1_ref[...], axis=-1, keepdims=True))
           - jnp.exp(jnp.sum(lq2_ref[...] * lk2_ref[...], axis=-1, keepdims=True))
           + lambda_init)
    first_comp = lax.broadcasted_iota(jnp.int32, (1, DIFF_DV), 1) < DIFF_DH
    sub = sub_ref[...]

    pair = range(ATTN_PAIR)
    for hp in range(DIFF_HEADS // ATTN_PAIR):
        colss = [slice((hp * ATTN_PAIR + t) * DIFF_DV, (hp * ATTN_PAIR + t + 1) * DIFF_DV)
                 for t in pair]
        for qi in range(nq):
            qss = []
            for t in pair:
                qb = dq_ref[0, qi * tq:(qi + 1) * tq, colss[t]]
                zero = jnp.zeros_like(qb)
                qss.append(jnp.concatenate([jnp.where(first_comp, qb, zero),
                                            jnp.where(first_comp, zero, qb)], axis=0))
            m_ref[...] = jnp.full(m_ref.shape, NEG_BIG, F32)
            l_ref[...] = jnp.zeros(l_ref.shape, F32)
            acc_ref[...] = jnp.zeros(acc_ref.shape, F32)

            def scores(kstart, qss=qss, colss=colss):
                return [lax.dot_general(dk_ref[0, kstart:kstart + tk, colss[t]], qss[t], NT_DIMS,
                                        preferred_element_type=F32) for t in pair]

            def update(ss, kstart, masked, colss=colss, qi=qi):
                if masked:
                    kpos = kstart + lax.broadcasted_iota(jnp.int32, (tk, R), 0)
                    qpos = qi * tq + lax.broadcasted_iota(jnp.int32, (tk, R), 1) % tq
                    ss = [jnp.where(qpos >= kpos, s, NEG_BIG) for s in ss]
                m_prevs = [m_ref[t] for t in pair]
                m_news = [jnp.maximum(mp, jnp.max(s, axis=0, keepdims=True))
                          for mp, s in zip(m_prevs, ss)]
                alphas = [jnp.exp2(mp - mn) for mp, mn in zip(m_prevs, m_news)]
                ps = [jnp.exp2(s - mn) for s, mn in zip(ss, m_news)]
                for t in pair:
                    l_ref[t] = alphas[t] * l_ref[t] + jnp.sum(ps[t], axis=0, keepdims=True)
                    m_ref[t] = m_news[t]
                pvs = [lax.dot_general(dv_ref[0, kstart:kstart + tk, colss[t]], ps[t].astype(BF16),
                                       TN_DIMS, preferred_element_type=F32) for t in pair]
                for t in pair:
                    acc_ref[t] = alphas[t] * acc_ref[t] + pvs[t]

            n_blocks = (qi + 1) * tq // tk
            diag_from = qi * tq // tk
            s_next = scores(0)
            for kb_i in range(n_blocks):
                s_cur = s_next
                if kb_i + 1 < n_blocks:
                    s_next = scores((kb_i + 1) * tk)
                update(s_cur, kb_i * tk, kb_i >= diag_from)

            for t in pair:
                o_all = acc_ref[t] / l_ref[t]
                o = o_all[:, :tq] - lam * o_all[:, tq:]
                y = o * lax.rsqrt(jnp.mean(o * o, axis=0, keepdims=True) + EPS) * sub
                y = y * (1.0 - lambda_init)
                ob_ref[0, qi * tq:(qi + 1) * tq, colss[t]] = y.T.astype(BF16)


def _diffattn(dq, dk, dv, lq1, lk1, lq2, lk2, diff_subln, *, B, S, tq, tk, lambda_init):
    seq = pl.BlockSpec((1, S, DIFF_W), lambda b: (b, 0, 0))
    vec = lambda w: pl.BlockSpec((1, w), lambda b: (0, 0))
    r3 = lambda a: a.reshape(B, S, DIFF_W)
    return pl.pallas_call(
        functools.partial(_diffattn_kernel, tq=tq, tk=tk, lambda_init=lambda_init),
        grid=(B,),
        in_specs=[seq, seq, seq, vec(DIFF_DH), vec(DIFF_DH), vec(DIFF_DH), vec(DIFF_DH),
                  pl.BlockSpec((DIFF_DV, 1), lambda b: (0, 0))],
        out_specs=seq,
        out_shape=jax.ShapeDtypeStruct((B, S, DIFF_W), BF16),
        scratch_shapes=[pltpu.VMEM((ATTN_PAIR, 1, 2 * tq), F32), pltpu.VMEM((ATTN_PAIR, 1, 2 * tq), F32),
                        pltpu.VMEM((ATTN_PAIR, DIFF_DV, 2 * tq), F32)],
        compiler_params=_cparams(("parallel",)),
        name="diffattn",
    )(r3(dq), r3(dk), r3(dv), lq1.reshape(1, -1), lk1.reshape(1, -1), lq2.reshape(1, -1),
      lk2.reshape(1, -1), diff_subln.reshape(-1, 1))


ROUTE_W = 8


def _first_index_of(mask, lane):
    return jnp.min(jnp.where(mask, lane, LANES), axis=-1, keepdims=True)


POSTMIX_SPLIT = 2


def _postmix_kernel(x_ref, oa_ref, ob_ref, g_ref, wg_ref, wba_ref, wbb_ref, wo_ref, fn_ref,
                    wr_ref, br_ref, h_ref, n2_ref, route_ref, cnt_ref):
    D = x_ref.shape[1]
    rows_per = x_ref.shape[0] // POSTMIX_SPLIT
    groups = [slice(s * rows_per, (s + 1) * rows_per) for s in range(POSTMIX_SPLIT)]
    dot = functools.partial(jnp.dot, preferred_element_type=F32)
    xs = [x_ref[r, :] for r in groups]
    nbs = [_rms(x, g_ref[...]).astype(BF16) for x in xs]
    y_as = [dot(oa_ref[r, :], wba_ref[...]) for r in groups]
    g_as = [dot(nb, wg_ref[:, :D]) for nb in nbs]
    mergeds = [jax.nn.sigmoid(g) * y for g, y in zip(g_as, y_as)]
    y_bs = [dot(ob_ref[r, :], wbb_ref[...]) for r in groups]
    g_bs = [dot(nb, wg_ref[:, D:]) for nb in nbs]
    mergeds = [m + jax.nn.sigmoid(g) * y for m, g, y in zip(mergeds, g_bs, y_bs)]
    hs = [x + dot(m.astype(BF16), wo_ref[...]) for x, m in zip(xs, mergeds)]
    n2s = [_rms(h, fn_ref[...]) for h in hs]
    n2_his = [n2.astype(BF16) for n2 in n2s]
    n2_los = [(n2 - hi.astype(F32)).astype(BF16) for n2, hi in zip(n2s, n2_his)]
    hi_prods = [dot(hi, wr_ref[...]) for hi in n2_his]
    lgs = [hp[:, :LANES] + hp[:, LANES:] + dot(lo, wr_ref[:, :LANES]) + br_ref[...]
           for hp, lo in zip(hi_prods, n2_los)]
    for r, h, n2_hi in zip(groups, hs, n2_his):
        h_ref[r, :] = h
        n2_ref[r, :] = n2_hi
    cnt = None
    for r, (rec, c) in zip(groups, _in_lockstep([_route_top2(lg) for lg in lgs])):
        route_ref[r, :] = rec
        cnt = c if cnt is None else cnt + c
    cnt_ref[0] = cnt


def _in_lockstep(stage_generators):
    results = [None] * len(stage_generators)
    active = list(range(len(stage_generators)))
    while active:
        for i in list(active):
            try:
                next(stage_generators[i])
            except StopIteration as done:
                results[i] = done.value
                active.remove(i)
    return results


def _route_top2(lg):
    lane = lax.broadcasted_iota(jnp.int32, lg.shape, 1)
    is_g = lane < N_GROUPS
    g_max = jnp.max(jnp.where(is_g, lg, -jnp.inf), axis=-1, keepdims=True)
    yield
    g_exp = jnp.where(is_g, jnp.exp(lg - g_max), 0.0)
    g_prob = g_exp / jnp.sum(g_exp, axis=-1, keepdims=True)
    yield
    g_p = jnp.max(g_prob, axis=-1, keepdims=True)
    yield
    g_idx = _first_index_of(is_g & (g_prob == g_p), lane)
    yield

    e_lo = N_GROUPS + EXPERTS_PER_GROUP * g_idx
    is_e = (lane >= e_lo) & (lane < e_lo + EXPERTS_PER_GROUP)
    e_max = jnp.max(jnp.where(is_e, lg, -jnp.inf), axis=-1, keepdims=True)
    yield
    e_exp = jnp.where(is_e, jnp.exp(lg - e_max), 0.0)
    e_prob = e_exp / jnp.sum(e_exp, axis=-1, keepdims=True)
    yield
    p1 = jnp.max(jnp.where(is_e, e_prob, -1.0), axis=-1, keepdims=True)
    yield
    i1 = _first_index_of(is_e & (e_prob == p1), lane)
    yield
    rest = is_e & (lane != i1)
    p2 = jnp.max(jnp.where(rest, e_prob, -1.0), axis=-1, keepdims=True)
    yield
    i2 = _first_index_of(rest & (e_prob == p2), lane)
    yield
    den = p1 + p2
    w1 = g_p * (p1 / den)
    w2 = g_p * (p2 / den)
    rec = jnp.where(lane == 0, w1, 0.0)
    rec = jnp.where(lane == 1, w2, rec)
    rec = jnp.where(lane == 2, (i1 - N_GROUPS).astype(F32), rec)
    rec = jnp.where(lane == 3, (i2 - N_GROUPS).astype(F32), rec)
    chosen = (lane == i1 - N_GROUPS) | (lane == i2 - N_GROUPS)
    cnt = jnp.sum(chosen.astype(F32), axis=0, keepdims=True)
    return rec[:, :ROUTE_W], cnt


def _postmix(x2, oa, ob, attn_norm, wg, wba, wbb, wo, ffn_norm, wr, br, *, tm):
    T, D = x2.shape
    row = lambda i: (i, 0)
    fixed = lambda i: (0, 0)
    full = lambda a: pl.BlockSpec(a.shape, fixed)
    g = attn_norm.reshape(1, D)
    fn = ffn_norm.reshape(1, D)
    return pl.pallas_call(
        _postmix_kernel,
        grid=(T // tm,),
        in_specs=[pl.BlockSpec((tm, D), row), pl.BlockSpec((tm, GLA_V), row),
                  pl.BlockSpec((tm, DIFF_W), row), full(g), full(wg), full(wba), full(wbb),
                  full(wo), full(fn), full(wr), full(br)],
        out_specs=[pl.BlockSpec((tm, D), row), pl.BlockSpec((tm, D), row),
                   pl.BlockSpec((tm, ROUTE_W), row),
                   pl.BlockSpec((1, 1, LANES), lambda i: (i, 0, 0))],
        out_shape=(jax.ShapeDtypeStruct((T, D), F32), jax.ShapeDtypeStruct((T, D), BF16),
                   jax.ShapeDtypeStruct((T, ROUTE_W), F32),
                   jax.ShapeDtypeStruct((T // tm, 1, LANES), F32)),
        compiler_params=_cparams(("parallel",)),
        name="postmix",
    )(x2, oa, ob, g, wg, wba, wbb, wo, fn, wr, br)


DISPATCH_CHUNK = 16
DISPATCH_TAIL = (8, 4, 2, 1)
SLOT_W = 8


def _token_copy(src, src_tok, dst, dst_tok, n_tok, sem):
    return pltpu.make_async_copy(
        src.at[pl.ds(pl.multiple_of(src_tok * SUBLANES, SUBLANES), n_tok * SUBLANES), :],
        dst.at[pl.ds(pl.multiple_of(dst_tok * SUBLANES, SUBLANES), n_tok * SUBLANES), :], sem)


def _for_each_piece(length, fn):
    n_full = lax.shift_right_logical(length, DISPATCH_CHUNK.bit_length() - 1)

    def body(c, carry):
        fn(c * DISPATCH_CHUNK, DISPATCH_CHUNK)
        return carry

    lax.fori_loop(0, n_full, body, 0)
    off = n_full * DISPATCH_CHUNK
    for n in DISPATCH_TAIL:
        has = (length & n) != 0

        @pl.when(has)
        def _(off=off, n=n):
            fn(off, n)

        off = off + jnp.where(has, n, 0)


def _dispatch_kernel(seg_ref, lo_ref, cnt_ref, fill_ref, fill_len_ref, n_tiles_ref,
                     n2_ref, route_ref, lo_row_ref,
                     xg_hbm, slot_ref, xs_buf, zero_buf, sem, fill_sem):
    tm = n2_ref.shape[0]
    n_slab = 2 * tm * SUBLANES
    i = pl.program_id(0)
    nt = pl.num_programs(0)
    slot = i % 2

    @pl.when(i == 0)
    def _():
        zero_buf[...] = jnp.zeros_like(zero_buf)
        te = zero_buf.shape[0] // SUBLANES
        for wait in (False, True):
            def fill(e, carry, wait=wait):
                def piece(off, n):
                    cp = _token_copy(zero_buf, 0, xg_hbm, fill_ref[e] + off, n, fill_sem)
                    cp.wait() if wait else cp.start()
                _for_each_piece(fill_len_ref[e], piece)
                return carry
            lax.fori_loop(0, N_EXPERTS, fill, 0)

            def fill_tile(t, carry, wait=wait):
                cp = _token_copy(zero_buf, 0, xg_hbm, t * te, te, fill_sem)
                cp.wait() if wait else cp.start()
                return carry
            lax.fori_loop(n_tiles_ref[0], xg_hbm.shape[0] // zero_buf.shape[0], fill_tile, 0)

    route = route_ref[...]
    lane = lax.broadcasted_iota(jnp.int32, (tm, LANES), 1)
    oh1 = lane == route[:, 2:3].astype(jnp.int32)
    oh2 = lane == route[:, 3:4].astype(jnp.int32)
    both = jnp.where(oh1 | oh2, 1.0, 0.0).astype(BF16)
    earlier = (lax.broadcasted_iota(jnp.int32, (tm, tm), 0)
               > lax.broadcasted_iota(jnp.int32, (tm, tm), 1))
    rank = jnp.dot(jnp.where(earlier, 1.0, 0.0).astype(BF16), both, preferred_element_type=F32)

    def pick(onehot, v):
        return jnp.sum(jnp.where(onehot, v, 0.0), axis=-1, keepdims=True)

    r1 = pick(oh1, rank)
    r2 = pick(oh2, rank)
    q1 = pick(oh1, lo_row_ref[0]) + r1
    q2 = pick(oh2, lo_row_ref[0]) + r2
    rec = jnp.where(lane == 0, q1, jnp.where(lane == 1, q2, 0.0))
    slot_ref[...] = rec[:, :SLOT_W]

    rec_t = rec.T.astype(jnp.int32)
    slot_id = lax.broadcasted_iota(jnp.int32, (2 * tm, tm), 0)
    perm = jnp.where((slot_id == rec_t[0:1, :]) | (slot_id == rec_t[1:2, :]), 1.0, 0.0)
    xs = jnp.dot(perm.astype(BF16), n2_ref[...], preferred_element_type=F32)

    def wait_slot(s):
        pltpu.make_async_copy(xg_hbm.at[pl.ds(0, n_slab), :], xs_buf.at[s], sem.at[s]).wait()

    @pl.when(i >= 2)
    def _():
        wait_slot(slot)

    _store_slab(xs_buf.at[slot], xs)

    def send(e, carry):
        k = i * N_EXPERTS + e

        def piece(off, n):
            _token_copy(xs_buf.at[slot], lo_ref[k] + off, xg_hbm, seg_ref[k] + off, n,
                        sem.at[slot]).start()

        _for_each_piece(cnt_ref[k], piece)
        return carry

    lax.fori_loop(0, N_EXPERTS, send, 0)

    @pl.when(i == nt - 1)
    def _():
        wait_slot(slot)

        @pl.when(nt >= 2)
        def _():
            wait_slot(1 - slot)


def _dispatch(n2, route, cnt, *, tm, te):
    T, D = n2.shape
    nt = T // tm
    n_tiles_max = 2 * T // te + N_EXPERTS
    c = cnt[:, 0, :N_EXPERTS].astype(jnp.int32)
    count = jnp.sum(c, axis=0)
    padded = (count + te - 1) // te * te
    p_end = jnp.cumsum(padded)
    p_start = p_end - padded
    n_tiles = (p_end[-1] // te).astype(jnp.int32).reshape(1)
    tile_start = jnp.arange(n_tiles_max, dtype=jnp.int32) * te
    tile_e = jnp.sum(tile_start[:, None] >= p_end[None, :], axis=1, dtype=jnp.int32)
    tile_e = jnp.minimum(tile_e, jnp.sum(tile_start[n_tiles[0] - 1] >= p_end, dtype=jnp.int32))
    seg = p_start[None, :] + jnp.cumsum(c, axis=0) - c
    lo = jnp.cumsum(c, axis=1) - c
    lanes = lambda a: jnp.pad(a.astype(F32), ((0, 0), (0, LANES - N_EXPERTS))).reshape(nt, 1, LANES)
    flat = lambda a: a.reshape(-1).astype(jnp.int32)
    row = lambda i, *_: (i, 0)
    per_tile = pl.BlockSpec((1, 1, LANES), lambda i, *_: (i, 0, 0))
    grid_spec = pltpu.PrefetchScalarGridSpec(
        num_scalar_prefetch=6,
        grid=(nt,),
        in_specs=[pl.BlockSpec((tm, D), row), pl.BlockSpec((tm, ROUTE_W), row), per_tile],
        out_specs=[pl.BlockSpec(memory_space=pl.ANY), pl.BlockSpec((tm, SLOT_W), row)],
        scratch_shapes=[pltpu.VMEM((2, 2 * tm * SUBLANES, LANES), F32),
                        pltpu.VMEM((te * SUBLANES, LANES), F32),
                        pltpu.SemaphoreType.DMA((2,)), pltpu.SemaphoreType.DMA(())],
    )
    tables = (flat(seg), flat(lo), flat(c))
    xg, slots = pl.pallas_call(
        _dispatch_kernel,
        grid_spec=grid_spec,
        out_shape=(jax.ShapeDtypeStruct((n_tiles_max * te * SUBLANES, LANES), F32),
                   jax.ShapeDtypeStruct((T, SLOT_W), F32)),
        compiler_params=_cparams(("arbitrary",)),
        name="dispatch",
    )(*tables, flat(p_start + count), flat(padded - count), n_tiles, n2, route, lanes(lo))
    return xg, slots, tables, tile_e, n_tiles


def _experts_kernel(tile_e_ref, n_tiles_ref, xg_ref, wg_ref, wu_ref, wd_ref, y_ref,
                    wg_bf, wu_bf, wd_bf):
    tm = xg_ref.shape[0] // SUBLANES
    i = pl.program_id(0)
    used = i < n_tiles_ref[0]

    @pl.when(used & ((i == 0) | (tile_e_ref[i] != tile_e_ref[jnp.maximum(i - 1, 0)])))
    def _():
        wg_bf[...] = wg_ref[0].astype(BF16)
        wu_bf[...] = wu_ref[0].astype(BF16)
        wd_bf[...] = wd_ref[0].astype(BF16)

    @pl.when(used)
    def _():
        x = _load_slab(xg_ref, tm).astype(BF16)
        g = jnp.dot(x, wg_bf[...], preferred_element_type=F32)
        u = jnp.dot(x, wu_bf[...], preferred_element_type=F32)
        he = (g * jax.nn.sigmoid(g)) * u
        _store_slab(y_ref, jnp.dot(he.astype(BF16), wd_bf[...], preferred_element_type=F32))

    @pl.when(i >= n_tiles_ref[0])
    def _():
        y_ref[...] = jnp.zeros_like(y_ref)


def _experts(tile_e, n_tiles, xg, wg, wu, wd, *, tm):
    D = wg.shape[1]
    n_tiles_max = xg.shape[0] // (tm * SUBLANES)
    slab = lambda f: pl.BlockSpec((tm * SUBLANES, LANES), f)
    grid_spec = pltpu.PrefetchScalarGridSpec(
        num_scalar_prefetch=2,
        grid=(n_tiles_max,),
        in_specs=[slab(lambda i, te, nt: (jnp.minimum(i, nt[0] - 1), 0)),
                  pl.BlockSpec((1, D, D_EXPERT), lambda i, te, nt: (te[i], 0, 0)),
                  pl.BlockSpec((1, D, D_EXPERT), lambda i, te, nt: (te[i], 0, 0)),
                  pl.BlockSpec((1, D_EXPERT, D), lambda i, te, nt: (te[i], 0, 0))],
        out_specs=slab(lambda i, te, nt: (i, 0)),
        scratch_shapes=[pltpu.VMEM((D, D_EXPERT), BF16), pltpu.VMEM((D, D_EXPERT), BF16),
                        pltpu.VMEM((D_EXPERT, D), BF16)],
    )
    return pl.pallas_call(
        _experts_kernel,
        grid_spec=grid_spec,
        out_shape=jax.ShapeDtypeStruct(xg.shape, F32),
        compiler_params=_cparams(("arbitrary",)),
        name="experts",
    )(tile_e, n_tiles, xg, wg, wu, wd)


def _final_kernel(seg_ref, lo_ref, cnt_ref, h_ref, ys_hbm, route_ref, slot_ref, p_ref, wple_ref,
                  pn_ref, wpg_ref, fn_ref, o_ref, ybuf, sem, *, apply_final_norm):
    tm = h_ref.shape[0]
    i = pl.program_id(0)
    nt = pl.num_programs(0)
    slot = i % 2

    def fetch(tile, s):
        def run(e, carry):
            k = tile * N_EXPERTS + e

            def piece(off, n):
                _token_copy(ys_hbm, seg_ref[k] + off, ybuf.at[s], lo_ref[k] + off, n,
                            sem.at[s]).start()

            _for_each_piece(cnt_ref[k], piece)
            return carry

        lax.fori_loop(0, N_EXPERTS, run, 0)

    @pl.when(i == 0)
    def _():
        fetch(0, 0)

    @pl.when(i + 1 < nt)
    def _():
        fetch(i + 1, 1 - slot)

    pltpu.make_async_copy(ys_hbm.at[pl.ds(0, 2 * tm * SUBLANES), :], ybuf.at[slot],
                          sem.at[slot]).wait()
    y_sorted = _load_slab(ybuf.at[slot], 2 * tm).astype(BF16)
    route = route_ref[...]
    slots = slot_ref[...].astype(jnp.int32)
    col = lax.broadcasted_iota(jnp.int32, (tm, 2 * tm), 1)
    comb = jnp.where(col == slots[:, 0:1], route[:, 0:1],
                     jnp.where(col == slots[:, 1:2], route[:, 1:2], 0.0))
    h = h_ref[...] + jnp.dot(comb.astype(BF16), y_sorted, preferred_element_type=F32)
    e = _rms(jnp.dot(p_ref[0].astype(BF16), wple_ref[...], preferred_element_type=F32), pn_ref[...])
    gate = jax.nn.sigmoid(jnp.dot(h.astype(BF16), wpg_ref[...], preferred_element_type=F32))
    h = h + gate * e
    if apply_final_norm:
        h = _rms(h, fn_ref[...])
    o_ref[...] = h


def _final(h1, ys, slots, tables, route, p_all, layer, wple, ple_norm, wpg, final_norm, *, tm,
           apply_final_norm):
    T, D = h1.shape
    nt = T // tm
    row = lambda i, *_: (i, 0)
    fixed = lambda i, *_: (0, 0)
    full = lambda a: pl.BlockSpec(a.shape, fixed)
    pn = ple_norm.reshape(1, D)
    fn = final_norm.reshape(1, D)
    grid_spec = pltpu.PrefetchScalarGridSpec(
        num_scalar_prefetch=3,
        grid=(nt,),
        in_specs=[pl.BlockSpec((tm, D), row), pl.BlockSpec(memory_space=pl.ANY),
                  pl.BlockSpec((tm, ROUTE_W), row), pl.BlockSpec((tm, SLOT_W), row),
                  pl.BlockSpec((1, tm, p_all.shape[2]), lambda i, *_: (layer, i, 0)),
                  full(wple), full(pn), full(wpg), full(fn)],
        out_specs=pl.BlockSpec((tm, D), row),
        scratch_shapes=[pltpu.VMEM((2, 2 * tm * SUBLANES, LANES), F32),
                        pltpu.SemaphoreType.DMA((2,))],
    )
    return pl.pallas_call(
        functools.partial(_final_kernel, apply_final_norm=apply_final_norm),
        grid_spec=grid_spec,
        out_shape=jax.ShapeDtypeStruct((T, D), F32),
        compiler_params=_cparams(("arbitrary",)),
        name="final",
    )(*tables, h1, ys, route, slots, p_all, wple, pn, wpg, fn)


SPLIT_SIZES = (GLA_QK, GLA_QK, GLA_V, GLA_V, GLA_LOWRANK, DIFF_W, DIFF_W, DIFF_W)


def _layer(h2, p_all, layer, positions, B, S, lambda_init, apply_final_norm, attn_norm, w_in, w_a2, b_a,
           gla_norm, lq1, lk1, lq2, lk2, diff_subln, w_branch_a, w_branch_b, w_out, ffn_norm,
           w_rg, b_rg, w_re, b_re, w_gate, w_up, w_down, w_ple, ple_norm, w_ple_gate, final_norm,
           *, tm, ts, tq, tk, te):
    T, D = h2.shape
    assert D == SUBLANES * LANES, "token-slab gathers need one (8, 128) tile per token"
    n_mix = sum(SPLIT_SIZES)
    wgla = w_in[:, :C_GLA_END].astype(BF16)
    wdiff = w_in[:, C_GLA_END:n_mix].astype(BF16)
    wg = w_in[:, n_mix:].astype(BF16)

    gq, gk, gv, gr, la, dq, dk, dv = _inproj(h2, positions, attn_norm, wgla, wdiff,
                                             w_a2.astype(BF16), b_a, tm=tm)
    oa = _gla(gq, gk, gv, gr, la, gla_norm, B=B, S=S, ts=ts).reshape(T, GLA_V)
    ob = _diffattn(dq, dk, dv, lq1, lk1, lq2, lk2, diff_subln, B=B, S=S, tq=tq, tk=tk,
                   lambda_init=lambda_init).reshape(T, DIFF_W)

    wr = jnp.concatenate([w_rg, w_re, jnp.zeros((D, LANES - N_GROUPS - N_EXPERTS), F32)], axis=1)
    wr_hi = wr.astype(BF16)
    wr = jnp.concatenate([wr_hi, (wr - wr_hi.astype(F32)).astype(BF16)], axis=1)
    br = jnp.concatenate([b_rg, b_re, jnp.zeros((LANES - N_GROUPS - N_EXPERTS,), F32)]).reshape(1, LANES)
    h1, n2, route, cnt = _postmix(h2, oa, ob, attn_norm, wg, w_branch_a.astype(BF16),
                                  w_branch_b.astype(BF16), w_out.astype(BF16), ffn_norm, wr, br,
                                  tm=tm)

    xg, slots, tables, tile_e, n_tiles = _dispatch(n2, route, cnt, tm=tm, te=te)
    ys = _experts(tile_e, n_tiles, xg, w_gate, w_up, w_down, tm=te)
    return _final(h1, ys, slots, tables, route, p_all, layer, w_ple.astype(BF16), ple_norm,
                  w_ple_gate.astype(BF16), final_norm, tm=tm, apply_final_norm=apply_final_norm)


def _block(x, p, positions, attn_norm, w_in, w_a2, b_a, gla_norm, lambda_q1, lambda_k1, lambda_q2,
           lambda_k2, diff_subln, w_branch_a, w_branch_b, w_out, ffn_norm, w_router_group,
           b_router_group, w_router_expert, b_router_expert, w_gate, w_up, w_down, w_ple, ple_norm,
           w_ple_gate, final_norm, *, tm, ts, tq, tk, te):
    B, S, D = x.shape
    depth = w_in.shape[0]
    h = x.reshape(B * S, D)
    for i in range(depth):
        lambda_init = 0.8 - 0.6 * math.exp(-0.3 * i)
        h = _layer(h, p.reshape(depth, B * S, -1), i, positions, B, S, lambda_init, i == depth - 1,
                   attn_norm[i], w_in[i], w_a2[i], b_a[i], gla_norm[i], lambda_q1[i], lambda_k1[i],
                   lambda_q2[i], lambda_k2[i], diff_subln[i], w_branch_a[i], w_branch_b[i], w_out[i],
                   ffn_norm[i], w_router_group[i], b_router_group[i], w_router_expert[i],
                   b_router_expert[i], w_gate[i], w_up[i], w_down[i], w_ple[i], ple_norm[i],
                   w_ple_gate[i], final_norm, tm=tm, ts=ts, tq=tq, tk=tk, te=te)
    return h.reshape(B, S, D)


def kernel(x, p, positions, attn_norm, w_in, w_a2, b_a, gla_norm, lambda_q1, lambda_k1, lambda_q2, lambda_k2, diff_subln, w_branch_a, w_branch_b, w_out, ffn_norm, w_router_group, b_router_group, w_router_expert, b_router_expert, w_gate, w_up, w_down, w_ple, ple_norm, w_ple_gate, final_norm):
    return _block(x, p, positions, attn_norm, w_in, w_a2, b_a, gla_norm, lambda_q1, lambda_k1,
                  lambda_q2, lambda_k2, diff_subln, w_branch_a, w_branch_b, w_out, ffn_norm,
                  w_router_group, b_router_group, w_router_expert, b_router_expert, w_gate, w_up,
                  w_down, w_ple, ple_norm, w_ple_gate, final_norm, **_tile_sizes(x.shape[1]))


def _tile_sizes(seq):
    return dict(
        tm=512,
        ts=min(1024, seq),
        tq=min(512, seq),
        tk=min(512, seq),
        te=512,
    )
```

```python
import functools
import math

import jax
import jax.numpy as jnp
import numpy as np
from jax import lax
from jax.experimental import pallas as pl
from jax.experimental.pallas import tpu as pltpu

EPS = 1e-6

GLA_HEADS = 4
GLA_DK = 64
GLA_DV = 128
GLA_LOWRANK = 16
GLA_TAU = 16.0
GLA_CHUNK = 64
GLA_QK = GLA_HEADS * GLA_DK
GLA_V = GLA_HEADS * GLA_DV

DIFF_HEADS = 4
DIFF_DH = 64
DIFF_DV = 2 * DIFF_DH
DIFF_W = DIFF_HEADS * DIFF_DV
ROPE_THETA = 500000.0
ROPE_DIM = DIFF_DH // 4
ROPE_HALF = ROPE_DIM // 2

N_GROUPS = 4
EXPERTS_PER_GROUP = 8
N_EXPERTS = N_GROUPS * EXPERTS_PER_GROUP
D_EXPERT = 256

LANES = 128
MXU_N = 256
VMEM_LIMIT = 56 * 1024 * 1024

BF16 = jnp.bfloat16
F32 = jnp.float32
NT_DIMS = (((1,), (1,)), ((), ()))
TN_DIMS = (((0,), (0,)), ((), ()))
NEG_BIG = -1e30
LOG2_E = math.log2(math.e)


def _rms(x, g):
    return x * lax.rsqrt(jnp.mean(x * x, axis=-1, keepdims=True) + EPS) * g


def _cparams(semantics):
    return pltpu.CompilerParams(dimension_semantics=semantics, vmem_limit_bytes=VMEM_LIMIT)


SUBLANES = 8


def _store_slab(ref, x):
    n = x.shape[0]
    for j in range(SUBLANES):
        ref[pl.ds(j, n, stride=SUBLANES), :] = x[:, j * LANES:(j + 1) * LANES]


def _load_slab(ref, n):
    return jnp.concatenate([ref[pl.ds(j, n, stride=SUBLANES), :] for j in range(SUBLANES)], axis=1)


C_GQ = 0
C_GK = C_GQ + GLA_QK
C_GV = C_GK + GLA_QK
C_GR = C_GV + GLA_V
C_AL = C_GR + GLA_V
C_GLA_END = C_AL + GLA_LOWRANK
C_DQ = 0
C_DK = C_DQ + DIFF_W
C_DV = C_DK + DIFF_W
C_END = C_DV + DIFF_W


def _inproj_kernel(x_ref, pos_ref, g_ref, wgla_ref, wdiff_ref, wa2_ref, ba_ref, inv_ref, sel_ref,
                   one_ref,
                   gq_ref, gk_ref, gv_ref, gr_ref, la_ref, dq_ref, dk_ref, dv_ref):
    ang = inv_ref[...] * pos_ref[0].astype(F32)
    cs = jnp.concatenate([jnp.cos(ang), jnp.sin(ang)], axis=0)
    tabs = None
    rest = cs
    for _ in range(3):
        piece = rest.astype(BF16)
        rest = rest - piece.astype(F32)
        t = lax.dot_general(piece, sel_ref[...], TN_DIMS, preferred_element_type=F32)
        tabs = t if tabs is None else tabs + t
    cosf = tabs[:, :LANES] + one_ref[...]
    sneg = tabs[:, LANES:2 * LANES]
    spos = tabs[:, 2 * LANES:]

    nb = _rms(x_ref[...], g_ref[...]).astype(BF16)

    def proj_gla(c0, c1):
        return jnp.dot(nb, wgla_ref[:, c0:c1], preferred_element_type=F32)

    def proj(c0, c1):
        return jnp.dot(nb, wdiff_ref[:, c0:c1], preferred_element_type=F32)

    gq_ref[...] = proj_gla(C_GQ, C_GK) * (GLA_DK ** -0.5)
    gk_ref[...] = proj_gla(C_GK, C_GV)
    gv_ref[...] = proj_gla(C_GV, C_GR).astype(BF16)
    gr_ref[...] = proj_gla(C_GR, C_AL).astype(BF16)

    gal = proj_gla(C_AL, C_GLA_END).astype(BF16)
    a_logit = jnp.dot(gal, wa2_ref[...], preferred_element_type=F32) + ba_ref[...]
    la_ref[...] = (jnp.minimum(a_logit, 0.0) - jnp.log1p(jnp.exp(-jnp.abs(a_logit)))) / GLA_TAU

    def rope_store(c0, out_ref, scale):
        for j in range(DIFF_W // MXU_N):
            pair = proj(c0 + j * MXU_N, c0 + (j + 1) * MXU_N)
            for i in range(MXU_N // LANES):
                blk = pair[:, i * LANES:(i + 1) * LANES]
                rot = (blk * cosf + pltpu.roll(blk, LANES - ROPE_HALF, 1) * sneg
                       + pltpu.roll(blk, ROPE_HALF, 1) * spos)
                c = j * MXU_N + i * LANES
                out_ref[:, c:c + LANES] = (rot * scale).astype(BF16)

    rope_store(C_DQ, dq_ref, DIFF_DH ** -0.5 * LOG2_E)
    rope_store(C_DK, dk_ref, 1.0)
    dv_ref[...] = proj(C_DV, C_END).astype(BF16)


def _rope_tables():
    lane = np.arange(LANES)
    r = lane % DIFF_DH
    selc = np.zeros((2 * ROPE_HALF, LANES), np.float32)
    seln = np.zeros_like(selc)
    selp = np.zeros_like(selc)
    one = np.zeros((1, LANES), np.float32)
    for l in range(LANES):
        if r[l] < ROPE_DIM:
            selc[r[l] % ROPE_HALF, l] = 1.0
            if r[l] < ROPE_HALF:
                seln[ROPE_HALF + r[l], l] = -1.0
            else:
                selp[ROPE_HALF + r[l] - ROPE_HALF, l] = 1.0
        else:
            one[0, l] = 1.0
    sel = np.concatenate([selc, seln, selp], axis=1)
    return jnp.asarray(sel, dtype=BF16), jnp.asarray(one)


def _inproj(x2, positions, attn_norm, wgla, wdiff, wa2, b_a, *, tm):
    T, D = x2.shape
    nt = T // tm
    pos3 = positions.reshape(nt, 1, tm)
    inv = (ROPE_THETA ** (-jnp.arange(0, ROPE_DIM, 2, dtype=F32) / ROPE_DIM)).reshape(ROPE_HALF, 1)
    sel, one = _rope_tables()
    row = lambda i: (i, 0)
    fixed = lambda i: (0, 0)
    out_shapes = (
        jax.ShapeDtypeStruct((T, GLA_QK), F32), jax.ShapeDtypeStruct((T, GLA_QK), F32),
        jax.ShapeDtypeStruct((T, GLA_V), BF16), jax.ShapeDtypeStruct((T, GLA_V), BF16),
        jax.ShapeDtypeStruct((T, GLA_QK), F32),
        jax.ShapeDtypeStruct((T, DIFF_W), BF16), jax.ShapeDtypeStruct((T, DIFF_W), BF16),
        jax.ShapeDtypeStruct((T, DIFF_W), BF16),
    )
    return pl.pallas_call(
        _inproj_kernel,
        grid=(nt,),
        in_specs=[
            pl.BlockSpec((tm, D), row),
            pl.BlockSpec((1, 1, tm), lambda i: (i, 0, 0)),
            pl.BlockSpec((1, D), fixed),
            pl.BlockSpec((D, C_GLA_END), fixed),
            pl.BlockSpec((D, C_END), fixed),
            pl.BlockSpec((GLA_LOWRANK, GLA_QK), fixed),
            pl.BlockSpec((1, GLA_QK), fixed),
            pl.BlockSpec((ROPE_HALF, 1), fixed),
            pl.BlockSpec((2 * ROPE_HALF, 3 * LANES), fixed),
            pl.BlockSpec((1, LANES), fixed),
        ],
        out_specs=[pl.BlockSpec((tm, s.shape[1]), row) for s in out_shapes],
        out_shape=out_shapes,
        compiler_params=_cparams(("parallel",)),
        name="inproj",
    )(x2, pos3, attn_norm.reshape(1, D), wgla, wdiff, wa2, b_a.reshape(1, GLA_QK), inv, sel, one)


def _gla_kernel(gq_ref, gk_ref, gv_ref, gr_ref, la_ref, gn_ref, oa_ref, st_ref, *, n_chunks):
    @pl.when(pl.program_id(1) == 0)
    def _():
        st_ref[...] = jnp.zeros_like(st_ref)

    C = GLA_CHUNK
    tril = lax.broadcasted_iota(jnp.int32, (C, C), 0) >= lax.broadcasted_iota(jnp.int32, (C, C), 1)
    trilb = jnp.where(tril, 1.0, 0.0).astype(BF16)
    head_of_lane = lax.broadcasted_iota(jnp.int32, (1, GLA_QK), 1) // GLA_DK
    hmask = [head_of_lane == h for h in range(GLA_HEADS)]
    gn = gn_ref[...]

    def by_head(x):
        return jnp.concatenate([jnp.where(hmask[h], x, jnp.zeros_like(x))
                                for h in range(GLA_HEADS)], axis=0)

    chunks = range(n_chunks)
    sls = [pl.ds(c * C, C) for c in chunks]
    vs = [gv_ref[0, sl, :] for sl in sls]

    def cumsum(la):
        total, rest = None, la
        for _ in range(3):
            piece = rest.astype(BF16)
            rest = rest - piece.astype(F32)
            t = jnp.dot(trilb, piece, preferred_element_type=F32)
            total = t if total is None else total + t
        return total

    bs = [cumsum(la_ref[0, sl, :]) for sl in sls]
    b_lasts = [b[C - 1:C, :] for b in bs]
    q_heads = [by_head((gq_ref[0, sl, :] * jnp.exp(b)).astype(BF16)) for sl, b in zip(sls, bs)]
    k_invs = [(gk_ref[0, sl, :] * jnp.exp(-b)).astype(BF16) for sl, b in zip(sls, bs)]
    k_ends = [by_head((gk_ref[0, sl, :] * jnp.exp(bl - b)).astype(BF16))
              for sl, b, bl in zip(sls, bs, b_lasts)]
    decays = [jnp.exp(bl) for bl in b_lasts]
    atts = [lax.dot_general(qh, ki, NT_DIMS, preferred_element_type=F32)
            for qh, ki in zip(q_heads, k_invs)]
    v_heads = [jnp.concatenate([v[:, h * GLA_DV:(h + 1) * GLA_DV] for h in range(GLA_HEADS)], axis=0)
               for v in vs]
    us = [lax.dot_general(vh, ke, TN_DIMS, preferred_element_type=F32)
          for vh, ke in zip(v_heads, k_ends)]

    st = st_ref[...]
    states = []
    for c in chunks:
        states.append(st.astype(BF16))
        st = st * decays[c] + us[c]
    st_ref[...] = st

    inters = [lax.dot_general(qh, s_in, NT_DIMS, preferred_element_type=F32)
              for qh, s_in in zip(q_heads, states)]
    for c in chunks:
        for h in range(GLA_HEADS):
            rows = slice(h * C, (h + 1) * C)
            cols = slice(h * GLA_DV, (h + 1) * GLA_DV)
            a_h = jnp.where(tril, atts[c][rows], 0.0).astype(BF16)
            o_h = jnp.dot(a_h, vs[c][:, cols], preferred_element_type=F32) + inters[c][rows]
            y = _rms(o_h, gn)
            r = gr_ref[0, sls[c], cols].astype(F32)
            oa_ref[0, sls[c], cols] = (y * (r * jax.nn.sigmoid(r))).astype(BF16)


def _gla(gq, gk, gv, gr, la, gla_norm, *, B, S, ts):
    n_chunks = ts // GLA_CHUNK
    blk = lambda w: pl.BlockSpec((1, ts, w), lambda b, j: (b, j, 0))
    r3 = lambda a: a.reshape(B, S, a.shape[-1])
    return pl.pallas_call(
        functools.partial(_gla_kernel, n_chunks=n_chunks),
        grid=(B, S // ts),
        in_specs=[blk(GLA_QK), blk(GLA_QK), blk(GLA_V), blk(GLA_V), blk(GLA_QK),
                  pl.BlockSpec((1, GLA_DV), lambda b, j: (0, 0))],
        out_specs=blk(GLA_V),
        out_shape=jax.ShapeDtypeStruct((B, S, GLA_V), BF16),
        scratch_shapes=[pltpu.VMEM((GLA_DV, GLA_QK), F32)],
        compiler_params=_cparams(("parallel", "arbitrary")),
        name="gla",
    )(r3(gq), r3(gk), r3(gv), r3(gr), r3(la), gla_norm.reshape(1, GLA_DV))


ATTN_PAIR = 4


def _diffattn_kernel(dq_ref, dk_ref, dv_ref, lq1_ref, lk1_ref, lq2_ref, lk2_ref, sub_ref,
                     ob_ref, m_ref, l_ref, acc_ref, *, tq, tk, lambda_init):
    S = dq_ref.shape[1]
    nq = S // tq
    R = 2 * tq
    lam = (jnp.exp(jnp.sum(lq1_ref[...] * lk1_ref[...], axis=-1, keepdims=True))
           - jnp.exp(jnp.sum(lq2_ref[...] * lk2_ref[...], axis=-1, keepdims=True))
           + lambda_init)
    first_comp = lax.broadcasted_iota(jnp.int32, (1, DIFF_DV), 1) < DIFF_DH
    sub = sub_ref[...]

    pair = range(ATTN_PAIR)
    for hp in range(DIFF_HEADS // ATTN_PAIR):
        colss = [slice((hp * ATTN_PAIR + t) * DIFF_DV, (hp * ATTN_PAIR + t + 1) * DIFF_DV)
                 for t in pair]
        for qi in range(nq):
            qss = []
            for t in pair:
                qb = dq_ref[0, qi * tq:(qi + 1) * tq, colss[t]]
                zero = jnp.zeros_like(qb)
                qss.append(jnp.concatenate([jnp.where(first_comp, qb, zero),
                                            jnp.where(first_comp, zero, qb)], axis=0))
            m_ref[...] = jnp.full(m_ref.shape, NEG_BIG, F32)
            l_ref[...] = jnp.zeros(l_ref.shape, F32)
            acc_ref[...] = jnp.zeros(acc_ref.shape, F32)

            def scores(kstart, qss=qss, colss=colss):
                return [lax.dot_general(dk_ref[0, kstart:kstart + tk, colss[t]], qss[t], NT_DIMS,
                                        preferred_element_type=F32) for t in pair]

            def update(ss, kstart, masked, colss=colss, qi=qi):
                if masked:
                    kpos = kstart + lax.broadcasted_iota(jnp.int32, (tk, R), 0)
                    qpos = qi * tq + lax.broadcasted_iota(jnp.int32, (tk, R), 1) % tq
                    ss = [jnp.where(qpos >= kpos, s, NEG_BIG) for s in ss]
                m_prevs = [m_ref[t] for t in pair]
                m_news = [jnp.maximum(mp, jnp.max(s, axis=0, keepdims=True))
                          for mp, s in zip(m_prevs, ss)]
                alphas = [jnp.exp2(mp - mn) for mp, mn in zip(m_prevs, m_news)]
                ps = [jnp.exp2(s - mn) for s, mn in zip(ss, m_news)]
                for t in pair:
                    l_ref[t] = alphas[t] * l_ref[t] + jnp.sum(ps[t], axis=0, keepdims=True)
                    m_ref[t] = m_news[t]
                pvs = [lax.dot_general(dv_ref[0, kstart:kstart + tk, colss[t]], ps[t].astype(BF16),
                                       TN_DIMS, preferred_element_type=F32) for t in pair]
                for t in pair:
                    acc_ref[t] = alphas[t] * acc_ref[t] + pvs[t]

            n_blocks = (qi + 1) * tq // tk
            diag_from = qi * tq // tk
            s_next = scores(0)
            for kb_i in range(n_blocks):
                s_cur = s_next
                if kb_i + 1 < n_blocks:
                    s_next = scores((kb_i + 1) * tk)
                update(s_cur, kb_i * tk, kb_i >= diag_from)

            for t in pair:
                o_all = acc_ref[t] / l_ref[t]
                o = o_all[:, :tq] - lam * o_all[:, tq:]
                y = o * lax.rsqrt(jnp.mean(o * o, axis=0, keepdims=True) + EPS) * sub
                y = y * (1.0 - lambda_init)
                ob_ref[0, qi * tq:(qi + 1) * tq, colss[t]] = y.T.astype(BF16)


def _diffattn(dq, dk, dv, lq1, lk1, lq2, lk2, diff_subln, *, B, S, tq, tk, lambda_init):
    seq = pl.BlockSpec((1, S, DIFF_W), lambda b: (b, 0, 0))
    vec = lambda w: pl.BlockSpec((1, w), lambda b: (0, 0))
    r3 = lambda a: a.reshape(B, S, DIFF_W)
    return pl.pallas_call(
        functools.partial(_diffattn_kernel, tq=tq, tk=tk, lambda_init=lambda_init),
        grid=(B,),
        in_specs=[seq, seq, seq, vec(DIFF_DH), vec(DIFF_DH), vec(DIFF_DH), vec(DIFF_DH),
                  pl.BlockSpec((DIFF_DV, 1), lambda b: (0, 0))],
        out_specs=seq,
        out_shape=jax.ShapeDtypeStruct((B, S, DIFF_W), BF16),
        scratch_shapes=[pltpu.VMEM((ATTN_PAIR, 1, 2 * tq), F32), pltpu.VMEM((ATTN_PAIR, 1, 2 * tq), F32),
                        pltpu.VMEM((ATTN_PAIR, DIFF_DV, 2 * tq), F32)],
        compiler_params=_cparams(("parallel",)),
        name="diffattn",
    )(r3(dq), r3(dk), r3(dv), lq1.reshape(1, -1), lk1.reshape(1, -1), lq2.reshape(1, -1),
      lk2.reshape(1, -1), diff_subln.reshape(-1, 1))


ROUTE_W = 8


def _first_index_of(mask, lane):
    return jnp.min(jnp.where(mask, lane, LANES), axis=-1, keepdims=True)


POSTMIX_SPLIT = 2


def _postmix_kernel(x_ref, oa_ref, ob_ref, g_ref, wg_ref, wba_ref, wbb_ref, wo_ref, fn_ref,
                    wr_ref, br_ref, h_ref, n2_ref, route_ref, cnt_ref):
    D = x_ref.shape[1]
    rows_per = x_ref.shape[0] // POSTMIX_SPLIT
    groups = [slice(s * rows_per, (s + 1) * rows_per) for s in range(POSTMIX_SPLIT)]
    dot = functools.partial(jnp.dot, preferred_element_type=F32)
    xs = [x_ref[r, :] for r in groups]
    nbs = [_rms(x, g_ref[...]).astype(BF16) for x in xs]
    y_as = [dot(oa_ref[r, :], wba_ref[...]) for r in groups]
    g_as = [dot(nb, wg_ref[:, :D]) for nb in nbs]
    mergeds = [jax.nn.sigmoid(g) * y for g, y in zip(g_as, y_as)]
    y_bs = [dot(ob_ref[r, :], wbb_ref[...]) for r in groups]
    g_bs = [dot(nb, wg_ref[:, D:]) for nb in nbs]
    mergeds = [m + jax.nn.sigmoid(g) * y for m, g, y in zip(mergeds, g_bs, y_bs)]
    hs = [x + dot(m.astype(BF16), wo_ref[...]) for x, m in zip(xs, mergeds)]
    n2s = [_rms(h, fn_ref[...]) for h in hs]
    n2_his = [n2.astype(BF16) for n2 in n2s]
    n2_los = [(n2 - hi.astype(F32)).astype(BF16) for n2, hi in zip(n2s, n2_his)]
    hi_prods = [dot(hi, wr_ref[...]) for hi in n2_his]
    lgs = [hp[:, :LANES] + hp[:, LANES:] + dot(lo, wr_ref[:, :LANES]) + br_ref[...]
           for hp, lo in zip(hi_prods, n2_los)]
    cnt = None
    for r, h, n2_hi, lg in zip(groups, hs, n2_his, lgs):
        h_ref[r, :] = h
        n2_ref[r, :] = n2_hi
        rec, c = _route_top2(lg)
        route_ref[r, :] = rec
        cnt = c if cnt is None else cnt + c
    cnt_ref[0] = cnt


def _route_top2(lg):
    lane = lax.broadcasted_iota(jnp.int32, lg.shape, 1)
    is_g = lane < N_GROUPS
    g_max = jnp.max(jnp.where(is_g, lg, -jnp.inf), axis=-1, keepdims=True)
    g_exp = jnp.where(is_g, jnp.exp(lg - g_max), 0.0)
    g_prob = g_exp / jnp.sum(g_exp, axis=-1, keepdims=True)
    g_p = jnp.max(g_prob, axis=-1, keepdims=True)
    g_idx = _first_index_of(is_g & (g_prob == g_p), lane)

    e_lo = N_GROUPS + EXPERTS_PER_GROUP * g_idx
    is_e = (lane >= e_lo) & (lane < e_lo + EXPERTS_PER_GROUP)
    e_max = jnp.max(jnp.where(is_e, lg, -jnp.inf), axis=-1, keepdims=True)
    e_exp = jnp.where(is_e, jnp.exp(lg - e_max), 0.0)
    e_prob = e_exp / jnp.sum(e_exp, axis=-1, keepdims=True)
    p1 = jnp.max(jnp.where(is_e, e_prob, -1.0), axis=-1, keepdims=True)
    i1 = _first_index_of(is_e & (e_prob == p1), lane)
    rest = is_e & (lane != i1)
    p2 = jnp.max(jnp.where(rest, e_prob, -1.0), axis=-1, keepdims=True)
    i2 = _first_index_of(rest & (e_prob == p2), lane)
    den = p1 + p2
    w1 = g_p * (p1 / den)
    w2 = g_p * (p2 / den)
    rec = jnp.where(lane == 0, w1, 0.0)
    rec = jnp.where(lane == 1, w2, rec)
    rec = jnp.where(lane == 2, (i1 - N_GROUPS).astype(F32), rec)
    rec = jnp.where(lane == 3, (i2 - N_GROUPS).astype(F32), rec)
    chosen = (lane == i1 - N_GROUPS) | (lane == i2 - N_GROUPS)
    cnt = jnp.sum(chosen.astype(F32), axis=0, keepdims=True)
    return rec[:, :ROUTE_W], cnt


def _postmix(x2, oa, ob, attn_norm, wg, wba, wbb, wo, ffn_norm, wr, br, *, tm):
    T, D = x2.shape
    row = lambda i: (i, 0)
    fixed = lambda i: (0, 0)
    full = lambda a: pl.BlockSpec(a.shape, fixed)
    g = attn_norm.reshape(1, D)
    fn = ffn_norm.reshape(1, D)
    return pl.pallas_call(
        _postmix_kernel,
        grid=(T // tm,),
        in_specs=[pl.BlockSpec((tm, D), row), pl.BlockSpec((tm, GLA_V), row),
                  pl.BlockSpec((tm, DIFF_W), row), full(g), full(wg), full(wba), full(wbb),
                  full(wo), full(fn), full(wr), full(br)],
        out_specs=[pl.BlockSpec((tm, D), row), pl.BlockSpec((tm, D), row),
                   pl.BlockSpec((tm, ROUTE_W), row),
                   pl.BlockSpec((1, 1, LANES), lambda i: (i, 0, 0))],
        out_shape=(jax.ShapeDtypeStruct((T, D), F32), jax.ShapeDtypeStruct((T, D), BF16),
                   jax.ShapeDtypeStruct((T, ROUTE_W), F32),
                   jax.ShapeDtypeStruct((T // tm, 1, LANES), F32)),
        compiler_params=_cparams(("parallel",)),
        name="postmix",
    )(x2, oa, ob, g, wg, wba, wbb, wo, fn, wr, br)


DISPATCH_CHUNK = 16
DISPATCH_TAIL = (8, 4, 2, 1)
SLOT_W = 8


def _token_copy(src, src_tok, dst, dst_tok, n_tok, sem):
    return pltpu.make_async_copy(
        src.at[pl.ds(pl.multiple_of(src_tok * SUBLANES, SUBLANES), n_tok * SUBLANES), :],
        dst.at[pl.ds(pl.multiple_of(dst_tok * SUBLANES, SUBLANES), n_tok * SUBLANES), :], sem)


def _for_each_piece(length, fn):
    n_full = lax.shift_right_logical(length, DISPATCH_CHUNK.bit_length() - 1)

    def body(c, carry):
        fn(c * DISPATCH_CHUNK, DISPATCH_CHUNK)
        return carry

    lax.fori_loop(0, n_full, body, 0)
    off = n_full * DISPATCH_CHUNK
    for n in DISPATCH_TAIL:
        has = (length & n) != 0

        @pl.when(has)
        def _(off=off, n=n):
            fn(off, n)

        off = off + jnp.where(has, n, 0)


def _dispatch_kernel(seg_ref, lo_ref, cnt_ref, fill_ref, fill_len_ref, n_tiles_ref,
                     n2_ref, route_ref, lo_row_ref,
                     xg_hbm, slot_ref, xs_buf, zero_buf, sem, fill_sem):
    tm = n2_ref.shape[0]
    n_slab = 2 * tm * SUBLANES
    i = pl.program_id(0)
    nt = pl.num_programs(0)
    slot = i % 2

    @pl.when(i == 0)
    def _():
        zero_buf[...] = jnp.zeros_like(zero_buf)
        te = zero_buf.shape[0] // SUBLANES
        for wait in (False, True):
            def fill(e, carry, wait=wait):
                def piece(off, n):
                    cp = _token_copy(zero_buf, 0, xg_hbm, fill_ref[e] + off, n, fill_sem)
                    cp.wait() if wait else cp.start()
                _for_each_piece(fill_len_ref[e], piece)
                return carry
            lax.fori_loop(0, N_EXPERTS, fill, 0)

            def fill_tile(t, carry, wait=wait):
                cp = _token_copy(zero_buf, 0, xg_hbm, t * te, te, fill_sem)
                cp.wait() if wait else cp.start()
                return carry
            lax.fori_loop(n_tiles_ref[0], xg_hbm.shape[0] // zero_buf.shape[0], fill_tile, 0)

    route = route_ref[...]
    lane = lax.broadcasted_iota(jnp.int32, (tm, LANES), 1)
    oh1 = lane == route[:, 2:3].astype(jnp.int32)
    oh2 = lane == route[:, 3:4].astype(jnp.int32)
    both = jnp.where(oh1 | oh2, 1.0, 0.0).astype(BF16)
    earlier = (lax.broadcasted_iota(jnp.int32, (tm, tm), 0)
               > lax.broadcasted_iota(jnp.int32, (tm, tm), 1))
    rank = jnp.dot(jnp.where(earlier, 1.0, 0.0).astype(BF16), both, preferred_element_type=F32)

    def pick(onehot, v):
        return jnp.sum(jnp.where(onehot, v, 0.0), axis=-1, keepdims=True)

    r1 = pick(oh1, rank)
    r2 = pick(oh2, rank)
    q1 = pick(oh1, lo_row_ref[0]) + r1
    q2 = pick(oh2, lo_row_ref[0]) + r2
    rec = jnp.where(lane == 0, q1, jnp.where(lane == 1, q2, 0.0))
    slot_ref[...] = rec[:, :SLOT_W]

    rec_t = rec.T.astype(jnp.int32)
    slot_id = lax.broadcasted_iota(jnp.int32, (2 * tm, tm), 0)
    perm = jnp.where((slot_id == rec_t[0:1, :]) | (slot_id == rec_t[1:2, :]), 1.0, 0.0)
    xs = jnp.dot(perm.astype(BF16), n2_ref[...], preferred_element_type=F32)

    def wait_slot(s):
        pltpu.make_async_copy(xg_hbm.at[pl.ds(0, n_slab), :], xs_buf.at[s], sem.at[s]).wait()

    @pl.when(i >= 2)
    def _():
        wait_slot(slot)

    _store_slab(xs_buf.at[slot], xs)

    def send(e, carry):
        k = i * N_EXPERTS + e

        def piece(off, n):
            _token_copy(xs_buf.at[slot], lo_ref[k] + off, xg_hbm, seg_ref[k] + off, n,
                        sem.at[slot]).start()

        _for_each_piece(cnt_ref[k], piece)
        return carry

    lax.fori_loop(0, N_EXPERTS, send, 0, unroll=4)

    @pl.when(i == nt - 1)
    def _():
        wait_slot(slot)

        @pl.when(nt >= 2)
        def _():
            wait_slot(1 - slot)


def _dispatch(n2, route, cnt, *, tm, te):
    T, D = n2.shape
    nt = T // tm
    n_tiles_max = 2 * T // te + N_EXPERTS
    c = cnt[:, 0, :N_EXPERTS].astype(jnp.int32)
    count = jnp.sum(c, axis=0)
    padded = (count + te - 1) // te * te
    p_end = jnp.cumsum(padded)
    p_start = p_end - padded
    n_tiles = (p_end[-1] // te).astype(jnp.int32).reshape(1)
    tile_start = jnp.arange(n_tiles_max, dtype=jnp.int32) * te
    tile_e = jnp.sum(tile_start[:, None] >= p_end[None, :], axis=1, dtype=jnp.int32)
    tile_e = jnp.minimum(tile_e, jnp.sum(tile_start[n_tiles[0] - 1] >= p_end, dtype=jnp.int32))
    seg = p_start[None, :] + jnp.cumsum(c, axis=0) - c
    lo = jnp.cumsum(c, axis=1) - c
    lanes = lambda a: jnp.pad(a.astype(F32), ((0, 0), (0, LANES - N_EXPERTS))).reshape(nt, 1, LANES)
    flat = lambda a: a.reshape(-1).astype(jnp.int32)
    row = lambda i, *_: (i, 0)
    per_tile = pl.BlockSpec((1, 1, LANES), lambda i, *_: (i, 0, 0))
    grid_spec = pltpu.PrefetchScalarGridSpec(
        num_scalar_prefetch=6,
        grid=(nt,),
        in_specs=[pl.BlockSpec((tm, D), row), pl.BlockSpec((tm, ROUTE_W), row), per_tile],
        out_specs=[pl.BlockSpec(memory_space=pl.ANY), pl.BlockSpec((tm, SLOT_W), row)],
        scratch_shapes=[pltpu.VMEM((2, 2 * tm * SUBLANES, LANES), F32),
                        pltpu.VMEM((te * SUBLANES, LANES), F32),
                        pltpu.SemaphoreType.DMA((2,)), pltpu.SemaphoreType.DMA(())],
    )
    tables = (flat(seg), flat(lo), flat(c))
    xg, slots = pl.pallas_call(
        _dispatch_kernel,
        grid_spec=grid_spec,
        out_shape=(jax.ShapeDtypeStruct((n_tiles_max * te * SUBLANES, LANES), F32),
                   jax.ShapeDtypeStruct((T, SLOT_W), F32)),
        compiler_params=_cparams(("arbitrary",)),
        name="dispatch",
    )(*tables, flat(p_start + count), flat(padded - count), n_tiles, n2, route, lanes(lo))
    return xg, slots, tables, tile_e, n_tiles


def _experts_kernel(tile_e_ref, n_tiles_ref, xg_ref, wg_ref, wu_ref, wd_ref, y_ref,
                    wg_bf, wu_bf, wd_bf):
    tm = xg_ref.shape[0] // SUBLANES
    i = pl.program_id(0)
    used = i < n_tiles_ref[0]

    @pl.when(used & ((i == 0) | (tile_e_ref[i] != tile_e_ref[jnp.maximum(i - 1, 0)])))
    def _():
        wg_bf[...] = wg_ref[0].astype(BF16)
        wu_bf[...] = wu_ref[0].astype(BF16)
        wd_bf[...] = wd_ref[0].astype(BF16)

    @pl.when(used)
    def _():
        x = _load_slab(xg_ref, tm).astype(BF16)
        g = jnp.dot(x, wg_bf[...], preferred_element_type=F32)
        u = jnp.dot(x, wu_bf[...], preferred_element_type=F32)
        he = (g * jax.nn.sigmoid(g)) * u
        _store_slab(y_ref, jnp.dot(he.astype(BF16), wd_bf[...], preferred_element_type=F32))

    @pl.when(i >= n_tiles_ref[0])
    def _():
        y_ref[...] = jnp.zeros_like(y_ref)


def _experts(tile_e, n_tiles, xg, wg, wu, wd, *, tm):
    D = wg.shape[1]
    n_tiles_max = xg.shape[0] // (tm * SUBLANES)
    slab = lambda f: pl.BlockSpec((tm * SUBLANES, LANES), f)
    grid_spec = pltpu.PrefetchScalarGridSpec(
        num_scalar_prefetch=2,
        grid=(n_tiles_max,),
        in_specs=[slab(lambda i, te, nt: (jnp.minimum(i, nt[0] - 1), 0)),
                  pl.BlockSpec((1, D, D_EXPERT), lambda i, te, nt: (te[i], 0, 0)),
                  pl.BlockSpec((1, D, D_EXPERT), lambda i, te, nt: (te[i], 0, 0)),
                  pl.BlockSpec((1, D_EXPERT, D), lambda i, te, nt: (te[i], 0, 0))],
        out_specs=slab(lambda i, te, nt: (i, 0)),
        scratch_shapes=[pltpu.VMEM((D, D_EXPERT), BF16), pltpu.VMEM((D, D_EXPERT), BF16),
                        pltpu.VMEM((D_EXPERT, D), BF16)],
    )
    return pl.pallas_call(
        _experts_kernel,
        grid_spec=grid_spec,
        out_shape=jax.ShapeDtypeStruct(xg.shape, F32),
        compiler_params=_cparams(("arbitrary",)),
        name="experts",
    )(tile_e, n_tiles, xg, wg, wu, wd)


def _final_kernel(seg_ref, lo_ref, cnt_ref, h_ref, ys_hbm, route_ref, slot_ref, p_ref, wple_ref,
                  pn_ref, wpg_ref, fn_ref, o_ref, ybuf, sem, *, apply_final_norm):
    tm = h_ref.shape[0]
    i = pl.program_id(0)
    nt = pl.num_programs(0)
    slot = i % 2

    def fetch(tile, s):
        def run(e, carry):
            k = tile * N_EXPERTS + e

            def piece(off, n):
                _token_copy(ys_hbm, seg_ref[k] + off, ybuf.at[s], lo_ref[k] + off, n,
                            sem.at[s]).start()

            _for_each_piece(cnt_ref[k], piece)
            return carry

        lax.fori_loop(0, N_EXPERTS, run, 0, unroll=4)

    @pl.when(i == 0)
    def _():
        fetch(0, 0)

    @pl.when(i + 1 < nt)
    def _():
        fetch(i + 1, 1 - slot)

    pltpu.make_async_copy(ys_hbm.at[pl.ds(0, 2 * tm * SUBLANES), :], ybuf.at[slot],
                          sem.at[slot]).wait()
    y_sorted = _load_slab(ybuf.at[slot], 2 * tm).astype(BF16)
    route = route_ref[...]
    slots = slot_ref[...].astype(jnp.int32)
    col = lax.broadcasted_iota(jnp.int32, (tm, 2 * tm), 1)
    comb = jnp.where(col == slots[:, 0:1], route[:, 0:1],
                     jnp.where(col == slots[:, 1:2], route[:, 1:2], 0.0))
    h = h_ref[...] + jnp.dot(comb.astype(BF16), y_sorted, preferred_element_type=F32)
    e = _rms(jnp.dot(p_ref[0].astype(BF16), wple_ref[...], preferred_element_type=F32), pn_ref[...])
    gate = jax.nn.sigmoid(jnp.dot(h.astype(BF16), wpg_ref[...], preferred_element_type=F32))
    h = h + gate * e
    if apply_final_norm:
        h = _rms(h, fn_ref[...])
    o_ref[...] = h


def _final(h1, ys, slots, tables, route, p_all, layer, wple, ple_norm, wpg, final_norm, *, tm,
           apply_final_norm):
    T, D = h1.shape
    nt = T // tm
    row = lambda i, *_: (i, 0)
    fixed = lambda i, *_: (0, 0)
    full = lambda a: pl.BlockSpec(a.shape, fixed)
    pn = ple_norm.reshape(1, D)
    fn = final_norm.reshape(1, D)
    grid_spec = pltpu.PrefetchScalarGridSpec(
        num_scalar_prefetch=3,
        grid=(nt,),
        in_specs=[pl.BlockSpec((tm, D), row), pl.BlockSpec(memory_space=pl.ANY),
                  pl.BlockSpec((tm, ROUTE_W), row), pl.BlockSpec((tm, SLOT_W), row),
                  pl.BlockSpec((1, tm, p_all.shape[2]), lambda i, *_: (layer, i, 0)),
                  full(wple), full(pn), full(wpg), full(fn)],
        out_specs=pl.BlockSpec((tm, D), row),
        scratch_shapes=[pltpu.VMEM((2, 2 * tm * SUBLANES, LANES), F32),
                        pltpu.SemaphoreType.DMA((2,))],
    )
    return pl.pallas_call(
        functools.partial(_final_kernel, apply_final_norm=apply_final_norm),
        grid_spec=grid_spec,
        out_shape=jax.ShapeDtypeStruct((T, D), F32),
        compiler_params=_cparams(("arbitrary",)),
        name="final",
    )(*tables, h1, ys, route, slots, p_all, wple, pn, wpg, fn)


SPLIT_SIZES = (GLA_QK, GLA_QK, GLA_V, GLA_V, GLA_LOWRANK, DIFF_W, DIFF_W, DIFF_W)


def _layer(h2, p_all, layer, positions, B, S, lambda_init, apply_final_norm, attn_norm, w_in, w_a2, b_a,
           gla_norm, lq1, lk1, lq2, lk2, diff_subln, w_branch_a, w_branch_b, w_out, ffn_norm,
           w_rg, b_rg, w_re, b_re, w_gate, w_up, w_down, w_ple, ple_norm, w_ple_gate, final_norm,
           *, tm, ts, tq, tk, te):
    T, D = h2.shape
    assert D == SUBLANES * LANES, "token-slab gathers need one (8, 128) tile per token"
    n_mix = sum(SPLIT_SIZES)
    wgla = w_in[:, :C_GLA_END].astype(BF16)
    wdiff = w_in[:, C_GLA_END:n_mix].astype(BF16)
    wg = w_in[:, n_mix:].astype(BF16)

    gq, gk, gv, gr, la, dq, dk, dv = _inproj(h2, positions, attn_norm, wgla, wdiff,
                                             w_a2.astype(BF16), b_a, tm=tm)
    oa = _gla(gq, gk, gv, gr, la, gla_norm, B=B, S=S, ts=ts).reshape(T, GLA_V)
    ob = _diffattn(dq, dk, dv, lq1, lk1, lq2, lk2, diff_subln, B=B, S=S, tq=tq, tk=tk,
                   lambda_init=lambda_init).reshape(T, DIFF_W)

    wr = jnp.concatenate([w_rg, w_re, jnp.zeros((D, LANES - N_GROUPS - N_EXPERTS), F32)], axis=1)
    wr_hi = wr.astype(BF16)
    wr = jnp.concatenate([wr_hi, (wr - wr_hi.astype(F32)).astype(BF16)], axis=1)
    br = jnp.concatenate([b_rg, b_re, jnp.zeros((LANES - N_GROUPS - N_EXPERTS,), F32)]).reshape(1, LANES)
    h1, n2, route, cnt = _postmix(h2, oa, ob, attn_norm, wg, w_branch_a.astype(BF16),
                                  w_branch_b.astype(BF16), w_out.astype(BF16), ffn_norm, wr, br,
                                  tm=tm)

    xg, slots, tables, tile_e, n_tiles = _dispatch(n2, route, cnt, tm=tm, te=te)
    ys = _experts(tile_e, n_tiles, xg, w_gate, w_up, w_down, tm=te)
    return _final(h1, ys, slots, tables, route, p_all, layer, w_ple.astype(BF16), ple_norm,
                  w_ple_gate.astype(BF16), final_norm, tm=tm, apply_final_norm=apply_final_norm)


def _block(x, p, positions, attn_norm, w_in, w_a2, b_a, gla_norm, lambda_q1, lambda_k1, lambda_q2,
           lambda_k2, diff_subln, w_branch_a, w_branch_b, w_out, ffn_norm, w_router_group,
           b_router_group, w_router_expert, b_router_expert, w_gate, w_up, w_down, w_ple, ple_norm,
           w_ple_gate, final_norm, *, tm, ts, tq, tk, te):
    B, S, D = x.shape
    depth = w_in.shape[0]
    h = x.reshape(B * S, D)
    for i in range(depth):
        lambda_init = 0.8 - 0.6 * math.exp(-0.3 * i)
        h = _layer(h, p.reshape(depth, B * S, -1), i, positions, B, S, lambda_init, i == depth - 1,
                   attn_norm[i], w_in[i], w_a2[i], b_a[i], gla_norm[i], lambda_q1[i], lambda_k1[i],
                   lambda_q2[i], lambda_k2[i], diff_subln[i], w_branch_a[i], w_branch_b[i], w_out[i],
                   ffn_norm[i], w_router_group[i], b_router_group[i], w_router_expert[i],
                   b_router_expert[i], w_gate[i], w_up[i], w_down[i], w_ple[i], ple_norm[i],
                   w_ple_gate[i], final_norm, tm=tm, ts=ts, tq=tq, tk=tk, te=te)
    return h.reshape(B, S, D)


def kernel(x, p, positions, attn_norm, w_in, w_a2, b_a, gla_norm, lambda_q1, lambda_k1, lambda_q2, lambda_k2, diff_subln, w_branch_a, w_branch_b, w_out, ffn_norm, w_router_group, b_router_group, w_router_expert, b_router_expert, w_gate, w_up, w_down, w_ple, ple_norm, w_ple_gate, final_norm):
    return _block(x, p, positions, attn_norm, w_in, w_a2, b_a, gla_norm, lambda_q1, lambda_k1,
                  lambda_q2, lambda_k2, diff_subln, w_branch_a, w_branch_b, w_out, ffn_norm,
                  w_router_group, b_router_group, w_router_expert, b_router_expert, w_gate, w_up,
                  w_down, w_ple, ple_norm, w_ple_gate, final_norm, **_tile_sizes(x.shape[1]))


def _tile_sizes(seq):
    return dict(
        tm=512,
        ts=min(1024, seq),
        tq=min(512, seq),
        tk=min(512, seq),
        te=512,
    )
```

```python
import functools
import math

import jax
import jax.numpy as jnp
import numpy as np
from jax import lax
from jax.experimental import pallas as pl
from jax.experimental.pallas import tpu as pltpu

EPS = 1e-6

GLA_HEADS = 4
GLA_DK = 64
GLA_DV = 128
GLA_LOWRANK = 16
GLA_TAU = 16.0
GLA_CHUNK = 64
GLA_QK = GLA_HEADS * GLA_DK
GLA_V = GLA_HEADS * GLA_DV

DIFF_HEADS = 4
DIFF_DH = 64
DIFF_DV = 2 * DIFF_DH
DIFF_W = DIFF_HEADS * DIFF_DV
ROPE_THETA = 500000.0
ROPE_DIM = DIFF_DH // 4
ROPE_HALF = ROPE_DIM // 2

N_GROUPS = 4
EXPERTS_PER_GROUP = 8
N_EXPERTS = N_GROUPS * EXPERTS_PER_GROUP
D_EXPERT = 256

LANES = 128
MXU_N = 256
VMEM_LIMIT = 56 * 1024 * 1024

BF16 = jnp.bfloat16
F32 = jnp.float32
NT_DIMS = (((1,), (1,)), ((), ()))
TN_DIMS = (((0,), (0,)), ((), ()))
NEG_BIG = -1e30
LOG2_E = math.log2(math.e)


def _rms(x, g):
    return x * lax.rsqrt(jnp.mean(x * x, axis=-1, keepdims=True) + EPS) * g


def _cparams(semantics):
    return pltpu.CompilerParams(dimension_semantics=semantics, vmem_limit_bytes=VMEM_LIMIT)


SUBLANES = 8


def _store_slab(ref, x):
    n = x.shape[0]
    for j in range(SUBLANES):
        ref[pl.ds(j, n, stride=SUBLANES), :] = x[:, j * LANES:(j + 1) * LANES]


def _load_slab(ref, n):
    return jnp.concatenate([ref[pl.ds(j, n, stride=SUBLANES), :] for j in range(SUBLANES)], axis=1)


C_GQ = 0
C_GK = C_GQ + GLA_QK
C_GV = C_GK + GLA_QK
C_GR = C_GV + GLA_V
C_AL = C_GR + GLA_V
C_GLA_END = C_AL + GLA_LOWRANK
C_DQ = 0
C_DK = C_DQ + DIFF_W
C_DV = C_DK + DIFF_W
C_END = C_DV + DIFF_W


def _inproj_kernel(x_ref, pos_ref, g_ref, wgla_ref, wdiff_ref, wa2_ref, ba_ref, inv_ref, sel_ref,
                   one_ref,
                   gq_ref, gk_ref, gv_ref, gr_ref, la_ref, dq_ref, dk_ref, dv_ref):
    ang = inv_ref[...] * pos_ref[0].astype(F32)
    cs = jnp.concatenate([jnp.cos(ang), jnp.sin(ang)], axis=0)
    tabs = None
    rest = cs
    for _ in range(3):
        piece = rest.astype(BF16)
        rest = rest - piece.astype(F32)
        t = lax.dot_general(piece, sel_ref[...], TN_DIMS, preferred_element_type=F32)
        tabs = t if tabs is None else tabs + t
    cosf = tabs[:, :LANES] + one_ref[...]
    sneg = tabs[:, LANES:2 * LANES]
    spos = tabs[:, 2 * LANES:]

    nb = _rms(x_ref[...], g_ref[...]).astype(BF16)

    def proj_gla(c0, c1):
        return jnp.dot(nb, wgla_ref[:, c0:c1], preferred_element_type=F32)

    def proj(c0, c1):
        return jnp.dot(nb, wdiff_ref[:, c0:c1], preferred_element_type=F32)

    gq_ref[...] = proj_gla(C_GQ, C_GK) * (GLA_DK ** -0.5)
    gk_ref[...] = proj_gla(C_GK, C_GV)
    gv_ref[...] = proj_gla(C_GV, C_GR).astype(BF16)
    gr_ref[...] = proj_gla(C_GR, C_AL).astype(BF16)

    gal = proj_gla(C_AL, C_GLA_END).astype(BF16)
    a_logit = jnp.dot(gal, wa2_ref[...], preferred_element_type=F32) + ba_ref[...]
    la_ref[...] = (jnp.minimum(a_logit, 0.0) - jnp.log1p(jnp.exp(-jnp.abs(a_logit)))) / GLA_TAU

    def rope_store(c0, out_ref, scale):
        for j in range(DIFF_W // MXU_N):
            pair = proj(c0 + j * MXU_N, c0 + (j + 1) * MXU_N)
            for i in range(MXU_N // LANES):
                blk = pair[:, i * LANES:(i + 1) * LANES]
                rot = (blk * cosf + pltpu.roll(blk, LANES - ROPE_HALF, 1) * sneg
                       + pltpu.roll(blk, ROPE_HALF, 1) * spos)
                c = j * MXU_N + i * LANES
                out_ref[:, c:c + LANES] = (rot * scale).astype(BF16)

    rope_store(C_DQ, dq_ref, DIFF_DH ** -0.5 * LOG2_E)
    rope_store(C_DK, dk_ref, 1.0)
    dv_ref[...] = proj(C_DV, C_END).astype(BF16)


def _rope_tables():
    lane = np.arange(LANES)
    r = lane % DIFF_DH
    selc = np.zeros((2 * ROPE_HALF, LANES), np.float32)
    seln = np.zeros_like(selc)
    selp = np.zeros_like(selc)
    one = np.zeros((1, LANES), np.float32)
    for l in range(LANES):
        if r[l] < ROPE_DIM:
            selc[r[l] % ROPE_HALF, l] = 1.0
            if r[l] < ROPE_HALF:
                seln[ROPE_HALF + r[l], l] = -1.0
            else:
                selp[ROPE_HALF + r[l] - ROPE_HALF, l] = 1.0
        else:
            one[0, l] = 1.0
    sel = np.concatenate([selc, seln, selp], axis=1)
    return jnp.asarray(sel, dtype=BF16), jnp.asarray(one)


def _inproj(x2, positions, attn_norm, wgla, wdiff, wa2, b_a, *, tm):
    T, D = x2.shape
    nt = T // tm
    pos3 = positions.reshape(nt, 1, tm)
    inv = (ROPE_THETA ** (-jnp.arange(0, ROPE_DIM, 2, dtype=F32) / ROPE_DIM)).reshape(ROPE_HALF, 1)
    sel, one = _rope_tables()
    row = lambda i: (i, 0)
    fixed = lambda i: (0, 0)
    out_shapes = (
        jax.ShapeDtypeStruct((T, GLA_QK), F32), jax.ShapeDtypeStruct((T, GLA_QK), F32),
        jax.ShapeDtypeStruct((T, GLA_V), BF16), jax.ShapeDtypeStruct((T, GLA_V), BF16),
        jax.ShapeDtypeStruct((T, GLA_QK), F32),
        jax.ShapeDtypeStruct((T, DIFF_W), BF16), jax.ShapeDtypeStruct((T, DIFF_W), BF16),
        jax.ShapeDtypeStruct((T, DIFF_W), BF16),
    )
    return pl.pallas_call(
        _inproj_kernel,
        grid=(nt,),
        in_specs=[
            pl.BlockSpec((tm, D), row),
            pl.BlockSpec((1, 1, tm), lambda i: (i, 0, 0)),
            pl.BlockSpec((1, D), fixed),
            pl.BlockSpec((D, C_GLA_END), fixed),
            pl.BlockSpec((D, C_END), fixed),
            pl.BlockSpec((GLA_LOWRANK, GLA_QK), fixed),
            pl.BlockSpec((1, GLA_QK), fixed),
            pl.BlockSpec((ROPE_HALF, 1), fixed),
            pl.BlockSpec((2 * ROPE_HALF, 3 * LANES), fixed),
            pl.BlockSpec((1, LANES), fixed),
        ],
        out_specs=[pl.BlockSpec((tm, s.shape[1]), row) for s in out_shapes],
        out_shape=out_shapes,
        compiler_params=_cparams(("parallel",)),
        name="inproj",
    )(x2, pos3, attn_norm.reshape(1, D), wgla, wdiff, wa2, b_a.reshape(1, GLA_QK), inv, sel, one)


def _gla_kernel(gq_ref, gk_ref, gv_ref, gr_ref, la_ref, gn_ref, oa_ref, st_ref, *, n_chunks):
    @pl.when(pl.program_id(1) == 0)
    def _():
        st_ref[...] = jnp.zeros_like(st_ref)

    C = GLA_CHUNK
    tril = lax.broadcasted_iota(jnp.int32, (C, C), 0) >= lax.broadcasted_iota(jnp.int32, (C, C), 1)
    trilb = jnp.where(tril, 1.0, 0.0).astype(BF16)
    head_of_lane = lax.broadcasted_iota(jnp.int32, (1, GLA_QK), 1) // GLA_DK
    hmask = [head_of_lane == h for h in range(GLA_HEADS)]
    gn = gn_ref[...]

    def by_head(x):
        return jnp.concatenate([jnp.where(hmask[h], x, jnp.zeros_like(x))
                                for h in range(GLA_HEADS)], axis=0)

    chunks = range(n_chunks)
    sls = [pl.ds(c * C, C) for c in chunks]
    vs = [gv_ref[0, sl, :] for sl in sls]

    def cumsum(la):
        total, rest = None, la
        for _ in range(3):
            piece = rest.astype(BF16)
            rest = rest - piece.astype(F32)
            t = jnp.dot(trilb, piece, preferred_element_type=F32)
            total = t if total is None else total + t
        return total

    bs = [cumsum(la_ref[0, sl, :]) for sl in sls]
    b_lasts = [b[C - 1:C, :] for b in bs]
    q_heads = [by_head((gq_ref[0, sl, :] * jnp.exp(b)).astype(BF16)) for sl, b in zip(sls, bs)]
    k_invs = [(gk_ref[0, sl, :] * jnp.exp(-b)).astype(BF16) for sl, b in zip(sls, bs)]
    k_ends = [by_head((gk_ref[0, sl, :] * jnp.exp(bl - b)).astype(BF16))
              for sl, b, bl in zip(sls, bs, b_lasts)]
    decays = [jnp.exp(bl) for bl in b_lasts]
    atts = [lax.dot_general(qh, ki, NT_DIMS, preferred_element_type=F32)
            for qh, ki in zip(q_heads, k_invs)]
    v_heads = [jnp.concatenate([v[:, h * GLA_DV:(h + 1) * GLA_DV] for h in range(GLA_HEADS)], axis=0)
               for v in vs]
    us = [lax.dot_general(vh, ke, TN_DIMS, preferred_element_type=F32)
          for vh, ke in zip(v_heads, k_ends)]

    st = st_ref[...]
    states = []
    for c in chunks:
        states.append(st.astype(BF16))
        st = st * decays[c] + us[c]
    st_ref[...] = st

    inters = [lax.dot_general(qh, s_in, NT_DIMS, preferred_element_type=F32)
              for qh, s_in in zip(q_heads, states)]
    for c in chunks:
        for h in range(GLA_HEADS):
            rows = slice(h * C, (h + 1) * C)
            cols = slice(h * GLA_DV, (h + 1) * GLA_DV)
            a_h = jnp.where(tril, atts[c][rows], 0.0).astype(BF16)
            o_h = jnp.dot(a_h, vs[c][:, cols], preferred_element_type=F32) + inters[c][rows]
            y = _rms(o_h, gn)
            r = gr_ref[0, sls[c], cols].astype(F32)
            oa_ref[0, sls[c], cols] = (y * (r * jax.nn.sigmoid(r))).astype(BF16)


def _gla(gq, gk, gv, gr, la, gla_norm, *, B, S, ts):
    n_chunks = ts // GLA_CHUNK
    blk = lambda w: pl.BlockSpec((1, ts, w), lambda b, j: (b, j, 0))
    r3 = lambda a: a.reshape(B, S, a.shape[-1])
    return pl.pallas_call(
        functools.partial(_gla_kernel, n_chunks=n_chunks),
        grid=(B, S // ts),
        in_specs=[blk(GLA_QK), blk(GLA_QK), blk(GLA_V), blk(GLA_V), blk(GLA_QK),
                  pl.BlockSpec((1, GLA_DV), lambda b, j: (0, 0))],
        out_specs=blk(GLA_V),
        out_shape=jax.ShapeDtypeStruct((B, S, GLA_V), BF16),
        scratch_shapes=[pltpu.VMEM((GLA_DV, GLA_QK), F32)],
        compiler_params=_cparams(("parallel", "arbitrary")),
        name="gla",
    )(r3(gq), r3(gk), r3(gv), r3(gr), r3(la), gla_norm.reshape(1, GLA_DV))


ATTN_PAIR = 4


def _diffattn_kernel(dq_ref, dk_ref, dv_ref, lq1_ref, lk1_ref, lq2_ref, lk2_ref, sub_ref,
                     ob_ref, m_ref, l_ref, acc_ref, *, tq, tk, lambda_init):
    S = dq_ref.shape[1]
    nq = S // tq
    R = 2 * tq
    lam = (jnp.exp(jnp.sum(lq1_ref[...] * lk1_ref[...], axis=-1, keepdims=True))
           - jnp.exp(jnp.sum(lq2_ref[...] * lk2_ref[...], axis=-1, keepdims=True))
           + lambda_init)
    first_comp = lax.broadcasted_iota(jnp.int32, (1, DIFF_DV), 1) < DIFF_DH
    sub = sub_ref[...]

    pair = range(ATTN_PAIR)
    for hp in range(DIFF_HEADS // ATTN_PAIR):
        colss = [slice((hp * ATTN_PAIR + t) * DIFF_DV, (hp * ATTN_PAIR + t + 1) * DIFF_DV)
                 for t in pair]
        for qi in range(nq):
            qss = []
            for t in pair:
                qb = dq_ref[0, qi * tq:(qi + 1) * tq, colss[t]]
                zero = jnp.zeros_like(qb)
                qss.append(jnp.concatenate([jnp.where(first_comp, qb, zero),
                                            jnp.where(first_comp, zero, qb)], axis=0))
            m_ref[...] = jnp.full(m_ref.shape, NEG_BIG, F32)
            l_ref[...] = jnp.zeros(l_ref.shape, F32)
            acc_ref[...] = jnp.zeros(acc_ref.shape, F32)

            def scores(kstart, qss=qss, colss=colss):
                return [lax.dot_general(dk_ref[0, kstart:kstart + tk, colss[t]], qss[t], NT_DIMS,
                                        preferred_element_type=F32) for t in pair]

            def update(ss, kstart, masked, colss=colss, qi=qi):
                if masked:
                    kpos = kstart + lax.broadcasted_iota(jnp.int32, (tk, R), 0)
                    qpos = qi * tq + lax.broadcasted_iota(jnp.int32, (tk, R), 1) % tq
                    ss = [jnp.where(qpos >= kpos, s, NEG_BIG) for s in ss]
                m_prevs = [m_ref[t] for t in pair]
                m_news = [jnp.maximum(mp, jnp.max(s, axis=0, keepdims=True))
                          for mp, s in zip(m_prevs, ss)]
                alphas = [jnp.exp2(mp - mn) for mp, mn in zip(m_prevs, m_news)]
                ps = [jnp.exp2(s - mn) for s, mn in zip(ss, m_news)]
                for t in pair:
                    l_ref[t] = alphas[t] * l_ref[t] + jnp.sum(ps[t], axis=0, keepdims=True)
                    m_ref[t] = m_news[t]
                pvs = [lax.dot_general(dv_ref[0, kstart:kstart + tk, colss[t]], ps[t].astype(BF16),
                                       TN_DIMS, preferred_element_type=F32) for t in pair]
                for t in pair:
                    acc_ref[t] = alphas[t] * acc_ref[t] + pvs[t]

            n_blocks = (qi + 1) * tq // tk
            diag_from = qi * tq // tk
            s_next = scores(0)
            for kb_i in range(n_blocks):
                s_cur = s_next
                if kb_i + 1 < n_blocks:
                    s_next = scores((kb_i + 1) * tk)
                update(s_cur, kb_i * tk, kb_i >= diag_from)

            for t in pair:
                o_all = acc_ref[t] / l_ref[t]
                o = o_all[:, :tq] - lam * o_all[:, tq:]
                y = o * lax.rsqrt(jnp.mean(o * o, axis=0, keepdims=True) + EPS) * sub
                y = y * (1.0 - lambda_init)
                ob_ref[0, qi * tq:(qi + 1) * tq, colss[t]] = y.T.astype(BF16)


def _diffattn(dq, dk, dv, lq1, lk1, lq2, lk2, diff_subln, *, B, S, tq, tk, lambda_init):
    seq = pl.BlockSpec((1, S, DIFF_W), lambda b: (b, 0, 0))
    vec = lambda w: pl.BlockSpec((1, w), lambda b: (0, 0))
    r3 = lambda a: a.reshape(B, S, DIFF_W)
    return pl.pallas_call(
        functools.partial(_diffattn_kernel, tq=tq, tk=tk, lambda_init=lambda_init),
        grid=(B,),
        in_specs=[seq, seq, seq, vec(DIFF_DH), vec(DIFF_DH), vec(DIFF_DH), vec(DIFF_DH),
                  pl.BlockSpec((DIFF_DV, 1), lambda b: (0, 0))],
        out_specs=seq,
        out_shape=jax.ShapeDtypeStruct((B, S, DIFF_W), BF16),
        scratch_shapes=[pltpu.VMEM((ATTN_PAIR, 1, 2 * tq), F32), pltpu.VMEM((ATTN_PAIR, 1, 2 * tq), F32),
                        pltpu.VMEM((ATTN_PAIR, DIFF_DV, 2 * tq), F32)],
        compiler_params=_cparams(("parallel",)),
        name="diffattn",
    )(r3(dq), r3(dk), r3(dv), lq1.reshape(1, -1), lk1.reshape(1, -1), lq2.reshape(1, -1),
      lk2.reshape(1, -1), diff_subln.reshape(-1, 1))


ROUTE_W = 8


def _first_index_of(mask, lane):
    return jnp.min(jnp.where(mask, lane, LANES), axis=-1, keepdims=True)


POSTMIX_SPLIT = 2


def _postmix_kernel(x_ref, oa_ref, ob_ref, g_ref, wg_ref, wba_ref, wbb_ref, wo_ref, fn_ref,
                    wr_ref, br_ref, h_ref, n2_ref, route_ref, cnt_ref):
    D = x_ref.shape[1]
    rows_per = x_ref.shape[0] // POSTMIX_SPLIT
    groups = [slice(s * rows_per, (s + 1) * rows_per) for s in range(POSTMIX_SPLIT)]
    dot = functools.partial(jnp.dot, preferred_element_type=F32)
    xs = [x_ref[r, :] for r in groups]
    nbs = [_rms(x, g_ref[...]).astype(BF16) for x in xs]
    y_as = [dot(oa_ref[r, :], wba_ref[...]) for r in groups]
    g_as = [dot(nb, wg_ref[:, :D]) for nb in nbs]
    mergeds = [jax.nn.sigmoid(g) * y for g, y in zip(g_as, y_as)]
    y_bs = [dot(ob_ref[r, :], wbb_ref[...]) for r in groups]
    g_bs = [dot(nb, wg_ref[:, D:]) for nb in nbs]
    mergeds = [m + jax.nn.sigmoid(g) * y for m, g, y in zip(mergeds, g_bs, y_bs)]
    hs = [x + dot(m.astype(BF16), wo_ref[...]) for x, m in zip(xs, mergeds)]
    n2s = [_rms(h, fn_ref[...]) for h in hs]
    n2_his = [n2.astype(BF16) for n2 in n2s]
    n2_los = [(n2 - hi.astype(F32)).astype(BF16) for n2, hi in zip(n2s, n2_his)]
    hi_prods = [dot(hi, wr_ref[...]) for hi in n2_his]
    lgs = [hp[:, :LANES] + hp[:, LANES:] + dot(lo, wr_ref[:, :LANES]) + br_ref[...]
           for hp, lo in zip(hi_prods, n2_los)]
    cnt = None
    for r, h, n2_hi, lg in zip(groups, hs, n2_his, lgs):
        h_ref[r, :] = h
        n2_ref[r, :] = n2_hi
        rec, c = _route_top2(lg)
        route_ref[r, :] = rec
        cnt = c if cnt is None else cnt + c
    cnt_ref[0] = cnt


def _route_top2(lg):
    lane = lax.broadcasted_iota(jnp.int32, lg.shape, 1)
    is_g = lane < N_GROUPS
    g_max = jnp.max(jnp.where(is_g, lg, -jnp.inf), axis=-1, keepdims=True)
    g_exp = jnp.where(is_g, jnp.exp(lg - g_max), 0.0)
    g_prob = g_exp / jnp.sum(g_exp, axis=-1, keepdims=True)
    g_p = jnp.max(g_prob, axis=-1, keepdims=True)
    g_idx = _first_index_of(is_g & (g_prob == g_p), lane)

    e_lo = N_GROUPS + EXPERTS_PER_GROUP * g_idx
    is_e = (lane >= e_lo) & (lane < e_lo + EXPERTS_PER_GROUP)
    e_max = jnp.max(jnp.where(is_e, lg, -jnp.inf), axis=-1, keepdims=True)
    e_exp = jnp.where(is_e, jnp.exp(lg - e_max), 0.0)
    e_prob = e_exp / jnp.sum(e_exp, axis=-1, keepdims=True)
    p1 = jnp.max(jnp.where(is_e, e_prob, -1.0), axis=-1, keepdims=True)
    i1 = _first_index_of(is_e & (e_prob == p1), lane)
    rest = is_e & (lane != i1)
    p2 = jnp.max(jnp.where(rest, e_prob, -1.0), axis=-1, keepdims=True)
    i2 = _first_index_of(rest & (e_prob == p2), lane)
    den = p1 + p2
    w1 = g_p * (p1 / den)
    w2 = g_p * (p2 / den)
    rec = jnp.where(lane == 0, w1, 0.0)
    rec = jnp.where(lane == 1, w2, rec)
    rec = jnp.where(lane == 2, (i1 - N_GROUPS).astype(F32), rec)
    rec = jnp.where(lane == 3, (i2 - N_GROUPS).astype(F32), rec)
    chosen = (lane == i1 - N_GROUPS) | (lane == i2 - N_GROUPS)
    cnt = jnp.sum(chosen.astype(F32), axis=0, keepdims=True)
    return rec[:, :ROUTE_W], cnt


def _postmix(x2, oa, ob, attn_norm, wg, wba, wbb, wo, ffn_norm, wr, br, *, tm):
    T, D = x2.shape
    row = lambda i: (i, 0)
    fixed = lambda i: (0, 0)
    full = lambda a: pl.BlockSpec(a.shape, fixed)
    g = attn_norm.reshape(1, D)
    fn = ffn_norm.reshape(1, D)
    return pl.pallas_call(
        _postmix_kernel,
        grid=(T // tm,),
        in_specs=[pl.BlockSpec((tm, D), row), pl.BlockSpec((tm, GLA_V), row),
                  pl.BlockSpec((tm, DIFF_W), row), full(g), full(wg), full(wba), full(wbb),
                  full(wo), full(fn), full(wr), full(br)],
        out_specs=[pl.BlockSpec((tm, D), row), pl.BlockSpec((tm, D), row),
                   pl.BlockSpec((tm, ROUTE_W), row),
                   pl.BlockSpec((1, 1, LANES), lambda i: (i, 0, 0))],
        out_shape=(jax.ShapeDtypeStruct((T, D), F32), jax.ShapeDtypeStruct((T, D), BF16),
                   jax.ShapeDtypeStruct((T, ROUTE_W), F32),
                   jax.ShapeDtypeStruct((T // tm, 1, LANES), F32)),
        compiler_params=_cparams(("parallel",)),
        name="postmix",
    )(x2, oa, ob, g, wg, wba, wbb, wo, fn, wr, br)


DISPATCH_CHUNK = 16
DISPATCH_TAIL = (8, 4, 2, 1)
RUN_LOOP_UNROLL = 16
SLOT_W = 8


def _token_copy(src, src_tok, dst, dst_tok, n_tok, sem):
    return pltpu.make_async_copy(
        src.at[pl.ds(pl.multiple_of(src_tok * SUBLANES, SUBLANES), n_tok * SUBLANES), :],
        dst.at[pl.ds(pl.multiple_of(dst_tok * SUBLANES, SUBLANES), n_tok * SUBLANES), :], sem)


def _for_each_piece(length, fn):
    n_full = lax.shift_right_logical(length, DISPATCH_CHUNK.bit_length() - 1)

    def body(c, carry):
        fn(c * DISPATCH_CHUNK, DISPATCH_CHUNK)
        return carry

    lax.fori_loop(0, n_full, body, 0)
    off = n_full * DISPATCH_CHUNK
    for n in DISPATCH_TAIL:
        has = (length & n) != 0

        @pl.when(has)
        def _(off=off, n=n):
            fn(off, n)

        off = off + jnp.where(has, n, 0)


def _dispatch_kernel(seg_ref, lo_ref, cnt_ref, fill_ref, fill_len_ref, n_tiles_ref,
                     n2_ref, route_ref, lo_row_ref,
                     xg_hbm, slot_ref, xs_buf, zero_buf, sem, fill_sem):
    tm = n2_ref.shape[0]
    n_slab = 2 * tm * SUBLANES
    i = pl.program_id(0)
    nt = pl.num_programs(0)
    slot = i % 2

    @pl.when(i == 0)
    def _():
        zero_buf[...] = jnp.zeros_like(zero_buf)
        te = zero_buf.shape[0] // SUBLANES
        for wait in (False, True):
            def fill(e, carry, wait=wait):
                def piece(off, n):
                    cp = _token_copy(zero_buf, 0, xg_hbm, fill_ref[e] + off, n, fill_sem)
                    cp.wait() if wait else cp.start()
                _for_each_piece(fill_len_ref[e], piece)
                return carry
            lax.fori_loop(0, N_EXPERTS, fill, 0)

            def fill_tile(t, carry, wait=wait):
                cp = _token_copy(zero_buf, 0, xg_hbm, t * te, te, fill_sem)
                cp.wait() if wait else cp.start()
                return carry
            lax.fori_loop(n_tiles_ref[0], xg_hbm.shape[0] // zero_buf.shape[0], fill_tile, 0)

    route = route_ref[...]
    lane = lax.broadcasted_iota(jnp.int32, (tm, LANES), 1)
    oh1 = lane == route[:, 2:3].astype(jnp.int32)
    oh2 = lane == route[:, 3:4].astype(jnp.int32)
    both = jnp.where(oh1 | oh2, 1.0, 0.0).astype(BF16)
    earlier = (lax.broadcasted_iota(jnp.int32, (tm, tm), 0)
               > lax.broadcasted_iota(jnp.int32, (tm, tm), 1))
    rank = jnp.dot(jnp.where(earlier, 1.0, 0.0).astype(BF16), both, preferred_element_type=F32)

    def pick(onehot, v):
        return jnp.sum(jnp.where(onehot, v, 0.0), axis=-1, keepdims=True)

    r1 = pick(oh1, rank)
    r2 = pick(oh2, rank)
    q1 = pick(oh1, lo_row_ref[0]) + r1
    q2 = pick(oh2, lo_row_ref[0]) + r2
    rec = jnp.where(lane == 0, q1, jnp.where(lane == 1, q2, 0.0))
    slot_ref[...] = rec[:, :SLOT_W]

    rec_t = rec.T.astype(jnp.int32)
    slot_id = lax.broadcasted_iota(jnp.int32, (2 * tm, tm), 0)
    perm = jnp.where((slot_id == rec_t[0:1, :]) | (slot_id == rec_t[1:2, :]), 1.0, 0.0)
    xs = jnp.dot(perm.astype(BF16), n2_ref[...], preferred_element_type=F32)

    def wait_slot(s):
        pltpu.make_async_copy(xg_hbm.at[pl.ds(0, n_slab), :], xs_buf.at[s], sem.at[s]).wait()

    @pl.when(i >= 2)
    def _():
        wait_slot(slot)

    _store_slab(xs_buf.at[slot], xs)

    def send(e, carry):
        k = i * N_EXPERTS + e

        def piece(off, n):
            _token_copy(xs_buf.at[slot], lo_ref[k] + off, xg_hbm, seg_ref[k] + off, n,
                        sem.at[slot]).start()

        _for_each_piece(cnt_ref[k], piece)
        return carry

    lax.fori_loop(0, N_EXPERTS, send, 0, unroll=RUN_LOOP_UNROLL)

    @pl.when(i == nt - 1)
    def _():
        wait_slot(slot)

        @pl.when(nt >= 2)
        def _():
            wait_slot(1 - slot)


def _dispatch(n2, route, cnt, *, tm, te):
    T, D = n2.shape
    nt = T // tm
    n_tiles_max = 2 * T // te + N_EXPERTS
    c = cnt[:, 0, :N_EXPERTS].astype(jnp.int32)
    count = jnp.sum(c, axis=0)
    padded = (count + te - 1) // te * te
    p_end = jnp.cumsum(padded)
    p_start = p_end - padded
    n_tiles = (p_end[-1] // te).astype(jnp.int32).reshape(1)
    tile_start = jnp.arange(n_tiles_max, dtype=jnp.int32) * te
    tile_e = jnp.sum(tile_start[:, None] >= p_end[None, :], axis=1, dtype=jnp.int32)
    tile_e = jnp.minimum(tile_e, jnp.sum(tile_start[n_tiles[0] - 1] >= p_end, dtype=jnp.int32))
    seg = p_start[None, :] + jnp.cumsum(c, axis=0) - c
    lo = jnp.cumsum(c, axis=1) - c
    lanes = lambda a: jnp.pad(a.astype(F32), ((0, 0), (0, LANES - N_EXPERTS))).reshape(nt, 1, LANES)
    flat = lambda a: a.reshape(-1).astype(jnp.int32)
    row = lambda i, *_: (i, 0)
    per_tile = pl.BlockSpec((1, 1, LANES), lambda i, *_: (i, 0, 0))
    grid_spec = pltpu.PrefetchScalarGridSpec(
        num_scalar_prefetch=6,
        grid=(nt,),
        in_specs=[pl.BlockSpec((tm, D), row), pl.BlockSpec((tm, ROUTE_W), row), per_tile],
        out_specs=[pl.BlockSpec(memory_space=pl.ANY), pl.BlockSpec((tm, SLOT_W), row)],
        scratch_shapes=[pltpu.VMEM((2, 2 * tm * SUBLANES, LANES), F32),
                        pltpu.VMEM((te * SUBLANES, LANES), F32),
                        pltpu.SemaphoreType.DMA((2,)), pltpu.SemaphoreType.DMA(())],
    )
    tables = (flat(seg), flat(lo), flat(c))
    xg, slots = pl.pallas_call(
        _dispatch_kernel,
        grid_spec=grid_spec,
        out_shape=(jax.ShapeDtypeStruct((n_tiles_max * te * SUBLANES, LANES), F32),
                   jax.ShapeDtypeStruct((T, SLOT_W), F32)),
        compiler_params=_cparams(("arbitrary",)),
        name="dispatch",
    )(*tables, flat(p_start + count), flat(padded - count), n_tiles, n2, route, lanes(lo))
    return xg, slots, tables, tile_e, n_tiles


def _experts_kernel(tile_e_ref, n_tiles_ref, xg_ref, wg_ref, wu_ref, wd_ref, y_ref,
                    wg_bf, wu_bf, wd_bf):
    tm = xg_ref.shape[0] // SUBLANES
    i = pl.program_id(0)
    used = i < n_tiles_ref[0]

    @pl.when(used & ((i == 0) | (tile_e_ref[i] != tile_e_ref[jnp.maximum(i - 1, 0)])))
    def _():
        wg_bf[...] = wg_ref[0].astype(BF16)
        wu_bf[...] = wu_ref[0].astype(BF16)
        wd_bf[...] = wd_ref[0].astype(BF16)

    @pl.when(used)
    def _():
        x = _load_slab(xg_ref, tm).astype(BF16)
        g = jnp.dot(x, wg_bf[...], preferred_element_type=F32)
        u = jnp.dot(x, wu_bf[...], preferred_element_type=F32)
        he = (g * jax.nn.sigmoid(g)) * u
        _store_slab(y_ref, jnp.dot(he.astype(BF16), wd_bf[...], preferred_element_type=F32))

    @pl.when(i >= n_tiles_ref[0])
    def _():
        y_ref[...] = jnp.zeros_like(y_ref)


def _experts(tile_e, n_tiles, xg, wg, wu, wd, *, tm):
    D = wg.shape[1]
    n_tiles_max = xg.shape[0] // (tm * SUBLANES)
    slab = lambda f: pl.BlockSpec((tm * SUBLANES, LANES), f)
    grid_spec = pltpu.PrefetchScalarGridSpec(
        num_scalar_prefetch=2,
        grid=(n_tiles_max,),
        in_specs=[slab(lambda i, te, nt: (jnp.minimum(i, nt[0] - 1), 0)),
                  pl.BlockSpec((1, D, D_EXPERT), lambda i, te, nt: (te[i], 0, 0)),
                  pl.BlockSpec((1, D, D_EXPERT), lambda i, te, nt: (te[i], 0, 0)),
                  pl.BlockSpec((1, D_EXPERT, D), lambda i, te, nt: (te[i], 0, 0))],
        out_specs=slab(lambda i, te, nt: (i, 0)),
        scratch_shapes=[pltpu.VMEM((D, D_EXPERT), BF16), pltpu.VMEM((D, D_EXPERT), BF16),
                        pltpu.VMEM((D_EXPERT, D), BF16)],
    )
    return pl.pallas_call(
        _experts_kernel,
        grid_spec=grid_spec,
        out_shape=jax.ShapeDtypeStruct(xg.shape, F32),
        compiler_params=_cparams(("arbitrary",)),
        name="experts",
    )(tile_e, n_tiles, xg, wg, wu, wd)


def _final_kernel(seg_ref, lo_ref, cnt_ref, h_ref, ys_hbm, route_ref, slot_ref, p_ref, wple_ref,
                  pn_ref, wpg_ref, fn_ref, o_ref, ybuf, sem, *, apply_final_norm):
    tm = h_ref.shape[0]
    i = pl.program_id(0)
    nt = pl.num_programs(0)
    slot = i % 2

    def fetch(tile, s):
        def run(e, carry):
            k = tile * N_EXPERTS + e

            def piece(off, n):
                _token_copy(ys_hbm, seg_ref[k] + off, ybuf.at[s], lo_ref[k] + off, n,
                            sem.at[s]).start()

            _for_each_piece(cnt_ref[k], piece)
            return carry

        lax.fori_loop(0, N_EXPERTS, run, 0, unroll=RUN_LOOP_UNROLL)

    @pl.when(i == 0)
    def _():
        fetch(0, 0)

    @pl.when(i + 1 < nt)
    def _():
        fetch(i + 1, 1 - slot)

    pltpu.make_async_copy(ys_hbm.at[pl.ds(0, 2 * tm * SUBLANES), :], ybuf.at[slot],
                          sem.at[slot]).wait()
    y_sorted = _load_slab(ybuf.at[slot], 2 * tm).astype(BF16)
    route = route_ref[...]
    slots = slot_ref[...].astype(jnp.int32)
    col = lax.broadcasted_iota(jnp.int32, (tm, 2 * tm), 1)
    comb = jnp.where(col == slots[:, 0:1], route[:, 0:1],
                     jnp.where(col == slots[:, 1:2], route[:, 1:2], 0.0))
    h = h_ref[...] + jnp.dot(comb.astype(BF16), y_sorted, preferred_element_type=F32)
    e = _rms(jnp.dot(p_ref[0].astype(BF16), wple_ref[...], preferred_element_type=F32), pn_ref[...])
    gate = jax.nn.sigmoid(jnp.dot(h.astype(BF16), wpg_ref[...], preferred_element_type=F32))
    h = h + gate * e
    if apply_final_norm:
        h = _rms(h, fn_ref[...])
    o_ref[...] = h


def _final(h1, ys, slots, tables, route, p_all, layer, wple, ple_norm, wpg, final_norm, *, tm,
           apply_final_norm):
    T, D = h1.shape
    nt = T // tm
    row = lambda i, *_: (i, 0)
    fixed = lambda i, *_: (0, 0)
    full = lambda a: pl.BlockSpec(a.shape, fixed)
    pn = ple_norm.reshape(1, D)
    fn = final_norm.reshape(1, D)
    grid_spec = pltpu.PrefetchScalarGridSpec(
        num_scalar_prefetch=3,
        grid=(nt,),
        in_specs=[pl.BlockSpec((tm, D), row), pl.BlockSpec(memory_space=pl.ANY),
                  pl.BlockSpec((tm, ROUTE_W), row), pl.BlockSpec((tm, SLOT_W), row),
                  pl.BlockSpec((1, tm, p_all.shape[2]), lambda i, *_: (layer, i, 0)),
                  full(wple), full(pn), full(wpg), full(fn)],
        out_specs=pl.BlockSpec((tm, D), row),
        scratch_shapes=[pltpu.VMEM((2, 2 * tm * SUBLANES, LANES), F32),
                        pltpu.SemaphoreType.DMA((2,))],
    )
    return pl.pallas_call(
        functools.partial(_final_kernel, apply_final_norm=apply_final_norm),
        grid_spec=grid_spec,
        out_shape=jax.ShapeDtypeStruct((T, D), F32),
        compiler_params=_cparams(("arbitrary",)),
        name="final",
    )(*tables, h1, ys, route, slots, p_all, wple, pn, wpg, fn)


SPLIT_SIZES = (GLA_QK, GLA_QK, GLA_V, GLA_V, GLA_LOWRANK, DIFF_W, DIFF_W, DIFF_W)


def _layer(h2, p_all, layer, positions, B, S, lambda_init, apply_final_norm, attn_norm, w_in, w_a2, b_a,
           gla_norm, lq1, lk1, lq2, lk2, diff_subln, w_branch_a, w_branch_b, w_out, ffn_norm,
           w_rg, b_rg, w_re, b_re, w_gate, w_up, w_down, w_ple, ple_norm, w_ple_gate, final_norm,
           *, tm, ts, tq, tk, te):
    T, D = h2.shape
    assert D == SUBLANES * LANES, "token-slab gathers need one (8, 128) tile per token"
    n_mix = sum(SPLIT_SIZES)
    wgla = w_in[:, :C_GLA_END].astype(BF16)
    wdiff = w_in[:, C_GLA_END:n_mix].astype(BF16)
    wg = w_in[:, n_mix:].astype(BF16)

    gq, gk, gv, gr, la, dq, dk, dv = _inproj(h2, positions, attn_norm, wgla, wdiff,
                                             w_a2.astype(BF16), b_a, tm=tm)
    oa = _gla(gq, gk, gv, gr, la, gla_norm, B=B, S=S, ts=ts).reshape(T, GLA_V)
    ob = _diffattn(dq, dk, dv, lq1, lk1, lq2, lk2, diff_subln, B=B, S=S, tq=tq, tk=tk,
                   lambda_init=lambda_init).reshape(T, DIFF_W)

    wr = jnp.concatenate([w_rg, w_re, jnp.zeros((D, LANES - N_GROUPS - N_EXPERTS), F32)], axis=1)
    wr_hi = wr.astype(BF16)
    wr = jnp.concatenate([wr_hi, (wr - wr_hi.astype(F32)).astype(BF16)], axis=1)
    br = jnp.concatenate([b_rg, b_re, jnp.zeros((LANES - N_GROUPS - N_EXPERTS,), F32)]).reshape(1, LANES)
    h1, n2, route, cnt = _postmix(h2, oa, ob, attn_norm, wg, w_branch_a.astype(BF16),
                                  w_branch_b.astype(BF16), w_out.astype(BF16), ffn_norm, wr, br,
                                  tm=tm)

    xg, slots, tables, tile_e, n_tiles = _dispatch(n2, route, cnt, tm=tm, te=te)
    ys = _experts(tile_e, n_tiles, xg, w_gate, w_up, w_down, tm=te)
    return _final(h1, ys, slots, tables, route, p_all, layer, w_ple.astype(BF16), ple_norm,
                  w_ple_gate.astype(BF16), final_norm, tm=tm, apply_final_norm=apply_final_norm)


def _block(x, p, positions, attn_norm, w_in, w_a2, b_a, gla_norm, lambda_q1, lambda_k1, lambda_q2,
           lambda_k2, diff_subln, w_branch_a, w_branch_b, w_out, ffn_norm, w_router_group,
           b_router_group, w_router_expert, b_router_expert, w_gate, w_up, w_down, w_ple, ple_norm,
           w_ple_gate, final_norm, *, tm, ts, tq, tk, te):
    B, S, D = x.shape
    depth = w_in.shape[0]
    h = x.reshape(B * S, D)
    for i in range(depth):
        lambda_init = 0.8 - 0.6 * math.exp(-0.3 * i)
        h = _layer(h, p.reshape(depth, B * S, -1), i, positions, B, S, lambda_init, i == depth - 1,
                   attn_norm[i], w_in[i], w_a2[i], b_a[i], gla_norm[i], lambda_q1[i], lambda_k1[i],
                   lambda_q2[i], lambda_k2[i], diff_subln[i], w_branch_a[i], w_branch_b[i], w_out[i],
                   ffn_norm[i], w_router_group[i], b_router_group[i], w_router_expert[i],
                   b_router_expert[i], w_gate[i], w_up[i], w_down[i], w_ple[i], ple_norm[i],
                   w_ple_gate[i], final_norm, tm=tm, ts=ts, tq=tq, tk=tk, te=te)
    return h.reshape(B, S, D)


def kernel(x, p, positions, attn_norm, w_in, w_a2, b_a, gla_norm, lambda_q1, lambda_k1, lambda_q2, lambda_k2, diff_subln, w_branch_a, w_branch_b, w_out, ffn_norm, w_router_group, b_router_group, w_router_expert, b_router_expert, w_gate, w_up, w_down, w_ple, ple_norm, w_ple_gate, final_norm):
    return _block(x, p, positions, attn_norm, w_in, w_a2, b_a, gla_norm, lambda_q1, lambda_k1,
                  lambda_q2, lambda_k2, diff_subln, w_branch_a, w_branch_b, w_out, ffn_norm,
                  w_router_group, b_router_group, w_router_expert, b_router_expert, w_gate, w_up,
                  w_down, w_ple, ple_norm, w_ple_gate, final_norm, **_tile_sizes(x.shape[1]))


def _tile_sizes(seq):
    return dict(
        tm=512,
        ts=min(1024, seq),
        tq=min(512, seq),
        tk=min(512, seq),
        te=512,
    )
```
